```python
import jax, jax.numpy as jnp
from jax import lax
import numpy as np

D_MODEL = 1024
BATCH = 2
SEQ = 16384
DEPTH = 1

MIX_WIDTH = D_MODEL
RET_WIDTH = MIX_WIDTH // 2
RET_HEADS = 4
RET_HEAD_DIM = RET_WIDTH // RET_HEADS
SB_WIDTH = MIX_WIDTH - RET_WIDTH
SB_HEADS = 8
SB_HEAD_DIM = SB_WIDTH // SB_HEADS
IN_PROJ_WIDTH = 4 * RET_WIDTH + 3 * SB_WIDTH
RET_CHUNK = 128
SB_BLOCK = 128
ROPE_BASE = 10000.0
N_EXPERTS = 32
TOP_K = 4
D_FF = D_MODEL
SWIGLU_LIMIT = 7.0
SWIGLU_ALPHA = 1.702
MOE_BLOCK = 128
EPS = 1e-6

kernel_name = "hymba_retention_stickbreaking_moe"


def rmsnorm(x, g):
    xf = x.astype(jnp.float32)
    y = xf * lax.rsqrt(jnp.mean(xf * xf, axis=-1, keepdims=True) + EPS)
    return (y * g.astype(jnp.float32)).astype(x.dtype)


def rotary(x, pos):
    half = x.shape[-1] // 2
    inv_freq = ROPE_BASE ** (-jnp.arange(half, dtype=jnp.float32) / half)
    ang = pos.astype(jnp.float32)[:, None] * inv_freq[None, :]
    cos = jnp.cos(ang)[None, :, None, :]
    sin = jnp.sin(ang)[None, :, None, :]
    xf = x.astype(jnp.float32)
    x1, x2 = xf[..., :half], xf[..., half:]
    out = jnp.concatenate([x1 * cos - x2 * sin, x1 * sin + x2 * cos], axis=-1)
    return out.astype(x.dtype)


def retention_chunkwise(q, k, v):
    b, s, h, d = q.shape
    nc = s // RET_CHUNK
    log_gamma = jnp.log(1.0 - 2.0 ** (-5.0 - jnp.arange(h, dtype=jnp.float32)))
    idx = jnp.arange(RET_CHUNK, dtype=jnp.float32)
    diff = idx[:, None] - idx[None, :]
    intra = jnp.where(diff >= 0, jnp.exp(log_gamma[:, None, None] * jnp.maximum(diff, 0.0)), 0.0)
    q_decay = jnp.exp(log_gamma[:, None] * (idx + 1.0))
    k_decay = jnp.exp(log_gamma[:, None] * (RET_CHUNK - 1.0 - idx))
    chunk_decay = jnp.exp(log_gamma * RET_CHUNK)

    def to_chunks(t):
        return t.astype(jnp.float32).reshape(b, nc, RET_CHUNK, h, d).transpose(1, 0, 3, 2, 4)

    qc, kc, vc = to_chunks(q), to_chunks(k) * (d ** -0.5), to_chunks(v)

    def step(state, inp):
        qi, ki, vi = inp
        scores = jnp.einsum('bhnd,bhmd->bhnm', qi, ki) * intra[None]
        inner = jnp.einsum('bhnm,bhme->bhne', scores, vi)
        cross = jnp.einsum('bhnd,bhde->bhne', qi, state) * q_decay[None, :, :, None]
        new_state = state * chunk_decay[None, :, None, None] + jnp.einsum(
            'bhmd,bhme->bhde', ki * k_decay[None, :, :, None], vi)
        return new_state, inner + cross

    init = jnp.zeros((b, h, d, d), jnp.float32)
    _, out = lax.scan(step, init, (qc, kc, vc))
    return out.transpose(1, 0, 3, 2, 4).reshape(b, s, h, d).astype(q.dtype)


def stick_breaking_attention(q, k, v):
    b, s, h, d = q.shape
    nb = s // SB_BLOCK
    q_blocks = q.reshape(b, nb, SB_BLOCK, h, d).transpose(1, 0, 3, 2, 4)
    kt = k.transpose(0, 2, 1, 3)
    vt = v.transpose(0, 2, 1, 3)
    key_pos = jnp.arange(s)
    scale = d ** -0.5

    def block(args):
        q_blk, start = args
        z = jnp.einsum('bhqd,bhkd->bhqk', q_blk, kt).astype(jnp.float32) * scale
        q_pos = start + jnp.arange(SB_BLOCK)
        causal = key_pos[None, :] < q_pos[:, None]
        log_one_minus = jnp.where(causal, jax.nn.log_sigmoid(-z), 0.0)
        log_remaining = lax.cumsum(log_one_minus, axis=3, reverse=True) - log_one_minus
        weights = jnp.where(causal, jnp.exp(jax.nn.log_sigmoid(z) + log_remaining), 0.0)
        return jnp.einsum('bhqk,bhkd->bhqd', weights.astype(vt.dtype), vt)

    out = lax.map(block, (q_blocks, jnp.arange(nb) * SB_BLOCK))
    return out.transpose(1, 0, 3, 2, 4).reshape(b, s, h, d)


def hybrid_mixer(x, norm_g, w_in, ret_norm_g, sb_q_norm_g, sb_k_norm_g, w_out):
    b, s, _ = x.shape
    hx = rmsnorm(x, norm_g)
    proj = hx @ w_in
    splits = [RET_WIDTH * i for i in range(1, 5)] + [4 * RET_WIDTH + SB_WIDTH * i for i in range(1, 3)]
    rq, rk, rv, rg, sq, sk, sv = jnp.split(proj, splits, axis=-1)
    pos = jnp.arange(s)

    rq = rotary(rq.reshape(b, s, RET_HEADS, RET_HEAD_DIM), pos)
    rk = rotary(rk.reshape(b, s, RET_HEADS, RET_HEAD_DIM), pos)
    rv = rv.reshape(b, s, RET_HEADS, RET_HEAD_DIM)
    ret = rmsnorm(retention_chunkwise(rq, rk, rv), ret_norm_g)
    ret = ret.reshape(b, s, RET_WIDTH) * jax.nn.silu(rg)

    sq = rmsnorm(sq.reshape(b, s, SB_HEADS, SB_HEAD_DIM), sb_q_norm_g)
    sk = rmsnorm(sk.reshape(b, s, SB_HEADS, SB_HEAD_DIM), sb_k_norm_g)
    sv = sv.reshape(b, s, SB_HEADS, SB_HEAD_DIM)
    sb = stick_breaking_attention(sq, sk, sv).reshape(b, s, SB_WIDTH)

    mixed = jnp.concatenate([ret, sb.astype(ret.dtype)], axis=-1)
    return mixed @ w_out


def moe_ffn(x, norm_g, w_router, b_router, w_gate_up, b_gate_up, w_down, b_down):
    b, s, d = x.shape
    t = b * s
    xf = rmsnorm(x, norm_g).reshape(t, d)
    logits = (xf @ w_router + b_router).astype(jnp.float32)
    top_logits, top_idx = lax.top_k(logits, TOP_K)
    top_w = jax.nn.softmax(top_logits, axis=-1)

    n_assign = t * TOP_K
    flat_e = top_idx.reshape(-1)
    flat_tok = jnp.arange(n_assign, dtype=jnp.int32) // TOP_K
    flat_w = top_w.reshape(-1)
    order = jnp.argsort(flat_e, stable=True)
    sorted_e = flat_e[order]
    counts = jnp.bincount(flat_e, length=N_EXPERTS)
    starts = jnp.cumsum(counts) - counts
    padded = (counts + MOE_BLOCK - 1) // MOE_BLOCK * MOE_BLOCK
    padded_end = jnp.cumsum(padded)
    padded_start = padded_end - padded
    dest = padded_start[sorted_e] + jnp.arange(n_assign, dtype=jnp.int32) - starts[sorted_e]
    n_pad = n_assign + N_EXPERTS * MOE_BLOCK
    n_blocks = n_pad // MOE_BLOCK
    slot_tok = jnp.zeros((n_pad,), jnp.int32).at[dest].set(flat_tok[order])
    slot_w = jnp.zeros((n_pad,), jnp.float32).at[dest].set(flat_w[order])
    block_expert = jnp.minimum(
        jnp.searchsorted(padded_end, jnp.arange(n_blocks) * MOE_BLOCK, side='right'), N_EXPERTS - 1)
    x_slots = xf[slot_tok].reshape(n_blocks, MOE_BLOCK, d)

    def expert_block(args):
        xb, e = args
        gu = xb @ w_gate_up[e] + b_gate_up[e]
        gate = jnp.minimum(gu[:, 0::2], SWIGLU_LIMIT)
        up = jnp.clip(gu[:, 1::2], -SWIGLU_LIMIT, SWIGLU_LIMIT)
        hidden = (up + 1.0) * gate * jax.nn.sigmoid(SWIGLU_ALPHA * gate)
        return hidden @ w_down[e] + b_down[e]

    y_slots = lax.map(expert_block, (x_slots, block_expert)).reshape(n_pad, d)
    y = jax.ops.segment_sum(y_slots * slot_w[:, None].astype(y_slots.dtype), slot_tok, num_segments=t)
    return y.reshape(b, s, d).astype(x.dtype)


def setup_inputs(seed: int = 0) -> dict:
    key = jax.random.key(seed)
    ks = jax.random.split(key, 16)
    f32 = jnp.float32
    nrm = lambda k, shape, scale: jax.random.normal(k, shape, f32) * scale
    return {
        "x": nrm(ks[0], (BATCH, SEQ, D_MODEL), 1.0),
        "attn_norm_g": 1.0 + nrm(ks[1], (DEPTH, D_MODEL), 0.02),
        "w_in": nrm(ks[2], (DEPTH, D_MODEL, IN_PROJ_WIDTH), D_MODEL ** -0.5),
        "ret_norm_g": 1.0 + nrm(ks[3], (DEPTH, RET_HEADS, RET_HEAD_DIM), 0.02),
        "sb_q_norm_g": 1.0 + nrm(ks[4], (DEPTH, SB_HEAD_DIM), 0.02),
        "sb_k_norm_g": 1.0 + nrm(ks[5], (DEPTH, SB_HEAD_DIM), 0.02),
        "w_out": nrm(ks[6], (DEPTH, MIX_WIDTH, D_MODEL), MIX_WIDTH ** -0.5),
        "ffn_norm_g": 1.0 + nrm(ks[7], (DEPTH, D_MODEL), 0.02),
        "w_router": nrm(ks[8], (DEPTH, D_MODEL, N_EXPERTS), D_MODEL ** -0.5),
        "b_router": nrm(ks[9], (DEPTH, N_EXPERTS), 0.01),
        "w_gate_up": nrm(ks[10], (DEPTH, N_EXPERTS, D_MODEL, 2 * D_FF), D_MODEL ** -0.5),
        "b_gate_up": nrm(ks[11], (DEPTH, N_EXPERTS, 2 * D_FF), 0.02),
        "w_down": nrm(ks[12], (DEPTH, N_EXPERTS, D_FF, D_MODEL), D_FF ** -0.5),
        "b_down": nrm(ks[13], (DEPTH, N_EXPERTS, D_MODEL), 0.02),
    }


def reference(x, attn_norm_g, w_in, ret_norm_g, sb_q_norm_g, sb_k_norm_g, w_out,
              ffn_norm_g, w_router, b_router, w_gate_up, b_gate_up, w_down, b_down):
    for l in range(DEPTH):
        x = x + hybrid_mixer(x, attn_norm_g[l], w_in[l], ret_norm_g[l], sb_q_norm_g[l],
                             sb_k_norm_g[l], w_out[l])
        x = x + moe_ffn(x, ffn_norm_g[l], w_router[l], b_router[l], w_gate_up[l],
                        b_gate_up[l], w_down[l], b_down[l])
    return x
```

```python
import functools

import numpy as np
import jax
import jax.numpy as jnp
from jax import lax
from jax.experimental import pallas as pl
from jax.experimental.pallas import tpu as pltpu

D_MODEL = 1024
RET_WIDTH = 512
RET_HEADS = 4
RET_HEAD_DIM = 128
SB_WIDTH = 512
SB_HEADS = 8
SB_HEAD_DIM = 64
IN_PROJ_WIDTH = 4 * RET_WIDTH + 3 * SB_WIDTH
RET_CHUNK = 128
SB_BLOCK = 128
ROPE_BASE = 10000.0
N_EXPERTS = 32
TOP_K = 4
D_FF = D_MODEL
SWIGLU_LIMIT = 7.0
SWIGLU_ALPHA = 1.702
EPS = 1e-6

LANES = 128
F32_EXP_UNDERFLOW = -88.0
VMEM_LIMIT = 56 * 1024 * 1024

BF16 = jnp.bfloat16
F32 = jnp.float32


def _split_bf16(v):
    hi = v.astype(BF16)
    lo = (v - hi.astype(F32)).astype(BF16)
    return hi, lo


def _dot(a, b):
    return jnp.dot(a, b, preferred_element_type=F32)


def _dot_nt(a, b):
    return lax.dot_general(a, b, (((1,), (1,)), ((), ())), preferred_element_type=F32)


def _in_proj_kernel(x_ref, g_ref, w_ref, cos_ref, sin_ref, qg_ref, kg_ref, seg_ref,
                    rq_ref, rk_ref, rv_ref, rg_ref, sq_ref, sk_ref, sv_ref):
    x = x_ref[...]
    h = x * lax.rsqrt(jnp.mean(x * x, axis=-1, keepdims=True) + EPS) * g_ref[...]
    p = _dot(h.astype(BF16), w_ref[...])
    cos2 = cos_ref[...]
    sin2 = sin_ref[...]
    k_scale = RET_HEAD_DIM ** -0.5
    for hd in range(RET_HEADS):
        lo = hd * RET_HEAD_DIM
        q = p[:, lo:lo + RET_HEAD_DIM]
        k = p[:, RET_WIDTH + lo:RET_WIDTH + lo + RET_HEAD_DIM]
        q = q * cos2 + pltpu.roll(q, RET_HEAD_DIM // 2, axis=1) * sin2
        k = k * cos2 + pltpu.roll(k, RET_HEAD_DIM // 2, axis=1) * sin2
        rq_ref[:, lo:lo + RET_HEAD_DIM] = q.astype(rq_ref.dtype)
        rk_ref[:, lo:lo + RET_HEAD_DIM] = (k * k_scale).astype(rk_ref.dtype)
    rv_ref[...] = p[:, 2 * RET_WIDTH:3 * RET_WIDTH].astype(rv_ref.dtype)
    rg_ref[...] = p[:, 3 * RET_WIDTH:4 * RET_WIDTH]
    base = 4 * RET_WIDTH
    seg = seg_ref[...]

    def head_norm(v, gain):
        hi, lo = _split_bf16(v * v)
        ms = (_dot(hi, seg) + _dot(lo, seg)) * (1.0 / SB_HEAD_DIM)
        return v * lax.rsqrt(ms + EPS) * gain

    sq = head_norm(p[:, base:base + SB_WIDTH], qg_ref[...])
    sk = head_norm(p[:, base + SB_WIDTH:base + 2 * SB_WIDTH], kg_ref[...])
    sq_ref[...] = (sq * (SB_HEAD_DIM ** -0.5)).astype(sq_ref.dtype)
    sk_ref[...] = sk.astype(sk_ref.dtype)
    sv_ref[...] = p[:, base + 2 * SB_WIDTH:base + 3 * SB_WIDTH].astype(sv_ref.dtype)


def _rope_tables(seq):
    half = RET_HEAD_DIM // 2
    inv = ROPE_BASE ** (-np.arange(half, dtype=np.float64) / half)
    n_a = seq // LANES
    ang_a = (np.arange(n_a, dtype=np.float64) * LANES)[:, None] * inv[None, :]
    ang_b = np.arange(LANES, dtype=np.float64)[:, None] * inv[None, :]
    ca, sa = jnp.asarray(np.cos(ang_a), F32), jnp.asarray(np.sin(ang_a), F32)
    cb, sb = jnp.asarray(np.cos(ang_b), F32), jnp.asarray(np.sin(ang_b), F32)
    cos = (ca[:, None, :] * cb[None] - sa[:, None, :] * sb[None]).reshape(seq, half)
    sin = (sa[:, None, :] * cb[None] + ca[:, None, :] * sb[None]).reshape(seq, half)
    return jnp.concatenate([cos, cos], axis=-1), jnp.concatenate([-sin, sin], axis=-1)


def _in_proj(x2, norm_g, w_in, sb_q_g, sb_k_g, seq, tm):
    t = x2.shape[0]
    assert seq % tm == 0 and t % tm == 0
    cos2, sin2 = _rope_tables(seq)
    seg = np.kron(np.eye(SB_HEADS), np.ones((SB_HEAD_DIM, SB_HEAD_DIM)))
    seg = jnp.asarray(seg, BF16)
    qg = jnp.tile(sb_q_g.astype(F32), SB_HEADS)[None, :]
    kg = jnp.tile(sb_k_g.astype(F32), SB_HEADS)[None, :]
    n_pos = seq // tm
    row = lambda w: pl.BlockSpec((tm, w), lambda i: (i, 0))
    full = lambda a: pl.BlockSpec(a.shape, lambda i: (0,) * a.ndim)
    pos = pl.BlockSpec((tm, RET_HEAD_DIM), lambda i: (i % n_pos, 0))
    w_bf = w_in.astype(BF16)
    g2 = norm_g.astype(F32)[None, :]
    out = lambda dt: jax.ShapeDtypeStruct((t, RET_WIDTH), dt)
    return pl.pallas_call(
        _in_proj_kernel,
        grid=(t // tm,),
        in_specs=[row(D_MODEL), full(g2), full(w_bf), pos, pos, full(qg), full(kg), full(seg)],
        out_specs=[row(RET_WIDTH)] * 7,
        out_shape=[out(BF16), out(F32), out(BF16), out(F32), out(BF16), out(BF16), out(BF16)],
        compiler_params=pltpu.CompilerParams(
            dimension_semantics=("arbitrary",), vmem_limit_bytes=VMEM_LIMIT),
        name="in_proj",
    )(x2, g2, w_bf, cos2, sin2, qg, kg, seg)


def _retention_kernel(q_ref, k_ref, v_ref, g_ref, ng_ref, intra_ref, qd_ref, kd_ref, cd_ref,
                      o_ref, state_ref, *, chunks):
    @pl.when(pl.program_id(1) == 0)
    def _():
        state_ref[...] = jnp.zeros_like(state_ref)

    def chunk(c, carry):
        rows = pl.ds(pl.multiple_of(c * RET_CHUNK, RET_CHUNK), RET_CHUNK)
        for hd in range(RET_HEADS):
            cols = slice(hd * RET_HEAD_DIM, (hd + 1) * RET_HEAD_DIM)
            q = q_ref[rows, cols]
            k = k_ref[rows, cols]
            v = v_ref[rows, cols]
            state = state_ref[hd]
            scores = _dot_nt(q, k.astype(BF16)) * intra_ref[hd]
            inner = _dot(scores.astype(BF16), v)
            cross = _dot(q, state.astype(BF16)) * qd_ref[:, cols]
            kd = (k * kd_ref[:, cols]).astype(BF16)
            state_ref[hd] = state * cd_ref[:, cols] + _dot(kd.T, v)
            o = inner + cross
            o = o * lax.rsqrt(jnp.mean(o * o, axis=-1, keepdims=True) + EPS) * ng_ref[:, cols]
            gate = g_ref[rows, cols]
            o_ref[rows, cols] = (o * (gate * jax.nn.sigmoid(gate))).astype(o_ref.dtype)
        return carry

    lax.fori_loop(0, chunks, chunk, 0)


def _retention(rq, rk, rv, rg, ret_norm_g, batch, seq, rows):
    assert seq % rows == 0 and rows % RET_CHUNK == 0
    log_gamma = np.log(1.0 - 2.0 ** (-5.0 - np.arange(RET_HEADS, dtype=np.float64)))
    idx = np.arange(RET_CHUNK, dtype=np.float64)
    diff = idx[:, None] - idx[None, :]
    intra = np.where(diff >= 0, np.exp(log_gamma[:, None, None] * np.maximum(diff, 0.0)), 0.0)
    q_decay = np.exp(log_gamma[:, None] * (idx + 1.0))
    k_decay = np.exp(log_gamma[:, None] * (RET_CHUNK - 1.0 - idx))
    chunk_decay = np.exp(log_gamma * RET_CHUNK)
    lane_rep = lambda a: np.repeat(a.T, RET_HEAD_DIM, axis=1)
    intra = jnp.asarray(intra, F32)
    qd = jnp.asarray(lane_rep(q_decay), F32)
    kd = jnp.asarray(lane_rep(k_decay), F32)
    cd = jnp.asarray(np.repeat(chunk_decay, RET_HEAD_DIM)[None, :], F32)
    ng = ret_norm_g.astype(F32).reshape(1, RET_WIDTH)
    n_r = seq // rows
    blk = pl.BlockSpec((rows, RET_WIDTH), lambda b, r: (b * n_r + r, 0))
    full = lambda a: pl.BlockSpec(a.shape, lambda b, r: (0,) * a.ndim)
    return pl.pallas_call(
        functools.partial(_retention_kernel, chunks=rows // RET_CHUNK),
        grid=(batch, n_r),
        in_specs=[blk, blk, blk, blk, full(ng), full(intra), full(qd), full(kd), full(cd)],
        out_specs=blk,
        out_shape=jax.ShapeDtypeStruct((batch * seq, RET_WIDTH), BF16),
        scratch_shapes=[pltpu.VMEM((RET_HEADS, RET_HEAD_DIM, RET_HEAD_DIM), F32)],
        compiler_params=pltpu.CompilerParams(
            dimension_semantics=("arbitrary", "arbitrary"), vmem_limit_bytes=VMEM_LIMIT),
        name="retention",
    )(rq, rk, rv, rg, ng, intra, qd, kd, cd)


def _sb_attn_kernel(q_ref, k_ref, v_ref, tri_ref, o_ref):
    i = pl.program_id(2)
    q = q_ref[...]
    lane = lax.broadcasted_iota(jnp.int32, (SB_BLOCK, LANES), 1)
    qpos = lax.broadcasted_iota(jnp.int32, (SB_BLOCK, SB_BLOCK), 0)
    kpos = lax.broadcasted_iota(jnp.int32, (SB_BLOCK, SB_BLOCK), 1)
    below_diag = kpos < qpos
    tri = tri_ref[...]

    def one_head(hd):
        in_head = (lane >= hd * SB_HEAD_DIM) & (lane < (hd + 1) * SB_HEAD_DIM)
        qm = jnp.where(in_head, q, jnp.zeros_like(q))

        def cond(carry):
            j, c, _ = carry
            return jnp.logical_and(j >= 0, jnp.max(c) > F32_EXP_UNDERFLOW)

        def body(carry):
            j, c, acc = carry
            rows = pl.ds(pl.multiple_of(j * SB_BLOCK, SB_BLOCK), SB_BLOCK)
            kb = k_ref[rows, :]
            vb = v_ref[rows, :]
            z = _dot_nt(qm, kb)
            log_beta = jnp.minimum(z, 0.0) - jnp.log1p(jnp.exp(-jnp.abs(z)))
            causal = jnp.logical_or(j < i, below_diag)
            log_rest = jnp.where(causal, log_beta - z, 0.0)
            hi, lo = _split_bf16(log_rest)
            sums = _dot(hi, tri) + _dot(lo, tri)
            log_remaining = c + sums[:, :SB_BLOCK]
            w = jnp.where(causal, jnp.exp(log_beta + log_remaining), 0.0)
            acc = acc + _dot(w.astype(BF16), vb)
            return j - 1, c + sums[:, SB_BLOCK:], acc

        zeros = jnp.zeros((SB_BLOCK, LANES), F32)
        _, _, acc = lax.while_loop(cond, body, (i, zeros, zeros))
        return acc

    acc0 = one_head(0)
    acc1 = one_head(1)
    o_ref[...] = jnp.where(lane < SB_HEAD_DIM, acc0, acc1).astype(o_ref.dtype)


def _sb_attention(sq, sk, sv, batch, seq):
    assert seq % SB_BLOCK == 0
    nq = seq // SB_BLOCK
    pairs = SB_WIDTH // LANES
    j = np.arange(SB_BLOCK)
    tri = np.concatenate([(j[:, None] > j[None, :]).astype(np.float32),
                          np.ones((SB_BLOCK, SB_BLOCK), np.float32)], axis=1)
    tri = jnp.asarray(tri, BF16)
    qblk = pl.BlockSpec((SB_BLOCK, LANES), lambda b, p, i: (b * nq + i, p))
    kvblk = pl.BlockSpec((seq, LANES), lambda b, p, i: (b, p))
    return pl.pallas_call(
        _sb_attn_kernel,
        grid=(batch, pairs, nq),
        in_specs=[qblk, kvblk, kvblk, pl.BlockSpec(tri.shape, lambda b, p, i: (0, 0))],
        out_specs=qblk,
        out_shape=jax.ShapeDtypeStruct((batch * seq, SB_WIDTH), BF16),
        compiler_params=pltpu.CompilerParams(
            dimension_semantics=("arbitrary", "arbitrary", "arbitrary"),
            vmem_limit_bytes=VMEM_LIMIT),
        name="sb_attn",
    )(sq, sk, sv, tri)


def _out_proj_kernel(ret_ref, sb_ref, x_ref, wtop_ref, wbot_ref, g_ref, wrh_ref, wrl_ref,
                     br_ref, tri_ref,
                     x1_ref, xf_ref, idx_ref, w_ref, rank_ref, cnt_ref, run_ref):
    @pl.when(pl.program_id(0) == 0)
    def _():
        run_ref[...] = jnp.zeros_like(run_ref)

    x1 = x_ref[...] + _dot(ret_ref[...], wtop_ref[...]) + _dot(sb_ref[...], wbot_ref[...])
    x1_ref[...] = x1
    xf = x1 * lax.rsqrt(jnp.mean(x1 * x1, axis=-1, keepdims=True) + EPS) * g_ref[...]
    xf_ref[...] = xf
    xh, xl = _split_bf16(xf)
    wrh = wrh_ref[...]
    logits = _dot_nt(wrh, xh) + _dot_nt(wrh, xl) + _dot_nt(wrl_ref[...], xh) + br_ref[...]
    n_e, tm = logits.shape
    e_iota = lax.broadcasted_iota(jnp.int32, (n_e, tm), 0)
    cur = logits
    tops, sels, hots = [], [], []
    for _ in range(TOP_K):
        m = jnp.max(cur, axis=0, keepdims=True)
        sel = jnp.min(jnp.where(cur == m, e_iota, n_e), axis=0, keepdims=True)
        hot = e_iota == sel
        cur = jnp.where(hot, -jnp.inf, cur)
        tops.append(m)
        sels.append(sel)
        hots.append(hot)
    ps = [jnp.exp(m - tops[0]) for m in tops]
    denom = ps[0] + ps[1] + ps[2] + ps[3]
    chosen = jnp.zeros((n_e, tm), F32)
    for hot in hots:
        chosen = chosen + hot.astype(F32)
    before = run_ref[:, 0:1] + _dot(chosen.astype(BF16), tri_ref[...])
    for kk in range(TOP_K):
        idx_ref[kk:kk + 1, :] = sels[kk]
        w_ref[kk:kk + 1, :] = ps[kk] / denom
        rank = jnp.sum(jnp.where(hots[kk], before, 0.0), axis=0, keepdims=True)
        rank_ref[kk:kk + 1, :] = rank.astype(jnp.int32)
    run = run_ref[...] + jnp.sum(chosen, axis=1, keepdims=True)
    run_ref[...] = run
    cnt_ref[...] = run.astype(jnp.int32)


def _out_proj_route(ret, sb, x2, w_out, ffn_g, w_router, b_router, tm):
    t = x2.shape[0]
    assert t % tm == 0
    w_bf = w_out.astype(BF16)
    wtop, wbot = w_bf[:RET_WIDTH], w_bf[RET_WIDTH:]
    g2 = ffn_g.astype(F32)[None, :]
    wr_t = w_router.astype(F32).T
    wrh = wr_t.astype(BF16)
    wrl = (wr_t - wrh.astype(F32)).astype(BF16)
    br = b_router.astype(F32)[:, None]
    tt = np.arange(tm)
    tri = jnp.asarray((tt[:, None] < tt[None, :]).astype(np.float32), BF16)
    row = lambda w: pl.BlockSpec((tm, w), lambda i: (i, 0))
    full = lambda a: pl.BlockSpec(a.shape, lambda i: (0,) * a.ndim)
    col = pl.BlockSpec((TOP_K, tm), lambda i: (0, i))
    return pl.pallas_call(
        _out_proj_kernel,
        grid=(t // tm,),
        in_specs=[row(RET_WIDTH), row(SB_WIDTH), row(D_MODEL), full(wtop), full(wbot), full(g2),
                  full(wrh), full(wrl), full(br), full(tri)],
        out_specs=[row(D_MODEL), row(D_MODEL), col, col, col,
                   pl.BlockSpec((N_EXPERTS, LANES), lambda i: (0, 0))],
        out_shape=[jax.ShapeDtypeStruct((t, D_MODEL), F32),
                   jax.ShapeDtypeStruct((t, D_MODEL), F32),
                   jax.ShapeDtypeStruct((TOP_K, t), jnp.int32),
                   jax.ShapeDtypeStruct((TOP_K, t), F32),
                   jax.ShapeDtypeStruct((TOP_K, t), jnp.int32),
                   jax.ShapeDtypeStruct((N_EXPERTS, LANES), jnp.int32)],
        scratch_shapes=[pltpu.VMEM((N_EXPERTS, LANES), F32)],
        compiler_params=pltpu.CompilerParams(
            dimension_semantics=("arbitrary",), vmem_limit_bytes=VMEM_LIMIT),
        name="out_proj_route",
    )(ret, sb, x2, wtop, wbot, g2, wrh, wrl, br, tri)


def _dispatch_kernel(dest_ref, xf_hbm, zeros_hbm, slots_hbm, sem, *, tm):
    del zeros_hbm
    base = pl.program_id(0) * tm

    def row_copy(t, kk):
        return pltpu.make_async_copy(
            xf_hbm.at[pl.ds(base + t, 1), :],
            slots_hbm.at[pl.ds(dest_ref[kk, t], 1), :], sem)

    def start(t, carry):
        for kk in range(TOP_K):
            row_copy(t, kk).start()
        return carry

    def wait(t, carry):
        for kk in range(TOP_K):
            row_copy(t, kk).wait()
        return carry

    lax.fori_loop(0, tm, start, 0)
    lax.fori_loop(0, tm, wait, 0)


def _dispatch(dest, xf, n_pad, tm):
    t = xf.shape[0]
    assert t % tm == 0
    zeros = jnp.zeros((n_pad, D_MODEL), F32)
    return pl.pallas_call(
        functools.partial(_dispatch_kernel, tm=tm),
        grid=(t // tm,),
        in_specs=[pl.BlockSpec((TOP_K, tm), lambda i: (0, i), memory_space=pltpu.SMEM),
                  pl.BlockSpec(memory_space=pl.ANY),
                  pl.BlockSpec(memory_space=pl.ANY)],
        out_specs=pl.BlockSpec(memory_space=pl.ANY),
        out_shape=jax.ShapeDtypeStruct((n_pad, D_MODEL), F32),
        scratch_shapes=[pltpu.SemaphoreType.DMA],
        input_output_aliases={2: 0},
        compiler_params=pltpu.CompilerParams(
            dimension_semantics=("arbitrary",), has_side_effects=True),
        name="dispatch",
    )(dest, xf, zeros)


def _experts_kernel(be_ref, nu_ref, x_ref, wg_ref, wu_ref, wd_ref, bg_ref, bu_ref, bd_ref, y_ref):
    del be_ref
    used = pl.program_id(0) < nu_ref[0]

    @pl.when(used)
    def _():
        xb = x_ref[...].astype(BF16)
        gate = jnp.minimum(_dot(xb, wg_ref[...]) + bg_ref[...], SWIGLU_LIMIT)
        up = jnp.clip(_dot(xb, wu_ref[...]) + bu_ref[...], -SWIGLU_LIMIT, SWIGLU_LIMIT)
        hidden = (up + 1.0) * gate * jax.nn.sigmoid(SWIGLU_ALPHA * gate)
        y_ref[...] = _dot(hidden.astype(BF16), wd_ref[...]) + bd_ref[...]

    @pl.when(jnp.logical_not(used))
    def _():
        y_ref[...] = jnp.zeros_like(y_ref)


def _experts(block_expert, n_used, slots, wg, wu, wd, bg, bu, bd, blk):
    n_pad = slots.shape[0]
    n_blocks = n_pad // blk
    xmap = lambda j, be, nu: (jnp.minimum(j, nu[0] - 1), 0)
    wmap = lambda j, be, nu: (be[j], 0, 0)
    wspec = pl.BlockSpec((None, D_MODEL, D_FF), wmap)
    bspec = pl.BlockSpec((None, 1, D_FF), wmap)
    return pl.pallas_call(
        _experts_kernel,
        grid_spec=pltpu.PrefetchScalarGridSpec(
            num_scalar_prefetch=2,
            grid=(n_blocks,),
            in_specs=[pl.BlockSpec((blk, D_MODEL), xmap), wspec, wspec, wspec, bspec, bspec, bspec],
            out_specs=pl.BlockSpec((blk, D_MODEL), lambda j, be, nu: (j, 0)),
        ),
        out_shape=jax.ShapeDtypeStruct((n_pad, D_MODEL), F32),
        compiler_params=pltpu.CompilerParams(
            dimension_semantics=("arbitrary",), vmem_limit_bytes=VMEM_LIMIT),
        name="experts",
    )(block_expert, n_used, slots, wg, wu, wd, bg, bu, bd)


def _combine_kernel(dest_ref, y_hbm, w_ref, x1_ref, o_ref, buf, sem, *, tm):
    def row_copy(t, kk):
        return pltpu.make_async_copy(
            y_hbm.at[pl.ds(dest_ref[kk, t], 1), :],
            buf.at[kk, pl.ds(t, 1), :], sem)

    def start(t, carry):
        for kk in range(TOP_K):
            row_copy(t, kk).start()
        return carry

    def wait(t, carry):
        for kk in range(TOP_K):
            row_copy(t, kk).wait()
        return carry

    lax.fori_loop(0, tm, start, 0)
    lax.fori_loop(0, tm, wait, 0)
    w = w_ref[...]
    acc = x1_ref[...]
    for kk in range(TOP_K):
        acc = acc + buf[kk] * w[:, kk:kk + 1]
    o_ref[...] = acc


def _combine(dest, y_slots, w_t, x1, tm):
    t = x1.shape[0]
    assert t % tm == 0
    row = pl.BlockSpec((tm, D_MODEL), lambda i: (i, 0))
    return pl.pallas_call(
        functools.partial(_combine_kernel, tm=tm),
        grid=(t // tm,),
        in_specs=[pl.BlockSpec((TOP_K, tm), lambda i: (0, i), memory_space=pltpu.SMEM),
                  pl.BlockSpec(memory_space=pl.ANY),
                  pl.BlockSpec((tm, TOP_K), lambda i: (i, 0)),
                  row],
        out_specs=row,
        out_shape=jax.ShapeDtypeStruct((t, D_MODEL), F32),
        scratch_shapes=[pltpu.VMEM((TOP_K, tm, D_MODEL), F32), pltpu.SemaphoreType.DMA],
        compiler_params=pltpu.CompilerParams(
            dimension_semantics=("arbitrary",), vmem_limit_bytes=VMEM_LIMIT),
        name="combine",
    )(dest, y_slots, w_t, x1)


def _tiles(batch, seq):
    return dict(
        in_proj=min(512, seq),
        retention=min(1024, seq),
        out_proj=min(512, seq),
        dispatch=min(1024, seq),
        combine=min(256, seq),
        expert_block=256,
    )


def _layer(x, attn_norm_g, w_in, ret_norm_g, sb_q_norm_g, sb_k_norm_g, w_out,
           ffn_norm_g, w_router, b_router, w_gate_up, b_gate_up, w_down, b_down):
    batch, seq, d = x.shape
    t = batch * seq
    tiles = _tiles(batch, seq)
    x2 = x.reshape(t, d)

    rq, rk, rv, rg, sq, sk, sv = _in_proj(
        x2, attn_norm_g, w_in, sb_q_norm_g, sb_k_norm_g, seq, tiles["in_proj"])
    ret = _retention(rq, rk, rv, rg, ret_norm_g, batch, seq, tiles["retention"])
    sb = _sb_attention(sq, sk, sv, batch, seq)
    x1, xf, top_idx, top_w, rank, counts = _out_proj_route(
        ret, sb, x2, w_out, ffn_norm_g, w_router, b_router, tiles["out_proj"])

    blk = tiles["expert_block"]
    n_pad = t * TOP_K + N_EXPERTS * blk
    n_blocks = n_pad // blk
    counts = counts[:, 0]
    padded = (counts + blk - 1) // blk * blk
    padded_end = jnp.cumsum(padded)
    padded_start = padded_end - padded
    dest = padded_start[top_idx] + rank
    block_expert = jnp.minimum(
        jnp.searchsorted(padded_end, jnp.arange(n_blocks, dtype=jnp.int32) * blk, side="right"),
        N_EXPERTS - 1).astype(jnp.int32)
    n_used = (padded_end[-1:] // blk).astype(jnp.int32)

    slots = _dispatch(dest, xf, n_pad, tiles["dispatch"])

    wgu = w_gate_up.reshape(N_EXPERTS, D_MODEL, D_FF, 2)
    wg = wgu[..., 0].astype(BF16)
    wu = wgu[..., 1].astype(BF16)
    bgu = b_gate_up.astype(F32).reshape(N_EXPERTS, 1, D_FF, 2)
    y_slots = _experts(block_expert, n_used, slots, wg, wu, w_down.astype(BF16),
                       bgu[..., 0], bgu[..., 1], b_down.astype(F32)[:, None, :], blk)

    out = _combine(dest, y_slots, top_w.T, x1, tiles["combine"])
    return out.reshape(batch, seq, d)


def kernel(x, attn_norm_g, w_in, ret_norm_g, sb_q_norm_g, sb_k_norm_g, w_out, ffn_norm_g,
           w_router, b_router, w_gate_up, b_gate_up, w_down, b_down):
    depth = attn_norm_g.shape[0]
    for l in range(depth):
        x = _layer(x, attn_norm_g[l], w_in[l], ret_norm_g[l], sb_q_norm_g[l], sb_k_norm_g[l],
                   w_out[l], ffn_norm_g[l], w_router[l], b_router[l], w_gate_up[l],
                   b_gate_up[l], w_down[l], b_down[l])
    return x
```

```python
import functools

import numpy as np
import jax
import jax.numpy as jnp
from jax import lax
from jax.experimental import pallas as pl
from jax.experimental.pallas import tpu as pltpu

D_MODEL = 1024
RET_WIDTH = 512
RET_HEADS = 4
RET_HEAD_DIM = 128
SB_WIDTH = 512
SB_HEADS = 8
SB_HEAD_DIM = 64
IN_PROJ_WIDTH = 4 * RET_WIDTH + 3 * SB_WIDTH
RET_CHUNK = 128
SB_BLOCK = 128
ROPE_BASE = 10000.0
N_EXPERTS = 32
TOP_K = 4
D_FF = D_MODEL
SWIGLU_LIMIT = 7.0
SWIGLU_ALPHA = 1.702
EPS = 1e-6

LANES = 128
F32_EXP_UNDERFLOW = -88.0
VMEM_LIMIT = 56 * 1024 * 1024

BF16 = jnp.bfloat16
F32 = jnp.float32


def _split_bf16(v):
    hi = v.astype(BF16)
    lo = (v - hi.astype(F32)).astype(BF16)
    return hi, lo


def _dot(a, b):
    return jnp.dot(a, b, preferred_element_type=F32)


def _dot_nt(a, b):
    return lax.dot_general(a, b, (((1,), (1,)), ((), ())), preferred_element_type=F32)


def _in_proj_kernel(x_ref, g_ref, w_ref, cos_ref, sin_ref, qg_ref, kg_ref, seg_ref,
                    rq_ref, rk_ref, rv_ref, rg_ref, sq_ref, sk_ref, sv_ref):
    x = x_ref[...]
    h = x * lax.rsqrt(jnp.mean(x * x, axis=-1, keepdims=True) + EPS) * g_ref[...]
    p = _dot(h.astype(BF16), w_ref[...])
    cos2 = cos_ref[...]
    sin2 = sin_ref[...]
    k_scale = RET_HEAD_DIM ** -0.5
    for hd in range(RET_HEADS):
        lo = hd * RET_HEAD_DIM
        q = p[:, lo:lo + RET_HEAD_DIM]
        k = p[:, RET_WIDTH + lo:RET_WIDTH + lo + RET_HEAD_DIM]
        q = q * cos2 + pltpu.roll(q, RET_HEAD_DIM // 2, axis=1) * sin2
        k = k * cos2 + pltpu.roll(k, RET_HEAD_DIM // 2, axis=1) * sin2
        rq_ref[:, lo:lo + RET_HEAD_DIM] = q.astype(rq_ref.dtype)
        rk_ref[:, lo:lo + RET_HEAD_DIM] = (k * k_scale).astype(rk_ref.dtype)
    rv_ref[...] = p[:, 2 * RET_WIDTH:3 * RET_WIDTH].astype(rv_ref.dtype)
    rg_ref[...] = p[:, 3 * RET_WIDTH:4 * RET_WIDTH]
    base = 4 * RET_WIDTH
    seg = seg_ref[...]

    def head_norm(v, gain):
        hi, lo = _split_bf16(v * v)
        ms = (_dot(hi, seg) + _dot(lo, seg)) * (1.0 / SB_HEAD_DIM)
        return v * lax.rsqrt(ms + EPS) * gain

    sq = head_norm(p[:, base:base + SB_WIDTH], qg_ref[...])
    sk = head_norm(p[:, base + SB_WIDTH:base + 2 * SB_WIDTH], kg_ref[...])
    sq_ref[...] = (sq * (SB_HEAD_DIM ** -0.5)).astype(sq_ref.dtype)
    sk_ref[...] = sk.astype(sk_ref.dtype)
    sv_ref[...] = p[:, base + 2 * SB_WIDTH:base + 3 * SB_WIDTH].astype(sv_ref.dtype)


def _rope_tables(seq):
    half = RET_HEAD_DIM // 2
    inv = ROPE_BASE ** (-np.arange(half, dtype=np.float64) / half)
    n_a = seq // LANES
    ang_a = (np.arange(n_a, dtype=np.float64) * LANES)[:, None] * inv[None, :]
    ang_b = np.arange(LANES, dtype=np.float64)[:, None] * inv[None, :]
    ca, sa = jnp.asarray(np.cos(ang_a), F32), jnp.asarray(np.sin(ang_a), F32)
    cb, sb = jnp.asarray(np.cos(ang_b), F32), jnp.asarray(np.sin(ang_b), F32)
    cos = (ca[:, None, :] * cb[None] - sa[:, None, :] * sb[None]).reshape(seq, half)
    sin = (sa[:, None, :] * cb[None] + ca[:, None, :] * sb[None]).reshape(seq, half)
    return jnp.concatenate([cos, cos], axis=-1), jnp.concatenate([-sin, sin], axis=-1)


def _in_proj(x2, norm_g, w_in, sb_q_g, sb_k_g, seq, tm):
    t = x2.shape[0]
    assert seq % tm == 0 and t % tm == 0
    cos2, sin2 = _rope_tables(seq)
    seg = np.kron(np.eye(SB_HEADS), np.ones((SB_HEAD_DIM, SB_HEAD_DIM)))
    seg = jnp.asarray(seg, BF16)
    qg = jnp.tile(sb_q_g.astype(F32), SB_HEADS)[None, :]
    kg = jnp.tile(sb_k_g.astype(F32), SB_HEADS)[None, :]
    n_pos = seq // tm
    row = lambda w: pl.BlockSpec((tm, w), lambda i: (i, 0))
    full = lambda a: pl.BlockSpec(a.shape, lambda i: (0,) * a.ndim)
    pos = pl.BlockSpec((tm, RET_HEAD_DIM), lambda i: (i % n_pos, 0))
    w_bf = w_in.astype(BF16)
    g2 = norm_g.astype(F32)[None, :]
    out = lambda dt: jax.ShapeDtypeStruct((t, RET_WIDTH), dt)
    return pl.pallas_call(
        _in_proj_kernel,
        grid=(t // tm,),
        in_specs=[row(D_MODEL), full(g2), full(w_bf), pos, pos, full(qg), full(kg), full(seg)],
        out_specs=[row(RET_WIDTH)] * 7,
        out_shape=[out(BF16), out(F32), out(BF16), out(F32), out(BF16), out(BF16), out(BF16)],
        compiler_params=pltpu.CompilerParams(
            dimension_semantics=("arbitrary",), vmem_limit_bytes=VMEM_LIMIT),
        name="in_proj",
    )(x2, g2, w_bf, cos2, sin2, qg, kg, seg)


def _retention_kernel(q_ref, k_ref, v_ref, g_ref, ng_ref, intra_ref, qd_ref, kd_ref, cd_ref,
                      o_ref, state_ref, *, chunks):
    @pl.when(pl.program_id(1) == 0)
    def _():
        state_ref[...] = jnp.zeros_like(state_ref)

    def chunk(c, carry):
        rows = pl.ds(pl.multiple_of(c * RET_CHUNK, RET_CHUNK), RET_CHUNK)
        for hd in range(RET_HEADS):
            cols = slice(hd * RET_HEAD_DIM, (hd + 1) * RET_HEAD_DIM)
            q = q_ref[rows, cols]
            k = k_ref[rows, cols]
            v = v_ref[rows, cols]
            state = state_ref[hd]
            scores = _dot_nt(q, k.astype(BF16)) * intra_ref[hd]
            inner = _dot(scores.astype(BF16), v)
            cross = _dot(q, state.astype(BF16)) * qd_ref[:, cols]
            kd = (k * kd_ref[:, cols]).astype(BF16)
            state_ref[hd] = state * cd_ref[:, cols] + _dot(kd.T, v)
            o = inner + cross
            o = o * lax.rsqrt(jnp.mean(o * o, axis=-1, keepdims=True) + EPS) * ng_ref[:, cols]
            gate = g_ref[rows, cols]
            o_ref[rows, cols] = (o * (gate * jax.nn.sigmoid(gate))).astype(o_ref.dtype)
        return carry

    lax.fori_loop(0, chunks, chunk, 0)


def _retention(rq, rk, rv, rg, ret_norm_g, batch, seq, rows):
    assert seq % rows == 0 and rows % RET_CHUNK == 0
    log_gamma = np.log(1.0 - 2.0 ** (-5.0 - np.arange(RET_HEADS, dtype=np.float64)))
    idx = np.arange(RET_CHUNK, dtype=np.float64)
    diff = idx[:, None] - idx[None, :]
    intra = np.where(diff >= 0, np.exp(log_gamma[:, None, None] * np.maximum(diff, 0.0)), 0.0)
    q_decay = np.exp(log_gamma[:, None] * (idx + 1.0))
    k_decay = np.exp(log_gamma[:, None] * (RET_CHUNK - 1.0 - idx))
    chunk_decay = np.exp(log_gamma * RET_CHUNK)
    lane_rep = lambda a: np.repeat(a.T, RET_HEAD_DIM, axis=1)
    intra = jnp.asarray(intra, F32)
    qd = jnp.asarray(lane_rep(q_decay), F32)
    kd = jnp.asarray(lane_rep(k_decay), F32)
    cd = jnp.asarray(np.repeat(chunk_decay, RET_HEAD_DIM)[None, :], F32)
    ng = ret_norm_g.astype(F32).reshape(1, RET_WIDTH)
    n_r = seq // rows
    blk = pl.BlockSpec((rows, RET_WIDTH), lambda b, r: (b * n_r + r, 0))
    full = lambda a: pl.BlockSpec(a.shape, lambda b, r: (0,) * a.ndim)
    return pl.pallas_call(
        functools.partial(_retention_kernel, chunks=rows // RET_CHUNK),
        grid=(batch, n_r),
        in_specs=[blk, blk, blk, blk, full(ng), full(intra), full(qd), full(kd), full(cd)],
        out_specs=blk,
        out_shape=jax.ShapeDtypeStruct((batch * seq, RET_WIDTH), BF16),
        scratch_shapes=[pltpu.VMEM((RET_HEADS, RET_HEAD_DIM, RET_HEAD_DIM), F32)],
        compiler_params=pltpu.CompilerParams(
            dimension_semantics=("arbitrary", "arbitrary"), vmem_limit_bytes=VMEM_LIMIT),
        name="retention",
    )(rq, rk, rv, rg, ng, intra, qd, kd, cd)


def _sb_attn_kernel(q_ref, k_ref, v_ref, tri_ref, o_ref):
    i = pl.program_id(2)
    q = q_ref[...]
    lane = lax.broadcasted_iota(jnp.int32, (SB_BLOCK, LANES), 1)
    qpos = lax.broadcasted_iota(jnp.int32, (SB_BLOCK, SB_BLOCK), 0)
    kpos = lax.broadcasted_iota(jnp.int32, (SB_BLOCK, SB_BLOCK), 1)
    below_diag = kpos < qpos
    tri = tri_ref[...]

    first_head = lane < SB_HEAD_DIM
    zero_q = jnp.zeros_like(q)
    q_heads = (jnp.where(first_head, q, zero_q), jnp.where(first_head, zero_q, q))

    def cond(carry):
        j, c0, c1, _, _ = carry
        return jnp.logical_and(j >= 0, jnp.max(jnp.maximum(c0, c1)) > F32_EXP_UNDERFLOW)

    def body(carry):
        j, c0, c1, acc0, acc1 = carry
        rows = pl.ds(pl.multiple_of(j * SB_BLOCK, SB_BLOCK), SB_BLOCK)
        kb = k_ref[rows, :]
        vb = v_ref[rows, :]
        causal = jnp.logical_or(j < i, below_diag)

        def head(qm, c, acc):
            z = _dot_nt(qm, kb)
            log_beta = jnp.minimum(z, 0.0) - jnp.log1p(jnp.exp(-jnp.abs(z)))
            log_rest = jnp.where(causal, log_beta - z, 0.0)
            hi, lo = _split_bf16(log_rest)
            sums = _dot(hi, tri) + _dot(lo, tri)
            log_remaining = c + sums[:, :SB_BLOCK]
            w = jnp.where(causal, jnp.exp(log_beta + log_remaining), 0.0)
            return c + sums[:, SB_BLOCK:], acc + _dot(w.astype(BF16), vb)

        c0, acc0 = head(q_heads[0], c0, acc0)
        c1, acc1 = head(q_heads[1], c1, acc1)
        return j - 1, c0, c1, acc0, acc1

    zeros = jnp.zeros((SB_BLOCK, LANES), F32)
    _, _, _, acc0, acc1 = lax.while_loop(cond, body, (i, zeros, zeros, zeros, zeros))
    o_ref[...] = jnp.where(first_head, acc0, acc1).astype(o_ref.dtype)


def _sb_attention(sq, sk, sv, batch, seq):
    assert seq % SB_BLOCK == 0
    nq = seq // SB_BLOCK
    pairs = SB_WIDTH // LANES
    j = np.arange(SB_BLOCK)
    tri = np.concatenate([(j[:, None] > j[None, :]).astype(np.float32),
                          np.ones((SB_BLOCK, SB_BLOCK), np.float32)], axis=1)
    tri = jnp.asarray(tri, BF16)
    qblk = pl.BlockSpec((SB_BLOCK, LANES), lambda b, p, i: (b * nq + i, p))
    kvblk = pl.BlockSpec((seq, LANES), lambda b, p, i: (b, p))
    return pl.pallas_call(
        _sb_attn_kernel,
        grid=(batch, pairs, nq),
        in_specs=[qblk, kvblk, kvblk, pl.BlockSpec(tri.shape, lambda b, p, i: (0, 0))],
        out_specs=qblk,
        out_shape=jax.ShapeDtypeStruct((batch * seq, SB_WIDTH), BF16),
        compiler_params=pltpu.CompilerParams(
            dimension_semantics=("arbitrary", "arbitrary", "arbitrary"),
            vmem_limit_bytes=VMEM_LIMIT),
        name="sb_attn",
    )(sq, sk, sv, tri)


def _out_proj_kernel(ret_ref, sb_ref, x_ref, wtop_ref, wbot_ref, g_ref, wrh_ref, wrl_ref,
                     br_ref, tri_ref,
                     x1_ref, xf_ref, idx_ref, w_ref, rank_ref, cnt_ref, run_ref):
    @pl.when(pl.program_id(0) == 0)
    def _():
        run_ref[...] = jnp.zeros_like(run_ref)

    x1 = x_ref[...] + _dot(ret_ref[...], wtop_ref[...]) + _dot(sb_ref[...], wbot_ref[...])
    x1_ref[...] = x1
    xf = x1 * lax.rsqrt(jnp.mean(x1 * x1, axis=-1, keepdims=True) + EPS) * g_ref[...]
    xf_ref[...] = xf
    xh, xl = _split_bf16(xf)
    wrh = wrh_ref[...]
    logits = _dot_nt(wrh, xh) + _dot_nt(wrh, xl) + _dot_nt(wrl_ref[...], xh) + br_ref[...]
    n_e, tm = logits.shape
    e_iota = lax.broadcasted_iota(jnp.int32, (n_e, tm), 0)
    cur = logits
    tops, sels, hots = [], [], []
    for _ in range(TOP_K):
        m = jnp.max(cur, axis=0, keepdims=True)
        sel = jnp.min(jnp.where(cur == m, e_iota, n_e), axis=0, keepdims=True)
        hot = e_iota == sel
        cur = jnp.where(hot, -jnp.inf, cur)
        tops.append(m)
        sels.append(sel)
        hots.append(hot)
    ps = [jnp.exp(m - tops[0]) for m in tops]
    denom = ps[0] + ps[1] + ps[2] + ps[3]
    chosen = jnp.zeros((n_e, tm), F32)
    for hot in hots:
        chosen = chosen + hot.astype(F32)
    before = run_ref[:, 0:1] + _dot(chosen.astype(BF16), tri_ref[...])
    for kk in range(TOP_K):
        idx_ref[kk:kk + 1, :] = sels[kk]
        w_ref[kk:kk + 1, :] = ps[kk] / denom
        rank = jnp.sum(jnp.where(hots[kk], before, 0.0), axis=0, keepdims=True)
        rank_ref[kk:kk + 1, :] = rank.astype(jnp.int32)
    run = run_ref[...] + jnp.sum(chosen, axis=1, keepdims=True)
    run_ref[...] = run
    cnt_ref[...] = run.astype(jnp.int32)


def _out_proj_route(ret, sb, x2, w_out, ffn_g, w_router, b_router, tm):
    t = x2.shape[0]
    assert t % tm == 0
    w_bf = w_out.astype(BF16)
    wtop, wbot = w_bf[:RET_WIDTH], w_bf[RET_WIDTH:]
    g2 = ffn_g.astype(F32)[None, :]
    wr_t = w_router.astype(F32).T
    wrh = wr_t.astype(BF16)
    wrl = (wr_t - wrh.astype(F32)).astype(BF16)
    br = b_router.astype(F32)[:, None]
    tt = np.arange(tm)
    tri = jnp.asarray((tt[:, None] < tt[None, :]).astype(np.float32), BF16)
    row = lambda w: pl.BlockSpec((tm, w), lambda i: (i, 0))
    full = lambda a: pl.BlockSpec(a.shape, lambda i: (0,) * a.ndim)
    col = pl.BlockSpec((TOP_K, tm), lambda i: (0, i))
    return pl.pallas_call(
        _out_proj_kernel,
        grid=(t // tm,),
        in_specs=[row(RET_WIDTH), row(SB_WIDTH), row(D_MODEL), full(wtop), full(wbot), full(g2),
                  full(wrh), full(wrl), full(br), full(tri)],
        out_specs=[row(D_MODEL), row(D_MODEL), col, col, col,
                   pl.BlockSpec((N_EXPERTS, LANES), lambda i: (0, 0))],
        out_shape=[jax.ShapeDtypeStruct((t, D_MODEL), F32),
                   jax.ShapeDtypeStruct((t, D_MODEL), F32),
                   jax.ShapeDtypeStruct((TOP_K, t), jnp.int32),
                   jax.ShapeDtypeStruct((TOP_K, t), F32),
                   jax.ShapeDtypeStruct((TOP_K, t), jnp.int32),
                   jax.ShapeDtypeStruct((N_EXPERTS, LANES), jnp.int32)],
        scratch_shapes=[pltpu.VMEM((N_EXPERTS, LANES), F32)],
        compiler_params=pltpu.CompilerParams(
            dimension_semantics=("arbitrary",), vmem_limit_bytes=VMEM_LIMIT),
        name="out_proj_route",
    )(ret, sb, x2, wtop, wbot, g2, wrh, wrl, br, tri)


def _dest_kernel(pstart_ref, idx_ref, rank_ref, dest_ref):
    idx = idx_ref[...]
    dest = rank_ref[...]
    for e in range(N_EXPERTS):
        dest = dest + jnp.where(idx == e, pstart_ref[e], 0)
    dest_ref[...] = dest


def _dest(padded_start, top_idx, rank, tn):
    t = top_idx.shape[1]
    assert t % tn == 0
    col = pl.BlockSpec((TOP_K, tn), lambda i, ps: (0, i))
    return pl.pallas_call(
        _dest_kernel,
        grid_spec=pltpu.PrefetchScalarGridSpec(
            num_scalar_prefetch=1, grid=(t // tn,), in_specs=[col, col], out_specs=col),
        out_shape=jax.ShapeDtypeStruct((TOP_K, t), jnp.int32),
        name="dest",
    )(padded_start, top_idx, rank)


def _dispatch_kernel(pend_ref, dest_ref, xf_ref, slots_hbm, zbuf, sem, zsem, *, tm, blk):
    @pl.when(pl.program_id(0) == 0)
    def _():
        zbuf[...] = jnp.zeros_like(zbuf)

        def tail_copy(e):
            first = pl.multiple_of(pend_ref[e] - blk, blk)
            return pltpu.make_async_copy(zbuf, slots_hbm.at[pl.ds(first, blk), :], zsem)

        def nonempty(e):
            return pend_ref[e] > (pend_ref[e - 1] if e else 0)

        for e in range(N_EXPERTS):
            pl.when(nonempty(e))(lambda e=e: tail_copy(e).start())
        for e in range(N_EXPERTS):
            pl.when(nonempty(e))(lambda e=e: tail_copy(e).wait())

    def row_copy(t, kk):
        return pltpu.make_async_copy(
            xf_ref.at[pl.ds(t, 1), :],
            slots_hbm.at[pl.ds(dest_ref[kk, t], 1), :], sem)

    def start(t, carry):
        for kk in range(TOP_K):
            row_copy(t, kk).start(priority=kk % 2)
        return carry

    def wait(t, carry):
        for kk in range(TOP_K):
            row_copy(t, kk).wait()
        return carry

    lax.fori_loop(0, tm, start, 0, unroll=8)
    lax.fori_loop(0, tm, wait, 0, unroll=8)


def _dispatch(padded_end, dest, xf, n_pad, tm, blk):
    t = xf.shape[0]
    assert t % tm == 0
    return pl.pallas_call(
        functools.partial(_dispatch_kernel, tm=tm, blk=blk),
        grid_spec=pltpu.PrefetchScalarGridSpec(
            num_scalar_prefetch=1,
            grid=(t // tm,),
            in_specs=[pl.BlockSpec((TOP_K, tm), lambda i, pe: (0, i), memory_space=pltpu.SMEM),
                      pl.BlockSpec((tm, D_MODEL), lambda i, pe: (i, 0))],
            out_specs=pl.BlockSpec(memory_space=pl.ANY),
            scratch_shapes=[pltpu.VMEM((blk, D_MODEL), F32),
                            pltpu.SemaphoreType.DMA, pltpu.SemaphoreType.DMA],
        ),
        out_shape=jax.ShapeDtypeStruct((n_pad, D_MODEL), F32),
        compiler_params=pltpu.CompilerParams(
            dimension_semantics=("arbitrary",), vmem_limit_bytes=VMEM_LIMIT,
            has_side_effects=True),
        name="dispatch",
    )(padded_end, dest, xf)


def _experts_kernel(be_ref, nu_ref, x_ref, wg_ref, wu_ref, wd_ref, bg_ref, bu_ref, bd_ref, y_ref):
    del be_ref
    used = pl.program_id(0) < nu_ref[0]

    @pl.when(used)
    def _():
        xb = x_ref[...].astype(BF16)
        gate = jnp.minimum(_dot(xb, wg_ref[...]) + bg_ref[...], SWIGLU_LIMIT)
        up = jnp.clip(_dot(xb, wu_ref[...]) + bu_ref[...], -SWIGLU_LIMIT, SWIGLU_LIMIT)
        hidden = (up + 1.0) * gate * jax.nn.sigmoid(SWIGLU_ALPHA * gate)
        y_ref[...] = _dot(hidden.astype(BF16), wd_ref[...]) + bd_ref[...]

    @pl.when(jnp.logical_not(used))
    def _():
        y_ref[...] = jnp.zeros_like(y_ref)


def _experts(block_expert, n_used, slots, wg, wu, wd, bg, bu, bd, blk):
    n_pad = slots.shape[0]
    n_blocks = n_pad // blk
    xmap = lambda j, be, nu: (jnp.minimum(j, nu[0] - 1), 0)
    wmap = lambda j, be, nu: (be[j], 0, 0)
    wspec = pl.BlockSpec((None, D_MODEL, D_FF), wmap)
    bspec = pl.BlockSpec((None, 1, D_FF), wmap)
    return pl.pallas_call(
        _experts_kernel,
        grid_spec=pltpu.PrefetchScalarGridSpec(
            num_scalar_prefetch=2,
            grid=(n_blocks,),
            in_specs=[pl.BlockSpec((blk, D_MODEL), xmap), wspec, wspec, wspec, bspec, bspec, bspec],
            out_specs=pl.BlockSpec((blk, D_MODEL), lambda j, be, nu: (j, 0)),
        ),
        out_shape=jax.ShapeDtypeStruct((n_pad, D_MODEL), F32),
        compiler_params=pltpu.CompilerParams(
            dimension_semantics=("arbitrary",), vmem_limit_bytes=VMEM_LIMIT),
        name="experts",
    )(block_expert, n_used, slots, wg, wu, wd, bg, bu, bd)


def _combine_kernel(dest_ref, y_hbm, w_ref, x1_ref, o_ref, buf, sem, *, tm):
    def row_copy(t, kk):
        return pltpu.make_async_copy(
            y_hbm.at[pl.ds(dest_ref[kk, t], 1), :],
            buf.at[kk, pl.ds(t, 1), :], sem)

    def start(t, carry):
        for kk in range(TOP_K):
            row_copy(t, kk).start(priority=kk % 2)
        return carry

    def wait(t, carry):
        for kk in range(TOP_K):
            row_copy(t, kk).wait()
        return carry

    lax.fori_loop(0, tm, start, 0, unroll=8)
    lax.fori_loop(0, tm, wait, 0, unroll=8)
    w = w_ref[...]
    acc = x1_ref[...]
    for kk in range(TOP_K):
        acc = acc + buf[kk] * w[:, kk:kk + 1]
    o_ref[...] = acc


def _combine(dest, y_slots, w_t, x1, tm):
    t = x1.shape[0]
    assert t % tm == 0
    row = pl.BlockSpec((tm, D_MODEL), lambda i: (i, 0))
    return pl.pallas_call(
        functools.partial(_combine_kernel, tm=tm),
        grid=(t // tm,),
        in_specs=[pl.BlockSpec((TOP_K, tm), lambda i: (0, i), memory_space=pltpu.SMEM),
                  pl.BlockSpec(memory_space=pl.ANY),
                  pl.BlockSpec((tm, TOP_K), lambda i: (i, 0)),
                  row],
        out_specs=row,
        out_shape=jax.ShapeDtypeStruct((t, D_MODEL), F32),
        scratch_shapes=[pltpu.VMEM((TOP_K, tm, D_MODEL), F32), pltpu.SemaphoreType.DMA],
        compiler_params=pltpu.CompilerParams(
            dimension_semantics=("arbitrary",), vmem_limit_bytes=VMEM_LIMIT),
        name="combine",
    )(dest, y_slots, w_t, x1)


def _tiles(batch, seq):
    return dict(
        in_proj=min(512, seq),
        retention=min(1024, seq),
        out_proj=min(512, seq),
        dest=min(8192, batch * seq),
        dispatch=min(1024, seq),
        combine=min(256, seq),
        expert_block=256,
    )


def _layer(x, attn_norm_g, w_in, ret_norm_g, sb_q_norm_g, sb_k_norm_g, w_out,
           ffn_norm_g, w_router, b_router, w_gate_up, b_gate_up, w_down, b_down):
    batch, seq, d = x.shape
    t = batch * seq
    tiles = _tiles(batch, seq)
    x2 = x.reshape(t, d)

    rq, rk, rv, rg, sq, sk, sv = _in_proj(
        x2, attn_norm_g, w_in, sb_q_norm_g, sb_k_norm_g, seq, tiles["in_proj"])
    ret = _retention(rq, rk, rv, rg, ret_norm_g, batch, seq, tiles["retention"])
    sb = _sb_attention(sq, sk, sv, batch, seq)
    x1, xf, top_idx, top_w, rank, counts = _out_proj_route(
        ret, sb, x2, w_out, ffn_norm_g, w_router, b_router, tiles["out_proj"])

    blk = tiles["expert_block"]
    n_pad = t * TOP_K + N_EXPERTS * blk
    n_blocks = n_pad // blk
    counts = counts[:, 0]
    padded = (counts + blk - 1) // blk * blk
    padded_end = jnp.cumsum(padded)
    padded_start = padded_end - padded
    block_first = jnp.arange(n_blocks, dtype=jnp.int32) * blk
    block_expert = jnp.minimum(
        jnp.sum(padded_end[None, :] <= block_first[:, None], axis=1), N_EXPERTS - 1
    ).astype(jnp.int32)
    n_used = (padded_end[-1:] // blk).astype(jnp.int32)

    dest = _dest(padded_start, top_idx, rank, tiles["dest"])
    slots = _dispatch(padded_end, dest, xf, n_pad, tiles["dispatch"], blk)

    wgu = w_gate_up.reshape(N_EXPERTS, D_MODEL, D_FF, 2)
    wg = wgu[..., 0].astype(BF16)
    wu = wgu[..., 1].astype(BF16)
    bgu = b_gate_up.astype(F32).reshape(N_EXPERTS, 1, D_FF, 2)
    y_slots = _experts(block_expert, n_used, slots, wg, wu, w_down.astype(BF16),
                       bgu[..., 0], bgu[..., 1], b_down.astype(F32)[:, None, :], blk)

    out = _combine(dest, y_slots, top_w.T, x1, tiles["combine"])
    return out.reshape(batch, seq, d)


def kernel(x, attn_norm_g, w_in, ret_norm_g, sb_q_norm_g, sb_k_norm_g, w_out, ffn_norm_g,
           w_router, b_router, w_gate_up, b_gate_up, w_down, b_down):
    depth = attn_norm_g.shape[0]
    for l in range(depth):
        x = _layer(x, attn_norm_g[l], w_in[l], ret_norm_g[l], sb_q_norm_g[l], sb_k_norm_g[l],
                   w_out[l], ffn_norm_g[l], w_router[l], b_router[l], w_gate_up[l],
                   b_gate_up[l], w_down[l], b_down[l])
    return x
```

```python
import functools

import numpy as np
import jax
import jax.numpy as jnp
from jax import lax
from jax.experimental import pallas as pl
from jax.experimental.pallas import tpu as pltpu

D_MODEL = 1024
RET_WIDTH = 512
RET_HEADS = 4
RET_HEAD_DIM = 128
SB_WIDTH = 512
SB_HEADS = 8
SB_HEAD_DIM = 64
IN_PROJ_WIDTH = 4 * RET_WIDTH + 3 * SB_WIDTH
RET_CHUNK = 128
ROPE_BASE = 10000.0
N_EXPERTS = 32
TOP_K = 4
D_FF = D_MODEL
SWIGLU_LIMIT = 7.0
SWIGLU_ALPHA = 1.702
EPS = 1e-6

LANES = 128
F32_EXP_UNDERFLOW = -88.0
VMEM_LIMIT = 56 * 1024 * 1024

BF16 = jnp.bfloat16
F32 = jnp.float32


def _split_bf16(v):
    hi = v.astype(BF16)
    lo = (v - hi.astype(F32)).astype(BF16)
    return hi, lo


def _dot(a, b):
    return jnp.dot(a, b, preferred_element_type=F32)


def _dot_nt(a, b):
    return lax.dot_general(a, b, (((1,), (1,)), ((), ())), preferred_element_type=F32)


def _in_proj_kernel(x_ref, g_ref, w_ref, cos_ref, sin_ref, qg_ref, kg_ref, seg_ref,
                    rq_ref, rk_ref, rv_ref, rg_ref, sq_ref, sk_ref, sv_ref):
    x = x_ref[...]
    h = x * lax.rsqrt(jnp.mean(x * x, axis=-1, keepdims=True) + EPS) * g_ref[...]
    p = _dot(h.astype(BF16), w_ref[...])
    cos2 = cos_ref[...]
    sin2 = sin_ref[...]
    k_scale = RET_HEAD_DIM ** -0.5
    for hd in range(RET_HEADS):
        lo = hd * RET_HEAD_DIM
        q = p[:, lo:lo + RET_HEAD_DIM]
        k = p[:, RET_WIDTH + lo:RET_WIDTH + lo + RET_HEAD_DIM]
        q = q * cos2 + pltpu.roll(q, RET_HEAD_DIM // 2, axis=1) * sin2
        k = k * cos2 + pltpu.roll(k, RET_HEAD_DIM // 2, axis=1) * sin2
        rq_ref[:, lo:lo + RET_HEAD_DIM] = q.astype(rq_ref.dtype)
        rk_ref[:, lo:lo + RET_HEAD_DIM] = (k * k_scale).astype(rk_ref.dtype)
    rv_ref[...] = p[:, 2 * RET_WIDTH:3 * RET_WIDTH].astype(rv_ref.dtype)
    rg_ref[...] = p[:, 3 * RET_WIDTH:4 * RET_WIDTH]
    base = 4 * RET_WIDTH
    seg = seg_ref[...]

    def head_norm(v, gain):
        hi, lo = _split_bf16(v * v)
        ms = (_dot(hi, seg) + _dot(lo, seg)) * (1.0 / SB_HEAD_DIM)
        return v * lax.rsqrt(ms + EPS) * gain

    sq = head_norm(p[:, base:base + SB_WIDTH], qg_ref[...])
    sk = head_norm(p[:, base + SB_WIDTH:base + 2 * SB_WIDTH], kg_ref[...])
    sq_ref[...] = (sq * (SB_HEAD_DIM ** -0.5)).astype(sq_ref.dtype)
    sk_ref[...] = sk.astype(sk_ref.dtype)
    sv_ref[...] = p[:, base + 2 * SB_WIDTH:base + 3 * SB_WIDTH].astype(sv_ref.dtype)


def _rope_tables(seq):
    half = RET_HEAD_DIM // 2
    inv = ROPE_BASE ** (-np.arange(half, dtype=np.float64) / half)
    n_a = seq // LANES
    ang_a = (np.arange(n_a, dtype=np.float64) * LANES)[:, None] * inv[None, :]
    ang_b = np.arange(LANES, dtype=np.float64)[:, None] * inv[None, :]
    ca, sa = jnp.asarray(np.cos(ang_a), F32), jnp.asarray(np.sin(ang_a), F32)
    cb, sb = jnp.asarray(np.cos(ang_b), F32), jnp.asarray(np.sin(ang_b), F32)
    cos = (ca[:, None, :] * cb[None] - sa[:, None, :] * sb[None]).reshape(seq, half)
    sin = (sa[:, None, :] * cb[None] + ca[:, None, :] * sb[None]).reshape(seq, half)
    return jnp.concatenate([cos, cos], axis=-1), jnp.concatenate([-sin, sin], axis=-1)


def _in_proj(x2, norm_g, w_in, sb_q_g, sb_k_g, seq, tm):
    t = x2.shape[0]
    assert seq % tm == 0 and t % tm == 0
    cos2, sin2 = _rope_tables(seq)
    seg = np.kron(np.eye(SB_HEADS), np.ones((SB_HEAD_DIM, SB_HEAD_DIM)))
    seg = jnp.asarray(seg, BF16)
    qg = jnp.tile(sb_q_g.astype(F32), SB_HEADS)[None, :]
    kg = jnp.tile(sb_k_g.astype(F32), SB_HEADS)[None, :]
    n_pos = seq // tm
    row = lambda w: pl.BlockSpec((tm, w), lambda i: (i, 0))
    full = lambda a: pl.BlockSpec(a.shape, lambda i: (0,) * a.ndim)
    pos = pl.BlockSpec((tm, RET_HEAD_DIM), lambda i: (i % n_pos, 0))
    w_bf = w_in.astype(BF16)
    g2 = norm_g.astype(F32)[None, :]
    out = lambda dt: jax.ShapeDtypeStruct((t, RET_WIDTH), dt)
    return pl.pallas_call(
        _in_proj_kernel,
        grid=(t // tm,),
        in_specs=[row(D_MODEL), full(g2), full(w_bf), pos, pos, full(qg), full(kg), full(seg)],
        out_specs=[row(RET_WIDTH)] * 7,
        out_shape=[out(BF16), out(F32), out(BF16), out(F32), out(BF16), out(BF16), out(BF16)],
        compiler_params=pltpu.CompilerParams(
            dimension_semantics=("arbitrary",), vmem_limit_bytes=VMEM_LIMIT),
        name="in_proj",
    )(x2, g2, w_bf, cos2, sin2, qg, kg, seg)


def _retention_kernel(q_ref, k_ref, v_ref, g_ref, ng_ref, intra_ref, qd_ref, kd_ref, cd_ref,
                      o_ref, state_ref, *, chunks):
    @pl.when(pl.program_id(1) == 0)
    def _():
        state_ref[...] = jnp.zeros_like(state_ref)

    def chunk(c, carry):
        rows = pl.ds(pl.multiple_of(c * RET_CHUNK, RET_CHUNK), RET_CHUNK)
        for hd in range(RET_HEADS):
            cols = slice(hd * RET_HEAD_DIM, (hd + 1) * RET_HEAD_DIM)
            q = q_ref[rows, cols]
            k = k_ref[rows, cols]
            v = v_ref[rows, cols]
            state = state_ref[hd]
            scores = _dot_nt(q, k.astype(BF16)) * intra_ref[hd]
            inner = _dot(scores.astype(BF16), v)
            cross = _dot(q, state.astype(BF16)) * qd_ref[:, cols]
            kd = (k * kd_ref[:, cols]).astype(BF16)
            state_ref[hd] = state * cd_ref[:, cols] + _dot(kd.T, v)
            o = inner + cross
            o = o * lax.rsqrt(jnp.mean(o * o, axis=-1, keepdims=True) + EPS) * ng_ref[:, cols]
            gate = g_ref[rows, cols]
            o_ref[rows, cols] = (o * (gate * jax.nn.sigmoid(gate))).astype(o_ref.dtype)
        return carry

    lax.fori_loop(0, chunks, chunk, 0)


def _retention(rq, rk, rv, rg, ret_norm_g, batch, seq, rows):
    assert seq % rows == 0 and rows % RET_CHUNK == 0
    log_gamma = np.log(1.0 - 2.0 ** (-5.0 - np.arange(RET_HEADS, dtype=np.float64)))
    idx = np.arange(RET_CHUNK, dtype=np.float64)
    diff = idx[:, None] - idx[None, :]
    intra = np.where(diff >= 0, np.exp(log_gamma[:, None, None] * np.maximum(diff, 0.0)), 0.0)
    q_decay = np.exp(log_gamma[:, None] * (idx + 1.0))
    k_decay = np.exp(log_gamma[:, None] * (RET_CHUNK - 1.0 - idx))
    chunk_decay = np.exp(log_gamma * RET_CHUNK)
    lane_rep = lambda a: np.repeat(a.T, RET_HEAD_DIM, axis=1)
    intra = jnp.asarray(intra, F32)
    qd = jnp.asarray(lane_rep(q_decay), F32)
    kd = jnp.asarray(lane_rep(k_decay), F32)
    cd = jnp.asarray(np.repeat(chunk_decay, RET_HEAD_DIM)[None, :], F32)
    ng = ret_norm_g.astype(F32).reshape(1, RET_WIDTH)
    n_r = seq // rows
    blk = pl.BlockSpec((rows, RET_WIDTH), lambda b, r: (b * n_r + r, 0))
    full = lambda a: pl.BlockSpec(a.shape, lambda b, r: (0,) * a.ndim)
    return pl.pallas_call(
        functools.partial(_retention_kernel, chunks=rows // RET_CHUNK),
        grid=(batch, n_r),
        in_specs=[blk, blk, blk, blk, full(ng), full(intra), full(qd), full(kd), full(cd)],
        out_specs=blk,
        out_shape=jax.ShapeDtypeStruct((batch * seq, RET_WIDTH), BF16),
        scratch_shapes=[pltpu.VMEM((RET_HEADS, RET_HEAD_DIM, RET_HEAD_DIM), F32)],
        compiler_params=pltpu.CompilerParams(
            dimension_semantics=("arbitrary", "arbitrary"), vmem_limit_bytes=VMEM_LIMIT),
        name="retention",
    )(rq, rk, rv, rg, ng, intra, qd, kd, cd)


def _sb_attn_kernel(q_ref, k_ref, v_ref, tri_ref, o_ref, *, blk):
    i = pl.program_id(2)
    q = q_ref[...]
    lane = lax.broadcasted_iota(jnp.int32, (blk, LANES), 1)
    qpos = lax.broadcasted_iota(jnp.int32, (blk, blk), 0)
    kpos = lax.broadcasted_iota(jnp.int32, (blk, blk), 1)
    below_diag = kpos < qpos
    tri = tri_ref[...]

    first_head = lane < SB_HEAD_DIM
    zero_q = jnp.zeros_like(q)
    q_heads = (jnp.where(first_head, q, zero_q), jnp.where(first_head, zero_q, q))

    def cond(carry):
        j, c0, c1, _, _ = carry
        return jnp.logical_and(j >= 0, jnp.max(jnp.maximum(c0, c1)) > F32_EXP_UNDERFLOW)

    def body(carry):
        j, c0, c1, acc0, acc1 = carry
        rows = pl.ds(pl.multiple_of(j * blk, blk), blk)
        kb = k_ref[rows, :]
        vb = v_ref[rows, :]
        causal = jnp.logical_or(j < i, below_diag)

        def head(qm, c, acc):
            z = _dot_nt(qm, kb)
            log_beta = jnp.minimum(z, 0.0) - jnp.log1p(jnp.exp(-jnp.abs(z)))
            log_rest = jnp.where(causal, log_beta - z, 0.0)
            hi, lo = _split_bf16(log_rest)
            sums = _dot(hi, tri) + _dot(lo, tri)
            log_remaining = c + sums[:, :blk]
            w = jnp.where(causal, jnp.exp(log_beta + log_remaining), 0.0)
            return c + sums[:, blk:blk + 1], acc + _dot(w.astype(BF16), vb)

        c0, acc0 = head(q_heads[0], c0, acc0)
        c1, acc1 = head(q_heads[1], c1, acc1)
        return j - 1, c0, c1, acc0, acc1

    c_init = jnp.zeros((blk, 1), F32)
    acc_init = jnp.zeros((blk, LANES), F32)
    _, _, _, acc0, acc1 = lax.while_loop(cond, body, (i, c_init, c_init, acc_init, acc_init))
    o_ref[...] = jnp.where(first_head, acc0, acc1).astype(o_ref.dtype)


def _sb_attention(sq, sk, sv, batch, seq, blk):
    assert seq % blk == 0
    nq = seq // blk
    pairs = SB_WIDTH // LANES
    j = np.arange(blk)
    tri = np.concatenate([(j[:, None] > j[None, :]).astype(np.float32),
                          np.ones((blk, LANES), np.float32)], axis=1)
    tri = jnp.asarray(tri, BF16)
    qblk = pl.BlockSpec((blk, LANES), lambda b, p, i: (b * nq + i, p))
    kvblk = pl.BlockSpec((seq, LANES), lambda b, p, i: (b, p))
    return pl.pallas_call(
        functools.partial(_sb_attn_kernel, blk=blk),
        grid=(batch, pairs, nq),
        in_specs=[qblk, kvblk, kvblk, pl.BlockSpec(tri.shape, lambda b, p, i: (0, 0))],
        out_specs=qblk,
        out_shape=jax.ShapeDtypeStruct((batch * seq, SB_WIDTH), BF16),
        compiler_params=pltpu.CompilerParams(
            dimension_semantics=("arbitrary", "arbitrary", "arbitrary"),
            vmem_limit_bytes=VMEM_LIMIT),
        name="sb_attn",
    )(sq, sk, sv, tri)


def _out_proj_kernel(ret_ref, sb_ref, x_ref, wtop_ref, wbot_ref, g_ref, wrh_ref, wrl_ref,
                     br_ref, tri_ref,
                     x1_ref, xf_ref, idx_ref, w_ref, rank_ref, cnt_ref, run_ref):
    @pl.when(pl.program_id(0) == 0)
    def _():
        run_ref[...] = jnp.zeros_like(run_ref)

    x1 = x_ref[...] + _dot(ret_ref[...], wtop_ref[...]) + _dot(sb_ref[...], wbot_ref[...])
    x1_ref[...] = x1
    xf = x1 * lax.rsqrt(jnp.mean(x1 * x1, axis=-1, keepdims=True) + EPS) * g_ref[...]
    xf_ref[...] = xf
    xh, xl = _split_bf16(xf)
    wrh = wrh_ref[...]
    logits = _dot_nt(wrh, xh) + _dot_nt(wrh, xl) + _dot_nt(wrl_ref[...], xh) + br_ref[...]
    n_e, tm = logits.shape
    e_iota = lax.broadcasted_iota(jnp.int32, (n_e, tm), 0)
    cur = logits
    tops, sels, hots = [], [], []
    for _ in range(TOP_K):
        m = jnp.max(cur, axis=0, keepdims=True)
        sel = jnp.min(jnp.where(cur == m, e_iota, n_e), axis=0, keepdims=True)
        hot = e_iota == sel
        cur = jnp.where(hot, -jnp.inf, cur)
        tops.append(m)
        sels.append(sel)
        hots.append(hot)
    ps = [jnp.exp(m - tops[0]) for m in tops]
    denom = ps[0] + ps[1] + ps[2] + ps[3]
    chosen = jnp.zeros((n_e, tm), F32)
    for hot in hots:
        chosen = chosen + hot.astype(F32)
    before = run_ref[:, 0:1] + _dot(chosen.astype(BF16), tri_ref[...])
    for kk in range(TOP_K):
        idx_ref[kk:kk + 1, :] = sels[kk]
        w_ref[kk:kk + 1, :] = ps[kk] / denom
        rank = jnp.sum(jnp.where(hots[kk], before, 0.0), axis=0, keepdims=True)
        rank_ref[kk:kk + 1, :] = rank.astype(jnp.int32)
    run = run_ref[...] + jnp.sum(chosen, axis=1, keepdims=True)
    run_ref[...] = run
    cnt_ref[...] = run.astype(jnp.int32)


def _out_proj_route(ret, sb, x2, w_out, ffn_g, w_router, b_router, tm):
    t = x2.shape[0]
    assert t % tm == 0
    w_bf = w_out.astype(BF16)
    wtop, wbot = w_bf[:RET_WIDTH], w_bf[RET_WIDTH:]
    g2 = ffn_g.astype(F32)[None, :]
    wr_t = w_router.astype(F32).T
    wrh = wr_t.astype(BF16)
    wrl = (wr_t - wrh.astype(F32)).astype(BF16)
    br = b_router.astype(F32)[:, None]
    tt = np.arange(tm)
    tri = jnp.asarray((tt[:, None] < tt[None, :]).astype(np.float32), BF16)
    row = lambda w: pl.BlockSpec((tm, w), lambda i: (i, 0))
    full = lambda a: pl.BlockSpec(a.shape, lambda i: (0,) * a.ndim)
    col = pl.BlockSpec((TOP_K, tm), lambda i: (0, i))
    return pl.pallas_call(
        _out_proj_kernel,
        grid=(t // tm,),
        in_specs=[row(RET_WIDTH), row(SB_WIDTH), row(D_MODEL), full(wtop), full(wbot), full(g2),
                  full(wrh), full(wrl), full(br), full(tri)],
        out_specs=[row(D_MODEL), row(D_MODEL), col, col, col,
                   pl.BlockSpec((N_EXPERTS, LANES), lambda i: (0, 0))],
        out_shape=[jax.ShapeDtypeStruct((t, D_MODEL), F32),
                   jax.ShapeDtypeStruct((t, D_MODEL), F32),
                   jax.ShapeDtypeStruct((TOP_K, t), jnp.int32),
                   jax.ShapeDtypeStruct((TOP_K, t), F32),
                   jax.ShapeDtypeStruct((TOP_K, t), jnp.int32),
                   jax.ShapeDtypeStruct((N_EXPERTS, LANES), jnp.int32)],
        scratch_shapes=[pltpu.VMEM((N_EXPERTS, LANES), F32)],
        compiler_params=pltpu.CompilerParams(
            dimension_semantics=("arbitrary",), vmem_limit_bytes=VMEM_LIMIT),
        name="out_proj_route",
    )(ret, sb, x2, wtop, wbot, g2, wrh, wrl, br, tri)


def _dest_kernel(pstart_ref, idx_ref, rank_ref, dest_ref):
    idx = idx_ref[...]
    dest = rank_ref[...]
    for e in range(N_EXPERTS):
        dest = dest + jnp.where(idx == e, pstart_ref[e], 0)
    dest_ref[...] = dest


def _dest(padded_start, top_idx, rank, tn):
    t = top_idx.shape[1]
    assert t % tn == 0
    col = pl.BlockSpec((TOP_K, tn), lambda i, ps: (0, i))
    return pl.pallas_call(
        _dest_kernel,
        grid_spec=pltpu.PrefetchScalarGridSpec(
            num_scalar_prefetch=1, grid=(t // tn,), in_specs=[col, col], out_specs=col),
        out_shape=jax.ShapeDtypeStruct((TOP_K, t), jnp.int32),
        name="dest",
    )(padded_start, top_idx, rank)


def _dispatch_kernel(pend_ref, dest_ref, xf_ref, slots_hbm, zbuf, sem, zsem, *, tm, blk):
    @pl.when(pl.program_id(0) == 0)
    def _():
        zbuf[...] = jnp.zeros_like(zbuf)

        def tail_copy(e):
            first = pl.multiple_of(pend_ref[e] - blk, blk)
            return pltpu.make_async_copy(zbuf, slots_hbm.at[pl.ds(first, blk), :], zsem)

        def nonempty(e):
            return pend_ref[e] > (pend_ref[e - 1] if e else 0)

        for e in range(N_EXPERTS):
            pl.when(nonempty(e))(lambda e=e: tail_copy(e).start())
        for e in range(N_EXPERTS):
            pl.when(nonempty(e))(lambda e=e: tail_copy(e).wait())

    def row_copy(t, kk):
        return pltpu.make_async_copy(
            xf_ref.at[pl.ds(t, 1), :],
            slots_hbm.at[pl.ds(dest_ref[kk, t], 1), :], sem)

    def start(t, carry):
        for kk in range(TOP_K):
            row_copy(t, kk).start(priority=kk % 2)
        return carry

    def wait(t, carry):
        for kk in range(TOP_K):
            row_copy(t, kk).wait()
        return carry

    lax.fori_loop(0, tm, start, 0, unroll=8)
    lax.fori_loop(0, tm, wait, 0, unroll=8)


def _dispatch(padded_end, dest, xf, n_pad, tm, blk):
    t = xf.shape[0]
    assert t % tm == 0
    return pl.pallas_call(
        functools.partial(_dispatch_kernel, tm=tm, blk=blk),
        grid_spec=pltpu.PrefetchScalarGridSpec(
            num_scalar_prefetch=1,
            grid=(t // tm,),
            in_specs=[pl.BlockSpec((TOP_K, tm), lambda i, pe: (0, i), memory_space=pltpu.SMEM),
                      pl.BlockSpec((tm, D_MODEL), lambda i, pe: (i, 0))],
            out_specs=pl.BlockSpec(memory_space=pl.ANY),
            scratch_shapes=[pltpu.VMEM((blk, D_MODEL), F32),
                            pltpu.SemaphoreType.DMA, pltpu.SemaphoreType.DMA],
        ),
        out_shape=jax.ShapeDtypeStruct((n_pad, D_MODEL), F32),
        compiler_params=pltpu.CompilerParams(
            dimension_semantics=("arbitrary",), vmem_limit_bytes=VMEM_LIMIT,
            has_side_effects=True),
        name="dispatch",
    )(padded_end, dest, xf)


def _experts_kernel(be_ref, nu_ref, x_ref, wg_ref, wu_ref, wd_ref, bg_ref, bu_ref, bd_ref, y_ref):
    del be_ref
    used = pl.program_id(0) < nu_ref[0]

    @pl.when(used)
    def _():
        xb = x_ref[...].astype(BF16)
        gate = jnp.minimum(_dot(xb, wg_ref[...]) + bg_ref[...], SWIGLU_LIMIT)
        up = jnp.clip(_dot(xb, wu_ref[...]) + bu_ref[...], -SWIGLU_LIMIT, SWIGLU_LIMIT)
        hidden = (up + 1.0) * gate * jax.nn.sigmoid(SWIGLU_ALPHA * gate)
        y_ref[...] = _dot(hidden.astype(BF16), wd_ref[...]) + bd_ref[...]

    @pl.when(jnp.logical_not(used))
    def _():
        y_ref[...] = jnp.zeros_like(y_ref)


def _experts(block_expert, n_used, slots, wg, wu, wd, bg, bu, bd, blk):
    n_pad = slots.shape[0]
    n_blocks = n_pad // blk
    xmap = lambda j, be, nu: (jnp.minimum(j, nu[0] - 1), 0)
    wmap = lambda j, be, nu: (be[j], 0, 0)
    wspec = pl.BlockSpec((None, D_MODEL, D_FF), wmap)
    bspec = pl.BlockSpec((None, 1, D_FF), wmap)
    return pl.pallas_call(
        _experts_kernel,
        grid_spec=pltpu.PrefetchScalarGridSpec(
            num_scalar_prefetch=2,
            grid=(n_blocks,),
            in_specs=[pl.BlockSpec((blk, D_MODEL), xmap), wspec, wspec, wspec, bspec, bspec, bspec],
            out_specs=pl.BlockSpec((blk, D_MODEL), lambda j, be, nu: (j, 0)),
        ),
        out_shape=jax.ShapeDtypeStruct((n_pad, D_MODEL), F32),
        compiler_params=pltpu.CompilerParams(
            dimension_semantics=("arbitrary",), vmem_limit_bytes=VMEM_LIMIT),
        name="experts",
    )(block_expert, n_used, slots, wg, wu, wd, bg, bu, bd)


def _combine_kernel(dest_ref, y_hbm, w_ref, x1_ref, o_ref, buf, sem, *, tm):
    def row_copy(t, kk):
        return pltpu.make_async_copy(
            y_hbm.at[pl.ds(dest_ref[kk, t], 1), :],
            buf.at[kk, pl.ds(t, 1), :], sem)

    def start(t, carry):
        for kk in range(TOP_K):
            row_copy(t, kk).start(priority=kk % 2)
        return carry

    def wait(t, carry):
        for kk in range(TOP_K):
            row_copy(t, kk).wait()
        return carry

    lax.fori_loop(0, tm, start, 0, unroll=8)
    lax.fori_loop(0, tm, wait, 0, unroll=8)
    w = w_ref[...]
    acc = x1_ref[...]
    for kk in range(TOP_K):
        acc = acc + buf[kk] * w[:, kk:kk + 1]
    o_ref[...] = acc


def _combine(dest, y_slots, w_t, x1, tm):
    t = x1.shape[0]
    assert t % tm == 0
    row = pl.BlockSpec((tm, D_MODEL), lambda i: (i, 0))
    return pl.pallas_call(
        functools.partial(_combine_kernel, tm=tm),
        grid=(t // tm,),
        in_specs=[pl.BlockSpec((TOP_K, tm), lambda i: (0, i), memory_space=pltpu.SMEM),
                  pl.BlockSpec(memory_space=pl.ANY),
                  pl.BlockSpec((tm, TOP_K), lambda i: (i, 0)),
                  row],
        out_specs=row,
        out_shape=jax.ShapeDtypeStruct((t, D_MODEL), F32),
        scratch_shapes=[pltpu.VMEM((TOP_K, tm, D_MODEL), F32), pltpu.SemaphoreType.DMA],
        compiler_params=pltpu.CompilerParams(
            dimension_semantics=("arbitrary",), vmem_limit_bytes=VMEM_LIMIT),
        name="combine",
    )(dest, y_slots, w_t, x1)


def _tiles(batch, seq):
    return dict(
        in_proj=min(512, seq),
        retention=min(1024, seq),
        sb_block=min(256, seq),
        out_proj=min(512, seq),
        dest=min(8192, batch * seq),
        dispatch=min(1024, seq),
        combine=min(256, seq),
        expert_block=256,
    )


def _layer(x, attn_norm_g, w_in, ret_norm_g, sb_q_norm_g, sb_k_norm_g, w_out,
           ffn_norm_g, w_router, b_router, w_gate_up, b_gate_up, w_down, b_down):
    batch, seq, d = x.shape
    t = batch * seq
    tiles = _tiles(batch, seq)
    x2 = x.reshape(t, d)

    rq, rk, rv, rg, sq, sk, sv = _in_proj(
        x2, attn_norm_g, w_in, sb_q_norm_g, sb_k_norm_g, seq, tiles["in_proj"])
    ret = _retention(rq, rk, rv, rg, ret_norm_g, batch, seq, tiles["retention"])
    sb = _sb_attention(sq, sk, sv, batch, seq, tiles["sb_block"])
    x1, xf, top_idx, top_w, rank, counts = _out_proj_route(
        ret, sb, x2, w_out, ffn_norm_g, w_router, b_router, tiles["out_proj"])

    blk = tiles["expert_block"]
    n_pad = t * TOP_K + N_EXPERTS * blk
    n_blocks = n_pad // blk
    counts = counts[:, 0]
    padded = (counts + blk - 1) // blk * blk
    padded_end = jnp.cumsum(padded)
    padded_start = padded_end - padded
    block_first = jnp.arange(n_blocks, dtype=jnp.int32) * blk
    block_expert = jnp.minimum(
        jnp.sum(padded_end[None, :] <= block_first[:, None], axis=1), N_EXPERTS - 1
    ).astype(jnp.int32)
    n_used = (padded_end[-1:] // blk).astype(jnp.int32)

    dest = _dest(padded_start, top_idx, rank, tiles["dest"])
    slots = _dispatch(padded_end, dest, xf, n_pad, tiles["dispatch"], blk)

    wgu = w_gate_up.reshape(N_EXPERTS, D_MODEL, D_FF, 2)
    wg = wgu[..., 0].astype(BF16)
    wu = wgu[..., 1].astype(BF16)
    bgu = b_gate_up.astype(F32).reshape(N_EXPERTS, 1, D_FF, 2)
    y_slots = _experts(block_expert, n_used, slots, wg, wu, w_down.astype(BF16),
                       bgu[..., 0], bgu[..., 1], b_down.astype(F32)[:, None, :], blk)

    out = _combine(dest, y_slots, top_w.T, x1, tiles["combine"])
    return out.reshape(batch, seq, d)


def kernel(x, attn_norm_g, w_in, ret_norm_g, sb_q_norm_g, sb_k_norm_g, w_out, ffn_norm_g,
           w_router, b_router, w_gate_up, b_gate_up, w_down, b_down):
    depth = attn_norm_g.shape[0]
    for l in range(depth):
        x = _layer(x, attn_norm_g[l], w_in[l], ret_norm_g[l], sb_q_norm_g[l], sb_k_norm_g[l],
                   w_out[l], ffn_norm_g[l], w_router[l], b_router[l], w_gate_up[l],
                   b_gate_up[l], w_down[l], b_down[l])
    return x
```

```python
import functools

import numpy as np
import jax
import jax.numpy as jnp
from jax import lax
from jax.experimental import pallas as pl
from jax.experimental.pallas import tpu as pltpu

D_MODEL = 1024
RET_WIDTH = 512
RET_HEADS = 4
RET_HEAD_DIM = 128
SB_WIDTH = 512
SB_HEADS = 8
SB_HEAD_DIM = 64
IN_PROJ_WIDTH = 4 * RET_WIDTH + 3 * SB_WIDTH
RET_CHUNK = 128
ROPE_BASE = 10000.0
N_EXPERTS = 32
TOP_K = 4
D_FF = D_MODEL
SWIGLU_LIMIT = 7.0
SWIGLU_ALPHA = 1.702
EPS = 1e-6

LANES = 128
F32_EXP_UNDERFLOW = -88.0
VMEM_LIMIT = 56 * 1024 * 1024

BF16 = jnp.bfloat16
F32 = jnp.float32


def _split_bf16(v):
    hi = v.astype(BF16)
    lo = (v - hi.astype(F32)).astype(BF16)
    return hi, lo


def _dot(a, b):
    return jnp.dot(a, b, preferred_element_type=F32)


def _dot_nt(a, b):
    return lax.dot_general(a, b, (((1,), (1,)), ((), ())), preferred_element_type=F32)


def _in_proj_kernel(x_ref, g_ref, w_ref, cos_ref, sin_ref, qg_ref, kg_ref, seg_ref,
                    rq_ref, rk_ref, rv_ref, rg_ref, sq_ref, sk_ref, sv_ref):
    x = x_ref[...]
    h = x * lax.rsqrt(jnp.mean(x * x, axis=-1, keepdims=True) + EPS) * g_ref[...]
    p = _dot(h.astype(BF16), w_ref[...])
    cos2 = cos_ref[...]
    sin2 = sin_ref[...]
    k_scale = RET_HEAD_DIM ** -0.5
    for hd in range(RET_HEADS):
        lo = hd * RET_HEAD_DIM
        q = p[:, lo:lo + RET_HEAD_DIM]
        k = p[:, RET_WIDTH + lo:RET_WIDTH + lo + RET_HEAD_DIM]
        q = q * cos2 + pltpu.roll(q, RET_HEAD_DIM // 2, axis=1) * sin2
        k = k * cos2 + pltpu.roll(k, RET_HEAD_DIM // 2, axis=1) * sin2
        rq_ref[:, lo:lo + RET_HEAD_DIM] = q.astype(rq_ref.dtype)
        rk_ref[:, lo:lo + RET_HEAD_DIM] = (k * k_scale).astype(rk_ref.dtype)
    rv_ref[...] = p[:, 2 * RET_WIDTH:3 * RET_WIDTH].astype(rv_ref.dtype)
    rg_ref[...] = p[:, 3 * RET_WIDTH:4 * RET_WIDTH]
    base = 4 * RET_WIDTH
    seg = seg_ref[...]

    def head_norm(v, gain):
        hi, lo = _split_bf16(v * v)
        ms = (_dot(hi, seg) + _dot(lo, seg)) * (1.0 / SB_HEAD_DIM)
        return v * lax.rsqrt(ms + EPS) * gain

    sq = head_norm(p[:, base:base + SB_WIDTH], qg_ref[...])
    sk = head_norm(p[:, base + SB_WIDTH:base + 2 * SB_WIDTH], kg_ref[...])
    sq_ref[...] = (sq * (SB_HEAD_DIM ** -0.5)).astype(sq_ref.dtype)
    sk_ref[...] = sk.astype(sk_ref.dtype)
    sv_ref[...] = p[:, base + 2 * SB_WIDTH:base + 3 * SB_WIDTH].astype(sv_ref.dtype)


def _rope_tables(seq):
    half = RET_HEAD_DIM // 2
    inv = ROPE_BASE ** (-np.arange(half, dtype=np.float64) / half)
    n_a = seq // LANES
    ang_a = (np.arange(n_a, dtype=np.float64) * LANES)[:, None] * inv[None, :]
    ang_b = np.arange(LANES, dtype=np.float64)[:, None] * inv[None, :]
    ca, sa = jnp.asarray(np.cos(ang_a), F32), jnp.asarray(np.sin(ang_a), F32)
    cb, sb = jnp.asarray(np.cos(ang_b), F32), jnp.asarray(np.sin(ang_b), F32)
    cos = (ca[:, None, :] * cb[None] - sa[:, None, :] * sb[None]).reshape(seq, half)
    sin = (sa[:, None, :] * cb[None] + ca[:, None, :] * sb[None]).reshape(seq, half)
    return jnp.concatenate([cos, cos], axis=-1), jnp.concatenate([-sin, sin], axis=-1)


def _in_proj(x2, norm_g, w_in, sb_q_g, sb_k_g, seq, tm):
    t = x2.shape[0]
    assert seq % tm == 0 and t % tm == 0
    cos2, sin2 = _rope_tables(seq)
    seg = np.kron(np.eye(SB_HEADS), np.ones((SB_HEAD_DIM, SB_HEAD_DIM)))
    seg = jnp.asarray(seg, BF16)
    qg = jnp.tile(sb_q_g.astype(F32), SB_HEADS)[None, :]
    kg = jnp.tile(sb_k_g.astype(F32), SB_HEADS)[None, :]
    n_pos = seq // tm
    row = lambda w: pl.BlockSpec((tm, w), lambda i: (i, 0))
    full = lambda a: pl.BlockSpec(a.shape, lambda i: (0,) * a.ndim)
    pos = pl.BlockSpec((tm, RET_HEAD_DIM), lambda i: (i % n_pos, 0))
    w_bf = w_in.astype(BF16)
    g2 = norm_g.astype(F32)[None, :]
    out = lambda dt: jax.ShapeDtypeStruct((t, RET_WIDTH), dt)
    return pl.pallas_call(
        _in_proj_kernel,
        grid=(t // tm,),
        in_specs=[row(D_MODEL), full(g2), full(w_bf), pos, pos, full(qg), full(kg), full(seg)],
        out_specs=[row(RET_WIDTH)] * 7,
        out_shape=[out(BF16), out(F32), out(BF16), out(F32), out(BF16), out(BF16), out(BF16)],
        compiler_params=pltpu.CompilerParams(
            dimension_semantics=("arbitrary",), vmem_limit_bytes=VMEM_LIMIT),
        name="in_proj",
    )(x2, g2, w_bf, cos2, sin2, qg, kg, seg)


def _retention_kernel(q_ref, k_ref, v_ref, g_ref, ng_ref, intra_ref, qd_ref, kd_ref, cd_ref,
                      o_ref, state_ref, *, chunks):
    @pl.when(pl.program_id(1) == 0)
    def _():
        state_ref[...] = jnp.zeros_like(state_ref)

    def chunk(c, carry):
        rows = pl.ds(pl.multiple_of(c * RET_CHUNK, RET_CHUNK), RET_CHUNK)
        for hd in range(RET_HEADS):
            cols = slice(hd * RET_HEAD_DIM, (hd + 1) * RET_HEAD_DIM)
            q = q_ref[rows, cols]
            k = k_ref[rows, cols]
            v = v_ref[rows, cols]
            state = state_ref[hd]
            scores = _dot_nt(q, k.astype(BF16)) * intra_ref[hd]
            inner = _dot(scores.astype(BF16), v)
            cross = _dot(q, state.astype(BF16)) * qd_ref[:, cols]
            kd = (k * kd_ref[:, cols]).astype(BF16)
            state_ref[hd] = state * cd_ref[:, cols] + _dot(kd.T, v)
            o = inner + cross
            o = o * lax.rsqrt(jnp.mean(o * o, axis=-1, keepdims=True) + EPS) * ng_ref[:, cols]
            gate = g_ref[rows, cols]
            o_ref[rows, cols] = (o * (gate * jax.nn.sigmoid(gate))).astype(o_ref.dtype)
        return carry

    lax.fori_loop(0, chunks, chunk, 0)


def _retention(rq, rk, rv, rg, ret_norm_g, batch, seq, rows):
    assert seq % rows == 0 and rows % RET_CHUNK == 0
    log_gamma = np.log(1.0 - 2.0 ** (-5.0 - np.arange(RET_HEADS, dtype=np.float64)))
    idx = np.arange(RET_CHUNK, dtype=np.float64)
    diff = idx[:, None] - idx[None, :]
    intra = np.where(diff >= 0, np.exp(log_gamma[:, None, None] * np.maximum(diff, 0.0)), 0.0)
    q_decay = np.exp(log_gamma[:, None] * (idx + 1.0))
    k_decay = np.exp(log_gamma[:, None] * (RET_CHUNK - 1.0 - idx))
    chunk_decay = np.exp(log_gamma * RET_CHUNK)
    lane_rep = lambda a: np.repeat(a.T, RET_HEAD_DIM, axis=1)
    intra = jnp.asarray(intra, F32)
    qd = jnp.asarray(lane_rep(q_decay), F32)
    kd = jnp.asarray(lane_rep(k_decay), F32)
    cd = jnp.asarray(np.repeat(chunk_decay, RET_HEAD_DIM)[None, :], F32)
    ng = ret_norm_g.astype(F32).reshape(1, RET_WIDTH)
    n_r = seq // rows
    blk = pl.BlockSpec((rows, RET_WIDTH), lambda b, r: (b * n_r + r, 0))
    full = lambda a: pl.BlockSpec(a.shape, lambda b, r: (0,) * a.ndim)
    return pl.pallas_call(
        functools.partial(_retention_kernel, chunks=rows // RET_CHUNK),
        grid=(batch, n_r),
        in_specs=[blk, blk, blk, blk, full(ng), full(intra), full(qd), full(kd), full(cd)],
        out_specs=blk,
        out_shape=jax.ShapeDtypeStruct((batch * seq, RET_WIDTH), BF16),
        scratch_shapes=[pltpu.VMEM((RET_HEADS, RET_HEAD_DIM, RET_HEAD_DIM), F32)],
        compiler_params=pltpu.CompilerParams(
            dimension_semantics=("arbitrary", "arbitrary"), vmem_limit_bytes=VMEM_LIMIT),
        name="retention",
    )(rq, rk, rv, rg, ng, intra, qd, kd, cd)


def _sb_attn_kernel(q_ref, k_ref, v_ref, tri_ref, o_ref, *, blk):
    i = pl.program_id(2)
    q = q_ref[...]
    lane = lax.broadcasted_iota(jnp.int32, (blk, LANES), 1)
    qpos = lax.broadcasted_iota(jnp.int32, (blk, blk), 0)
    kpos = lax.broadcasted_iota(jnp.int32, (blk, blk), 1)
    below_diag = kpos < qpos
    tri = tri_ref[...]

    first_head = lane < SB_HEAD_DIM
    zero_q = jnp.zeros_like(q)
    q_heads = (jnp.where(first_head, q, zero_q), jnp.where(first_head, zero_q, q))

    def head(qm, kb, vb, c, acc, mask):
        z = _dot_nt(qm, kb)
        log_beta = jnp.minimum(z, 0.0) - jnp.log1p(jnp.exp(-jnp.abs(z)))
        log_rest = log_beta - z
        if mask is not None:
            log_rest = jnp.where(mask, log_rest, 0.0)
        hi, lo = _split_bf16(log_rest)
        sums = _dot(hi, tri) + _dot(lo, tri)
        w = jnp.exp(log_beta + (c + sums[:, :blk]))
        if mask is not None:
            w = jnp.where(mask, w, 0.0)
        return c + sums[:, blk:blk + 1], acc + _dot(w.astype(BF16), vb)

    def both_heads(j, c0, c1, acc0, acc1, mask):
        rows = pl.ds(pl.multiple_of(j * blk, blk), blk)
        kb = k_ref[rows, :]
        vb = v_ref[rows, :]
        c0, acc0 = head(q_heads[0], kb, vb, c0, acc0, mask)
        c1, acc1 = head(q_heads[1], kb, vb, c1, acc1, mask)
        return c0, c1, acc0, acc1

    def cond(carry):
        j, c0, c1, _, _ = carry
        return jnp.logical_and(j >= 0, jnp.max(jnp.maximum(c0, c1)) > F32_EXP_UNDERFLOW)

    def body(carry):
        j, c0, c1, acc0, acc1 = carry
        return (j - 1,) + both_heads(j, c0, c1, acc0, acc1, None)

    c_init = jnp.zeros((blk, 1), F32)
    acc_init = jnp.zeros((blk, LANES), F32)
    diag = both_heads(i, c_init, c_init, acc_init, acc_init, below_diag)
    _, _, _, acc0, acc1 = lax.while_loop(cond, body, (i - 1,) + diag)
    o_ref[...] = jnp.where(first_head, acc0, acc1).astype(o_ref.dtype)


def _sb_attention(sq, sk, sv, batch, seq, blk):
    assert seq % blk == 0
    nq = seq // blk
    pairs = SB_WIDTH // LANES
    j = np.arange(blk)
    tri = np.concatenate([(j[:, None] > j[None, :]).astype(np.float32),
                          np.ones((blk, LANES), np.float32)], axis=1)
    tri = jnp.asarray(tri, BF16)
    qblk = pl.BlockSpec((blk, LANES), lambda b, p, i: (b * nq + i, p))
    kvblk = pl.BlockSpec((seq, LANES), lambda b, p, i: (b, p))
    return pl.pallas_call(
        functools.partial(_sb_attn_kernel, blk=blk),
        grid=(batch, pairs, nq),
        in_specs=[qblk, kvblk, kvblk, pl.BlockSpec(tri.shape, lambda b, p, i: (0, 0))],
        out_specs=qblk,
        out_shape=jax.ShapeDtypeStruct((batch * seq, SB_WIDTH), BF16),
        compiler_params=pltpu.CompilerParams(
            dimension_semantics=("arbitrary", "arbitrary", "arbitrary"),
            vmem_limit_bytes=VMEM_LIMIT),
        name="sb_attn",
    )(sq, sk, sv, tri)


def _out_proj_kernel(ret_ref, sb_ref, x_ref, wtop_ref, wbot_ref, g_ref, wrh_ref, wrl_ref,
                     br_ref, tri_ref,
                     x1_ref, xf_ref, idx_ref, w_ref, rank_ref, cnt_ref, run_ref):
    @pl.when(pl.program_id(0) == 0)
    def _():
        run_ref[...] = jnp.zeros_like(run_ref)

    x1 = x_ref[...] + _dot(ret_ref[...], wtop_ref[...]) + _dot(sb_ref[...], wbot_ref[...])
    x1_ref[...] = x1
    xf = x1 * lax.rsqrt(jnp.mean(x1 * x1, axis=-1, keepdims=True) + EPS) * g_ref[...]
    xf_ref[...] = xf
    xh, xl = _split_bf16(xf)
    wrh = wrh_ref[...]
    logits = _dot_nt(wrh, xh) + _dot_nt(wrh, xl) + _dot_nt(wrl_ref[...], xh) + br_ref[...]
    n_e, tm = logits.shape
    e_iota = lax.broadcasted_iota(jnp.int32, (n_e, tm), 0)
    cur = logits
    tops, sels, hots = [], [], []
    for _ in range(TOP_K):
        m = jnp.max(cur, axis=0, keepdims=True)
        sel = jnp.min(jnp.where(cur == m, e_iota, n_e), axis=0, keepdims=True)
        hot = e_iota == sel
        cur = jnp.where(hot, -jnp.inf, cur)
        tops.append(m)
        sels.append(sel)
        hots.append(hot)
    ps = [jnp.exp(m - tops[0]) for m in tops]
    denom = ps[0] + ps[1] + ps[2] + ps[3]
    chosen = jnp.zeros((n_e, tm), F32)
    for hot in hots:
        chosen = chosen + hot.astype(F32)
    before = run_ref[:, 0:1] + _dot(chosen.astype(BF16), tri_ref[...])
    for kk in range(TOP_K):
        idx_ref[kk:kk + 1, :] = sels[kk]
        w_ref[kk:kk + 1, :] = ps[kk] / denom
        rank = jnp.sum(jnp.where(hots[kk], before, 0.0), axis=0, keepdims=True)
        rank_ref[kk:kk + 1, :] = rank.astype(jnp.int32)
    run = run_ref[...] + jnp.sum(chosen, axis=1, keepdims=True)
    run_ref[...] = run
    cnt_ref[...] = run.astype(jnp.int32)


def _out_proj_route(ret, sb, x2, w_out, ffn_g, w_router, b_router, tm):
    t = x2.shape[0]
    assert t % tm == 0
    w_bf = w_out.astype(BF16)
    wtop, wbot = w_bf[:RET_WIDTH], w_bf[RET_WIDTH:]
    g2 = ffn_g.astype(F32)[None, :]
    wr_t = w_router.astype(F32).T
    wrh = wr_t.astype(BF16)
    wrl = (wr_t - wrh.astype(F32)).astype(BF16)
    br = b_router.astype(F32)[:, None]
    tt = np.arange(tm)
    tri = jnp.asarray((tt[:, None] < tt[None, :]).astype(np.float32), BF16)
    row = lambda w: pl.BlockSpec((tm, w), lambda i: (i, 0))
    full = lambda a: pl.BlockSpec(a.shape, lambda i: (0,) * a.ndim)
    col = pl.BlockSpec((TOP_K, tm), lambda i: (0, i))
    return pl.pallas_call(
        _out_proj_kernel,
        grid=(t // tm,),
        in_specs=[row(RET_WIDTH), row(SB_WIDTH), row(D_MODEL), full(wtop), full(wbot), full(g2),
                  full(wrh), full(wrl), full(br), full(tri)],
        out_specs=[row(D_MODEL), row(D_MODEL), col, col, col,
                   pl.BlockSpec((N_EXPERTS, LANES), lambda i: (0, 0))],
        out_shape=[jax.ShapeDtypeStruct((t, D_MODEL), F32),
                   jax.ShapeDtypeStruct((t, D_MODEL), F32),
                   jax.ShapeDtypeStruct((TOP_K, t), jnp.int32),
                   jax.ShapeDtypeStruct((TOP_K, t), F32),
                   jax.ShapeDtypeStruct((TOP_K, t), jnp.int32),
                   jax.ShapeDtypeStruct((N_EXPERTS, LANES), jnp.int32)],
        scratch_shapes=[pltpu.VMEM((N_EXPERTS, LANES), F32)],
        compiler_params=pltpu.CompilerParams(
            dimension_semantics=("arbitrary",), vmem_limit_bytes=VMEM_LIMIT),
        name="out_proj_route",
    )(ret, sb, x2, wtop, wbot, g2, wrh, wrl, br, tri)


def _dest_kernel(pstart_ref, idx_ref, rank_ref, dest_ref):
    idx = idx_ref[...]
    dest = rank_ref[...]
    for e in range(N_EXPERTS):
        dest = dest + jnp.where(idx == e, pstart_ref[e], 0)
    dest_ref[...] = dest


def _dest(padded_start, top_idx, rank, tn):
    t = top_idx.shape[1]
    assert t % tn == 0
    col = pl.BlockSpec((TOP_K, tn), lambda i, ps: (0, i))
    return pl.pallas_call(
        _dest_kernel,
        grid_spec=pltpu.PrefetchScalarGridSpec(
            num_scalar_prefetch=1, grid=(t // tn,), in_specs=[col, col], out_specs=col),
        out_shape=jax.ShapeDtypeStruct((TOP_K, t), jnp.int32),
        name="dest",
    )(padded_start, top_idx, rank)


def _dispatch_kernel(pend_ref, dest_ref, xf_ref, slots_hbm, zbuf, sem, zsem, *, tm, blk):
    @pl.when(pl.program_id(0) == 0)
    def _():
        zbuf[...] = jnp.zeros_like(zbuf)

        def tail_copy(e):
            first = pl.multiple_of(pend_ref[e] - blk, blk)
            return pltpu.make_async_copy(zbuf, slots_hbm.at[pl.ds(first, blk), :], zsem)

        def nonempty(e):
            return pend_ref[e] > (pend_ref[e - 1] if e else 0)

        for e in range(N_EXPERTS):
            pl.when(nonempty(e))(lambda e=e: tail_copy(e).start())
        for e in range(N_EXPERTS):
            pl.when(nonempty(e))(lambda e=e: tail_copy(e).wait())

    def row_copy(t, kk):
        return pltpu.make_async_copy(
            xf_ref.at[pl.ds(t, 1), :],
            slots_hbm.at[pl.ds(dest_ref[kk, t], 1), :], sem)

    def start(t, carry):
        for kk in range(TOP_K):
            row_copy(t, kk).start(priority=kk % 2)
        return carry

    lax.fori_loop(0, tm, start, 0, unroll=8)
    for kk in range(TOP_K):
        pltpu.make_async_copy(xf_ref, slots_hbm.at[pl.ds(0, tm), :], sem).wait()


def _dispatch(padded_end, dest, xf, n_pad, tm, blk):
    t = xf.shape[0]
    assert t % tm == 0
    return pl.pallas_call(
        functools.partial(_dispatch_kernel, tm=tm, blk=blk),
        grid_spec=pltpu.PrefetchScalarGridSpec(
            num_scalar_prefetch=1,
            grid=(t // tm,),
            in_specs=[pl.BlockSpec((TOP_K, tm), lambda i, pe: (0, i), memory_space=pltpu.SMEM),
                      pl.BlockSpec((tm, D_MODEL), lambda i, pe: (i, 0))],
            out_specs=pl.BlockSpec(memory_space=pl.ANY),
            scratch_shapes=[pltpu.VMEM((blk, D_MODEL), F32),
                            pltpu.SemaphoreType.DMA, pltpu.SemaphoreType.DMA],
        ),
        out_shape=jax.ShapeDtypeStruct((n_pad, D_MODEL), F32),
        compiler_params=pltpu.CompilerParams(
            dimension_semantics=("arbitrary",), vmem_limit_bytes=VMEM_LIMIT,
            has_side_effects=True),
        name="dispatch",
    )(padded_end, dest, xf)


def _experts_kernel(be_ref, nu_ref, x_ref, wg_ref, wu_ref, wd_ref, bg_ref, bu_ref, bd_ref, y_ref):
    del be_ref
    used = pl.program_id(0) < nu_ref[0]

    @pl.when(used)
    def _():
        xb = x_ref[...].astype(BF16)
        gate = jnp.minimum(_dot(xb, wg_ref[...]) + bg_ref[...], SWIGLU_LIMIT)
        up = jnp.clip(_dot(xb, wu_ref[...]) + bu_ref[...], -SWIGLU_LIMIT, SWIGLU_LIMIT)
        hidden = (up + 1.0) * gate * jax.nn.sigmoid(SWIGLU_ALPHA * gate)
        y_ref[...] = _dot(hidden.astype(BF16), wd_ref[...]) + bd_ref[...]

    @pl.when(jnp.logical_not(used))
    def _():
        y_ref[...] = jnp.zeros_like(y_ref)


def _experts(block_expert, n_used, slots, wg, wu, wd, bg, bu, bd, blk):
    n_pad = slots.shape[0]
    n_blocks = n_pad // blk
    xmap = lambda j, be, nu: (jnp.minimum(j, nu[0] - 1), 0)
    wmap = lambda j, be, nu: (be[j], 0, 0)
    wspec = pl.BlockSpec((None, D_MODEL, D_FF), wmap)
    bspec = pl.BlockSpec((None, 1, D_FF), wmap)
    return pl.pallas_call(
        _experts_kernel,
        grid_spec=pltpu.PrefetchScalarGridSpec(
            num_scalar_prefetch=2,
            grid=(n_blocks,),
            in_specs=[pl.BlockSpec((blk, D_MODEL), xmap), wspec, wspec, wspec, bspec, bspec, bspec],
            out_specs=pl.BlockSpec((blk, D_MODEL), lambda j, be, nu: (j, 0)),
        ),
        out_shape=jax.ShapeDtypeStruct((n_pad, D_MODEL), F32),
        compiler_params=pltpu.CompilerParams(
            dimension_semantics=("arbitrary",), vmem_limit_bytes=VMEM_LIMIT),
        name="experts",
    )(block_expert, n_used, slots, wg, wu, wd, bg, bu, bd)


def _combine_kernel(dest_ref, y_hbm, w_ref, x1_ref, o_ref, buf, sem, *, tm):
    def row_copy(t, kk):
        return pltpu.make_async_copy(
            y_hbm.at[pl.ds(dest_ref[kk, t], 1), :],
            buf.at[kk, pl.ds(t, 1), :], sem)

    def start(t, carry):
        for kk in range(TOP_K):
            row_copy(t, kk).start(priority=kk % 2)
        return carry

    lax.fori_loop(0, tm, start, 0, unroll=8)
    for kk in range(TOP_K):
        pltpu.make_async_copy(y_hbm.at[pl.ds(0, tm), :], buf.at[kk], sem).wait()
    w = w_ref[...]
    acc = x1_ref[...]
    for kk in range(TOP_K):
        acc = acc + buf[kk] * w[:, kk:kk + 1]
    o_ref[...] = acc


def _combine(dest, y_slots, w_t, x1, tm):
    t = x1.shape[0]
    assert t % tm == 0
    row = pl.BlockSpec((tm, D_MODEL), lambda i: (i, 0))
    return pl.pallas_call(
        functools.partial(_combine_kernel, tm=tm),
        grid=(t // tm,),
        in_specs=[pl.BlockSpec((TOP_K, tm), lambda i: (0, i), memory_space=pltpu.SMEM),
                  pl.BlockSpec(memory_space=pl.ANY),
                  pl.BlockSpec((tm, TOP_K), lambda i: (i, 0)),
                  row],
        out_specs=row,
        out_shape=jax.ShapeDtypeStruct((t, D_MODEL), F32),
        scratch_shapes=[pltpu.VMEM((TOP_K, tm, D_MODEL), F32), pltpu.SemaphoreType.DMA],
        compiler_params=pltpu.CompilerParams(
            dimension_semantics=("arbitrary",), vmem_limit_bytes=VMEM_LIMIT),
        name="combine",
    )(dest, y_slots, w_t, x1)


def _tiles(batch, seq):
    return dict(
        in_proj=min(512, seq),
        retention=min(1024, seq),
        sb_block=min(256, seq),
        out_proj=min(512, seq),
        dest=min(8192, batch * seq),
        dispatch=min(1024, seq),
        combine=min(256, seq),
        expert_block=512,
    )


def _layer(x, attn_norm_g, w_in, ret_norm_g, sb_q_norm_g, sb_k_norm_g, w_out,
           ffn_norm_g, w_router, b_router, w_gate_up, b_gate_up, w_down, b_down):
    batch, seq, d = x.shape
    t = batch * seq
    tiles = _tiles(batch, seq)
    x2 = x.reshape(t, d)

    rq, rk, rv, rg, sq, sk, sv = _in_proj(
        x2, attn_norm_g, w_in, sb_q_norm_g, sb_k_norm_g, seq, tiles["in_proj"])
    ret = _retention(rq, rk, rv, rg, ret_norm_g, batch, seq, tiles["retention"])
    sb = _sb_attention(sq, sk, sv, batch, seq, tiles["sb_block"])
    x1, xf, top_idx, top_w, rank, counts = _out_proj_route(
        ret, sb, x2, w_out, ffn_norm_g, w_router, b_router, tiles["out_proj"])

    blk = tiles["expert_block"]
    n_pad = t * TOP_K + N_EXPERTS * blk
    n_blocks = n_pad // blk
    counts = counts[:, 0]
    padded = (counts + blk - 1) // blk * blk
    padded_end = jnp.cumsum(padded)
    padded_start = padded_end - padded
    block_first = jnp.arange(n_blocks, dtype=jnp.int32) * blk
    block_expert = jnp.minimum(
        jnp.sum(padded_end[None, :] <= block_first[:, None], axis=1), N_EXPERTS - 1
    ).astype(jnp.int32)
    n_used = (padded_end[-1:] // blk).astype(jnp.int32)

    dest = _dest(padded_start, top_idx, rank, tiles["dest"])
    slots = _dispatch(padded_end, dest, xf, n_pad, tiles["dispatch"], blk)

    wgu = w_gate_up.reshape(N_EXPERTS, D_MODEL, D_FF, 2)
    wg = wgu[..., 0].astype(BF16)
    wu = wgu[..., 1].astype(BF16)
    bgu = b_gate_up.astype(F32).reshape(N_EXPERTS, 1, D_FF, 2)
    y_slots = _experts(block_expert, n_used, slots, wg, wu, w_down.astype(BF16),
                       bgu[..., 0], bgu[..., 1], b_down.astype(F32)[:, None, :], blk)

    out = _combine(dest, y_slots, top_w.T, x1, tiles["combine"])
    return out.reshape(batch, seq, d)


def kernel(x, attn_norm_g, w_in, ret_norm_g, sb_q_norm_g, sb_k_norm_g, w_out, ffn_norm_g,
           w_router, b_router, w_gate_up, b_gate_up, w_down, b_down):
    depth = attn_norm_g.shape[0]
    for l in range(depth):
        x = _layer(x, attn_norm_g[l], w_in[l], ret_norm_g[l], sb_q_norm_g[l], sb_k_norm_g[l],
                   w_out[l], ffn_norm_g[l], w_router[l], b_router[l], w_gate_up[l],
                   b_gate_up[l], w_down[l], b_down[l])
    return x
```

```python
import functools

import numpy as np
import jax
import jax.numpy as jnp
from jax import lax
from jax.experimental import pallas as pl
from jax.experimental.pallas import tpu as pltpu

D_MODEL = 1024
RET_WIDTH = 512
RET_HEADS = 4
RET_HEAD_DIM = 128
SB_WIDTH = 512
SB_HEADS = 8
SB_HEAD_DIM = 64
IN_PROJ_WIDTH = 4 * RET_WIDTH + 3 * SB_WIDTH
RET_CHUNK = 128
ROPE_BASE = 10000.0
N_EXPERTS = 32
TOP_K = 4
D_FF = D_MODEL
SWIGLU_LIMIT = 7.0
SWIGLU_ALPHA = 1.702
EPS = 1e-6

LANES = 128
F32_EXP_UNDERFLOW = -88.0
VMEM_LIMIT = 56 * 1024 * 1024

BF16 = jnp.bfloat16
F32 = jnp.float32


def _split_bf16(v):
    hi = v.astype(BF16)
    lo = (v - hi.astype(F32)).astype(BF16)
    return hi, lo


SUBLANES = 8
ROW_TILES = D_MODEL // LANES
assert ROW_TILES == SUBLANES


def _store_token_tiles(ref, value):
    n = value.shape[0]
    for s in range(ROW_TILES):
        ref[pl.ds(s, n, stride=SUBLANES), :] = value[:, s * LANES:(s + 1) * LANES]


def _load_token_tiles(ref, n):
    return jnp.concatenate(
        [ref[pl.ds(s, n, stride=SUBLANES), :] for s in range(ROW_TILES)], axis=1)


def _token_rows(i):
    return pl.ds(pl.multiple_of(i * SUBLANES, SUBLANES), SUBLANES)


def _dot(a, b):
    return jnp.dot(a, b, preferred_element_type=F32)


def _dot_nt(a, b):
    return lax.dot_general(a, b, (((1,), (1,)), ((), ())), preferred_element_type=F32)


def _in_proj_kernel(x_ref, g_ref, w_ref, cos_ref, sin_ref, qg_ref, kg_ref, seg_ref,
                    rq_ref, rk_ref, rv_ref, rg_ref, sq_ref, sk_ref, sv_ref):
    x = x_ref[...]
    h = x * lax.rsqrt(jnp.mean(x * x, axis=-1, keepdims=True) + EPS) * g_ref[...]
    p = _dot(h.astype(BF16), w_ref[...])
    cos2 = cos_ref[...]
    sin2 = sin_ref[...]
    k_scale = RET_HEAD_DIM ** -0.5
    for hd in range(RET_HEADS):
        lo = hd * RET_HEAD_DIM
        q = p[:, lo:lo + RET_HEAD_DIM]
        k = p[:, RET_WIDTH + lo:RET_WIDTH + lo + RET_HEAD_DIM]
        q = q * cos2 + pltpu.roll(q, RET_HEAD_DIM // 2, axis=1) * sin2
        k = k * cos2 + pltpu.roll(k, RET_HEAD_DIM // 2, axis=1) * sin2
        rq_ref[:, lo:lo + RET_HEAD_DIM] = q.astype(rq_ref.dtype)
        rk_ref[:, lo:lo + RET_HEAD_DIM] = (k * k_scale).astype(rk_ref.dtype)
    rv_ref[...] = p[:, 2 * RET_WIDTH:3 * RET_WIDTH].astype(rv_ref.dtype)
    rg_ref[...] = p[:, 3 * RET_WIDTH:4 * RET_WIDTH]
    base = 4 * RET_WIDTH
    seg = seg_ref[...]

    def head_norm(v, gain):
        hi, lo = _split_bf16(v * v)
        ms = (_dot(hi, seg) + _dot(lo, seg)) * (1.0 / SB_HEAD_DIM)
        return v * lax.rsqrt(ms + EPS) * gain

    sq = head_norm(p[:, base:base + SB_WIDTH], qg_ref[...])
    sk = head_norm(p[:, base + SB_WIDTH:base + 2 * SB_WIDTH], kg_ref[...])
    sq_ref[...] = (sq * (SB_HEAD_DIM ** -0.5)).astype(sq_ref.dtype)
    sk_ref[...] = sk.astype(sk_ref.dtype)
    sv_ref[...] = p[:, base + 2 * SB_WIDTH:base + 3 * SB_WIDTH].astype(sv_ref.dtype)


def _rope_tables(seq):
    half = RET_HEAD_DIM // 2
    inv = ROPE_BASE ** (-np.arange(half, dtype=np.float64) / half)
    n_a = seq // LANES
    ang_a = (np.arange(n_a, dtype=np.float64) * LANES)[:, None] * inv[None, :]
    ang_b = np.arange(LANES, dtype=np.float64)[:, None] * inv[None, :]
    ca, sa = jnp.asarray(np.cos(ang_a), F32), jnp.asarray(np.sin(ang_a), F32)
    cb, sb = jnp.asarray(np.cos(ang_b), F32), jnp.asarray(np.sin(ang_b), F32)
    cos = (ca[:, None, :] * cb[None] - sa[:, None, :] * sb[None]).reshape(seq, half)
    sin = (sa[:, None, :] * cb[None] + ca[:, None, :] * sb[None]).reshape(seq, half)
    return jnp.concatenate([cos, cos], axis=-1), jnp.concatenate([-sin, sin], axis=-1)


def _in_proj(x2, norm_g, w_in, sb_q_g, sb_k_g, seq, tm):
    t = x2.shape[0]
    assert seq % tm == 0 and t % tm == 0
    cos2, sin2 = _rope_tables(seq)
    seg = np.kron(np.eye(SB_HEADS), np.ones((SB_HEAD_DIM, SB_HEAD_DIM)))
    seg = jnp.asarray(seg, BF16)
    qg = jnp.tile(sb_q_g.astype(F32), SB_HEADS)[None, :]
    kg = jnp.tile(sb_k_g.astype(F32), SB_HEADS)[None, :]
    n_pos = seq // tm
    row = lambda w: pl.BlockSpec((tm, w), lambda i: (i, 0))
    full = lambda a: pl.BlockSpec(a.shape, lambda i: (0,) * a.ndim)
    pos = pl.BlockSpec((tm, RET_HEAD_DIM), lambda i: (i % n_pos, 0))
    w_bf = w_in.astype(BF16)
    g2 = norm_g.astype(F32)[None, :]
    out = lambda dt: jax.ShapeDtypeStruct((t, RET_WIDTH), dt)
    return pl.pallas_call(
        _in_proj_kernel,
        grid=(t // tm,),
        in_specs=[row(D_MODEL), full(g2), full(w_bf), pos, pos, full(qg), full(kg), full(seg)],
        out_specs=[row(RET_WIDTH)] * 7,
        out_shape=[out(BF16), out(F32), out(BF16), out(F32), out(BF16), out(BF16), out(BF16)],
        compiler_params=pltpu.CompilerParams(
            dimension_semantics=("arbitrary",), vmem_limit_bytes=VMEM_LIMIT),
        name="in_proj",
    )(x2, g2, w_bf, cos2, sin2, qg, kg, seg)


def _retention_kernel(q_ref, k_ref, v_ref, g_ref, ng_ref, intra_ref, qd_ref, kd_ref, cd_ref,
                      o_ref, state_ref, *, chunks):
    @pl.when(pl.program_id(1) == 0)
    def _():
        state_ref[...] = jnp.zeros_like(state_ref)

    def chunk(c, carry):
        rows = pl.ds(pl.multiple_of(c * RET_CHUNK, RET_CHUNK), RET_CHUNK)
        for hd in range(RET_HEADS):
            cols = slice(hd * RET_HEAD_DIM, (hd + 1) * RET_HEAD_DIM)
            q = q_ref[rows, cols]
            k = k_ref[rows, cols]
            v = v_ref[rows, cols]
            state = state_ref[hd]
            scores = _dot_nt(q, k.astype(BF16)) * intra_ref[hd]
            inner = _dot(scores.astype(BF16), v)
            cross = _dot(q, state.astype(BF16)) * qd_ref[:, cols]
            kd = (k * kd_ref[:, cols]).astype(BF16)
            state_ref[hd] = state * cd_ref[:, cols] + _dot(kd.T, v)
            o = inner + cross
            o = o * lax.rsqrt(jnp.mean(o * o, axis=-1, keepdims=True) + EPS) * ng_ref[:, cols]
            gate = g_ref[rows, cols]
            o_ref[rows, cols] = (o * (gate * jax.nn.sigmoid(gate))).astype(o_ref.dtype)
        return carry

    lax.fori_loop(0, chunks, chunk, 0)


def _retention(rq, rk, rv, rg, ret_norm_g, batch, seq, rows):
    assert seq % rows == 0 and rows % RET_CHUNK == 0
    log_gamma = np.log(1.0 - 2.0 ** (-5.0 - np.arange(RET_HEADS, dtype=np.float64)))
    idx = np.arange(RET_CHUNK, dtype=np.float64)
    diff = idx[:, None] - idx[None, :]
    intra = np.where(diff >= 0, np.exp(log_gamma[:, None, None] * np.maximum(diff, 0.0)), 0.0)
    q_decay = np.exp(log_gamma[:, None] * (idx + 1.0))
    k_decay = np.exp(log_gamma[:, None] * (RET_CHUNK - 1.0 - idx))
    chunk_decay = np.exp(log_gamma * RET_CHUNK)
    lane_rep = lambda a: np.repeat(a.T, RET_HEAD_DIM, axis=1)
    intra = jnp.asarray(intra, F32)
    qd = jnp.asarray(lane_rep(q_decay), F32)
    kd = jnp.asarray(lane_rep(k_decay), F32)
    cd = jnp.asarray(np.repeat(chunk_decay, RET_HEAD_DIM)[None, :], F32)
    ng = ret_norm_g.astype(F32).reshape(1, RET_WIDTH)
    n_r = seq // rows
    blk = pl.BlockSpec((rows, RET_WIDTH), lambda b, r: (b * n_r + r, 0))
    full = lambda a: pl.BlockSpec(a.shape, lambda b, r: (0,) * a.ndim)
    return pl.pallas_call(
        functools.partial(_retention_kernel, chunks=rows // RET_CHUNK),
        grid=(batch, n_r),
        in_specs=[blk, blk, blk, blk, full(ng), full(intra), full(qd), full(kd), full(cd)],
        out_specs=blk,
        out_shape=jax.ShapeDtypeStruct((batch * seq, RET_WIDTH), BF16),
        scratch_shapes=[pltpu.VMEM((RET_HEADS, RET_HEAD_DIM, RET_HEAD_DIM), F32)],
        compiler_params=pltpu.CompilerParams(
            dimension_semantics=("arbitrary", "arbitrary"), vmem_limit_bytes=VMEM_LIMIT),
        name="retention",
    )(rq, rk, rv, rg, ng, intra, qd, kd, cd)


def _sb_attn_kernel(q_ref, k_ref, v_ref, tri_ref, o_ref, *, blk):
    i = pl.program_id(2)
    q = q_ref[...]
    lane = lax.broadcasted_iota(jnp.int32, (blk, LANES), 1)
    qpos = lax.broadcasted_iota(jnp.int32, (blk, blk), 0)
    kpos = lax.broadcasted_iota(jnp.int32, (blk, blk), 1)
    below_diag = kpos < qpos
    tri = tri_ref[...]

    first_head = lane < SB_HEAD_DIM
    zero_q = jnp.zeros_like(q)
    q_heads = (jnp.where(first_head, q, zero_q), jnp.where(first_head, zero_q, q))

    def head(qm, kb, vb, c, acc, mask):
        z = _dot_nt(qm, kb)
        log_beta = jnp.minimum(z, 0.0) - jnp.log(1.0 + jnp.exp(-jnp.abs(z)))
        log_rest = log_beta - z
        if mask is not None:
            log_rest = jnp.where(mask, log_rest, 0.0)
        hi, lo = _split_bf16(log_rest)
        sums = _dot(hi, tri) + _dot(lo, tri)
        w = jnp.exp(log_beta + (c + sums[:, :blk]))
        if mask is not None:
            w = jnp.where(mask, w, 0.0)
        return c + sums[:, blk:blk + 1], acc + _dot(w.astype(BF16), vb)

    def both_heads(j, c0, c1, acc0, acc1, mask):
        rows = pl.ds(pl.multiple_of(j * blk, blk), blk)
        kb = k_ref[rows, :]
        vb = v_ref[rows, :]
        c0, acc0 = head(q_heads[0], kb, vb, c0, acc0, mask)
        c1, acc1 = head(q_heads[1], kb, vb, c1, acc1, mask)
        return c0, c1, acc0, acc1

    def cond(carry):
        j, c0, c1, _, _ = carry
        return jnp.logical_and(j >= 0, jnp.max(jnp.maximum(c0, c1)) > F32_EXP_UNDERFLOW)

    def body(carry):
        j, c0, c1, acc0, acc1 = carry
        return (j - 1,) + both_heads(j, c0, c1, acc0, acc1, None)

    c_init = jnp.zeros((blk, 1), F32)
    acc_init = jnp.zeros((blk, LANES), F32)
    diag = both_heads(i, c_init, c_init, acc_init, acc_init, below_diag)
    _, _, _, acc0, acc1 = lax.while_loop(cond, body, (i - 1,) + diag)
    o_ref[...] = jnp.where(first_head, acc0, acc1).astype(o_ref.dtype)


def _sb_attention(sq, sk, sv, batch, seq, blk):
    assert seq % blk == 0
    nq = seq // blk
    pairs = SB_WIDTH // LANES
    j = np.arange(blk)
    tri = np.concatenate([(j[:, None] > j[None, :]).astype(np.float32),
                          np.ones((blk, LANES), np.float32)], axis=1)
    tri = jnp.asarray(tri, BF16)
    qblk = pl.BlockSpec((blk, LANES), lambda b, p, i: (b * nq + i, p))
    kvblk = pl.BlockSpec((seq, LANES), lambda b, p, i: (b, p))
    return pl.pallas_call(
        functools.partial(_sb_attn_kernel, blk=blk),
        grid=(batch, pairs, nq),
        in_specs=[qblk, kvblk, kvblk, pl.BlockSpec(tri.shape, lambda b, p, i: (0, 0))],
        out_specs=qblk,
        out_shape=jax.ShapeDtypeStruct((batch * seq, SB_WIDTH), BF16),
        compiler_params=pltpu.CompilerParams(
            dimension_semantics=("arbitrary", "arbitrary", "arbitrary"),
            vmem_limit_bytes=VMEM_LIMIT),
        name="sb_attn",
    )(sq, sk, sv, tri)


def _out_proj_kernel(ret_ref, sb_ref, x_ref, wtop_ref, wbot_ref, g_ref, wrh_ref, wrl_ref,
                     br_ref, tri_ref,
                     x1_ref, xf_ref, idx_ref, w_ref, rank_ref, cnt_ref, run_ref):
    @pl.when(pl.program_id(0) == 0)
    def _():
        run_ref[...] = jnp.zeros_like(run_ref)

    x1 = x_ref[...] + _dot(ret_ref[...], wtop_ref[...]) + _dot(sb_ref[...], wbot_ref[...])
    x1_ref[...] = x1
    xf = x1 * lax.rsqrt(jnp.mean(x1 * x1, axis=-1, keepdims=True) + EPS) * g_ref[...]
    _store_token_tiles(xf_ref, xf)
    xh, xl = _split_bf16(xf)
    wrh = wrh_ref[...]
    logits = _dot_nt(wrh, xh) + _dot_nt(wrh, xl) + _dot_nt(wrl_ref[...], xh) + br_ref[...]
    n_e, tm = logits.shape
    e_iota = lax.broadcasted_iota(jnp.int32, (n_e, tm), 0)
    cur = logits
    tops, sels, hots = [], [], []
    for _ in range(TOP_K):
        m = jnp.max(cur, axis=0, keepdims=True)
        sel = jnp.min(jnp.where(cur == m, e_iota, n_e), axis=0, keepdims=True)
        hot = e_iota == sel
        cur = jnp.where(hot, -jnp.inf, cur)
        tops.append(m)
        sels.append(sel)
        hots.append(hot)
    ps = [jnp.exp(m - tops[0]) for m in tops]
    denom = ps[0] + ps[1] + ps[2] + ps[3]
    chosen = jnp.zeros((n_e, tm), F32)
    for hot in hots:
        chosen = chosen + hot.astype(F32)
    before = run_ref[:, 0:1] + _dot(chosen.astype(BF16), tri_ref[...])
    for kk in range(TOP_K):
        idx_ref[kk:kk + 1, :] = sels[kk]
        w_ref[kk:kk + 1, :] = ps[kk] / denom
        rank = jnp.sum(jnp.where(hots[kk], before, 0.0), axis=0, keepdims=True)
        rank_ref[kk:kk + 1, :] = rank.astype(jnp.int32)
    run = run_ref[...] + jnp.sum(chosen, axis=1, keepdims=True)
    run_ref[...] = run
    cnt_ref[...] = run.astype(jnp.int32)


def _out_proj_route(ret, sb, x2, w_out, ffn_g, w_router, b_router, tm):
    t = x2.shape[0]
    assert t % tm == 0
    w_bf = w_out.astype(BF16)
    wtop, wbot = w_bf[:RET_WIDTH], w_bf[RET_WIDTH:]
    g2 = ffn_g.astype(F32)[None, :]
    wr_t = w_router.astype(F32).T
    wrh = wr_t.astype(BF16)
    wrl = (wr_t - wrh.astype(F32)).astype(BF16)
    br = b_router.astype(F32)[:, None]
    tt = np.arange(tm)
    tri = jnp.asarray((tt[:, None] < tt[None, :]).astype(np.float32), BF16)
    row = lambda w: pl.BlockSpec((tm, w), lambda i: (i, 0))
    full = lambda a: pl.BlockSpec(a.shape, lambda i: (0,) * a.ndim)
    col = pl.BlockSpec((TOP_K, tm), lambda i: (0, i))
    return pl.pallas_call(
        _out_proj_kernel,
        grid=(t // tm,),
        in_specs=[row(RET_WIDTH), row(SB_WIDTH), row(D_MODEL), full(wtop), full(wbot), full(g2),
                  full(wrh), full(wrl), full(br), full(tri)],
        out_specs=[row(D_MODEL), pl.BlockSpec((tm * SUBLANES, LANES), lambda i: (i, 0)),
                   col, col, col, pl.BlockSpec((N_EXPERTS, LANES), lambda i: (0, 0))],
        out_shape=[jax.ShapeDtypeStruct((t, D_MODEL), F32),
                   jax.ShapeDtypeStruct((t * SUBLANES, LANES), F32),
                   jax.ShapeDtypeStruct((TOP_K, t), jnp.int32),
                   jax.ShapeDtypeStruct((TOP_K, t), F32),
                   jax.ShapeDtypeStruct((TOP_K, t), jnp.int32),
                   jax.ShapeDtypeStruct((N_EXPERTS, LANES), jnp.int32)],
        scratch_shapes=[pltpu.VMEM((N_EXPERTS, LANES), F32)],
        compiler_params=pltpu.CompilerParams(
            dimension_semantics=("arbitrary",), vmem_limit_bytes=VMEM_LIMIT),
        name="out_proj_route",
    )(ret, sb, x2, wtop, wbot, g2, wrh, wrl, br, tri)


def _dest_kernel(pstart_ref, idx_ref, rank_ref, dest_ref):
    idx = idx_ref[...]
    dest = rank_ref[...]
    for e in range(N_EXPERTS):
        dest = dest + jnp.where(idx == e, pstart_ref[e], 0)
    dest_ref[...] = dest


def _dest(padded_start, top_idx, rank, tn):
    t = top_idx.shape[1]
    assert t % tn == 0
    col = pl.BlockSpec((TOP_K, tn), lambda i, ps: (0, i))
    return pl.pallas_call(
        _dest_kernel,
        grid_spec=pltpu.PrefetchScalarGridSpec(
            num_scalar_prefetch=1, grid=(t // tn,), in_specs=[col, col], out_specs=col),
        out_shape=jax.ShapeDtypeStruct((TOP_K, t), jnp.int32),
        name="dest",
    )(padded_start, top_idx, rank)


def _dispatch_kernel(pend_ref, dest_ref, xf_ref, slots_hbm, zbuf, sem, zsem, *, tm, blk):
    @pl.when(pl.program_id(0) == 0)
    def _():
        zbuf[...] = jnp.zeros_like(zbuf)

        def tail_copy(e):
            first = pl.multiple_of((pend_ref[e] - blk) * SUBLANES, blk * SUBLANES)
            return pltpu.make_async_copy(
                zbuf, slots_hbm.at[pl.ds(first, blk * SUBLANES), :], zsem)

        def nonempty(e):
            return pend_ref[e] > (pend_ref[e - 1] if e else 0)

        for e in range(N_EXPERTS):
            pl.when(nonempty(e))(lambda e=e: tail_copy(e).start())
        for e in range(N_EXPERTS):
            pl.when(nonempty(e))(lambda e=e: tail_copy(e).wait())

    def row_copy(t, kk):
        return pltpu.make_async_copy(
            xf_ref.at[_token_rows(t), :],
            slots_hbm.at[_token_rows(dest_ref[kk, t]), :], sem)

    def start(t, carry):
        for kk in range(TOP_K):
            row_copy(t, kk).start(priority=kk % 2)
        return carry

    lax.fori_loop(0, tm, start, 0, unroll=8)
    for kk in range(TOP_K):
        pltpu.make_async_copy(
            xf_ref, slots_hbm.at[pl.ds(0, tm * SUBLANES), :], sem).wait()


def _dispatch(padded_end, dest, xf, n_pad, tm, blk):
    t = xf.shape[0] // SUBLANES
    assert t % tm == 0
    return pl.pallas_call(
        functools.partial(_dispatch_kernel, tm=tm, blk=blk),
        grid_spec=pltpu.PrefetchScalarGridSpec(
            num_scalar_prefetch=1,
            grid=(t // tm,),
            in_specs=[pl.BlockSpec((TOP_K, tm), lambda i, pe: (0, i), memory_space=pltpu.SMEM),
                      pl.BlockSpec((tm * SUBLANES, LANES), lambda i, pe: (i, 0))],
            out_specs=pl.BlockSpec(memory_space=pl.ANY),
            scratch_shapes=[pltpu.VMEM((blk * SUBLANES, LANES), F32),
                            pltpu.SemaphoreType.DMA, pltpu.SemaphoreType.DMA],
        ),
        out_shape=jax.ShapeDtypeStruct((n_pad * SUBLANES, LANES), F32),
        compiler_params=pltpu.CompilerParams(
            dimension_semantics=("arbitrary",), vmem_limit_bytes=VMEM_LIMIT,
            has_side_effects=True),
        name="dispatch",
    )(padded_end, dest, xf)


def _experts_kernel(be_ref, nu_ref, x_ref, wg_ref, wu_ref, wd_ref, bg_ref, bu_ref, bd_ref, y_ref,
                    *, blk):
    del be_ref
    used = pl.program_id(0) < nu_ref[0]

    @pl.when(used)
    def _():
        xb = _load_token_tiles(x_ref, blk).astype(BF16)
        gate = jnp.minimum(_dot(xb, wg_ref[...]) + bg_ref[...], SWIGLU_LIMIT)
        up = jnp.clip(_dot(xb, wu_ref[...]) + bu_ref[...], -SWIGLU_LIMIT, SWIGLU_LIMIT)
        hidden = (up + 1.0) * gate * jax.nn.sigmoid(SWIGLU_ALPHA * gate)
        _store_token_tiles(y_ref, _dot(hidden.astype(BF16), wd_ref[...]) + bd_ref[...])

    @pl.when(jnp.logical_not(used))
    def _():
        y_ref[...] = jnp.zeros_like(y_ref)


def _experts(block_expert, n_used, slots, wg, wu, wd, bg, bu, bd, blk):
    n_pad = slots.shape[0] // SUBLANES
    n_blocks = n_pad // blk
    xmap = lambda j, be, nu: (jnp.minimum(j, nu[0] - 1), 0)
    wmap = lambda j, be, nu: (be[j], 0, 0)
    wspec = pl.BlockSpec((None, D_MODEL, D_FF), wmap)
    bspec = pl.BlockSpec((None, 1, D_FF), wmap)
    tiles = (blk * SUBLANES, LANES)
    return pl.pallas_call(
        functools.partial(_experts_kernel, blk=blk),
        grid_spec=pltpu.PrefetchScalarGridSpec(
            num_scalar_prefetch=2,
            grid=(n_blocks,),
            in_specs=[pl.BlockSpec(tiles, xmap), wspec, wspec, wspec, bspec, bspec, bspec],
            out_specs=pl.BlockSpec(tiles, lambda j, be, nu: (j, 0)),
        ),
        out_shape=jax.ShapeDtypeStruct((n_pad * SUBLANES, LANES), F32),
        compiler_params=pltpu.CompilerParams(
            dimension_semantics=("arbitrary",), vmem_limit_bytes=VMEM_LIMIT),
        name="experts",
    )(block_expert, n_used, slots, wg, wu, wd, bg, bu, bd)


def _combine_kernel(dest_ref, y_hbm, w_ref, x1_ref, o_ref, buf, sem, *, tm):
    def row_copy(t, kk):
        return pltpu.make_async_copy(
            y_hbm.at[_token_rows(dest_ref[kk, t]), :],
            buf.at[kk, _token_rows(t), :], sem)

    def start(t, carry):
        for kk in range(TOP_K):
            row_copy(t, kk).start(priority=kk % 2)
        return carry

    lax.fori_loop(0, tm, start, 0, unroll=8)
    for kk in range(TOP_K):
        pltpu.make_async_copy(y_hbm.at[pl.ds(0, tm * SUBLANES), :], buf.at[kk], sem).wait()
    w = w_ref[...]
    for s in range(ROW_TILES):
        cols = slice(s * LANES, (s + 1) * LANES)
        acc = x1_ref[:, cols]
        for kk in range(TOP_K):
            acc = acc + buf[kk, pl.ds(s, tm, stride=SUBLANES), :] * w[:, kk:kk + 1]
        o_ref[:, cols] = acc


def _combine(dest, y_slots, w_t, x1, tm):
    t = x1.shape[0]
    assert t % tm == 0
    row = pl.BlockSpec((tm, D_MODEL), lambda i: (i, 0))
    return pl.pallas_call(
        functools.partial(_combine_kernel, tm=tm),
        grid=(t // tm,),
        in_specs=[pl.BlockSpec((TOP_K, tm), lambda i: (0, i), memory_space=pltpu.SMEM),
                  pl.BlockSpec(memory_space=pl.ANY),
                  pl.BlockSpec((tm, TOP_K), lambda i: (i, 0)),
                  row],
        out_specs=row,
        out_shape=jax.ShapeDtypeStruct((t, D_MODEL), F32),
        scratch_shapes=[pltpu.VMEM((TOP_K, tm * SUBLANES, LANES), F32),
                        pltpu.SemaphoreType.DMA],
        compiler_params=pltpu.CompilerParams(
            dimension_semantics=("arbitrary",), vmem_limit_bytes=VMEM_LIMIT),
        name="combine",
    )(dest, y_slots, w_t, x1)


def _tiles(batch, seq):
    return dict(
        in_proj=min(512, seq),
        retention=min(1024, seq),
        sb_block=min(256, seq),
        out_proj=min(512, seq),
        dest=min(8192, batch * seq),
        dispatch=min(1024, seq),
        combine=min(256, seq),
        expert_block=512,
    )


def _layer(x, attn_norm_g, w_in, ret_norm_g, sb_q_norm_g, sb_k_norm_g, w_out,
           ffn_norm_g, w_router, b_router, w_gate_up, b_gate_up, w_down, b_down):
    batch, seq, d = x.shape
    t = batch * seq
    tiles = _tiles(batch, seq)
    x2 = x.reshape(t, d)

    rq, rk, rv, rg, sq, sk, sv = _in_proj(
        x2, attn_norm_g, w_in, sb_q_norm_g, sb_k_norm_g, seq, tiles["in_proj"])
    ret = _retention(rq, rk, rv, rg, ret_norm_g, batch, seq, tiles["retention"])
    sb = _sb_attention(sq, sk, sv, batch, seq, tiles["sb_block"])
    x1, xf, top_idx, top_w, rank, counts = _out_proj_route(
        ret, sb, x2, w_out, ffn_norm_g, w_router, b_router, tiles["out_proj"])

    blk = tiles["expert_block"]
    n_pad = t * TOP_K + N_EXPERTS * blk
    n_blocks = n_pad // blk
    counts = counts[:, 0]
    padded = (counts + blk - 1) // blk * blk
    padded_end = jnp.cumsum(padded)
    padded_start = padded_end - padded
    block_first = jnp.arange(n_blocks, dtype=jnp.int32) * blk
    block_expert = jnp.minimum(
        jnp.sum(padded_end[None, :] <= block_first[:, None], axis=1), N_EXPERTS - 1
    ).astype(jnp.int32)
    n_used = (padded_end[-1:] // blk).astype(jnp.int32)

    dest = _dest(padded_start, top_idx, rank, tiles["dest"])
    slots = _dispatch(padded_end, dest, xf, n_pad, tiles["dispatch"], blk)

    wgu = w_gate_up.reshape(N_EXPERTS, D_MODEL, D_FF, 2)
    wg = wgu[..., 0].astype(BF16)
    wu = wgu[..., 1].astype(BF16)
    bgu = b_gate_up.astype(F32).reshape(N_EXPERTS, 1, D_FF, 2)
    y_slots = _experts(block_expert, n_used, slots, wg, wu, w_down.astype(BF16),
                       bgu[..., 0], bgu[..., 1], b_down.astype(F32)[:, None, :], blk)

    out = _combine(dest, y_slots, top_w.T, x1, tiles["combine"])
    return out.reshape(batch, seq, d)


def kernel(x, attn_norm_g, w_in, ret_norm_g, sb_q_norm_g, sb_k_norm_g, w_out, ffn_norm_g,
           w_router, b_router, w_gate_up, b_gate_up, w_down, b_down):
    depth = attn_norm_g.shape[0]
    for l in range(depth):
        x = _layer(x, attn_norm_g[l], w_in[l], ret_norm_g[l], sb_q_norm_g[l], sb_k_norm_g[l],
                   w_out[l], ffn_norm_g[l], w_router[l], b_router[l], w_gate_up[l],
                   b_gate_up[l], w_down[l], b_down[l])
    return x
```

```python
import functools

import numpy as np
import jax
import jax.numpy as jnp
from jax import lax
from jax.experimental import pallas as pl
from jax.experimental.pallas import tpu as pltpu

D_MODEL = 1024
RET_WIDTH = 512
RET_HEADS = 4
RET_HEAD_DIM = 128
SB_WIDTH = 512
SB_HEADS = 8
SB_HEAD_DIM = 64
IN_PROJ_WIDTH = 4 * RET_WIDTH + 3 * SB_WIDTH
RET_CHUNK = 128
ROPE_BASE = 10000.0
N_EXPERTS = 32
TOP_K = 4
D_FF = D_MODEL
SWIGLU_LIMIT = 7.0
SWIGLU_ALPHA = 1.702
EPS = 1e-6

LANES = 128
F32_EXP_UNDERFLOW = -88.0
VMEM_LIMIT = 56 * 1024 * 1024

BF16 = jnp.bfloat16
F32 = jnp.float32


def _split_bf16(v):
    hi = v.astype(BF16)
    lo = (v - hi.astype(F32)).astype(BF16)
    return hi, lo


SUBLANES = 8
ROW_TILES = D_MODEL // LANES
assert ROW_TILES == SUBLANES


def _store_token_tiles(ref, value):
    n = value.shape[0]
    for s in range(ROW_TILES):
        ref[pl.ds(s, n, stride=SUBLANES), :] = value[:, s * LANES:(s + 1) * LANES]


def _load_token_tiles(ref, n):
    return jnp.concatenate(
        [ref[pl.ds(s, n, stride=SUBLANES), :] for s in range(ROW_TILES)], axis=1)


def _token_rows(i):
    return pl.ds(pl.multiple_of(i * SUBLANES, SUBLANES), SUBLANES)


def _dot(a, b):
    return jnp.dot(a, b, preferred_element_type=F32)


def _dot_nt(a, b):
    return lax.dot_general(a, b, (((1,), (1,)), ((), ())), preferred_element_type=F32)


def _in_proj_kernel(x_ref, g_ref, w_ref, cos_ref, sin_ref, qg_ref, kg_ref, seg_ref,
                    rq_ref, rk_ref, rv_ref, rg_ref, sq_ref, sk_ref, sv_ref):
    x = x_ref[...]
    h = x * lax.rsqrt(jnp.mean(x * x, axis=-1, keepdims=True) + EPS) * g_ref[...]
    p = _dot(h.astype(BF16), w_ref[...])
    cos2 = cos_ref[...]
    sin2 = sin_ref[...]
    k_scale = RET_HEAD_DIM ** -0.5
    for hd in range(RET_HEADS):
        lo = hd * RET_HEAD_DIM
        q = p[:, lo:lo + RET_HEAD_DIM]
        k = p[:, RET_WIDTH + lo:RET_WIDTH + lo + RET_HEAD_DIM]
        q = q * cos2 + pltpu.roll(q, RET_HEAD_DIM // 2, axis=1) * sin2
        k = k * cos2 + pltpu.roll(k, RET_HEAD_DIM // 2, axis=1) * sin2
        rq_ref[:, lo:lo + RET_HEAD_DIM] = q.astype(rq_ref.dtype)
        rk_ref[:, lo:lo + RET_HEAD_DIM] = (k * k_scale).astype(rk_ref.dtype)
    rv_ref[...] = p[:, 2 * RET_WIDTH:3 * RET_WIDTH].astype(rv_ref.dtype)
    rg_ref[...] = p[:, 3 * RET_WIDTH:4 * RET_WIDTH]
    base = 4 * RET_WIDTH
    seg = seg_ref[...]

    def head_norm(v, gain):
        hi, lo = _split_bf16(v * v)
        ms = (_dot(hi, seg) + _dot(lo, seg)) * (1.0 / SB_HEAD_DIM)
        return v * lax.rsqrt(ms + EPS) * gain

    sq = head_norm(p[:, base:base + SB_WIDTH], qg_ref[...])
    sk = head_norm(p[:, base + SB_WIDTH:base + 2 * SB_WIDTH], kg_ref[...])
    sq_ref[...] = (sq * (SB_HEAD_DIM ** -0.5)).astype(sq_ref.dtype)
    sk_ref[...] = sk.astype(sk_ref.dtype)
    sv_ref[...] = p[:, base + 2 * SB_WIDTH:base + 3 * SB_WIDTH].astype(sv_ref.dtype)


def _rope_tables(seq):
    half = RET_HEAD_DIM // 2
    inv = ROPE_BASE ** (-np.arange(half, dtype=np.float64) / half)
    n_a = seq // LANES
    ang_a = (np.arange(n_a, dtype=np.float64) * LANES)[:, None] * inv[None, :]
    ang_b = np.arange(LANES, dtype=np.float64)[:, None] * inv[None, :]
    ca, sa = jnp.asarray(np.cos(ang_a), F32), jnp.asarray(np.sin(ang_a), F32)
    cb, sb = jnp.asarray(np.cos(ang_b), F32), jnp.asarray(np.sin(ang_b), F32)
    cos = (ca[:, None, :] * cb[None] - sa[:, None, :] * sb[None]).reshape(seq, half)
    sin = (sa[:, None, :] * cb[None] + ca[:, None, :] * sb[None]).reshape(seq, half)
    return jnp.concatenate([cos, cos], axis=-1), jnp.concatenate([-sin, sin], axis=-1)


def _in_proj(x2, norm_g, w_in, sb_q_g, sb_k_g, seq, tm):
    t = x2.shape[0]
    assert seq % tm == 0 and t % tm == 0
    cos2, sin2 = _rope_tables(seq)
    seg = np.kron(np.eye(SB_HEADS), np.ones((SB_HEAD_DIM, SB_HEAD_DIM)))
    seg = jnp.asarray(seg, BF16)
    qg = jnp.tile(sb_q_g.astype(F32), SB_HEADS)[None, :]
    kg = jnp.tile(sb_k_g.astype(F32), SB_HEADS)[None, :]
    n_pos = seq // tm
    row = lambda w: pl.BlockSpec((tm, w), lambda i: (i, 0))
    full = lambda a: pl.BlockSpec(a.shape, lambda i: (0,) * a.ndim)
    pos = pl.BlockSpec((tm, RET_HEAD_DIM), lambda i: (i % n_pos, 0))
    w_bf = w_in.astype(BF16)
    g2 = norm_g.astype(F32)[None, :]
    out = lambda dt: jax.ShapeDtypeStruct((t, RET_WIDTH), dt)
    return pl.pallas_call(
        _in_proj_kernel,
        grid=(t // tm,),
        in_specs=[row(D_MODEL), full(g2), full(w_bf), pos, pos, full(qg), full(kg), full(seg)],
        out_specs=[row(RET_WIDTH)] * 7,
        out_shape=[out(BF16), out(F32), out(BF16), out(F32), out(BF16), out(BF16), out(BF16)],
        compiler_params=pltpu.CompilerParams(
            dimension_semantics=("arbitrary",), vmem_limit_bytes=VMEM_LIMIT),
        name="in_proj",
    )(x2, g2, w_bf, cos2, sin2, qg, kg, seg)


def _retention_kernel(q_ref, k_ref, v_ref, g_ref, ng_ref, intra_ref, qd_ref, kd_ref, cd_ref,
                      o_ref, state_ref, *, chunks):
    @pl.when(pl.program_id(1) == 0)
    def _():
        state_ref[...] = jnp.zeros_like(state_ref)

    def chunk(c, carry):
        rows = pl.ds(pl.multiple_of(c * RET_CHUNK, RET_CHUNK), RET_CHUNK)
        for hd in range(RET_HEADS):
            cols = slice(hd * RET_HEAD_DIM, (hd + 1) * RET_HEAD_DIM)
            q = q_ref[rows, cols]
            k = k_ref[rows, cols]
            v = v_ref[rows, cols]
            state = state_ref[hd]
            scores = _dot_nt(q, k.astype(BF16)) * intra_ref[hd]
            inner = _dot(scores.astype(BF16), v)
            cross = _dot(q, state.astype(BF16)) * qd_ref[:, cols]
            kd = (k * kd_ref[:, cols]).astype(BF16)
            state_ref[hd] = state * cd_ref[:, cols] + _dot(kd.T, v)
            o = inner + cross
            o = o * lax.rsqrt(jnp.mean(o * o, axis=-1, keepdims=True) + EPS) * ng_ref[:, cols]
            gate = g_ref[rows, cols]
            o_ref[rows, cols] = (o * (gate * jax.nn.sigmoid(gate))).astype(o_ref.dtype)
        return carry

    lax.fori_loop(0, chunks, chunk, 0)


def _retention(rq, rk, rv, rg, ret_norm_g, batch, seq, rows):
    assert seq % rows == 0 and rows % RET_CHUNK == 0
    log_gamma = np.log(1.0 - 2.0 ** (-5.0 - np.arange(RET_HEADS, dtype=np.float64)))
    idx = np.arange(RET_CHUNK, dtype=np.float64)
    diff = idx[:, None] - idx[None, :]
    intra = np.where(diff >= 0, np.exp(log_gamma[:, None, None] * np.maximum(diff, 0.0)), 0.0)
    q_decay = np.exp(log_gamma[:, None] * (idx + 1.0))
    k_decay = np.exp(log_gamma[:, None] * (RET_CHUNK - 1.0 - idx))
    chunk_decay = np.exp(log_gamma * RET_CHUNK)
    lane_rep = lambda a: np.repeat(a.T, RET_HEAD_DIM, axis=1)
    intra = jnp.asarray(intra, F32)
    qd = jnp.asarray(lane_rep(q_decay), F32)
    kd = jnp.asarray(lane_rep(k_decay), F32)
    cd = jnp.asarray(np.repeat(chunk_decay, RET_HEAD_DIM)[None, :], F32)
    ng = ret_norm_g.astype(F32).reshape(1, RET_WIDTH)
    n_r = seq // rows
    blk = pl.BlockSpec((rows, RET_WIDTH), lambda b, r: (b * n_r + r, 0))
    full = lambda a: pl.BlockSpec(a.shape, lambda b, r: (0,) * a.ndim)
    return pl.pallas_call(
        functools.partial(_retention_kernel, chunks=rows // RET_CHUNK),
        grid=(batch, n_r),
        in_specs=[blk, blk, blk, blk, full(ng), full(intra), full(qd), full(kd), full(cd)],
        out_specs=blk,
        out_shape=jax.ShapeDtypeStruct((batch * seq, RET_WIDTH), BF16),
        scratch_shapes=[pltpu.VMEM((RET_HEADS, RET_HEAD_DIM, RET_HEAD_DIM), F32)],
        compiler_params=pltpu.CompilerParams(
            dimension_semantics=("arbitrary", "arbitrary"), vmem_limit_bytes=VMEM_LIMIT),
        name="retention",
    )(rq, rk, rv, rg, ng, intra, qd, kd, cd)


def _sb_attn_kernel(q_ref, k_ref, v_ref, tri_ref, o_ref, *, blk, halo):
    i = pl.program_id(2)
    q = q_ref[...]
    lane = lax.broadcasted_iota(jnp.int32, (blk, LANES), 1)
    first_head = lane < SB_HEAD_DIM
    zero_q = jnp.zeros_like(q)
    q_heads = (jnp.where(first_head, q, zero_q), jnp.where(first_head, zero_q, q))

    def head(qm, kb, vb, c, acc, mask, tri):
        kw = kb.shape[0]
        z = _dot_nt(qm, kb)
        log_beta = jnp.minimum(z, 0.0) - jnp.log(1.0 + jnp.exp(-jnp.abs(z)))
        log_rest = log_beta - z
        if mask is not None:
            log_rest = jnp.where(mask, log_rest, 0.0)
        hi, lo = _split_bf16(log_rest)
        sums = _dot(hi, tri) + _dot(lo, tri)
        w = jnp.exp(log_beta + (c + sums[:, :kw]))
        if mask is not None:
            w = jnp.where(mask, w, 0.0)
        return c + sums[:, kw:kw + 1], acc + _dot(w.astype(BF16), vb)

    def both_heads(first_key, kw, c0, c1, acc0, acc1, mask, tri):
        rows = pl.ds(pl.multiple_of(first_key, halo), kw)
        kb = k_ref[rows, :]
        vb = v_ref[rows, :]
        c0, acc0 = head(q_heads[0], kb, vb, c0, acc0, mask, tri)
        c1, acc1 = head(q_heads[1], kb, vb, c1, acc1, mask, tri)
        return c0, c1, acc0, acc1

    window = blk + halo
    start = jnp.maximum(i * blk - halo, 0)
    kidx = lax.broadcasted_iota(jnp.int32, (blk, window), 1)
    qidx = lax.broadcasted_iota(jnp.int32, (blk, window), 0)
    causal = (kidx - qidx) < (i * blk - start)
    c_init = jnp.zeros((blk, 1), F32)
    acc_init = jnp.zeros((blk, LANES), F32)
    first = both_heads(start, window, c_init, c_init, acc_init, acc_init, causal, tri_ref[...])

    tri_chunk = jnp.concatenate(
        [tri_ref[:halo, :halo], tri_ref[:halo, window:]], axis=1)

    def cond(carry):
        j, c0, c1, _, _ = carry
        return jnp.logical_and(j >= 0, jnp.max(jnp.maximum(c0, c1)) > F32_EXP_UNDERFLOW)

    def body(carry):
        j, c0, c1, acc0, acc1 = carry
        return (j - 1,) + both_heads(j * halo, halo, c0, c1, acc0, acc1, None, tri_chunk)

    _, _, _, acc0, acc1 = lax.while_loop(cond, body, (start // halo - 1,) + first)
    o_ref[...] = jnp.where(first_head, acc0, acc1).astype(o_ref.dtype)


def _sb_attention(sq, sk, sv, batch, seq, blk, halo):
    assert seq % blk == 0 and blk % halo == 0 and halo == LANES and seq >= blk + halo
    nq = seq // blk
    pairs = SB_WIDTH // LANES
    window = blk + halo
    j = np.arange(window)
    tri = np.concatenate([(j[:, None] > j[None, :]).astype(np.float32),
                          np.ones((window, LANES), np.float32)], axis=1)
    tri = jnp.asarray(tri, BF16)
    qblk = pl.BlockSpec((blk, LANES), lambda b, p, i: (b * nq + i, p))
    kvblk = pl.BlockSpec((seq, LANES), lambda b, p, i: (b, p))
    return pl.pallas_call(
        functools.partial(_sb_attn_kernel, blk=blk, halo=halo),
        grid=(batch, pairs, nq),
        in_specs=[qblk, kvblk, kvblk, pl.BlockSpec(tri.shape, lambda b, p, i: (0, 0))],
        out_specs=qblk,
        out_shape=jax.ShapeDtypeStruct((batch * seq, SB_WIDTH), BF16),
        compiler_params=pltpu.CompilerParams(
            dimension_semantics=("arbitrary", "arbitrary", "arbitrary"),
            vmem_limit_bytes=VMEM_LIMIT),
        name="sb_attn",
    )(sq, sk, sv, tri)


def _out_proj_kernel(ret_ref, sb_ref, x_ref, wtop_ref, wbot_ref, g_ref, wrh_ref, wrl_ref,
                     br_ref, tri_ref,
                     x1_ref, xf_ref, idx_ref, w_ref, rank_ref, cnt_ref, run_ref):
    @pl.when(pl.program_id(0) == 0)
    def _():
        run_ref[...] = jnp.zeros_like(run_ref)

    x1 = x_ref[...] + _dot(ret_ref[...], wtop_ref[...]) + _dot(sb_ref[...], wbot_ref[...])
    x1_ref[...] = x1
    xf = x1 * lax.rsqrt(jnp.mean(x1 * x1, axis=-1, keepdims=True) + EPS) * g_ref[...]
    _store_token_tiles(xf_ref, xf)
    xh, xl = _split_bf16(xf)
    wrh = wrh_ref[...]
    logits = _dot_nt(wrh, xh) + _dot_nt(wrh, xl) + _dot_nt(wrl_ref[...], xh) + br_ref[...]
    n_e, tm = logits.shape
    e_iota = lax.broadcasted_iota(jnp.int32, (n_e, tm), 0)
    cur = logits
    tops, sels, hots = [], [], []
    for _ in range(TOP_K):
        m = jnp.max(cur, axis=0, keepdims=True)
        sel = jnp.min(jnp.where(cur == m, e_iota, n_e), axis=0, keepdims=True)
        hot = e_iota == sel
        cur = jnp.where(hot, -jnp.inf, cur)
        tops.append(m)
        sels.append(sel)
        hots.append(hot)
    ps = [jnp.exp(m - tops[0]) for m in tops]
    denom = ps[0] + ps[1] + ps[2] + ps[3]
    chosen = jnp.zeros((n_e, tm), F32)
    for hot in hots:
        chosen = chosen + hot.astype(F32)
    before = run_ref[:, 0:1] + _dot(chosen.astype(BF16), tri_ref[...])
    for kk in range(TOP_K):
        idx_ref[kk:kk + 1, :] = sels[kk]
        w_ref[kk:kk + 1, :] = ps[kk] / denom
        rank = jnp.sum(jnp.where(hots[kk], before, 0.0), axis=0, keepdims=True)
        rank_ref[kk:kk + 1, :] = rank.astype(jnp.int32)
    run = run_ref[...] + jnp.sum(chosen, axis=1, keepdims=True)
    run_ref[...] = run
    cnt_ref[...] = run.astype(jnp.int32)


def _out_proj_route(ret, sb, x2, w_out, ffn_g, w_router, b_router, tm):
    t = x2.shape[0]
    assert t % tm == 0
    w_bf = w_out.astype(BF16)
    wtop, wbot = w_bf[:RET_WIDTH], w_bf[RET_WIDTH:]
    g2 = ffn_g.astype(F32)[None, :]
    wr_t = w_router.astype(F32).T
    wrh = wr_t.astype(BF16)
    wrl = (wr_t - wrh.astype(F32)).astype(BF16)
    br = b_router.astype(F32)[:, None]
    tt = np.arange(tm)
    tri = jnp.asarray((tt[:, None] < tt[None, :]).astype(np.float32), BF16)
    row = lambda w: pl.BlockSpec((tm, w), lambda i: (i, 0))
    full = lambda a: pl.BlockSpec(a.shape, lambda i: (0,) * a.ndim)
    col = pl.BlockSpec((TOP_K, tm), lambda i: (0, i))
    return pl.pallas_call(
        _out_proj_kernel,
        grid=(t // tm,),
        in_specs=[row(RET_WIDTH), row(SB_WIDTH), row(D_MODEL), full(wtop), full(wbot), full(g2),
                  full(wrh), full(wrl), full(br), full(tri)],
        out_specs=[row(D_MODEL), pl.BlockSpec((tm * SUBLANES, LANES), lambda i: (i, 0)),
                   col, col, col, pl.BlockSpec((N_EXPERTS, LANES), lambda i: (0, 0))],
        out_shape=[jax.ShapeDtypeStruct((t, D_MODEL), F32),
                   jax.ShapeDtypeStruct((t * SUBLANES, LANES), F32),
                   jax.ShapeDtypeStruct((TOP_K, t), jnp.int32),
                   jax.ShapeDtypeStruct((TOP_K, t), F32),
                   jax.ShapeDtypeStruct((TOP_K, t), jnp.int32),
                   jax.ShapeDtypeStruct((N_EXPERTS, LANES), jnp.int32)],
        scratch_shapes=[pltpu.VMEM((N_EXPERTS, LANES), F32)],
        compiler_params=pltpu.CompilerParams(
            dimension_semantics=("arbitrary",), vmem_limit_bytes=VMEM_LIMIT),
        name="out_proj_route",
    )(ret, sb, x2, wtop, wbot, g2, wrh, wrl, br, tri)


def _dest_kernel(pstart_ref, idx_ref, rank_ref, dest_ref):
    idx = idx_ref[...]
    dest = rank_ref[...]
    for e in range(N_EXPERTS):
        dest = dest + jnp.where(idx == e, pstart_ref[e], 0)
    dest_ref[...] = dest


def _dest(padded_start, top_idx, rank, tn):
    t = top_idx.shape[1]
    assert t % tn == 0
    col = pl.BlockSpec((TOP_K, tn), lambda i, ps: (0, i))
    return pl.pallas_call(
        _dest_kernel,
        grid_spec=pltpu.PrefetchScalarGridSpec(
            num_scalar_prefetch=1, grid=(t // tn,), in_specs=[col, col], out_specs=col),
        out_shape=jax.ShapeDtypeStruct((TOP_K, t), jnp.int32),
        name="dest",
    )(padded_start, top_idx, rank)


def _dispatch_kernel(pend_ref, dest_ref, xf_ref, slots_hbm, zbuf, sem, zsem, *, tm, blk):
    @pl.when(pl.program_id(0) == 0)
    def _():
        zbuf[...] = jnp.zeros_like(zbuf)

        def tail_copy(e):
            first = pl.multiple_of((pend_ref[e] - blk) * SUBLANES, blk * SUBLANES)
            return pltpu.make_async_copy(
                zbuf, slots_hbm.at[pl.ds(first, blk * SUBLANES), :], zsem)

        def nonempty(e):
            return pend_ref[e] > (pend_ref[e - 1] if e else 0)

        for e in range(N_EXPERTS):
            pl.when(nonempty(e))(lambda e=e: tail_copy(e).start())
        for e in range(N_EXPERTS):
            pl.when(nonempty(e))(lambda e=e: tail_copy(e).wait())

    def row_copy(t, kk):
        return pltpu.make_async_copy(
            xf_ref.at[_token_rows(t), :],
            slots_hbm.at[_token_rows(dest_ref[kk, t]), :], sem)

    def start(t, carry):
        for kk in range(TOP_K):
            row_copy(t, kk).start(priority=kk % 2)
        return carry

    lax.fori_loop(0, tm, start, 0, unroll=8)
    for kk in range(TOP_K):
        pltpu.make_async_copy(
            xf_ref, slots_hbm.at[pl.ds(0, tm * SUBLANES), :], sem).wait()


def _dispatch(padded_end, dest, xf, n_pad, tm, blk):
    t = xf.shape[0] // SUBLANES
    assert t % tm == 0
    return pl.pallas_call(
        functools.partial(_dispatch_kernel, tm=tm, blk=blk),
        grid_spec=pltpu.PrefetchScalarGridSpec(
            num_scalar_prefetch=1,
            grid=(t // tm,),
            in_specs=[pl.BlockSpec((TOP_K, tm), lambda i, pe: (0, i), memory_space=pltpu.SMEM),
                      pl.BlockSpec((tm * SUBLANES, LANES), lambda i, pe: (i, 0))],
            out_specs=pl.BlockSpec(memory_space=pl.ANY),
            scratch_shapes=[pltpu.VMEM((blk * SUBLANES, LANES), F32),
                            pltpu.SemaphoreType.DMA, pltpu.SemaphoreType.DMA],
        ),
        out_shape=jax.ShapeDtypeStruct((n_pad * SUBLANES, LANES), F32),
        compiler_params=pltpu.CompilerParams(
            dimension_semantics=("arbitrary",), vmem_limit_bytes=VMEM_LIMIT,
            has_side_effects=True),
        name="dispatch",
    )(padded_end, dest, xf)


WEIGHT_ROWS = 128


def _experts_kernel(be_ref, nu_ref, x_ref, wgu_ref, wd_ref, bg_ref, bu_ref, bd_ref, y_ref,
                    wg_s, wu_s, wd_s, *, blk):
    j = pl.program_id(0)
    used = j < nu_ref[0]
    new_expert = jnp.logical_or(j == 0, be_ref[j] != be_ref[jnp.maximum(j - 1, 0)])

    @pl.when(jnp.logical_and(used, new_expert))
    def _():
        def rows_step(r, carry):
            rows = pl.ds(pl.multiple_of(r * WEIGHT_ROWS, WEIGHT_ROWS), WEIGHT_ROWS)
            packed = wgu_ref[rows, :]
            gate_bits = lax.shift_left(packed, jnp.uint32(16))
            up_bits = lax.bitwise_and(packed, jnp.uint32(0xFFFF0000))
            wg_s[rows, :] = lax.bitcast_convert_type(gate_bits, F32).astype(BF16)
            wu_s[rows, :] = lax.bitcast_convert_type(up_bits, F32).astype(BF16)
            wd_s[rows, :] = wd_ref[rows, :].astype(BF16)
            return carry

        lax.fori_loop(0, D_MODEL // WEIGHT_ROWS, rows_step, 0)

    @pl.when(used)
    def _():
        xb = _load_token_tiles(x_ref, blk).astype(BF16)
        gate = jnp.minimum(_dot(xb, wg_s[...]) + bg_ref[...], SWIGLU_LIMIT)
        up = jnp.clip(_dot(xb, wu_s[...]) + bu_ref[...], -SWIGLU_LIMIT, SWIGLU_LIMIT)
        hidden = (up + 1.0) * gate * jax.nn.sigmoid(SWIGLU_ALPHA * gate)
        _store_token_tiles(y_ref, _dot(hidden.astype(BF16), wd_s[...]) + bd_ref[...])

    @pl.when(jnp.logical_not(used))
    def _():
        y_ref[...] = jnp.zeros_like(y_ref)


def _experts(block_expert, n_used, slots, wgu, wd, bg, bu, bd, blk):
    assert D_FF == D_MODEL
    n_pad = slots.shape[0] // SUBLANES
    n_blocks = n_pad // blk
    xmap = lambda j, be, nu: (jnp.minimum(j, nu[0] - 1), 0)
    wmap = lambda j, be, nu: (be[j], 0, 0)
    wspec = pl.BlockSpec((None, D_MODEL, D_FF), wmap)
    bspec = pl.BlockSpec((None, 1, D_FF), wmap)
    tiles = (blk * SUBLANES, LANES)
    return pl.pallas_call(
        functools.partial(_experts_kernel, blk=blk),
        grid_spec=pltpu.PrefetchScalarGridSpec(
            num_scalar_prefetch=2,
            grid=(n_blocks,),
            in_specs=[pl.BlockSpec(tiles, xmap), wspec, wspec, bspec, bspec, bspec],
            out_specs=pl.BlockSpec(tiles, lambda j, be, nu: (j, 0)),
            scratch_shapes=[pltpu.VMEM((D_MODEL, D_FF), BF16), pltpu.VMEM((D_MODEL, D_FF), BF16),
                            pltpu.VMEM((D_FF, D_MODEL), BF16)],
        ),
        out_shape=jax.ShapeDtypeStruct((n_pad * SUBLANES, LANES), F32),
        compiler_params=pltpu.CompilerParams(
            dimension_semantics=("arbitrary",), vmem_limit_bytes=VMEM_LIMIT),
        name="experts",
    )(block_expert, n_used, slots, wgu, wd, bg, bu, bd)


def _combine_kernel(dest_ref, y_hbm, w_ref, x1_ref, o_ref, buf, sem, *, tm):
    def row_copy(t, kk):
        return pltpu.make_async_copy(
            y_hbm.at[_token_rows(dest_ref[kk, t]), :],
            buf.at[kk, _token_rows(t), :], sem)

    def start(t, carry):
        for kk in range(TOP_K):
            row_copy(t, kk).start(priority=kk % 2)
        return carry

    lax.fori_loop(0, tm, start, 0, unroll=8)
    for kk in range(TOP_K):
        pltpu.make_async_copy(y_hbm.at[pl.ds(0, tm * SUBLANES), :], buf.at[kk], sem).wait()
    w = w_ref[...]
    for s in range(ROW_TILES):
        cols = slice(s * LANES, (s + 1) * LANES)
        acc = x1_ref[:, cols]
        for kk in range(TOP_K):
            acc = acc + buf[kk, pl.ds(s, tm, stride=SUBLANES), :] * w[:, kk:kk + 1]
        o_ref[:, cols] = acc


def _combine(dest, y_slots, w_t, x1, tm):
    t = x1.shape[0]
    assert t % tm == 0
    row = pl.BlockSpec((tm, D_MODEL), lambda i: (i, 0))
    return pl.pallas_call(
        functools.partial(_combine_kernel, tm=tm),
        grid=(t // tm,),
        in_specs=[pl.BlockSpec((TOP_K, tm), lambda i: (0, i), memory_space=pltpu.SMEM),
                  pl.BlockSpec(memory_space=pl.ANY),
                  pl.BlockSpec((tm, TOP_K), lambda i: (i, 0)),
                  row],
        out_specs=row,
        out_shape=jax.ShapeDtypeStruct((t, D_MODEL), F32),
        scratch_shapes=[pltpu.VMEM((TOP_K, tm * SUBLANES, LANES), F32),
                        pltpu.SemaphoreType.DMA],
        compiler_params=pltpu.CompilerParams(
            dimension_semantics=("arbitrary",), vmem_limit_bytes=VMEM_LIMIT),
        name="combine",
    )(dest, y_slots, w_t, x1)


def _tiles(batch, seq):
    return dict(
        in_proj=min(512, seq),
        retention=min(1024, seq),
        sb_block=min(256, seq),
        out_proj=min(512, seq),
        dest=min(8192, batch * seq),
        dispatch=min(1024, seq),
        combine=min(256, seq),
        expert_block=512,
    )


def _layer(x, attn_norm_g, w_in, ret_norm_g, sb_q_norm_g, sb_k_norm_g, w_out,
           ffn_norm_g, w_router, b_router, w_gate_up, b_gate_up, w_down, b_down):
    batch, seq, d = x.shape
    t = batch * seq
    tiles = _tiles(batch, seq)
    x2 = x.reshape(t, d)

    rq, rk, rv, rg, sq, sk, sv = _in_proj(
        x2, attn_norm_g, w_in, sb_q_norm_g, sb_k_norm_g, seq, tiles["in_proj"])
    ret = _retention(rq, rk, rv, rg, ret_norm_g, batch, seq, tiles["retention"])
    sb = _sb_attention(sq, sk, sv, batch, seq, tiles["sb_block"], LANES)
    x1, xf, top_idx, top_w, rank, counts = _out_proj_route(
        ret, sb, x2, w_out, ffn_norm_g, w_router, b_router, tiles["out_proj"])

    blk = tiles["expert_block"]
    n_pad = t * TOP_K + N_EXPERTS * blk
    n_blocks = n_pad // blk
    counts = counts[:, 0]
    padded = (counts + blk - 1) // blk * blk
    padded_end = jnp.cumsum(padded)
    padded_start = padded_end - padded
    block_first = jnp.arange(n_blocks, dtype=jnp.int32) * blk
    block_expert = jnp.minimum(
        jnp.sum(padded_end[None, :] <= block_first[:, None], axis=1), N_EXPERTS - 1
    ).astype(jnp.int32)
    n_used = (padded_end[-1:] // blk).astype(jnp.int32)

    dest = _dest(padded_start, top_idx, rank, tiles["dest"])
    slots = _dispatch(padded_end, dest, xf, n_pad, tiles["dispatch"], blk)

    wgu = lax.bitcast_convert_type(
        w_gate_up.astype(BF16).reshape(N_EXPERTS, D_MODEL, D_FF, 2), jnp.uint32)
    bgu = b_gate_up.astype(F32).reshape(N_EXPERTS, 1, D_FF, 2)
    y_slots = _experts(block_expert, n_used, slots, wgu, w_down.astype(F32),
                       bgu[..., 0], bgu[..., 1], b_down.astype(F32)[:, None, :], blk)

    out = _combine(dest, y_slots, top_w.T, x1, tiles["combine"])
    return out.reshape(batch, seq, d)


def kernel(x, attn_norm_g, w_in, ret_norm_g, sb_q_norm_g, sb_k_norm_g, w_out, ffn_norm_g,
           w_router, b_router, w_gate_up, b_gate_up, w_down, b_down):
    depth = attn_norm_g.shape[0]
    for l in range(depth):
        x = _layer(x, attn_norm_g[l], w_in[l], ret_norm_g[l], sb_q_norm_g[l], sb_k_norm_g[l],
                   w_out[l], ffn_norm_g[l], w_router[l], b_router[l], w_gate_up[l],
                   b_gate_up[l], w_down[l], b_down[l])
    return x
```

```python
import functools

import numpy as np
import jax
import jax.numpy as jnp
from jax import lax
from jax.experimental import pallas as pl
from jax.experimental.pallas import tpu as pltpu

D_MODEL = 1024
RET_WIDTH = 512
RET_HEADS = 4
RET_HEAD_DIM = 128
SB_WIDTH = 512
SB_HEADS = 8
SB_HEAD_DIM = 64
IN_PROJ_WIDTH = 4 * RET_WIDTH + 3 * SB_WIDTH
RET_CHUNK = 128
ROPE_BASE = 10000.0
N_EXPERTS = 32
TOP_K = 4
D_FF = D_MODEL
SWIGLU_LIMIT = 7.0
SWIGLU_ALPHA = 1.702
EPS = 1e-6

LANES = 128
F32_EXP_UNDERFLOW = -88.0
VMEM_LIMIT = 56 * 1024 * 1024

BF16 = jnp.bfloat16
F32 = jnp.float32


def _split_bf16(v):
    hi = v.astype(BF16)
    lo = (v - hi.astype(F32)).astype(BF16)
    return hi, lo


SUBLANES = 8
ROW_TILES = D_MODEL // LANES
assert ROW_TILES == SUBLANES


def _store_token_tiles(ref, value):
    n = value.shape[0]
    for s in range(ROW_TILES):
        ref[pl.ds(s, n, stride=SUBLANES), :] = value[:, s * LANES:(s + 1) * LANES]


def _load_token_tiles(ref, n):
    return jnp.concatenate(
        [ref[pl.ds(s, n, stride=SUBLANES), :] for s in range(ROW_TILES)], axis=1)


def _token_rows(i):
    return pl.ds(pl.multiple_of(i * SUBLANES, SUBLANES), SUBLANES)


def _dot(a, b):
    return jnp.dot(a, b, preferred_element_type=F32)


def _dot_nt(a, b):
    return lax.dot_general(a, b, (((1,), (1,)), ((), ())), preferred_element_type=F32)


def _in_proj_kernel(x_ref, g_ref, w_ref, cos_ref, sin_ref, qg_ref, kg_ref, seg_ref,
                    rq_ref, rk_ref, rv_ref, rg_ref, sq_ref, sk_ref, sv_ref):
    x = x_ref[...]
    h = x * lax.rsqrt(jnp.mean(x * x, axis=-1, keepdims=True) + EPS) * g_ref[...]
    p = _dot(h.astype(BF16), w_ref[...])
    cos2 = cos_ref[...]
    sin2 = sin_ref[...]
    k_scale = RET_HEAD_DIM ** -0.5
    for hd in range(RET_HEADS):
        lo = hd * RET_HEAD_DIM
        q = p[:, lo:lo + RET_HEAD_DIM]
        k = p[:, RET_WIDTH + lo:RET_WIDTH + lo + RET_HEAD_DIM]
        q = q * cos2 + pltpu.roll(q, RET_HEAD_DIM // 2, axis=1) * sin2
        k = k * cos2 + pltpu.roll(k, RET_HEAD_DIM // 2, axis=1) * sin2
        rq_ref[:, lo:lo + RET_HEAD_DIM] = q.astype(rq_ref.dtype)
        rk_ref[:, lo:lo + RET_HEAD_DIM] = (k * k_scale).astype(rk_ref.dtype)
    rv_ref[...] = p[:, 2 * RET_WIDTH:3 * RET_WIDTH].astype(rv_ref.dtype)
    rg_ref[...] = p[:, 3 * RET_WIDTH:4 * RET_WIDTH]
    base = 4 * RET_WIDTH
    seg = seg_ref[...]

    def head_norm(v, gain):
        hi, lo = _split_bf16(v * v)
        ms = (_dot(hi, seg) + _dot(lo, seg)) * (1.0 / SB_HEAD_DIM)
        return v * lax.rsqrt(ms + EPS) * gain

    sq = head_norm(p[:, base:base + SB_WIDTH], qg_ref[...])
    sk = head_norm(p[:, base + SB_WIDTH:base + 2 * SB_WIDTH], kg_ref[...])
    sq_ref[...] = (sq * (SB_HEAD_DIM ** -0.5)).astype(sq_ref.dtype)
    sk_ref[...] = sk.astype(sk_ref.dtype)
    sv_ref[...] = p[:, base + 2 * SB_WIDTH:base + 3 * SB_WIDTH].astype(sv_ref.dtype)


def _rope_tables(seq):
    half = RET_HEAD_DIM // 2
    inv = ROPE_BASE ** (-np.arange(half, dtype=np.float64) / half)
    n_a = seq // LANES
    ang_a = (np.arange(n_a, dtype=np.float64) * LANES)[:, None] * inv[None, :]
    ang_b = np.arange(LANES, dtype=np.float64)[:, None] * inv[None, :]
    ca, sa = jnp.asarray(np.cos(ang_a), F32), jnp.asarray(np.sin(ang_a), F32)
    cb, sb = jnp.asarray(np.cos(ang_b), F32), jnp.asarray(np.sin(ang_b), F32)
    cos = (ca[:, None, :] * cb[None] - sa[:, None, :] * sb[None]).reshape(seq, half)
    sin = (sa[:, None, :] * cb[None] + ca[:, None, :] * sb[None]).reshape(seq, half)
    return jnp.concatenate([cos, cos], axis=-1), jnp.concatenate([-sin, sin], axis=-1)


def _in_proj(x2, norm_g, w_in, sb_q_g, sb_k_g, seq, tm):
    t = x2.shape[0]
    assert seq % tm == 0 and t % tm == 0
    cos2, sin2 = _rope_tables(seq)
    seg = np.kron(np.eye(SB_HEADS), np.ones((SB_HEAD_DIM, SB_HEAD_DIM)))
    seg = jnp.asarray(seg, BF16)
    qg = jnp.tile(sb_q_g.astype(F32), SB_HEADS)[None, :]
    kg = jnp.tile(sb_k_g.astype(F32), SB_HEADS)[None, :]
    n_pos = seq // tm
    row = lambda w: pl.BlockSpec((tm, w), lambda i: (i, 0))
    full = lambda a: pl.BlockSpec(a.shape, lambda i: (0,) * a.ndim)
    pos = pl.BlockSpec((tm, RET_HEAD_DIM), lambda i: (i % n_pos, 0))
    w_bf = w_in.astype(BF16)
    g2 = norm_g.astype(F32)[None, :]
    out = lambda dt: jax.ShapeDtypeStruct((t, RET_WIDTH), dt)
    return pl.pallas_call(
        _in_proj_kernel,
        grid=(t // tm,),
        in_specs=[row(D_MODEL), full(g2), full(w_bf), pos, pos, full(qg), full(kg), full(seg)],
        out_specs=[row(RET_WIDTH)] * 7,
        out_shape=[out(BF16), out(F32), out(BF16), out(F32), out(BF16), out(BF16), out(BF16)],
        compiler_params=pltpu.CompilerParams(
            dimension_semantics=("arbitrary",), vmem_limit_bytes=VMEM_LIMIT),
        name="in_proj",
    )(x2, g2, w_bf, cos2, sin2, qg, kg, seg)


def _retention_kernel(q_ref, k_ref, v_ref, g_ref, ng_ref, intra_ref, qd_ref, kd_ref, cd_ref,
                      o_ref, state_ref, *, chunks):
    @pl.when(pl.program_id(1) == 0)
    def _():
        state_ref[...] = jnp.zeros_like(state_ref)

    def chunk(c, carry):
        rows = pl.ds(pl.multiple_of(c * RET_CHUNK, RET_CHUNK), RET_CHUNK)
        for hd in range(RET_HEADS):
            cols = slice(hd * RET_HEAD_DIM, (hd + 1) * RET_HEAD_DIM)
            q = q_ref[rows, cols]
            k = k_ref[rows, cols]
            v = v_ref[rows, cols]
            state = state_ref[hd]
            scores = _dot_nt(q, k.astype(BF16)) * intra_ref[hd]
            inner = _dot(scores.astype(BF16), v)
            cross = _dot(q, state.astype(BF16)) * qd_ref[:, cols]
            kd = (k * kd_ref[:, cols]).astype(BF16)
            state_ref[hd] = state * cd_ref[:, cols] + _dot(kd.T, v)
            o = inner + cross
            o = o * lax.rsqrt(jnp.mean(o * o, axis=-1, keepdims=True) + EPS) * ng_ref[:, cols]
            gate = g_ref[rows, cols]
            o_ref[rows, cols] = (o * (gate * jax.nn.sigmoid(gate))).astype(o_ref.dtype)
        return carry

    lax.fori_loop(0, chunks, chunk, 0)


def _retention(rq, rk, rv, rg, ret_norm_g, batch, seq, rows):
    assert seq % rows == 0 and rows % RET_CHUNK == 0
    log_gamma = np.log(1.0 - 2.0 ** (-5.0 - np.arange(RET_HEADS, dtype=np.float64)))
    idx = np.arange(RET_CHUNK, dtype=np.float64)
    diff = idx[:, None] - idx[None, :]
    intra = np.where(diff >= 0, np.exp(log_gamma[:, None, None] * np.maximum(diff, 0.0)), 0.0)
    q_decay = np.exp(log_gamma[:, None] * (idx + 1.0))
    k_decay = np.exp(log_gamma[:, None] * (RET_CHUNK - 1.0 - idx))
    chunk_decay = np.exp(log_gamma * RET_CHUNK)
    lane_rep = lambda a: np.repeat(a.T, RET_HEAD_DIM, axis=1)
    intra = jnp.asarray(intra, F32)
    qd = jnp.asarray(lane_rep(q_decay), F32)
    kd = jnp.asarray(lane_rep(k_decay), F32)
    cd = jnp.asarray(np.repeat(chunk_decay, RET_HEAD_DIM)[None, :], F32)
    ng = ret_norm_g.astype(F32).reshape(1, RET_WIDTH)
    n_r = seq // rows
    blk = pl.BlockSpec((rows, RET_WIDTH), lambda b, r: (b * n_r + r, 0))
    full = lambda a: pl.BlockSpec(a.shape, lambda b, r: (0,) * a.ndim)
    return pl.pallas_call(
        functools.partial(_retention_kernel, chunks=rows // RET_CHUNK),
        grid=(batch, n_r),
        in_specs=[blk, blk, blk, blk, full(ng), full(intra), full(qd), full(kd), full(cd)],
        out_specs=blk,
        out_shape=jax.ShapeDtypeStruct((batch * seq, RET_WIDTH), BF16),
        scratch_shapes=[pltpu.VMEM((RET_HEADS, RET_HEAD_DIM, RET_HEAD_DIM), F32)],
        compiler_params=pltpu.CompilerParams(
            dimension_semantics=("arbitrary", "arbitrary"), vmem_limit_bytes=VMEM_LIMIT),
        name="retention",
    )(rq, rk, rv, rg, ng, intra, qd, kd, cd)


def _sb_attn_kernel(q_ref, k_ref, v_ref, tri_ref, o_ref, *, blk, halo):
    i = pl.program_id(2)
    q = q_ref[...]
    lane = lax.broadcasted_iota(jnp.int32, (blk, LANES), 1)
    first_head = lane < SB_HEAD_DIM
    zero_q = jnp.zeros_like(q)
    q_heads = (jnp.where(first_head, q, zero_q), jnp.where(first_head, zero_q, q))

    def head(qm, kb, vb, c, acc, mask, tri):
        kw = kb.shape[0]
        z = _dot_nt(qm, kb)
        log_beta = jnp.minimum(z, 0.0) - jnp.log(1.0 + jnp.exp(-jnp.abs(z)))
        log_rest = log_beta - z
        if mask is not None:
            log_rest = jnp.where(mask, log_rest, 0.0)
        hi, lo = _split_bf16(log_rest)
        sums = _dot(hi, tri) + _dot(lo, tri)
        w = jnp.exp(log_beta + (c + sums[:, :kw]))
        if mask is not None:
            w = jnp.where(mask, w, 0.0)
        return c + sums[:, kw:kw + 1], acc + _dot(w.astype(BF16), vb)

    def both_heads(first_key, kw, c0, c1, acc0, acc1, mask, tri):
        rows = pl.ds(pl.multiple_of(first_key, halo), kw)
        kb = k_ref[rows, :]
        vb = v_ref[rows, :]
        c0, acc0 = head(q_heads[0], kb, vb, c0, acc0, mask, tri)
        c1, acc1 = head(q_heads[1], kb, vb, c1, acc1, mask, tri)
        return c0, c1, acc0, acc1

    window = blk + halo
    start = jnp.maximum(i * blk - halo, 0)
    kidx = lax.broadcasted_iota(jnp.int32, (blk, window), 1)
    qidx = lax.broadcasted_iota(jnp.int32, (blk, window), 0)
    causal = (kidx - qidx) < (i * blk - start)
    c_init = jnp.zeros((blk, 1), F32)
    acc_init = jnp.zeros((blk, LANES), F32)
    first = both_heads(start, window, c_init, c_init, acc_init, acc_init, causal, tri_ref[...])

    tri_chunk = jnp.concatenate(
        [tri_ref[:halo, :halo], tri_ref[:halo, window:]], axis=1)

    def cond(carry):
        j, c0, c1, _, _ = carry
        return jnp.logical_and(j >= 0, jnp.max(jnp.maximum(c0, c1)) > F32_EXP_UNDERFLOW)

    def body(carry):
        j, c0, c1, acc0, acc1 = carry
        return (j - 1,) + both_heads(j * halo, halo, c0, c1, acc0, acc1, None, tri_chunk)

    _, _, _, acc0, acc1 = lax.while_loop(cond, body, (start // halo - 1,) + first)
    o_ref[...] = jnp.where(first_head, acc0, acc1).astype(o_ref.dtype)


def _sb_attention(sq, sk, sv, batch, seq, blk, halo):
    assert seq % blk == 0 and blk % halo == 0 and halo == LANES and seq >= blk + halo
    nq = seq // blk
    pairs = SB_WIDTH // LANES
    window = blk + halo
    j = np.arange(window)
    tri = np.concatenate([(j[:, None] > j[None, :]).astype(np.float32),
                          np.ones((window, LANES), np.float32)], axis=1)
    tri = jnp.asarray(tri, BF16)
    qblk = pl.BlockSpec((blk, LANES), lambda b, p, i: (b * nq + i, p))
    kvblk = pl.BlockSpec((seq, LANES), lambda b, p, i: (b, p))
    return pl.pallas_call(
        functools.partial(_sb_attn_kernel, blk=blk, halo=halo),
        grid=(batch, pairs, nq),
        in_specs=[qblk, kvblk, kvblk, pl.BlockSpec(tri.shape, lambda b, p, i: (0, 0))],
        out_specs=qblk,
        out_shape=jax.ShapeDtypeStruct((batch * seq, SB_WIDTH), BF16),
        compiler_params=pltpu.CompilerParams(
            dimension_semantics=("arbitrary", "arbitrary", "arbitrary"),
            vmem_limit_bytes=VMEM_LIMIT),
        name="sb_attn",
    )(sq, sk, sv, tri)


def _out_proj_kernel(ret_ref, sb_ref, x_ref, wtop_ref, wbot_ref, g_ref, wrh_ref, wrl_ref,
                     br_ref, tri_ref,
                     x1_ref, xf_ref, idx_ref, w_ref, rank_ref, cnt_ref, run_ref):
    @pl.when(pl.program_id(0) == 0)
    def _():
        run_ref[...] = jnp.zeros_like(run_ref)

    x1 = x_ref[...] + _dot(ret_ref[...], wtop_ref[...]) + _dot(sb_ref[...], wbot_ref[...])
    x1_ref[...] = x1
    xf = x1 * lax.rsqrt(jnp.mean(x1 * x1, axis=-1, keepdims=True) + EPS) * g_ref[...]
    _store_token_tiles(xf_ref, xf)
    xh, xl = _split_bf16(xf)
    wrh = wrh_ref[...]
    logits = _dot_nt(wrh, xh) + _dot_nt(wrh, xl) + _dot_nt(wrl_ref[...], xh) + br_ref[...]
    n_e, tm = logits.shape
    e_iota = lax.broadcasted_iota(jnp.int32, (n_e, tm), 0)
    cur = logits
    tops, sels, hots = [], [], []
    for _ in range(TOP_K):
        m = jnp.max(cur, axis=0, keepdims=True)
        sel = jnp.min(jnp.where(cur == m, e_iota, n_e), axis=0, keepdims=True)
        hot = e_iota == sel
        cur = jnp.where(hot, -jnp.inf, cur)
        tops.append(m)
        sels.append(sel)
        hots.append(hot)
    ps = [jnp.exp(m - tops[0]) for m in tops]
    denom = ps[0] + ps[1] + ps[2] + ps[3]
    chosen = jnp.zeros((n_e, tm), F32)
    for hot in hots:
        chosen = chosen + hot.astype(F32)
    before = run_ref[:, 0:1] + _dot(chosen.astype(BF16), tri_ref[...])
    for kk in range(TOP_K):
        idx_ref[kk:kk + 1, :] = sels[kk]
        w_ref[kk:kk + 1, :] = ps[kk] / denom
        rank = jnp.sum(jnp.where(hots[kk], before, 0.0), axis=0, keepdims=True)
        rank_ref[kk:kk + 1, :] = rank.astype(jnp.int32)
    run = run_ref[...] + jnp.sum(chosen, axis=1, keepdims=True)
    run_ref[...] = run
    cnt_ref[...] = run.astype(jnp.int32)


def _out_proj_route(ret, sb, x2, w_out, ffn_g, w_router, b_router, tm):
    t = x2.shape[0]
    assert t % tm == 0
    w_bf = w_out.astype(BF16)
    wtop, wbot = w_bf[:RET_WIDTH], w_bf[RET_WIDTH:]
    g2 = ffn_g.astype(F32)[None, :]
    wr_t = w_router.astype(F32).T
    wrh = wr_t.astype(BF16)
    wrl = (wr_t - wrh.astype(F32)).astype(BF16)
    br = b_router.astype(F32)[:, None]
    tt = np.arange(tm)
    tri = jnp.asarray((tt[:, None] < tt[None, :]).astype(np.float32), BF16)
    row = lambda w: pl.BlockSpec((tm, w), lambda i: (i, 0))
    full = lambda a: pl.BlockSpec(a.shape, lambda i: (0,) * a.ndim)
    col = pl.BlockSpec((TOP_K, tm), lambda i: (0, i))
    return pl.pallas_call(
        _out_proj_kernel,
        grid=(t // tm,),
        in_specs=[row(RET_WIDTH), row(SB_WIDTH), row(D_MODEL), full(wtop), full(wbot), full(g2),
                  full(wrh), full(wrl), full(br), full(tri)],
        out_specs=[row(D_MODEL), pl.BlockSpec((tm * SUBLANES, LANES), lambda i: (i, 0)),
                   col, col, col, pl.BlockSpec((N_EXPERTS, LANES), lambda i: (0, 0))],
        out_shape=[jax.ShapeDtypeStruct((t, D_MODEL), F32),
                   jax.ShapeDtypeStruct((t * SUBLANES, LANES), F32),
                   jax.ShapeDtypeStruct((TOP_K, t), jnp.int32),
                   jax.ShapeDtypeStruct((TOP_K, t), F32),
                   jax.ShapeDtypeStruct((TOP_K, t), jnp.int32),
                   jax.ShapeDtypeStruct((N_EXPERTS, LANES), jnp.int32)],
        scratch_shapes=[pltpu.VMEM((N_EXPERTS, LANES), F32)],
        compiler_params=pltpu.CompilerParams(
            dimension_semantics=("arbitrary",), vmem_limit_bytes=VMEM_LIMIT),
        name="out_proj_route",
    )(ret, sb, x2, wtop, wbot, g2, wrh, wrl, br, tri)


def _dest_kernel(pstart_ref, idx_ref, rank_ref, dest_ref):
    idx = idx_ref[...]
    dest = rank_ref[...]
    for e in range(N_EXPERTS):
        dest = dest + jnp.where(idx == e, pstart_ref[e], 0)
    dest_ref[...] = dest


def _dest(padded_start, top_idx, rank, tn):
    t = top_idx.shape[1]
    assert t % tn == 0
    col = pl.BlockSpec((TOP_K, tn), lambda i, ps: (0, i))
    return pl.pallas_call(
        _dest_kernel,
        grid_spec=pltpu.PrefetchScalarGridSpec(
            num_scalar_prefetch=1, grid=(t // tn,), in_specs=[col, col], out_specs=col),
        out_shape=jax.ShapeDtypeStruct((TOP_K, t), jnp.int32),
        name="dest",
    )(padded_start, top_idx, rank)


def _dispatch_kernel(pend_ref, dest_ref, xf_ref, slots_hbm, zbuf, sem, zsem, *, tm, blk):
    @pl.when(pl.program_id(0) == 0)
    def _():
        zbuf[...] = jnp.zeros_like(zbuf)

        def tail_copy(e):
            first = pl.multiple_of((pend_ref[e] - blk) * SUBLANES, blk * SUBLANES)
            return pltpu.make_async_copy(
                zbuf, slots_hbm.at[pl.ds(first, blk * SUBLANES), :], zsem)

        def nonempty(e):
            return pend_ref[e] > (pend_ref[e - 1] if e else 0)

        for e in range(N_EXPERTS):
            pl.when(nonempty(e))(lambda e=e: tail_copy(e).start())
        for e in range(N_EXPERTS):
            pl.when(nonempty(e))(lambda e=e: tail_copy(e).wait())

    def row_copy(t, kk):
        return pltpu.make_async_copy(
            xf_ref.at[_token_rows(t), :],
            slots_hbm.at[_token_rows(dest_ref[kk, t]), :], sem)

    def start(t, carry):
        for kk in range(TOP_K):
            row_copy(t, kk).start(priority=kk % 2)
        return carry

    lax.fori_loop(0, tm, start, 0, unroll=8)
    for kk in range(TOP_K):
        pltpu.make_async_copy(
            xf_ref, slots_hbm.at[pl.ds(0, tm * SUBLANES), :], sem).wait()


def _dispatch(padded_end, dest, xf, n_pad, tm, blk):
    t = xf.shape[0] // SUBLANES
    assert t % tm == 0
    return pl.pallas_call(
        functools.partial(_dispatch_kernel, tm=tm, blk=blk),
        grid_spec=pltpu.PrefetchScalarGridSpec(
            num_scalar_prefetch=1,
            grid=(t // tm,),
            in_specs=[pl.BlockSpec((TOP_K, tm), lambda i, pe: (0, i), memory_space=pltpu.SMEM),
                      pl.BlockSpec((tm * SUBLANES, LANES), lambda i, pe: (i, 0))],
            out_specs=pl.BlockSpec(memory_space=pl.ANY),
            scratch_shapes=[pltpu.VMEM((blk * SUBLANES, LANES), F32),
                            pltpu.SemaphoreType.DMA, pltpu.SemaphoreType.DMA],
        ),
        out_shape=jax.ShapeDtypeStruct((n_pad * SUBLANES, LANES), F32),
        compiler_params=pltpu.CompilerParams(
            dimension_semantics=("arbitrary",), vmem_limit_bytes=VMEM_LIMIT,
            has_side_effects=True),
        name="dispatch",
    )(padded_end, dest, xf)


WEIGHT_ROWS = 128


def _experts_kernel(be_ref, nu_ref, x_ref, wgu_ref, wd_ref, bg_ref, bu_ref, bd_ref, y_ref,
                    wg_s, wu_s, wd_s, t_s, *, blk):
    j = pl.program_id(0)
    used = j < nu_ref[0]
    new_expert = jnp.logical_or(j == 0, be_ref[j] != be_ref[jnp.maximum(j - 1, 0)])

    @pl.when(jnp.logical_and(used, new_expert))
    def _():
        def rows_step(r, carry):
            rows = pl.ds(pl.multiple_of(r * WEIGHT_ROWS, WEIGHT_ROWS), WEIGHT_ROWS)
            t_s[...] = wgu_ref[rows, :].T
            wg_s[rows, :] = t_s[pl.ds(0, D_FF, stride=2), :].T.astype(BF16)
            wu_s[rows, :] = t_s[pl.ds(1, D_FF, stride=2), :].T.astype(BF16)
            wd_s[rows, :] = wd_ref[rows, :].astype(BF16)
            return carry

        lax.fori_loop(0, D_MODEL // WEIGHT_ROWS, rows_step, 0)

    @pl.when(used)
    def _():
        xb = _load_token_tiles(x_ref, blk).astype(BF16)
        gate = jnp.minimum(_dot(xb, wg_s[...]) + bg_ref[...], SWIGLU_LIMIT)
        up = jnp.clip(_dot(xb, wu_s[...]) + bu_ref[...], -SWIGLU_LIMIT, SWIGLU_LIMIT)
        hidden = (up + 1.0) * gate * jax.nn.sigmoid(SWIGLU_ALPHA * gate)
        _store_token_tiles(y_ref, _dot(hidden.astype(BF16), wd_s[...]) + bd_ref[...])

    @pl.when(jnp.logical_not(used))
    def _():
        y_ref[...] = jnp.zeros_like(y_ref)


def _experts(block_expert, n_used, slots, wgu, wd, bg, bu, bd, blk):
    assert D_FF == D_MODEL
    n_pad = slots.shape[0] // SUBLANES
    n_blocks = n_pad // blk
    xmap = lambda j, be, nu: (jnp.minimum(j, nu[0] - 1), 0)
    wmap = lambda j, be, nu: (be[j], 0, 0)
    bspec = pl.BlockSpec((None, 1, D_FF), wmap)
    tiles = (blk * SUBLANES, LANES)
    return pl.pallas_call(
        functools.partial(_experts_kernel, blk=blk),
        grid_spec=pltpu.PrefetchScalarGridSpec(
            num_scalar_prefetch=2,
            grid=(n_blocks,),
            in_specs=[pl.BlockSpec(tiles, xmap),
                      pl.BlockSpec((None, D_MODEL, 2 * D_FF), wmap),
                      pl.BlockSpec((None, D_FF, D_MODEL), wmap), bspec, bspec, bspec],
            out_specs=pl.BlockSpec(tiles, lambda j, be, nu: (j, 0)),
            scratch_shapes=[pltpu.VMEM((D_MODEL, D_FF), BF16), pltpu.VMEM((D_MODEL, D_FF), BF16),
                            pltpu.VMEM((D_FF, D_MODEL), BF16),
                            pltpu.VMEM((2 * D_FF, WEIGHT_ROWS), F32)],
        ),
        out_shape=jax.ShapeDtypeStruct((n_pad * SUBLANES, LANES), F32),
        compiler_params=pltpu.CompilerParams(
            dimension_semantics=("arbitrary",), vmem_limit_bytes=VMEM_LIMIT),
        name="experts",
    )(block_expert, n_used, slots, wgu, wd, bg, bu, bd)


def _combine_kernel(dest_ref, y_hbm, w_ref, x1_ref, o_ref, buf, sem, *, tm):
    def row_copy(t, kk):
        return pltpu.make_async_copy(
            y_hbm.at[_token_rows(dest_ref[kk, t]), :],
            buf.at[kk, _token_rows(t), :], sem)

    def start(t, carry):
        for kk in range(TOP_K):
            row_copy(t, kk).start(priority=kk % 2)
        return carry

    lax.fori_loop(0, tm, start, 0, unroll=8)
    for kk in range(TOP_K):
        pltpu.make_async_copy(y_hbm.at[pl.ds(0, tm * SUBLANES), :], buf.at[kk], sem).wait()
    w = w_ref[...]
    for s in range(ROW_TILES):
        cols = slice(s * LANES, (s + 1) * LANES)
        acc = x1_ref[:, cols]
        for kk in range(TOP_K):
            acc = acc + buf[kk, pl.ds(s, tm, stride=SUBLANES), :] * w[:, kk:kk + 1]
        o_ref[:, cols] = acc


def _combine(dest, y_slots, w_t, x1, tm):
    t = x1.shape[0]
    assert t % tm == 0
    row = pl.BlockSpec((tm, D_MODEL), lambda i: (i, 0))
    return pl.pallas_call(
        functools.partial(_combine_kernel, tm=tm),
        grid=(t // tm,),
        in_specs=[pl.BlockSpec((TOP_K, tm), lambda i: (0, i), memory_space=pltpu.SMEM),
                  pl.BlockSpec(memory_space=pl.ANY),
                  pl.BlockSpec((tm, TOP_K), lambda i: (i, 0)),
                  row],
        out_specs=row,
        out_shape=jax.ShapeDtypeStruct((t, D_MODEL), F32),
        scratch_shapes=[pltpu.VMEM((TOP_K, tm * SUBLANES, LANES), F32),
                        pltpu.SemaphoreType.DMA],
        compiler_params=pltpu.CompilerParams(
            dimension_semantics=("arbitrary",), vmem_limit_bytes=VMEM_LIMIT),
        name="combine",
    )(dest, y_slots, w_t, x1)


def _tiles(batch, seq):
    return dict(
        in_proj=min(512, seq),
        retention=min(1024, seq),
        sb_block=min(256, seq),
        out_proj=min(512, seq),
        dest=min(8192, batch * seq),
        dispatch=min(1024, seq),
        combine=min(256, seq),
        expert_block=512,
    )


def _layer(x, attn_norm_g, w_in, ret_norm_g, sb_q_norm_g, sb_k_norm_g, w_out,
           ffn_norm_g, w_router, b_router, w_gate_up, b_gate_up, w_down, b_down):
    batch, seq, d = x.shape
    t = batch * seq
    tiles = _tiles(batch, seq)
    x2 = x.reshape(t, d)

    rq, rk, rv, rg, sq, sk, sv = _in_proj(
        x2, attn_norm_g, w_in, sb_q_norm_g, sb_k_norm_g, seq, tiles["in_proj"])
    ret = _retention(rq, rk, rv, rg, ret_norm_g, batch, seq, tiles["retention"])
    sb = _sb_attention(sq, sk, sv, batch, seq, tiles["sb_block"], LANES)
    x1, xf, top_idx, top_w, rank, counts = _out_proj_route(
        ret, sb, x2, w_out, ffn_norm_g, w_router, b_router, tiles["out_proj"])

    blk = tiles["expert_block"]
    n_pad = t * TOP_K + N_EXPERTS * blk
    n_blocks = n_pad // blk
    counts = counts[:, 0]
    padded = (counts + blk - 1) // blk * blk
    padded_end = jnp.cumsum(padded)
    padded_start = padded_end - padded
    block_first = jnp.arange(n_blocks, dtype=jnp.int32) * blk
    block_expert = jnp.minimum(
        jnp.sum(padded_end[None, :] <= block_first[:, None], axis=1), N_EXPERTS - 1
    ).astype(jnp.int32)
    n_used = (padded_end[-1:] // blk).astype(jnp.int32)

    dest = _dest(padded_start, top_idx, rank, tiles["dest"])
    slots = _dispatch(padded_end, dest, xf, n_pad, tiles["dispatch"], blk)

    bgu = b_gate_up.astype(F32).reshape(N_EXPERTS, 1, D_FF, 2)
    y_slots = _experts(block_expert, n_used, slots, w_gate_up.astype(F32), w_down.astype(F32),
                       bgu[..., 0], bgu[..., 1], b_down.astype(F32)[:, None, :], blk)

    out = _combine(dest, y_slots, top_w.T, x1, tiles["combine"])
    return out.reshape(batch, seq, d)


def kernel(x, attn_norm_g, w_in, ret_norm_g, sb_q_norm_g, sb_k_norm_g, w_out, ffn_norm_g,
           w_router, b_router, w_gate_up, b_gate_up, w_down, b_down):
    depth = attn_norm_g.shape[0]
    for l in range(depth):
        x = _layer(x, attn_norm_g[l], w_in[l], ret_norm_g[l], sb_q_norm_g[l], sb_k_norm_g[l],
                   w_out[l], ffn_norm_g[l], w_router[l], b_router[l], w_gate_up[l],
                   b_gate_up[l], w_down[l], b_down[l])
    return x
```

```python
import functools

import numpy as np
import jax
import jax.numpy as jnp
from jax import lax
from jax.experimental import pallas as pl
from jax.experimental.pallas import tpu as pltpu

D_MODEL = 1024
RET_WIDTH = 512
RET_HEADS = 4
RET_HEAD_DIM = 128
SB_WIDTH = 512
SB_HEADS = 8
SB_HEAD_DIM = 64
IN_PROJ_WIDTH = 4 * RET_WIDTH + 3 * SB_WIDTH
RET_CHUNK = 128
ROPE_BASE = 10000.0
N_EXPERTS = 32
TOP_K = 4
D_FF = D_MODEL
SWIGLU_LIMIT = 7.0
SWIGLU_ALPHA = 1.702
EPS = 1e-6

LANES = 128
F32_EXP_UNDERFLOW = -88.0
VMEM_LIMIT = 56 * 1024 * 1024

BF16 = jnp.bfloat16
F32 = jnp.float32


def _split_bf16(v):
    hi = v.astype(BF16)
    lo = (v - hi.astype(F32)).astype(BF16)
    return hi, lo


SUBLANES = 8
ROW_TILES = D_MODEL // LANES
assert ROW_TILES == SUBLANES


def _store_token_tiles(ref, value):
    n = value.shape[0]
    for s in range(ROW_TILES):
        ref[pl.ds(s, n, stride=SUBLANES), :] = value[:, s * LANES:(s + 1) * LANES]


def _load_token_tiles(ref, n):
    return jnp.concatenate(
        [ref[pl.ds(s, n, stride=SUBLANES), :] for s in range(ROW_TILES)], axis=1)


def _token_rows(i):
    return pl.ds(pl.multiple_of(i * SUBLANES, SUBLANES), SUBLANES)


def _dot(a, b):
    return jnp.dot(a, b, preferred_element_type=F32)


def _dot_nt(a, b):
    return lax.dot_general(a, b, (((1,), (1,)), ((), ())), preferred_element_type=F32)


def _in_proj_kernel(x_ref, g_ref, w_ref, cos_ref, sin_ref, qg_ref, kg_ref, seg_ref,
                    rq_ref, rk_ref, rv_ref, rg_ref, sq_ref, sk_ref, sv_ref):
    x = x_ref[...]
    h = x * lax.rsqrt(jnp.mean(x * x, axis=-1, keepdims=True) + EPS) * g_ref[...]
    p = _dot(h.astype(BF16), w_ref[...])
    cos2 = cos_ref[...]
    sin2 = sin_ref[...]
    k_scale = RET_HEAD_DIM ** -0.5
    for hd in range(RET_HEADS):
        lo = hd * RET_HEAD_DIM
        q = p[:, lo:lo + RET_HEAD_DIM]
        k = p[:, RET_WIDTH + lo:RET_WIDTH + lo + RET_HEAD_DIM]
        q = q * cos2 + pltpu.roll(q, RET_HEAD_DIM // 2, axis=1) * sin2
        k = k * cos2 + pltpu.roll(k, RET_HEAD_DIM // 2, axis=1) * sin2
        rq_ref[:, lo:lo + RET_HEAD_DIM] = q.astype(rq_ref.dtype)
        rk_ref[:, lo:lo + RET_HEAD_DIM] = (k * k_scale).astype(rk_ref.dtype)
    rv_ref[...] = p[:, 2 * RET_WIDTH:3 * RET_WIDTH].astype(rv_ref.dtype)
    rg_ref[...] = p[:, 3 * RET_WIDTH:4 * RET_WIDTH]
    base = 4 * RET_WIDTH
    seg = seg_ref[...]

    def head_norm(v, gain):
        hi, lo = _split_bf16(v * v)
        ms = (_dot(hi, seg) + _dot(lo, seg)) * (1.0 / SB_HEAD_DIM)
        return v * lax.rsqrt(ms + EPS) * gain

    sq = head_norm(p[:, base:base + SB_WIDTH], qg_ref[...])
    sk = head_norm(p[:, base + SB_WIDTH:base + 2 * SB_WIDTH], kg_ref[...])
    sq_ref[...] = (sq * (SB_HEAD_DIM ** -0.5)).astype(sq_ref.dtype)
    sk_ref[...] = sk.astype(sk_ref.dtype)
    sv_ref[...] = p[:, base + 2 * SB_WIDTH:base + 3 * SB_WIDTH].astype(sv_ref.dtype)


def _rope_tables(seq):
    half = RET_HEAD_DIM // 2
    inv = ROPE_BASE ** (-np.arange(half, dtype=np.float64) / half)
    n_a = seq // LANES
    ang_a = (np.arange(n_a, dtype=np.float64) * LANES)[:, None] * inv[None, :]
    ang_b = np.arange(LANES, dtype=np.float64)[:, None] * inv[None, :]
    ca, sa = jnp.asarray(np.cos(ang_a), F32), jnp.asarray(np.sin(ang_a), F32)
    cb, sb = jnp.asarray(np.cos(ang_b), F32), jnp.asarray(np.sin(ang_b), F32)
    cos = (ca[:, None, :] * cb[None] - sa[:, None, :] * sb[None]).reshape(seq, half)
    sin = (sa[:, None, :] * cb[None] + ca[:, None, :] * sb[None]).reshape(seq, half)
    return jnp.concatenate([cos, cos], axis=-1), jnp.concatenate([-sin, sin], axis=-1)


def _in_proj(x2, norm_g, w_in, sb_q_g, sb_k_g, seq, tm):
    t = x2.shape[0]
    assert seq % tm == 0 and t % tm == 0
    cos2, sin2 = _rope_tables(seq)
    seg = np.kron(np.eye(SB_HEADS), np.ones((SB_HEAD_DIM, SB_HEAD_DIM)))
    seg = jnp.asarray(seg, BF16)
    qg = jnp.tile(sb_q_g.astype(F32), SB_HEADS)[None, :]
    kg = jnp.tile(sb_k_g.astype(F32), SB_HEADS)[None, :]
    n_pos = seq // tm
    row = lambda w: pl.BlockSpec((tm, w), lambda i: (i, 0))
    full = lambda a: pl.BlockSpec(a.shape, lambda i: (0,) * a.ndim)
    pos = pl.BlockSpec((tm, RET_HEAD_DIM), lambda i: (i % n_pos, 0))
    w_bf = w_in.astype(BF16)
    g2 = norm_g.astype(F32)[None, :]
    out = lambda dt: jax.ShapeDtypeStruct((t, RET_WIDTH), dt)
    return pl.pallas_call(
        _in_proj_kernel,
        grid=(t // tm,),
        in_specs=[row(D_MODEL), full(g2), full(w_bf), pos, pos, full(qg), full(kg), full(seg)],
        out_specs=[row(RET_WIDTH)] * 7,
        out_shape=[out(BF16), out(F32), out(BF16), out(F32), out(BF16), out(BF16), out(BF16)],
        compiler_params=pltpu.CompilerParams(
            dimension_semantics=("arbitrary",), vmem_limit_bytes=VMEM_LIMIT),
        name="in_proj",
    )(x2, g2, w_bf, cos2, sin2, qg, kg, seg)


def _retention_kernel(q_ref, k_ref, v_ref, g_ref, ng_ref, intra_ref, qd_ref, kd_ref, cd_ref,
                      o_ref, state_ref, *, chunks):
    @pl.when(pl.program_id(1) == 0)
    def _():
        state_ref[...] = jnp.zeros_like(state_ref)

    def chunk(c, carry):
        rows = pl.ds(pl.multiple_of(c * RET_CHUNK, RET_CHUNK), RET_CHUNK)
        for hd in range(RET_HEADS):
            cols = slice(hd * RET_HEAD_DIM, (hd + 1) * RET_HEAD_DIM)
            q = q_ref[rows, cols]
            k = k_ref[rows, cols]
            v = v_ref[rows, cols]
            state = state_ref[hd]
            scores = _dot_nt(q, k.astype(BF16)) * intra_ref[hd]
            inner = _dot(scores.astype(BF16), v)
            cross = _dot(q, state.astype(BF16)) * qd_ref[:, cols]
            kd = (k * kd_ref[:, cols]).astype(BF16)
            state_ref[hd] = state * cd_ref[:, cols] + _dot(kd.T, v)
            o = inner + cross
            o = o * lax.rsqrt(jnp.mean(o * o, axis=-1, keepdims=True) + EPS) * ng_ref[:, cols]
            gate = g_ref[rows, cols]
            o_ref[rows, cols] = (o * (gate * jax.nn.sigmoid(gate))).astype(o_ref.dtype)
        return carry

    lax.fori_loop(0, chunks, chunk, 0)


def _retention(rq, rk, rv, rg, ret_norm_g, batch, seq, rows):
    assert seq % rows == 0 and rows % RET_CHUNK == 0
    log_gamma = np.log(1.0 - 2.0 ** (-5.0 - np.arange(RET_HEADS, dtype=np.float64)))
    idx = np.arange(RET_CHUNK, dtype=np.float64)
    diff = idx[:, None] - idx[None, :]
    intra = np.where(diff >= 0, np.exp(log_gamma[:, None, None] * np.maximum(diff, 0.0)), 0.0)
    q_decay = np.exp(log_gamma[:, None] * (idx + 1.0))
    k_decay = np.exp(log_gamma[:, None] * (RET_CHUNK - 1.0 - idx))
    chunk_decay = np.exp(log_gamma * RET_CHUNK)
    lane_rep = lambda a: np.repeat(a.T, RET_HEAD_DIM, axis=1)
    intra = jnp.asarray(intra, F32)
    qd = jnp.asarray(lane_rep(q_decay), F32)
    kd = jnp.asarray(lane_rep(k_decay), F32)
    cd = jnp.asarray(np.repeat(chunk_decay, RET_HEAD_DIM)[None, :], F32)
    ng = ret_norm_g.astype(F32).reshape(1, RET_WIDTH)
    n_r = seq // rows
    blk = pl.BlockSpec((rows, RET_WIDTH), lambda b, r: (b * n_r + r, 0))
    full = lambda a: pl.BlockSpec(a.shape, lambda b, r: (0,) * a.ndim)
    return pl.pallas_call(
        functools.partial(_retention_kernel, chunks=rows // RET_CHUNK),
        grid=(batch, n_r),
        in_specs=[blk, blk, blk, blk, full(ng), full(intra), full(qd), full(kd), full(cd)],
        out_specs=blk,
        out_shape=jax.ShapeDtypeStruct((batch * seq, RET_WIDTH), BF16),
        scratch_shapes=[pltpu.VMEM((RET_HEADS, RET_HEAD_DIM, RET_HEAD_DIM), F32)],
        compiler_params=pltpu.CompilerParams(
            dimension_semantics=("arbitrary", "arbitrary"), vmem_limit_bytes=VMEM_LIMIT),
        name="retention",
    )(rq, rk, rv, rg, ng, intra, qd, kd, cd)


def _sb_attn_kernel(q_ref, k_ref, v_ref, tri_ref, o_ref, *, blk, halo, heads):
    i = pl.program_id(2)
    lane = lax.broadcasted_iota(jnp.int32, (blk, LANES), 1)
    first_head = lane < SB_HEAD_DIM
    tri_own = tri_ref[...]
    tri_halo = tri_ref[:halo, :halo]
    below_diag = (lax.broadcasted_iota(jnp.int32, (blk, blk), 1)
                  < lax.broadcasted_iota(jnp.int32, (blk, blk), 0))
    has_halo = i > 0

    def scores(qm, kb, mask):
        z = _dot_nt(qm, kb)
        log_beta = jnp.minimum(z, 0.0) - jnp.log(1.0 + jnp.exp(-jnp.abs(z)))
        log_rest = log_beta - z
        if mask is not None:
            log_rest = jnp.where(mask, log_rest, 0.0)
        return log_beta, log_rest

    def later_sums(log_rest, tri):
        hi, lo = _split_bf16(log_rest)
        later = _dot(hi, tri) + _dot(lo, tri)
        return later, later[:, 0:1] + log_rest[:, 0:1]

    def weights(log_beta, log_remaining, mask):
        w = jnp.exp(log_beta + log_remaining)
        if mask is not None:
            w = jnp.where(mask, w, 0.0)
        return w.astype(BF16)

    def head_q(h):
        cols = slice((h // 2) * LANES, (h // 2 + 1) * LANES)
        q = q_ref[:, cols]
        keep = first_head if h % 2 == 0 else jnp.logical_not(first_head)
        return jnp.where(keep, q, jnp.zeros_like(q)), cols

    own_rows = pl.ds(pl.multiple_of(i * blk, blk), blk)
    halo_rows = pl.ds(pl.multiple_of(jnp.maximum(i * blk - halo, 0), halo), halo)

    def first_step(h):
        qm, cols = head_q(h)
        beta_o, rest_o = scores(qm, k_ref[own_rows, cols], below_diag)
        beta_h, rest_h = scores(qm, k_ref[halo_rows, cols], has_halo)
        later_o, total_o = later_sums(rest_o, tri_own)
        later_h, total_h = later_sums(rest_h, tri_halo)
        acc = _dot(weights(beta_o, later_o, below_diag), v_ref[own_rows, cols])
        acc = acc + _dot(weights(beta_h, total_o + later_h, has_halo), v_ref[halo_rows, cols])
        return total_o + total_h, acc

    def chunk_step(h, j, c, acc):
        qm, cols = head_q(h)
        rows = pl.ds(pl.multiple_of(j * halo, halo), halo)
        beta, rest = scores(qm, k_ref[rows, cols], None)
        later, total = later_sums(rest, tri_halo)
        return c + total, acc + _dot(weights(beta, c + later, None), v_ref[rows, cols])

    def cond(carry):
        j, cs, _ = carry
        c_max = functools.reduce(jnp.maximum, cs)
        return jnp.logical_and(j >= 0, jnp.max(c_max) > F32_EXP_UNDERFLOW)

    def body(carry):
        j, cs, accs = carry
        out = [chunk_step(h, j, cs[h], accs[h]) for h in range(heads)]
        return j - 1, tuple(o[0] for o in out), tuple(o[1] for o in out)

    first = [first_step(h) for h in range(heads)]
    init = (i * (blk // halo) - 2, tuple(f[0] for f in first), tuple(f[1] for f in first))
    _, _, accs = lax.while_loop(cond, body, init)
    for p in range(heads // 2):
        o_ref[:, p * LANES:(p + 1) * LANES] = jnp.where(
            first_head, accs[2 * p], accs[2 * p + 1]).astype(o_ref.dtype)


def _sb_attention(sq, sk, sv, batch, seq, blk, halo, heads):
    assert seq % blk == 0 and blk % halo == 0 and halo == LANES
    assert heads % 2 == 0 and SB_HEADS % heads == 0
    nq = seq // blk
    width = heads * SB_HEAD_DIM
    groups = SB_WIDTH // width
    j = np.arange(blk)
    tri = jnp.asarray((j[:, None] > j[None, :]).astype(np.float32), BF16)
    qblk = pl.BlockSpec((blk, width), lambda b, g, i: (b * nq + i, g))
    kvblk = pl.BlockSpec((seq, width), lambda b, g, i: (b, g))
    return pl.pallas_call(
        functools.partial(_sb_attn_kernel, blk=blk, halo=halo, heads=heads),
        grid=(batch, groups, nq),
        in_specs=[qblk, kvblk, kvblk, pl.BlockSpec(tri.shape, lambda b, g, i: (0, 0))],
        out_specs=qblk,
        out_shape=jax.ShapeDtypeStruct((batch * seq, SB_WIDTH), BF16),
        compiler_params=pltpu.CompilerParams(
            dimension_semantics=("arbitrary", "arbitrary", "arbitrary"),
            vmem_limit_bytes=VMEM_LIMIT),
        name="sb_attn",
    )(sq, sk, sv, tri)


def _out_proj_kernel(ret_ref, sb_ref, x_ref, wtop_ref, wbot_ref, g_ref, wrh_ref, wrl_ref,
                     br_ref, tri_ref,
                     x1_ref, xf_ref, idx_ref, w_ref, rank_ref, cnt_ref, run_ref):
    @pl.when(pl.program_id(0) == 0)
    def _():
        run_ref[...] = jnp.zeros_like(run_ref)

    x1 = x_ref[...] + _dot(ret_ref[...], wtop_ref[...]) + _dot(sb_ref[...], wbot_ref[...])
    x1_ref[...] = x1
    xf = x1 * lax.rsqrt(jnp.mean(x1 * x1, axis=-1, keepdims=True) + EPS) * g_ref[...]
    _store_token_tiles(xf_ref, xf)
    xh, xl = _split_bf16(xf)
    wrh = wrh_ref[...]
    logits = _dot_nt(wrh, xh) + _dot_nt(wrh, xl) + _dot_nt(wrl_ref[...], xh) + br_ref[...]
    n_e, tm = logits.shape
    e_iota = lax.broadcasted_iota(jnp.int32, (n_e, tm), 0)
    cur = logits
    tops, sels, hots = [], [], []
    for _ in range(TOP_K):
        m = jnp.max(cur, axis=0, keepdims=True)
        sel = jnp.min(jnp.where(cur == m, e_iota, n_e), axis=0, keepdims=True)
        hot = e_iota == sel
        cur = jnp.where(hot, -jnp.inf, cur)
        tops.append(m)
        sels.append(sel)
        hots.append(hot)
    ps = [jnp.exp(m - tops[0]) for m in tops]
    denom = ps[0] + ps[1] + ps[2] + ps[3]
    chosen = jnp.zeros((n_e, tm), F32)
    for hot in hots:
        chosen = chosen + hot.astype(F32)
    before = run_ref[:, 0:1] + _dot(chosen.astype(BF16), tri_ref[...])
    for kk in range(TOP_K):
        idx_ref[kk:kk + 1, :] = sels[kk]
        w_ref[kk:kk + 1, :] = ps[kk] / denom
        rank = jnp.sum(jnp.where(hots[kk], before, 0.0), axis=0, keepdims=True)
        rank_ref[kk:kk + 1, :] = rank.astype(jnp.int32)
    run = run_ref[...] + jnp.sum(chosen, axis=1, keepdims=True)
    run_ref[...] = run
    cnt_ref[...] = run.astype(jnp.int32)


def _out_proj_route(ret, sb, x2, w_out, ffn_g, w_router, b_router, tm):
    t = x2.shape[0]
    assert t % tm == 0
    w_bf = w_out.astype(BF16)
    wtop, wbot = w_bf[:RET_WIDTH], w_bf[RET_WIDTH:]
    g2 = ffn_g.astype(F32)[None, :]
    wr_t = w_router.astype(F32).T
    wrh = wr_t.astype(BF16)
    wrl = (wr_t - wrh.astype(F32)).astype(BF16)
    br = b_router.astype(F32)[:, None]
    tt = np.arange(tm)
    tri = jnp.asarray((tt[:, None] < tt[None, :]).astype(np.float32), BF16)
    row = lambda w: pl.BlockSpec((tm, w), lambda i: (i, 0))
    full = lambda a: pl.BlockSpec(a.shape, lambda i: (0,) * a.ndim)
    col = pl.BlockSpec((TOP_K, tm), lambda i: (0, i))
    return pl.pallas_call(
        _out_proj_kernel,
        grid=(t // tm,),
        in_specs=[row(RET_WIDTH), row(SB_WIDTH), row(D_MODEL), full(wtop), full(wbot), full(g2),
                  full(wrh), full(wrl), full(br), full(tri)],
        out_specs=[row(D_MODEL), pl.BlockSpec((tm * SUBLANES, LANES), lambda i: (i, 0)),
                   col, col, col, pl.BlockSpec((N_EXPERTS, LANES), lambda i: (0, 0))],
        out_shape=[jax.ShapeDtypeStruct((t, D_MODEL), F32),
                   jax.ShapeDtypeStruct((t * SUBLANES, LANES), F32),
                   jax.ShapeDtypeStruct((TOP_K, t), jnp.int32),
                   jax.ShapeDtypeStruct((TOP_K, t), F32),
                   jax.ShapeDtypeStruct((TOP_K, t), jnp.int32),
                   jax.ShapeDtypeStruct((N_EXPERTS, LANES), jnp.int32)],
        scratch_shapes=[pltpu.VMEM((N_EXPERTS, LANES), F32)],
        compiler_params=pltpu.CompilerParams(
            dimension_semantics=("arbitrary",), vmem_limit_bytes=VMEM_LIMIT),
        name="out_proj_route",
    )(ret, sb, x2, wtop, wbot, g2, wrh, wrl, br, tri)


def _dest_kernel(pstart_ref, idx_ref, rank_ref, dest_ref):
    idx = idx_ref[...]
    dest = rank_ref[...]
    for e in range(N_EXPERTS):
        dest = dest + jnp.where(idx == e, pstart_ref[e], 0)
    dest_ref[...] = dest


def _dest(padded_start, top_idx, rank, tn):
    t = top_idx.shape[1]
    assert t % tn == 0
    col = pl.BlockSpec((TOP_K, tn), lambda i, ps: (0, i))
    return pl.pallas_call(
        _dest_kernel,
        grid_spec=pltpu.PrefetchScalarGridSpec(
            num_scalar_prefetch=1, grid=(t // tn,), in_specs=[col, col], out_specs=col),
        out_shape=jax.ShapeDtypeStruct((TOP_K, t), jnp.int32),
        name="dest",
    )(padded_start, top_idx, rank)


def _dispatch_kernel(pend_ref, dest_ref, xf_ref, slots_hbm, zbuf, sem, zsem, *, tm, blk):
    @pl.when(pl.program_id(0) == 0)
    def _():
        zbuf[...] = jnp.zeros_like(zbuf)

        def tail_copy(e):
            first = pl.multiple_of((pend_ref[e] - blk) * SUBLANES, blk * SUBLANES)
            return pltpu.make_async_copy(
                zbuf, slots_hbm.at[pl.ds(first, blk * SUBLANES), :], zsem)

        def nonempty(e):
            return pend_ref[e] > (pend_ref[e - 1] if e else 0)

        for e in range(N_EXPERTS):
            pl.when(nonempty(e))(lambda e=e: tail_copy(e).start())
        for e in range(N_EXPERTS):
            pl.when(nonempty(e))(lambda e=e: tail_copy(e).wait())

    def row_copy(t, kk):
        return pltpu.make_async_copy(
            xf_ref.at[_token_rows(t), :],
            slots_hbm.at[_token_rows(dest_ref[kk, t]), :], sem)

    def start(t, carry):
        for kk in range(TOP_K):
            row_copy(t, kk).start(priority=kk % 2)
        return carry

    lax.fori_loop(0, tm, start, 0, unroll=8)
    for kk in range(TOP_K):
        pltpu.make_async_copy(
            xf_ref, slots_hbm.at[pl.ds(0, tm * SUBLANES), :], sem).wait()


def _dispatch(padded_end, dest, xf, n_pad, tm, blk):
    t = xf.shape[0] // SUBLANES
    assert t % tm == 0
    return pl.pallas_call(
        functools.partial(_dispatch_kernel, tm=tm, blk=blk),
        grid_spec=pltpu.PrefetchScalarGridSpec(
            num_scalar_prefetch=1,
            grid=(t // tm,),
            in_specs=[pl.BlockSpec((TOP_K, tm), lambda i, pe: (0, i), memory_space=pltpu.SMEM),
                      pl.BlockSpec((tm * SUBLANES, LANES), lambda i, pe: (i, 0))],
            out_specs=pl.BlockSpec(memory_space=pl.ANY),
            scratch_shapes=[pltpu.VMEM((blk * SUBLANES, LANES), F32),
                            pltpu.SemaphoreType.DMA, pltpu.SemaphoreType.DMA],
        ),
        out_shape=jax.ShapeDtypeStruct((n_pad * SUBLANES, LANES), F32),
        compiler_params=pltpu.CompilerParams(
            dimension_semantics=("arbitrary",), vmem_limit_bytes=VMEM_LIMIT,
            has_side_effects=True),
        name="dispatch",
    )(padded_end, dest, xf)


WEIGHT_ROWS = 128


def _experts_kernel(be_ref, nu_ref, x_ref, wgu_ref, wd_ref, bg_ref, bu_ref, bd_ref, y_ref,
                    wg_s, wu_s, wd_s, t_s, *, blk):
    j = pl.program_id(0)
    used = j < nu_ref[0]
    new_expert = jnp.logical_or(j == 0, be_ref[j] != be_ref[jnp.maximum(j - 1, 0)])

    @pl.when(jnp.logical_and(used, new_expert))
    def _():
        def rows_step(r, carry):
            rows = pl.ds(pl.multiple_of(r * WEIGHT_ROWS, WEIGHT_ROWS), WEIGHT_ROWS)
            t_s[...] = wgu_ref[rows, :].T
            wg_s[rows, :] = t_s[pl.ds(0, D_FF, stride=2), :].T.astype(BF16)
            wu_s[rows, :] = t_s[pl.ds(1, D_FF, stride=2), :].T.astype(BF16)
            wd_s[rows, :] = wd_ref[rows, :].astype(BF16)
            return carry

        lax.fori_loop(0, D_MODEL // WEIGHT_ROWS, rows_step, 0)

    @pl.when(used)
    def _():
        xb = _load_token_tiles(x_ref, blk).astype(BF16)
        gate = jnp.minimum(_dot(xb, wg_s[...]) + bg_ref[...], SWIGLU_LIMIT)
        up = jnp.clip(_dot(xb, wu_s[...]) + bu_ref[...], -SWIGLU_LIMIT, SWIGLU_LIMIT)
        hidden = (up + 1.0) * gate * jax.nn.sigmoid(SWIGLU_ALPHA * gate)
        _store_token_tiles(y_ref, _dot(hidden.astype(BF16), wd_s[...]) + bd_ref[...])

    @pl.when(jnp.logical_not(used))
    def _():
        y_ref[...] = jnp.zeros_like(y_ref)


def _experts(block_expert, n_used, slots, wgu, wd, bg, bu, bd, blk):
    assert D_FF == D_MODEL
    n_pad = slots.shape[0] // SUBLANES
    n_blocks = n_pad // blk
    xmap = lambda j, be, nu: (jnp.minimum(j, nu[0] - 1), 0)
    wmap = lambda j, be, nu: (be[j], 0, 0)
    bspec = pl.BlockSpec((None, 1, D_FF), wmap)
    tiles = (blk * SUBLANES, LANES)
    return pl.pallas_call(
        functools.partial(_experts_kernel, blk=blk),
        grid_spec=pltpu.PrefetchScalarGridSpec(
            num_scalar_prefetch=2,
            grid=(n_blocks,),
            in_specs=[pl.BlockSpec(tiles, xmap),
                      pl.BlockSpec((None, D_MODEL, 2 * D_FF), wmap),
                      pl.BlockSpec((None, D_FF, D_MODEL), wmap), bspec, bspec, bspec],
            out_specs=pl.BlockSpec(tiles, lambda j, be, nu: (j, 0)),
            scratch_shapes=[pltpu.VMEM((D_MODEL, D_FF), BF16), pltpu.VMEM((D_MODEL, D_FF), BF16),
                            pltpu.VMEM((D_FF, D_MODEL), BF16),
                            pltpu.VMEM((2 * D_FF, WEIGHT_ROWS), F32)],
        ),
        out_shape=jax.ShapeDtypeStruct((n_pad * SUBLANES, LANES), F32),
        compiler_params=pltpu.CompilerParams(
            dimension_semantics=("arbitrary",), vmem_limit_bytes=VMEM_LIMIT),
        name="experts",
    )(block_expert, n_used, slots, wgu, wd, bg, bu, bd)


def _combine_kernel(dest_ref, y_hbm, w_ref, x1_ref, o_ref, buf, sem, *, tm):
    def row_copy(t, kk):
        return pltpu.make_async_copy(
            y_hbm.at[_token_rows(dest_ref[kk, t]), :],
            buf.at[kk, _token_rows(t), :], sem)

    def start(t, carry):
        for kk in range(TOP_K):
            row_copy(t, kk).start(priority=kk % 2)
        return carry

    lax.fori_loop(0, tm, start, 0, unroll=8)
    for kk in range(TOP_K):
        pltpu.make_async_copy(y_hbm.at[pl.ds(0, tm * SUBLANES), :], buf.at[kk], sem).wait()
    w = w_ref[...]
    for s in range(ROW_TILES):
        cols = slice(s * LANES, (s + 1) * LANES)
        acc = x1_ref[:, cols]
        for kk in range(TOP_K):
            acc = acc + buf[kk, pl.ds(s, tm, stride=SUBLANES), :] * w[:, kk:kk + 1]
        o_ref[:, cols] = acc


def _combine(dest, y_slots, w_t, x1, tm):
    t = x1.shape[0]
    assert t % tm == 0
    row = pl.BlockSpec((tm, D_MODEL), lambda i: (i, 0))
    return pl.pallas_call(
        functools.partial(_combine_kernel, tm=tm),
        grid=(t // tm,),
        in_specs=[pl.BlockSpec((TOP_K, tm), lambda i: (0, i), memory_space=pltpu.SMEM),
                  pl.BlockSpec(memory_space=pl.ANY),
                  pl.BlockSpec((tm, TOP_K), lambda i: (i, 0)),
                  row],
        out_specs=row,
        out_shape=jax.ShapeDtypeStruct((t, D_MODEL), F32),
        scratch_shapes=[pltpu.VMEM((TOP_K, tm * SUBLANES, LANES), F32),
                        pltpu.SemaphoreType.DMA],
        compiler_params=pltpu.CompilerParams(
            dimension_semantics=("arbitrary",), vmem_limit_bytes=VMEM_LIMIT),
        name="combine",
    )(dest, y_slots, w_t, x1)


def _tiles(batch, seq):
    return dict(
        in_proj=min(512, seq),
        retention=min(1024, seq),
        sb_block=min(256, seq),
        sb_heads=4,
        out_proj=min(512, seq),
        dest=min(8192, batch * seq),
        dispatch=min(1024, seq),
        combine=min(256, seq),
        expert_block=512,
    )


def _layer(x, attn_norm_g, w_in, ret_norm_g, sb_q_norm_g, sb_k_norm_g, w_out,
           ffn_norm_g, w_router, b_router, w_gate_up, b_gate_up, w_down, b_down):
    batch, seq, d = x.shape
    t = batch * seq
    tiles = _tiles(batch, seq)
    x2 = x.reshape(t, d)

    rq, rk, rv, rg, sq, sk, sv = _in_proj(
        x2, attn_norm_g, w_in, sb_q_norm_g, sb_k_norm_g, seq, tiles["in_proj"])
    ret = _retention(rq, rk, rv, rg, ret_norm_g, batch, seq, tiles["retention"])
    sb = _sb_attention(sq, sk, sv, batch, seq, tiles["sb_block"], LANES, tiles["sb_heads"])
    x1, xf, top_idx, top_w, rank, counts = _out_proj_route(
        ret, sb, x2, w_out, ffn_norm_g, w_router, b_router, tiles["out_proj"])

    blk = tiles["expert_block"]
    n_pad = t * TOP_K + N_EXPERTS * blk
    n_blocks = n_pad // blk
    counts = counts[:, 0]
    padded = (counts + blk - 1) // blk * blk
    padded_end = jnp.cumsum(padded)
    padded_start = padded_end - padded
    block_first = jnp.arange(n_blocks, dtype=jnp.int32) * blk
    block_expert = jnp.minimum(
        jnp.sum(padded_end[None, :] <= block_first[:, None], axis=1), N_EXPERTS - 1
    ).astype(jnp.int32)
    n_used = (padded_end[-1:] // blk).astype(jnp.int32)

    dest = _dest(padded_start, top_idx, rank, tiles["dest"])
    slots = _dispatch(padded_end, dest, xf, n_pad, tiles["dispatch"], blk)

    bgu = b_gate_up.astype(F32).reshape(N_EXPERTS, 1, D_FF, 2)
    y_slots = _experts(block_expert, n_used, slots, w_gate_up.astype(F32), w_down.astype(F32),
                       bgu[..., 0], bgu[..., 1], b_down.astype(F32)[:, None, :], blk)

    out = _combine(dest, y_slots, top_w.T, x1, tiles["combine"])
    return out.reshape(batch, seq, d)


def kernel(x, attn_norm_g, w_in, ret_norm_g, sb_q_norm_g, sb_k_norm_g, w_out, ffn_norm_g,
           w_router, b_router, w_gate_up, b_gate_up, w_down, b_down):
    depth = attn_norm_g.shape[0]
    for l in range(depth):
        x = _layer(x, attn_norm_g[l], w_in[l], ret_norm_g[l], sb_q_norm_g[l], sb_k_norm_g[l],
                   w_out[l], ffn_norm_g[l], w_router[l], b_router[l], w_gate_up[l],
                   b_gate_up[l], w_down[l], b_down[l])
    return x
```

```python
import functools

import numpy as np
import jax
import jax.numpy as jnp
from jax import lax
from jax.experimental import pallas as pl
from jax.experimental.pallas import tpu as pltpu

D_MODEL = 1024
RET_WIDTH = 512
RET_HEADS = 4
RET_HEAD_DIM = 128
SB_WIDTH = 512
SB_HEADS = 8
SB_HEAD_DIM = 64
IN_PROJ_WIDTH = 4 * RET_WIDTH + 3 * SB_WIDTH
RET_CHUNK = 128
ROPE_BASE = 10000.0
N_EXPERTS = 32
TOP_K = 4
D_FF = D_MODEL
SWIGLU_LIMIT = 7.0
SWIGLU_ALPHA = 1.702
EPS = 1e-6

LANES = 128
MXU_WIDTH = 256
F32_EXP_UNDERFLOW = -88.0
VMEM_LIMIT = 56 * 1024 * 1024

BF16 = jnp.bfloat16
F32 = jnp.float32


def _split_bf16(v):
    hi = v.astype(BF16)
    lo = (v - hi.astype(F32)).astype(BF16)
    return hi, lo


SUBLANES = 8
ROW_TILES = D_MODEL // LANES
assert ROW_TILES == SUBLANES


def _store_token_tiles(ref, value):
    n = value.shape[0]
    for s in range(ROW_TILES):
        ref[pl.ds(s, n, stride=SUBLANES), :] = value[:, s * LANES:(s + 1) * LANES]


def _load_token_tiles(ref, n):
    return jnp.concatenate(
        [ref[pl.ds(s, n, stride=SUBLANES), :] for s in range(ROW_TILES)], axis=1)


def _token_rows(i):
    return pl.ds(pl.multiple_of(i * SUBLANES, SUBLANES), SUBLANES)


def _dot(a, b):
    return jnp.dot(a, b, preferred_element_type=F32)


def _dot_nt(a, b):
    return lax.dot_general(a, b, (((1,), (1,)), ((), ())), preferred_element_type=F32)


def _in_proj_kernel(x_ref, g_ref, w_ref, cos_ref, sin_ref, qg_ref, kg_ref, seg_ref,
                    rq_ref, rk_ref, rv_ref, rg_ref, sq_ref, sk_ref, sv_ref):
    x = x_ref[...]
    h = x * lax.rsqrt(jnp.mean(x * x, axis=-1, keepdims=True) + EPS) * g_ref[...]
    p = _dot(h.astype(BF16), w_ref[...])
    cos2 = cos_ref[...]
    sin2 = sin_ref[...]
    k_scale = RET_HEAD_DIM ** -0.5
    for hd in range(RET_HEADS):
        lo = hd * RET_HEAD_DIM
        q = p[:, lo:lo + RET_HEAD_DIM]
        k = p[:, RET_WIDTH + lo:RET_WIDTH + lo + RET_HEAD_DIM]
        q = q * cos2 + pltpu.roll(q, RET_HEAD_DIM // 2, axis=1) * sin2
        k = k * cos2 + pltpu.roll(k, RET_HEAD_DIM // 2, axis=1) * sin2
        rq_ref[:, lo:lo + RET_HEAD_DIM] = q.astype(rq_ref.dtype)
        rk_ref[:, lo:lo + RET_HEAD_DIM] = (k * k_scale).astype(rk_ref.dtype)
    rv_ref[...] = p[:, 2 * RET_WIDTH:3 * RET_WIDTH].astype(rv_ref.dtype)
    rg_ref[...] = p[:, 3 * RET_WIDTH:4 * RET_WIDTH]
    base = 4 * RET_WIDTH
    seg = seg_ref[...]

    def head_norm(v, gain):
        hi, lo = _split_bf16(v * v)
        group = seg.shape[0]
        ms = jnp.concatenate(
            [_dot(hi[:, g:g + group], seg) + _dot(lo[:, g:g + group], seg)
             for g in range(0, SB_WIDTH, group)], axis=1) * (1.0 / SB_HEAD_DIM)
        return v * lax.rsqrt(ms + EPS) * gain

    sq = head_norm(p[:, base:base + SB_WIDTH], qg_ref[...])
    sk = head_norm(p[:, base + SB_WIDTH:base + 2 * SB_WIDTH], kg_ref[...])
    sq_ref[...] = (sq * (SB_HEAD_DIM ** -0.5)).astype(sq_ref.dtype)
    sk_ref[...] = sk.astype(sk_ref.dtype)
    sv_ref[...] = p[:, base + 2 * SB_WIDTH:base + 3 * SB_WIDTH].astype(sv_ref.dtype)


def _rope_tables(seq):
    half = RET_HEAD_DIM // 2
    inv = ROPE_BASE ** (-np.arange(half, dtype=np.float64) / half)
    n_a = seq // LANES
    ang_a = (np.arange(n_a, dtype=np.float64) * LANES)[:, None] * inv[None, :]
    ang_b = np.arange(LANES, dtype=np.float64)[:, None] * inv[None, :]
    ca, sa = jnp.asarray(np.cos(ang_a), F32), jnp.asarray(np.sin(ang_a), F32)
    cb, sb = jnp.asarray(np.cos(ang_b), F32), jnp.asarray(np.sin(ang_b), F32)
    cos = (ca[:, None, :] * cb[None] - sa[:, None, :] * sb[None]).reshape(seq, half)
    sin = (sa[:, None, :] * cb[None] + ca[:, None, :] * sb[None]).reshape(seq, half)
    return jnp.concatenate([cos, cos], axis=-1), jnp.concatenate([-sin, sin], axis=-1)


def _in_proj(x2, norm_g, w_in, sb_q_g, sb_k_g, seq, tm):
    t = x2.shape[0]
    assert seq % tm == 0 and t % tm == 0
    cos2, sin2 = _rope_tables(seq)
    seg = np.kron(np.eye(MXU_WIDTH // SB_HEAD_DIM), np.ones((SB_HEAD_DIM, SB_HEAD_DIM)))
    seg = jnp.asarray(seg, BF16)
    qg = jnp.tile(sb_q_g.astype(F32), SB_HEADS)[None, :]
    kg = jnp.tile(sb_k_g.astype(F32), SB_HEADS)[None, :]
    n_pos = seq // tm
    row = lambda w: pl.BlockSpec((tm, w), lambda i: (i, 0))
    full = lambda a: pl.BlockSpec(a.shape, lambda i: (0,) * a.ndim)
    pos = pl.BlockSpec((tm, RET_HEAD_DIM), lambda i: (i % n_pos, 0))
    w_bf = w_in.astype(BF16)
    g2 = norm_g.astype(F32)[None, :]
    out = lambda dt: jax.ShapeDtypeStruct((t, RET_WIDTH), dt)
    return pl.pallas_call(
        _in_proj_kernel,
        grid=(t // tm,),
        in_specs=[row(D_MODEL), full(g2), full(w_bf), pos, pos, full(qg), full(kg), full(seg)],
        out_specs=[row(RET_WIDTH)] * 7,
        out_shape=[out(BF16), out(F32), out(BF16), out(F32), out(BF16), out(BF16), out(BF16)],
        compiler_params=pltpu.CompilerParams(
            dimension_semantics=("arbitrary",), vmem_limit_bytes=VMEM_LIMIT),
        name="in_proj",
    )(x2, g2, w_bf, cos2, sin2, qg, kg, seg)


def _retention_kernel(q_ref, k_ref, v_ref, g_ref, ng_ref, intra_ref, qd_ref, kd_ref, cd_ref,
                      o_ref, state_ref, *, chunks):
    @pl.when(pl.program_id(1) == 0)
    def _():
        state_ref[...] = jnp.zeros_like(state_ref)

    def chunk(c, carry):
        rows = pl.ds(pl.multiple_of(c * RET_CHUNK, RET_CHUNK), RET_CHUNK)
        for hd in range(RET_HEADS):
            cols = slice(hd * RET_HEAD_DIM, (hd + 1) * RET_HEAD_DIM)
            q = q_ref[rows, cols]
            k = k_ref[rows, cols]
            v = v_ref[rows, cols]
            state = state_ref[hd]
            scores = _dot_nt(q, k.astype(BF16)) * intra_ref[hd]
            inner = _dot(scores.astype(BF16), v)
            cross = _dot(q, state.astype(BF16)) * qd_ref[:, cols]
            kd = (k * kd_ref[:, cols]).astype(BF16)
            state_ref[hd] = state * cd_ref[:, cols] + _dot(kd.T, v)
            o = inner + cross
            o = o * lax.rsqrt(jnp.mean(o * o, axis=-1, keepdims=True) + EPS) * ng_ref[:, cols]
            gate = g_ref[rows, cols]
            o_ref[rows, cols] = (o * (gate * jax.nn.sigmoid(gate))).astype(o_ref.dtype)
        return carry

    lax.fori_loop(0, chunks, chunk, 0, unroll=True)


def _retention(rq, rk, rv, rg, ret_norm_g, batch, seq, rows):
    assert seq % rows == 0 and rows % RET_CHUNK == 0
    log_gamma = np.log(1.0 - 2.0 ** (-5.0 - np.arange(RET_HEADS, dtype=np.float64)))
    idx = np.arange(RET_CHUNK, dtype=np.float64)
    diff = idx[:, None] - idx[None, :]
    intra = np.where(diff >= 0, np.exp(log_gamma[:, None, None] * np.maximum(diff, 0.0)), 0.0)
    q_decay = np.exp(log_gamma[:, None] * (idx + 1.0))
    k_decay = np.exp(log_gamma[:, None] * (RET_CHUNK - 1.0 - idx))
    chunk_decay = np.exp(log_gamma * RET_CHUNK)
    lane_rep = lambda a: np.repeat(a.T, RET_HEAD_DIM, axis=1)
    intra = jnp.asarray(intra, F32)
    qd = jnp.asarray(lane_rep(q_decay), F32)
    kd = jnp.asarray(lane_rep(k_decay), F32)
    cd = jnp.asarray(np.repeat(chunk_decay, RET_HEAD_DIM)[None, :], F32)
    ng = ret_norm_g.astype(F32).reshape(1, RET_WIDTH)
    n_r = seq // rows
    blk = pl.BlockSpec((rows, RET_WIDTH), lambda b, r: (b * n_r + r, 0))
    full = lambda a: pl.BlockSpec(a.shape, lambda b, r: (0,) * a.ndim)
    return pl.pallas_call(
        functools.partial(_retention_kernel, chunks=rows // RET_CHUNK),
        grid=(batch, n_r),
        in_specs=[blk, blk, blk, blk, full(ng), full(intra), full(qd), full(kd), full(cd)],
        out_specs=blk,
        out_shape=jax.ShapeDtypeStruct((batch * seq, RET_WIDTH), BF16),
        scratch_shapes=[pltpu.VMEM((RET_HEADS, RET_HEAD_DIM, RET_HEAD_DIM), F32)],
        compiler_params=pltpu.CompilerParams(
            dimension_semantics=("arbitrary", "arbitrary"), vmem_limit_bytes=VMEM_LIMIT),
        name="retention",
    )(rq, rk, rv, rg, ng, intra, qd, kd, cd)


def _sb_attn_kernel(q_ref, k_ref, v_ref, tri_ref, o_ref, *, blk, halo, heads):
    i = pl.program_id(2)
    lane = lax.broadcasted_iota(jnp.int32, (blk, LANES), 1)
    first_head = lane < SB_HEAD_DIM
    tri_own = tri_ref[...]
    tri_halo = tri_ref[:halo, :halo]
    below_diag = (lax.broadcasted_iota(jnp.int32, (blk, blk), 1)
                  < lax.broadcasted_iota(jnp.int32, (blk, blk), 0))
    has_halo = i > 0

    def scores(qm, kb, mask):
        z = _dot_nt(qm, kb)
        log_beta = jnp.minimum(z, 0.0) - jnp.log(1.0 + jnp.exp(-jnp.abs(z)))
        log_rest = log_beta - z
        if mask is not None:
            log_rest = jnp.where(mask, log_rest, 0.0)
        return log_beta, log_rest

    def later_sums(log_rest, tri):
        hi, lo = _split_bf16(log_rest)
        later = _dot(hi, tri) + _dot(lo, tri)
        return later, later[:, 0:1] + log_rest[:, 0:1]

    def weights(log_beta, log_remaining, mask):
        w = jnp.exp(log_beta + log_remaining)
        if mask is not None:
            w = jnp.where(mask, w, 0.0)
        return w.astype(BF16)

    def head_q(h):
        cols = slice((h // 2) * LANES, (h // 2 + 1) * LANES)
        q = q_ref[:, cols]
        keep = first_head if h % 2 == 0 else jnp.logical_not(first_head)
        return jnp.where(keep, q, jnp.zeros_like(q)), cols

    own_rows = pl.ds(pl.multiple_of(i * blk, blk), blk)
    halo_rows = pl.ds(pl.multiple_of(jnp.maximum(i * blk - halo, 0), halo), halo)

    def first_step(h):
        qm, cols = head_q(h)
        beta_o, rest_o = scores(qm, k_ref[own_rows, cols], below_diag)
        beta_h, rest_h = scores(qm, k_ref[halo_rows, cols], has_halo)
        later_o, total_o = later_sums(rest_o, tri_own)
        later_h, total_h = later_sums(rest_h, tri_halo)
        acc = _dot(weights(beta_o, later_o, below_diag), v_ref[own_rows, cols])
        acc = acc + _dot(weights(beta_h, total_o + later_h, has_halo), v_ref[halo_rows, cols])
        return total_o + total_h, acc

    def chunk_step(h, j, c, acc):
        qm, cols = head_q(h)
        rows = pl.ds(pl.multiple_of(j * halo, halo), halo)
        beta, rest = scores(qm, k_ref[rows, cols], None)
        later, total = later_sums(rest, tri_halo)
        return c + total, acc + _dot(weights(beta, c + later, None), v_ref[rows, cols])

    def cond(carry):
        j, cs, _ = carry
        c_max = functools.reduce(jnp.maximum, cs)
        return jnp.logical_and(j >= 0, jnp.max(c_max) > F32_EXP_UNDERFLOW)

    def body(carry):
        j, cs, accs = carry
        out = [chunk_step(h, j, cs[h], accs[h]) for h in range(heads)]
        return j - 1, tuple(o[0] for o in out), tuple(o[1] for o in out)

    first = [first_step(h) for h in range(heads)]
    init = (i * (blk // halo) - 2, tuple(f[0] for f in first), tuple(f[1] for f in first))
    _, _, accs = lax.while_loop(cond, body, init)
    for p in range(heads // 2):
        o_ref[:, p * LANES:(p + 1) * LANES] = jnp.where(
            first_head, accs[2 * p], accs[2 * p + 1]).astype(o_ref.dtype)


def _sb_attention(sq, sk, sv, batch, seq, blk, halo, heads):
    assert seq % blk == 0 and blk % halo == 0 and halo == LANES
    assert heads % 2 == 0 and SB_HEADS % heads == 0
    nq = seq // blk
    width = heads * SB_HEAD_DIM
    groups = SB_WIDTH // width
    j = np.arange(blk)
    tri = jnp.asarray((j[:, None] > j[None, :]).astype(np.float32), BF16)
    qblk = pl.BlockSpec((blk, width), lambda b, g, i: (b * nq + i, g))
    kvblk = pl.BlockSpec((seq, width), lambda b, g, i: (b, g))
    return pl.pallas_call(
        functools.partial(_sb_attn_kernel, blk=blk, halo=halo, heads=heads),
        grid=(batch, groups, nq),
        in_specs=[qblk, kvblk, kvblk, pl.BlockSpec(tri.shape, lambda b, g, i: (0, 0))],
        out_specs=qblk,
        out_shape=jax.ShapeDtypeStruct((batch * seq, SB_WIDTH), BF16),
        compiler_params=pltpu.CompilerParams(
            dimension_semantics=("arbitrary", "arbitrary", "arbitrary"),
            vmem_limit_bytes=VMEM_LIMIT),
        name="sb_attn",
    )(sq, sk, sv, tri)


def _out_proj_kernel(ret_ref, sb_ref, x_ref, wtop_ref, wbot_ref, g_ref, wrh_ref, wrl_ref,
                     br_ref, tri_ref,
                     x1_ref, xf_ref, idx_ref, w_ref, rank_ref, cnt_ref, run_ref):
    @pl.when(pl.program_id(0) == 0)
    def _():
        run_ref[...] = jnp.zeros_like(run_ref)

    x1 = x_ref[...] + _dot(ret_ref[...], wtop_ref[...]) + _dot(sb_ref[...], wbot_ref[...])
    x1_ref[...] = x1
    xf = x1 * lax.rsqrt(jnp.mean(x1 * x1, axis=-1, keepdims=True) + EPS) * g_ref[...]
    _store_token_tiles(xf_ref, xf)
    xh, xl = _split_bf16(xf)
    wrh = wrh_ref[...]
    logits = _dot_nt(wrh, xh) + _dot_nt(wrh, xl) + _dot_nt(wrl_ref[...], xh) + br_ref[...]
    n_e, tm = logits.shape
    e_iota = lax.broadcasted_iota(jnp.int32, (n_e, tm), 0)
    cur = logits
    tops, sels, hots = [], [], []
    for _ in range(TOP_K):
        m = jnp.max(cur, axis=0, keepdims=True)
        sel = jnp.min(jnp.where(cur == m, e_iota, n_e), axis=0, keepdims=True)
        hot = e_iota == sel
        cur = jnp.where(hot, -jnp.inf, cur)
        tops.append(m)
        sels.append(sel)
        hots.append(hot)
    ps = [jnp.exp(m - tops[0]) for m in tops]
    denom = ps[0] + ps[1] + ps[2] + ps[3]
    chosen = jnp.zeros((n_e, tm), F32)
    for hot in hots:
        chosen = chosen + hot.astype(F32)
    before = run_ref[:, 0:1] + _dot(chosen.astype(BF16), tri_ref[...])
    for kk in range(TOP_K):
        idx_ref[kk:kk + 1, :] = sels[kk]
        w_ref[kk:kk + 1, :] = ps[kk] / denom
        rank = jnp.sum(jnp.where(hots[kk], before, 0.0), axis=0, keepdims=True)
        rank_ref[kk:kk + 1, :] = rank.astype(jnp.int32)
    run = run_ref[...] + jnp.sum(chosen, axis=1, keepdims=True)
    run_ref[...] = run
    cnt_ref[...] = run.astype(jnp.int32)


def _out_proj_route(ret, sb, x2, w_out, ffn_g, w_router, b_router, tm):
    t = x2.shape[0]
    assert t % tm == 0
    w_bf = w_out.astype(BF16)
    wtop, wbot = w_bf[:RET_WIDTH], w_bf[RET_WIDTH:]
    g2 = ffn_g.astype(F32)[None, :]
    wr_t = w_router.astype(F32).T
    wrh = wr_t.astype(BF16)
    wrl = (wr_t - wrh.astype(F32)).astype(BF16)
    br = b_router.astype(F32)[:, None]
    tt = np.arange(tm)
    tri = jnp.asarray((tt[:, None] < tt[None, :]).astype(np.float32), BF16)
    row = lambda w: pl.BlockSpec((tm, w), lambda i: (i, 0))
    full = lambda a: pl.BlockSpec(a.shape, lambda i: (0,) * a.ndim)
    col = pl.BlockSpec((TOP_K, tm), lambda i: (0, i))
    return pl.pallas_call(
        _out_proj_kernel,
        grid=(t // tm,),
        in_specs=[row(RET_WIDTH), row(SB_WIDTH), row(D_MODEL), full(wtop), full(wbot), full(g2),
                  full(wrh), full(wrl), full(br), full(tri)],
        out_specs=[row(D_MODEL), pl.BlockSpec((tm * SUBLANES, LANES), lambda i: (i, 0)),
                   col, col, col, pl.BlockSpec((N_EXPERTS, LANES), lambda i: (0, 0))],
        out_shape=[jax.ShapeDtypeStruct((t, D_MODEL), F32),
                   jax.ShapeDtypeStruct((t * SUBLANES, LANES), F32),
                   jax.ShapeDtypeStruct((TOP_K, t), jnp.int32),
                   jax.ShapeDtypeStruct((TOP_K, t), F32),
                   jax.ShapeDtypeStruct((TOP_K, t), jnp.int32),
                   jax.ShapeDtypeStruct((N_EXPERTS, LANES), jnp.int32)],
        scratch_shapes=[pltpu.VMEM((N_EXPERTS, LANES), F32)],
        compiler_params=pltpu.CompilerParams(
            dimension_semantics=("arbitrary",), vmem_limit_bytes=VMEM_LIMIT),
        name="out_proj_route",
    )(ret, sb, x2, wtop, wbot, g2, wrh, wrl, br, tri)


def _dest_kernel(pstart_ref, idx_ref, rank_ref, dest_ref):
    idx = idx_ref[...]
    dest = rank_ref[...]
    for e in range(N_EXPERTS):
        dest = dest + jnp.where(idx == e, pstart_ref[e], 0)
    dest_ref[...] = dest


def _dest(padded_start, top_idx, rank, tn):
    t = top_idx.shape[1]
    assert t % tn == 0
    col = pl.BlockSpec((TOP_K, tn), lambda i, ps: (0, i))
    return pl.pallas_call(
        _dest_kernel,
        grid_spec=pltpu.PrefetchScalarGridSpec(
            num_scalar_prefetch=1, grid=(t // tn,), in_specs=[col, col], out_specs=col),
        out_shape=jax.ShapeDtypeStruct((TOP_K, t), jnp.int32),
        name="dest",
    )(padded_start, top_idx, rank)


def _dispatch_kernel(pend_ref, dest_ref, xf_ref, slots_hbm, zbuf, sem, zsem, *, tm, blk):
    @pl.when(pl.program_id(0) == 0)
    def _():
        zbuf[...] = jnp.zeros_like(zbuf)

        def tail_copy(e):
            first = pl.multiple_of((pend_ref[e] - blk) * SUBLANES, blk * SUBLANES)
            return pltpu.make_async_copy(
                zbuf, slots_hbm.at[pl.ds(first, blk * SUBLANES), :], zsem)

        def nonempty(e):
            return pend_ref[e] > (pend_ref[e - 1] if e else 0)

        for e in range(N_EXPERTS):
            pl.when(nonempty(e))(lambda e=e: tail_copy(e).start())
        for e in range(N_EXPERTS):
            pl.when(nonempty(e))(lambda e=e: tail_copy(e).wait())

    def row_copy(t, kk):
        return pltpu.make_async_copy(
            xf_ref.at[_token_rows(t), :],
            slots_hbm.at[_token_rows(dest_ref[kk, t]), :], sem)

    def start(t, carry):
        for kk in range(TOP_K):
            row_copy(t, kk).start(priority=kk % 2)
        return carry

    lax.fori_loop(0, tm, start, 0, unroll=8)
    for kk in range(TOP_K):
        pltpu.make_async_copy(
            xf_ref, slots_hbm.at[pl.ds(0, tm * SUBLANES), :], sem).wait()


def _dispatch(padded_end, dest, xf, n_pad, tm, blk):
    t = xf.shape[0] // SUBLANES
    assert t % tm == 0
    return pl.pallas_call(
        functools.partial(_dispatch_kernel, tm=tm, blk=blk),
        grid_spec=pltpu.PrefetchScalarGridSpec(
            num_scalar_prefetch=1,
            grid=(t // tm,),
            in_specs=[pl.BlockSpec((TOP_K, tm), lambda i, pe: (0, i), memory_space=pltpu.SMEM),
                      pl.BlockSpec((tm * SUBLANES, LANES), lambda i, pe: (i, 0))],
            out_specs=pl.BlockSpec(memory_space=pl.ANY),
            scratch_shapes=[pltpu.VMEM((blk * SUBLANES, LANES), F32),
                            pltpu.SemaphoreType.DMA, pltpu.SemaphoreType.DMA],
        ),
        out_shape=jax.ShapeDtypeStruct((n_pad * SUBLANES, LANES), F32),
        compiler_params=pltpu.CompilerParams(
            dimension_semantics=("arbitrary",), vmem_limit_bytes=VMEM_LIMIT,
            has_side_effects=True),
        name="dispatch",
    )(padded_end, dest, xf)


WEIGHT_ROWS = 128


def _experts_kernel(be_ref, nu_ref, x_ref, wgu_ref, wd_ref, bg_ref, bu_ref, bd_ref, y_ref,
                    wg_s, wu_s, wd_s, t_s, *, blk):
    j = pl.program_id(0)
    used = j < nu_ref[0]
    new_expert = jnp.logical_or(j == 0, be_ref[j] != be_ref[jnp.maximum(j - 1, 0)])

    @pl.when(jnp.logical_and(used, new_expert))
    def _():
        def rows_step(r, carry):
            rows = pl.ds(pl.multiple_of(r * WEIGHT_ROWS, WEIGHT_ROWS), WEIGHT_ROWS)
            t_s[...] = wgu_ref[rows, :].T
            wg_s[rows, :] = t_s[pl.ds(0, D_FF, stride=2), :].T.astype(BF16)
            wu_s[rows, :] = t_s[pl.ds(1, D_FF, stride=2), :].T.astype(BF16)
            wd_s[rows, :] = wd_ref[rows, :].astype(BF16)
            return carry

        lax.fori_loop(0, D_MODEL // WEIGHT_ROWS, rows_step, 0)

    @pl.when(used)
    def _():
        xb = _load_token_tiles(x_ref, blk).astype(BF16)
        gate = jnp.minimum(_dot(xb, wg_s[...]) + bg_ref[...], SWIGLU_LIMIT)
        up = jnp.clip(_dot(xb, wu_s[...]) + bu_ref[...], -SWIGLU_LIMIT, SWIGLU_LIMIT)
        hidden = (up + 1.0) * gate * jax.nn.sigmoid(SWIGLU_ALPHA * gate)
        _store_token_tiles(y_ref, _dot(hidden.astype(BF16), wd_s[...]) + bd_ref[...])

    @pl.when(jnp.logical_not(used))
    def _():
        y_ref[...] = jnp.zeros_like(y_ref)


def _experts(block_expert, n_used, slots, wgu, wd, bg, bu, bd, blk):
    assert D_FF == D_MODEL
    n_pad = slots.shape[0] // SUBLANES
    n_blocks = n_pad // blk
    xmap = lambda j, be, nu: (jnp.minimum(j, nu[0] - 1), 0)
    wmap = lambda j, be, nu: (be[j], 0, 0)
    bspec = pl.BlockSpec((None, 1, D_FF), wmap)
    tiles = (blk * SUBLANES, LANES)
    return pl.pallas_call(
        functools.partial(_experts_kernel, blk=blk),
        grid_spec=pltpu.PrefetchScalarGridSpec(
            num_scalar_prefetch=2,
            grid=(n_blocks,),
            in_specs=[pl.BlockSpec(tiles, xmap),
                      pl.BlockSpec((None, D_MODEL, 2 * D_FF), wmap),
                      pl.BlockSpec((None, D_FF, D_MODEL), wmap), bspec, bspec, bspec],
            out_specs=pl.BlockSpec(tiles, lambda j, be, nu: (j, 0)),
            scratch_shapes=[pltpu.VMEM((D_MODEL, D_FF), BF16), pltpu.VMEM((D_MODEL, D_FF), BF16),
                            pltpu.VMEM((D_FF, D_MODEL), BF16),
                            pltpu.VMEM((2 * D_FF, WEIGHT_ROWS), F32)],
        ),
        out_shape=jax.ShapeDtypeStruct((n_pad * SUBLANES, LANES), F32),
        compiler_params=pltpu.CompilerParams(
            dimension_semantics=("arbitrary",), vmem_limit_bytes=VMEM_LIMIT),
        name="experts",
    )(block_expert, n_used, slots, wgu, wd, bg, bu, bd)


def _combine_kernel(dest_ref, next_dest_ref, y_hbm, w_ref, x1_ref, o_ref, buf, sem, *, tm):
    i = pl.program_id(0)
    n = pl.num_programs(0)
    slot = lax.rem(i, 2)

    def gather(dests, into):
        def start(t, carry):
            for kk in range(TOP_K):
                pltpu.make_async_copy(
                    y_hbm.at[_token_rows(dests[kk, t]), :],
                    buf.at[into, kk, _token_rows(t), :], sem.at[into]).start(priority=kk % 2)
            return carry

        lax.fori_loop(0, tm, start, 0, unroll=8)

    pl.when(i == 0)(lambda: gather(dest_ref, 0))
    pl.when(i + 1 < n)(lambda: gather(next_dest_ref, 1 - slot))
    for kk in range(TOP_K):
        pltpu.make_async_copy(
            y_hbm.at[pl.ds(0, tm * SUBLANES), :], buf.at[slot, kk], sem.at[slot]).wait()
    w = w_ref[...]
    for s in range(ROW_TILES):
        cols = slice(s * LANES, (s + 1) * LANES)
        acc = x1_ref[:, cols]
        for kk in range(TOP_K):
            acc = acc + buf[slot, kk, pl.ds(s, tm, stride=SUBLANES), :] * w[:, kk:kk + 1]
        o_ref[:, cols] = acc


def _combine(dest, y_slots, w_t, x1, tm):
    t = x1.shape[0]
    assert t % tm == 0
    n = t // tm
    row = pl.BlockSpec((tm, D_MODEL), lambda i: (i, 0))
    return pl.pallas_call(
        functools.partial(_combine_kernel, tm=tm),
        grid=(n,),
        in_specs=[pl.BlockSpec((TOP_K, tm), lambda i: (0, i), memory_space=pltpu.SMEM),
                  pl.BlockSpec((TOP_K, tm), lambda i: (0, jnp.minimum(i + 1, n - 1)),
                               memory_space=pltpu.SMEM),
                  pl.BlockSpec(memory_space=pl.ANY),
                  pl.BlockSpec((tm, TOP_K), lambda i: (i, 0)),
                  row],
        out_specs=row,
        out_shape=jax.ShapeDtypeStruct((t, D_MODEL), F32),
        scratch_shapes=[pltpu.VMEM((2, TOP_K, tm * SUBLANES, LANES), F32),
                        pltpu.SemaphoreType.DMA((2,))],
        compiler_params=pltpu.CompilerParams(
            dimension_semantics=("arbitrary",), vmem_limit_bytes=VMEM_LIMIT),
        name="combine",
    )(dest, dest, y_slots, w_t, x1)


def _tiles(batch, seq):
    return dict(
        in_proj=min(512, seq),
        retention=min(1024, seq),
        sb_block=min(256, seq),
        sb_heads=4,
        out_proj=min(1024, seq),
        dest=min(8192, batch * seq),
        dispatch=min(2048, seq),
        combine=min(512, seq),
        expert_block=512,
    )


def _layer(x, attn_norm_g, w_in, ret_norm_g, sb_q_norm_g, sb_k_norm_g, w_out,
           ffn_norm_g, w_router, b_router, w_gate_up, b_gate_up, w_down, b_down):
    batch, seq, d = x.shape
    t = batch * seq
    tiles = _tiles(batch, seq)
    x2 = x.reshape(t, d)

    rq, rk, rv, rg, sq, sk, sv = _in_proj(
        x2, attn_norm_g, w_in, sb_q_norm_g, sb_k_norm_g, seq, tiles["in_proj"])
    ret = _retention(rq, rk, rv, rg, ret_norm_g, batch, seq, tiles["retention"])
    sb = _sb_attention(sq, sk, sv, batch, seq, tiles["sb_block"], LANES, tiles["sb_heads"])
    x1, xf, top_idx, top_w, rank, counts = _out_proj_route(
        ret, sb, x2, w_out, ffn_norm_g, w_router, b_router, tiles["out_proj"])

    blk = tiles["expert_block"]
    n_pad = t * TOP_K + N_EXPERTS * blk
    n_blocks = n_pad // blk
    counts = counts[:, 0]
    padded = (counts + blk - 1) // blk * blk
    padded_end = jnp.cumsum(padded)
    padded_start = padded_end - padded
    block_first = jnp.arange(n_blocks, dtype=jnp.int32) * blk
    block_expert = jnp.minimum(
        jnp.sum(padded_end[None, :] <= block_first[:, None], axis=1), N_EXPERTS - 1
    ).astype(jnp.int32)
    n_used = (padded_end[-1:] // blk).astype(jnp.int32)

    dest = _dest(padded_start, top_idx, rank, tiles["dest"])
    slots = _dispatch(padded_end, dest, xf, n_pad, tiles["dispatch"], blk)

    bgu = b_gate_up.astype(F32).reshape(N_EXPERTS, 1, D_FF, 2)
    y_slots = _experts(block_expert, n_used, slots, w_gate_up.astype(F32), w_down.astype(F32),
                       bgu[..., 0], bgu[..., 1], b_down.astype(F32)[:, None, :], blk)

    out = _combine(dest, y_slots, top_w.T, x1, tiles["combine"])
    return out.reshape(batch, seq, d)


def kernel(x, attn_norm_g, w_in, ret_norm_g, sb_q_norm_g, sb_k_norm_g, w_out, ffn_norm_g,
           w_router, b_router, w_gate_up, b_gate_up, w_down, b_down):
    depth = attn_norm_g.shape[0]
    for l in range(depth):
        x = _layer(x, attn_norm_g[l], w_in[l], ret_norm_g[l], sb_q_norm_g[l], sb_k_norm_g[l],
                   w_out[l], ffn_norm_g[l], w_router[l], b_router[l], w_gate_up[l],
                   b_gate_up[l], w_down[l], b_down[l])
    return x
```

```python
import functools

import numpy as np
import jax
import jax.numpy as jnp
from jax import lax
from jax.experimental import pallas as pl
from jax.experimental.pallas import tpu as pltpu

D_MODEL = 1024
RET_WIDTH = 512
RET_HEADS = 4
RET_HEAD_DIM = 128
SB_WIDTH = 512
SB_HEADS = 8
SB_HEAD_DIM = 64
IN_PROJ_WIDTH = 4 * RET_WIDTH + 3 * SB_WIDTH
RET_CHUNK = 128
ROPE_BASE = 10000.0
N_EXPERTS = 32
TOP_K = 4
D_FF = D_MODEL
SWIGLU_LIMIT = 7.0
SWIGLU_ALPHA = 1.702
EPS = 1e-6

LANES = 128
MXU_WIDTH = 256
F32_EXP_UNDERFLOW = -88.0
VMEM_LIMIT = 56 * 1024 * 1024

BF16 = jnp.bfloat16
F32 = jnp.float32


def _split_bf16(v):
    hi = v.astype(BF16)
    lo = (v - hi.astype(F32)).astype(BF16)
    return hi, lo


SUBLANES = 8
ROW_TILES = D_MODEL // LANES
assert ROW_TILES == SUBLANES


def _store_token_tiles(ref, value):
    n = value.shape[0]
    for s in range(ROW_TILES):
        ref[pl.ds(s, n, stride=SUBLANES), :] = value[:, s * LANES:(s + 1) * LANES]


def _load_token_tiles(ref, n):
    return jnp.concatenate(
        [ref[pl.ds(s, n, stride=SUBLANES), :] for s in range(ROW_TILES)], axis=1)


def _token_rows(i):
    return pl.ds(pl.multiple_of(i * SUBLANES, SUBLANES), SUBLANES)


def _dot(a, b):
    return jnp.dot(a, b, preferred_element_type=F32)


def _dot_nt(a, b):
    return lax.dot_general(a, b, (((1,), (1,)), ((), ())), preferred_element_type=F32)


def _in_proj_kernel(x_ref, g_ref, w_ref, cos_ref, sin_ref, qg_ref, kg_ref, seg_ref,
                    rq_ref, rk_ref, rv_ref, rg_ref, sq_ref, sk_ref, sv_ref):
    x = x_ref[...]
    h = x * lax.rsqrt(jnp.mean(x * x, axis=-1, keepdims=True) + EPS) * g_ref[...]
    p = _dot(h.astype(BF16), w_ref[...])
    cos2 = cos_ref[...]
    sin2 = sin_ref[...]
    k_scale = RET_HEAD_DIM ** -0.5
    for hd in range(RET_HEADS):
        lo = hd * RET_HEAD_DIM
        q = p[:, lo:lo + RET_HEAD_DIM]
        k = p[:, RET_WIDTH + lo:RET_WIDTH + lo + RET_HEAD_DIM]
        q = q * cos2 + pltpu.roll(q, RET_HEAD_DIM // 2, axis=1) * sin2
        k = k * cos2 + pltpu.roll(k, RET_HEAD_DIM // 2, axis=1) * sin2
        rq_ref[:, lo:lo + RET_HEAD_DIM] = q.astype(rq_ref.dtype)
        rk_ref[:, lo:lo + RET_HEAD_DIM] = (k * k_scale).astype(rk_ref.dtype)
    rv_ref[...] = p[:, 2 * RET_WIDTH:3 * RET_WIDTH].astype(rv_ref.dtype)
    rg_ref[...] = p[:, 3 * RET_WIDTH:4 * RET_WIDTH]
    base = 4 * RET_WIDTH
    seg = seg_ref[...]

    def head_norm(v, gain):
        hi, lo = _split_bf16(v * v)
        group = seg.shape[0]
        ms = jnp.concatenate(
            [_dot(hi[:, g:g + group], seg) + _dot(lo[:, g:g + group], seg)
             for g in range(0, SB_WIDTH, group)], axis=1) * (1.0 / SB_HEAD_DIM)
        return v * lax.rsqrt(ms + EPS) * gain

    sq = head_norm(p[:, base:base + SB_WIDTH], qg_ref[...])
    sk = head_norm(p[:, base + SB_WIDTH:base + 2 * SB_WIDTH], kg_ref[...])
    sq_ref[...] = (sq * (SB_HEAD_DIM ** -0.5)).astype(sq_ref.dtype)
    sk_ref[...] = sk.astype(sk_ref.dtype)
    sv_ref[...] = p[:, base + 2 * SB_WIDTH:base + 3 * SB_WIDTH].astype(sv_ref.dtype)


def _rope_tables(seq):
    half = RET_HEAD_DIM // 2
    inv = ROPE_BASE ** (-np.arange(half, dtype=np.float64) / half)
    n_a = seq // LANES
    ang_a = (np.arange(n_a, dtype=np.float64) * LANES)[:, None] * inv[None, :]
    ang_b = np.arange(LANES, dtype=np.float64)[:, None] * inv[None, :]
    ca, sa = jnp.asarray(np.cos(ang_a), F32), jnp.asarray(np.sin(ang_a), F32)
    cb, sb = jnp.asarray(np.cos(ang_b), F32), jnp.asarray(np.sin(ang_b), F32)
    cos = (ca[:, None, :] * cb[None] - sa[:, None, :] * sb[None]).reshape(seq, half)
    sin = (sa[:, None, :] * cb[None] + ca[:, None, :] * sb[None]).reshape(seq, half)
    return jnp.concatenate([cos, cos], axis=-1), jnp.concatenate([-sin, sin], axis=-1)


def _in_proj(x2, norm_g, w_in, sb_q_g, sb_k_g, seq, tm):
    t = x2.shape[0]
    assert seq % tm == 0 and t % tm == 0
    cos2, sin2 = _rope_tables(seq)
    seg = np.kron(np.eye(MXU_WIDTH // SB_HEAD_DIM), np.ones((SB_HEAD_DIM, SB_HEAD_DIM)))
    seg = jnp.asarray(seg, BF16)
    qg = jnp.tile(sb_q_g.astype(F32), SB_HEADS)[None, :]
    kg = jnp.tile(sb_k_g.astype(F32), SB_HEADS)[None, :]
    n_pos = seq // tm
    row = lambda w: pl.BlockSpec((tm, w), lambda i: (i, 0))
    full = lambda a: pl.BlockSpec(a.shape, lambda i: (0,) * a.ndim)
    pos = pl.BlockSpec((tm, RET_HEAD_DIM), lambda i: (i % n_pos, 0))
    w_bf = w_in.astype(BF16)
    g2 = norm_g.astype(F32)[None, :]
    out = lambda dt: jax.ShapeDtypeStruct((t, RET_WIDTH), dt)
    return pl.pallas_call(
        _in_proj_kernel,
        grid=(t // tm,),
        in_specs=[row(D_MODEL), full(g2), full(w_bf), pos, pos, full(qg), full(kg), full(seg)],
        out_specs=[row(RET_WIDTH)] * 7,
        out_shape=[out(BF16), out(F32), out(BF16), out(F32), out(BF16), out(BF16), out(BF16)],
        compiler_params=pltpu.CompilerParams(
            dimension_semantics=("arbitrary",), vmem_limit_bytes=VMEM_LIMIT),
        name="in_proj",
    )(x2, g2, w_bf, cos2, sin2, qg, kg, seg)


def _retention_kernel(q_ref, k_ref, v_ref, g_ref, ng_ref, intra_ref, qd_ref, kd_ref, cd_ref,
                      o_ref, state_ref, *, chunks):
    @pl.when(pl.program_id(1) == 0)
    def _():
        state_ref[...] = jnp.zeros_like(state_ref)

    def chunk(c, carry):
        rows = pl.ds(pl.multiple_of(c * RET_CHUNK, RET_CHUNK), RET_CHUNK)
        for hd in range(RET_HEADS):
            cols = slice(hd * RET_HEAD_DIM, (hd + 1) * RET_HEAD_DIM)
            q = q_ref[rows, cols]
            k = k_ref[rows, cols]
            v = v_ref[rows, cols]
            state = state_ref[hd]
            scores = _dot_nt(q, k.astype(BF16)) * intra_ref[hd]
            inner = _dot(scores.astype(BF16), v)
            cross = _dot(q, state.astype(BF16)) * qd_ref[:, cols]
            kd = (k * kd_ref[:, cols]).astype(BF16)
            state_ref[hd] = state * cd_ref[:, cols] + _dot(kd.T, v)
            o = inner + cross
            o = o * lax.rsqrt(jnp.mean(o * o, axis=-1, keepdims=True) + EPS) * ng_ref[:, cols]
            gate = g_ref[rows, cols]
            o_ref[rows, cols] = (o * (gate * jax.nn.sigmoid(gate))).astype(o_ref.dtype)
        return carry

    lax.fori_loop(0, chunks, chunk, 0, unroll=True)


def _retention(rq, rk, rv, rg, ret_norm_g, batch, seq, rows):
    assert seq % rows == 0 and rows % RET_CHUNK == 0
    log_gamma = np.log(1.0 - 2.0 ** (-5.0 - np.arange(RET_HEADS, dtype=np.float64)))
    idx = np.arange(RET_CHUNK, dtype=np.float64)
    diff = idx[:, None] - idx[None, :]
    intra = np.where(diff >= 0, np.exp(log_gamma[:, None, None] * np.maximum(diff, 0.0)), 0.0)
    q_decay = np.exp(log_gamma[:, None] * (idx + 1.0))
    k_decay = np.exp(log_gamma[:, None] * (RET_CHUNK - 1.0 - idx))
    chunk_decay = np.exp(log_gamma * RET_CHUNK)
    lane_rep = lambda a: np.repeat(a.T, RET_HEAD_DIM, axis=1)
    intra = jnp.asarray(intra, F32)
    qd = jnp.asarray(lane_rep(q_decay), F32)
    kd = jnp.asarray(lane_rep(k_decay), F32)
    cd = jnp.asarray(np.repeat(chunk_decay, RET_HEAD_DIM)[None, :], F32)
    ng = ret_norm_g.astype(F32).reshape(1, RET_WIDTH)
    n_r = seq // rows
    blk = pl.BlockSpec((rows, RET_WIDTH), lambda b, r: (b * n_r + r, 0))
    full = lambda a: pl.BlockSpec(a.shape, lambda b, r: (0,) * a.ndim)
    return pl.pallas_call(
        functools.partial(_retention_kernel, chunks=rows // RET_CHUNK),
        grid=(batch, n_r),
        in_specs=[blk, blk, blk, blk, full(ng), full(intra), full(qd), full(kd), full(cd)],
        out_specs=blk,
        out_shape=jax.ShapeDtypeStruct((batch * seq, RET_WIDTH), BF16),
        scratch_shapes=[pltpu.VMEM((RET_HEADS, RET_HEAD_DIM, RET_HEAD_DIM), F32)],
        compiler_params=pltpu.CompilerParams(
            dimension_semantics=("arbitrary", "arbitrary"), vmem_limit_bytes=VMEM_LIMIT),
        name="retention",
    )(rq, rk, rv, rg, ng, intra, qd, kd, cd)


def _sb_attn_kernel(q_ref, k_ref, v_ref, tri_ref, o_ref, *, blk, halo, heads):
    i = pl.program_id(2)
    lane = lax.broadcasted_iota(jnp.int32, (blk, LANES), 1)
    first_head = lane < SB_HEAD_DIM
    tri_own = tri_ref[...]
    tri_halo = tri_ref[:halo, :halo]
    below_diag = (lax.broadcasted_iota(jnp.int32, (blk, blk), 1)
                  < lax.broadcasted_iota(jnp.int32, (blk, blk), 0))
    has_halo = i > 0

    def scores(qm, kb, mask):
        z = _dot_nt(qm, kb)
        log_beta = jnp.minimum(z, 0.0) - jnp.log(1.0 + jnp.exp(-jnp.abs(z)))
        log_rest = log_beta - z
        if mask is not None:
            log_rest = jnp.where(mask, log_rest, 0.0)
        return log_beta, log_rest

    def later_sums(log_rest, tri):
        hi, lo = _split_bf16(log_rest)
        later = _dot(hi, tri) + _dot(lo, tri)
        return later, later[:, 0:1] + log_rest[:, 0:1]

    def weights(log_beta, log_remaining, mask):
        w = jnp.exp(log_beta + log_remaining)
        if mask is not None:
            w = jnp.where(mask, w, 0.0)
        return w.astype(BF16)

    def head_q(h):
        cols = slice((h // 2) * LANES, (h // 2 + 1) * LANES)
        q = q_ref[:, cols]
        keep = first_head if h % 2 == 0 else jnp.logical_not(first_head)
        return jnp.where(keep, q, jnp.zeros_like(q)), cols

    own_rows = pl.ds(pl.multiple_of(i * blk, blk), blk)
    halo_rows = pl.ds(pl.multiple_of(jnp.maximum(i * blk - halo, 0), halo), halo)

    def first_step(h):
        qm, cols = head_q(h)
        beta_o, rest_o = scores(qm, k_ref[own_rows, cols], below_diag)
        beta_h, rest_h = scores(qm, k_ref[halo_rows, cols], has_halo)
        later_o, total_o = later_sums(rest_o, tri_own)
        later_h, total_h = later_sums(rest_h, tri_halo)
        acc = _dot(weights(beta_o, later_o, below_diag), v_ref[own_rows, cols])
        acc = acc + _dot(weights(beta_h, total_o + later_h, has_halo), v_ref[halo_rows, cols])
        return total_o + total_h, acc

    def chunk_step(h, j, c, acc):
        qm, cols = head_q(h)
        rows = pl.ds(pl.multiple_of(j * halo, halo), halo)
        beta, rest = scores(qm, k_ref[rows, cols], None)
        later, total = later_sums(rest, tri_halo)
        return c + total, acc + _dot(weights(beta, c + later, None), v_ref[rows, cols])

    def cond(carry):
        j, cs, _ = carry
        c_max = functools.reduce(jnp.maximum, cs)
        return jnp.logical_and(j >= 0, jnp.max(c_max) > F32_EXP_UNDERFLOW)

    def body(carry):
        j, cs, accs = carry
        out = [chunk_step(h, j, cs[h], accs[h]) for h in range(heads)]
        return j - 1, tuple(o[0] for o in out), tuple(o[1] for o in out)

    first = [first_step(h) for h in range(heads)]
    init = (i * (blk // halo) - 2, tuple(f[0] for f in first), tuple(f[1] for f in first))
    _, _, accs = lax.while_loop(cond, body, init)
    for p in range(heads // 2):
        o_ref[:, p * LANES:(p + 1) * LANES] = jnp.where(
            first_head, accs[2 * p], accs[2 * p + 1]).astype(o_ref.dtype)


def _sb_attention(sq, sk, sv, batch, seq, blk, halo, heads):
    assert seq % blk == 0 and blk % halo == 0 and halo == LANES
    assert heads % 2 == 0 and SB_HEADS % heads == 0
    nq = seq // blk
    width = heads * SB_HEAD_DIM
    groups = SB_WIDTH // width
    j = np.arange(blk)
    tri = jnp.asarray((j[:, None] > j[None, :]).astype(np.float32), BF16)
    qblk = pl.BlockSpec((blk, width), lambda b, g, i: (b * nq + i, g))
    kvblk = pl.BlockSpec((seq, width), lambda b, g, i: (b, g))
    return pl.pallas_call(
        functools.partial(_sb_attn_kernel, blk=blk, halo=halo, heads=heads),
        grid=(batch, groups, nq),
        in_specs=[qblk, kvblk, kvblk, pl.BlockSpec(tri.shape, lambda b, g, i: (0, 0))],
        out_specs=qblk,
        out_shape=jax.ShapeDtypeStruct((batch * seq, SB_WIDTH), BF16),
        compiler_params=pltpu.CompilerParams(
            dimension_semantics=("arbitrary", "arbitrary", "arbitrary"),
            vmem_limit_bytes=VMEM_LIMIT),
        name="sb_attn",
    )(sq, sk, sv, tri)


def _out_proj_kernel(ret_ref, sb_ref, x_ref, wtop_ref, wbot_ref, g_ref, wrh_ref, wrl_ref,
                     br_ref, tri_ref,
                     x1_ref, xf_ref, idx_ref, w_ref, rank_ref, cnt_ref, run_ref):
    @pl.when(pl.program_id(0) == 0)
    def _():
        run_ref[...] = jnp.zeros_like(run_ref)

    x1 = x_ref[...] + _dot(ret_ref[...], wtop_ref[...]) + _dot(sb_ref[...], wbot_ref[...])
    x1_ref[...] = x1
    xf = x1 * lax.rsqrt(jnp.mean(x1 * x1, axis=-1, keepdims=True) + EPS) * g_ref[...]
    _store_token_tiles(xf_ref, xf)
    xh, xl = _split_bf16(xf)
    wrh = wrh_ref[...]
    logits = _dot_nt(wrh, xh) + _dot_nt(wrh, xl) + _dot_nt(wrl_ref[...], xh) + br_ref[...]
    n_e, tm = logits.shape
    e_iota = lax.broadcasted_iota(jnp.int32, (n_e, tm), 0)
    cur = logits
    tops, sels, hots = [], [], []
    for _ in range(TOP_K):
        m = jnp.max(cur, axis=0, keepdims=True)
        sel = jnp.min(jnp.where(cur == m, e_iota, n_e), axis=0, keepdims=True)
        hot = e_iota == sel
        cur = jnp.where(hot, -jnp.inf, cur)
        tops.append(m)
        sels.append(sel)
        hots.append(hot)
    ps = [jnp.exp(m - tops[0]) for m in tops]
    denom = ps[0] + ps[1] + ps[2] + ps[3]
    chosen = jnp.zeros((n_e, tm), F32)
    for hot in hots:
        chosen = chosen + hot.astype(F32)
    before = run_ref[:, 0:1] + _dot(chosen.astype(BF16), tri_ref[...])
    for kk in range(TOP_K):
        idx_ref[kk:kk + 1, :] = sels[kk]
        w_ref[kk:kk + 1, :] = ps[kk] / denom
        rank = jnp.sum(jnp.where(hots[kk], before, 0.0), axis=0, keepdims=True)
        rank_ref[kk:kk + 1, :] = rank.astype(jnp.int32)
    run = run_ref[...] + jnp.sum(chosen, axis=1, keepdims=True)
    run_ref[...] = run
    cnt_ref[...] = run.astype(jnp.int32)


def _out_proj_route(ret, sb, x2, w_out, ffn_g, w_router, b_router, tm):
    t = x2.shape[0]
    assert t % tm == 0
    w_bf = w_out.astype(BF16)
    wtop, wbot = w_bf[:RET_WIDTH], w_bf[RET_WIDTH:]
    g2 = ffn_g.astype(F32)[None, :]
    wr_t = w_router.astype(F32).T
    wrh = wr_t.astype(BF16)
    wrl = (wr_t - wrh.astype(F32)).astype(BF16)
    br = b_router.astype(F32)[:, None]
    tt = np.arange(tm)
    tri = jnp.asarray((tt[:, None] < tt[None, :]).astype(np.float32), BF16)
    row = lambda w: pl.BlockSpec((tm, w), lambda i: (i, 0))
    full = lambda a: pl.BlockSpec(a.shape, lambda i: (0,) * a.ndim)
    col = pl.BlockSpec((TOP_K, tm), lambda i: (0, i))
    return pl.pallas_call(
        _out_proj_kernel,
        grid=(t // tm,),
        in_specs=[row(RET_WIDTH), row(SB_WIDTH), row(D_MODEL), full(wtop), full(wbot), full(g2),
                  full(wrh), full(wrl), full(br), full(tri)],
        out_specs=[row(D_MODEL), pl.BlockSpec((tm * SUBLANES, LANES), lambda i: (i, 0)),
                   col, col, col, pl.BlockSpec((N_EXPERTS, LANES), lambda i: (0, 0))],
        out_shape=[jax.ShapeDtypeStruct((t, D_MODEL), F32),
                   jax.ShapeDtypeStruct((t * SUBLANES, LANES), F32),
                   jax.ShapeDtypeStruct((TOP_K, t), jnp.int32),
                   jax.ShapeDtypeStruct((TOP_K, t), F32),
                   jax.ShapeDtypeStruct((TOP_K, t), jnp.int32),
                   jax.ShapeDtypeStruct((N_EXPERTS, LANES), jnp.int32)],
        scratch_shapes=[pltpu.VMEM((N_EXPERTS, LANES), F32)],
        compiler_params=pltpu.CompilerParams(
            dimension_semantics=("arbitrary",), vmem_limit_bytes=VMEM_LIMIT),
        name="out_proj_route",
    )(ret, sb, x2, wtop, wbot, g2, wrh, wrl, br, tri)


def _dest_kernel(pstart_ref, idx_ref, rank_ref, dest_ref):
    idx = idx_ref[...]
    dest = rank_ref[...]
    for e in range(N_EXPERTS):
        dest = dest + jnp.where(idx == e, pstart_ref[e], 0)
    dest_ref[...] = dest


def _dest(padded_start, top_idx, rank, tn):
    t = top_idx.shape[1]
    assert t % tn == 0
    col = pl.BlockSpec((TOP_K, tn), lambda i, ps: (0, i))
    return pl.pallas_call(
        _dest_kernel,
        grid_spec=pltpu.PrefetchScalarGridSpec(
            num_scalar_prefetch=1, grid=(t // tn,), in_specs=[col, col], out_specs=col),
        out_shape=jax.ShapeDtypeStruct((TOP_K, t), jnp.int32),
        name="dest",
    )(padded_start, top_idx, rank)


def _dispatch_kernel(pend_ref, dest_ref, xf_ref, slots_hbm, zbuf, sem, zsem, *, tm, blk):
    @pl.when(pl.program_id(0) == 0)
    def _():
        zbuf[...] = jnp.zeros_like(zbuf)

        def tail_copy(e):
            first = pl.multiple_of((pend_ref[e] - blk) * SUBLANES, blk * SUBLANES)
            return pltpu.make_async_copy(
                zbuf, slots_hbm.at[pl.ds(first, blk * SUBLANES), :], zsem)

        def nonempty(e):
            return pend_ref[e] > (pend_ref[e - 1] if e else 0)

        for e in range(N_EXPERTS):
            pl.when(nonempty(e))(lambda e=e: tail_copy(e).start())
        for e in range(N_EXPERTS):
            pl.when(nonempty(e))(lambda e=e: tail_copy(e).wait())

    def row_copy(t, kk):
        return pltpu.make_async_copy(
            xf_ref.at[_token_rows(t), :],
            slots_hbm.at[_token_rows(dest_ref[t * TOP_K + kk]), :], sem)

    def start(t, carry):
        for kk in range(TOP_K):
            row_copy(t, kk).start(priority=kk % 2)
        return carry

    lax.fori_loop(0, tm, start, 0, unroll=8)
    for kk in range(TOP_K):
        pltpu.make_async_copy(
            xf_ref, slots_hbm.at[pl.ds(0, tm * SUBLANES), :], sem).wait()


def _dispatch(padded_end, dest, xf, n_pad, tm, blk):
    t = xf.shape[0] // SUBLANES
    assert t % tm == 0
    return pl.pallas_call(
        functools.partial(_dispatch_kernel, tm=tm, blk=blk),
        grid_spec=pltpu.PrefetchScalarGridSpec(
            num_scalar_prefetch=1,
            grid=(t // tm,),
            in_specs=[pl.BlockSpec((TOP_K * tm,), lambda i, pe: (i,), memory_space=pltpu.SMEM),
                      pl.BlockSpec((tm * SUBLANES, LANES), lambda i, pe: (i, 0))],
            out_specs=pl.BlockSpec(memory_space=pl.ANY),
            scratch_shapes=[pltpu.VMEM((blk * SUBLANES, LANES), F32),
                            pltpu.SemaphoreType.DMA, pltpu.SemaphoreType.DMA],
        ),
        out_shape=jax.ShapeDtypeStruct((n_pad * SUBLANES, LANES), F32),
        compiler_params=pltpu.CompilerParams(
            dimension_semantics=("arbitrary",), vmem_limit_bytes=VMEM_LIMIT,
            has_side_effects=True),
        name="dispatch",
    )(padded_end, dest, xf)


WEIGHT_ROWS = 128


def _experts_kernel(be_ref, nu_ref, x_ref, wgu_ref, wd_ref, bg_ref, bu_ref, bd_ref, y_ref,
                    wg_s, wu_s, wd_s, t_s, *, blk):
    j = pl.program_id(0)
    used = j < nu_ref[0]
    new_expert = jnp.logical_or(j == 0, be_ref[j] != be_ref[jnp.maximum(j - 1, 0)])

    @pl.when(jnp.logical_and(used, new_expert))
    def _():
        def rows_step(r, carry):
            rows = pl.ds(pl.multiple_of(r * WEIGHT_ROWS, WEIGHT_ROWS), WEIGHT_ROWS)
            t_s[...] = wgu_ref[rows, :].T
            wg_s[rows, :] = t_s[pl.ds(0, D_FF, stride=2), :].T.astype(BF16)
            wu_s[rows, :] = t_s[pl.ds(1, D_FF, stride=2), :].T.astype(BF16)
            wd_s[rows, :] = wd_ref[rows, :].astype(BF16)
            return carry

        lax.fori_loop(0, D_MODEL // WEIGHT_ROWS, rows_step, 0)

    @pl.when(used)
    def _():
        xb = _load_token_tiles(x_ref, blk).astype(BF16)
        gate = jnp.minimum(_dot(xb, wg_s[...]) + bg_ref[...], SWIGLU_LIMIT)
        up = jnp.clip(_dot(xb, wu_s[...]) + bu_ref[...], -SWIGLU_LIMIT, SWIGLU_LIMIT)
        hidden = (up + 1.0) * gate * jax.nn.sigmoid(SWIGLU_ALPHA * gate)
        _store_token_tiles(y_ref, _dot(hidden.astype(BF16), wd_s[...]) + bd_ref[...])

    @pl.when(jnp.logical_not(used))
    def _():
        y_ref[...] = jnp.zeros_like(y_ref)


def _experts(block_expert, n_used, slots, wgu, wd, bg, bu, bd, blk):
    assert D_FF == D_MODEL
    n_pad = slots.shape[0] // SUBLANES
    n_blocks = n_pad // blk
    xmap = lambda j, be, nu: (jnp.minimum(j, nu[0] - 1), 0)
    wmap = lambda j, be, nu: (be[j], 0, 0)
    bspec = pl.BlockSpec((None, 1, D_FF), wmap)
    tiles = (blk * SUBLANES, LANES)
    return pl.pallas_call(
        functools.partial(_experts_kernel, blk=blk),
        grid_spec=pltpu.PrefetchScalarGridSpec(
            num_scalar_prefetch=2,
            grid=(n_blocks,),
            in_specs=[pl.BlockSpec(tiles, xmap),
                      pl.BlockSpec((None, D_MODEL, 2 * D_FF), wmap),
                      pl.BlockSpec((None, D_FF, D_MODEL), wmap), bspec, bspec, bspec],
            out_specs=pl.BlockSpec(tiles, lambda j, be, nu: (j, 0)),
            scratch_shapes=[pltpu.VMEM((D_MODEL, D_FF), BF16), pltpu.VMEM((D_MODEL, D_FF), BF16),
                            pltpu.VMEM((D_FF, D_MODEL), BF16),
                            pltpu.VMEM((2 * D_FF, WEIGHT_ROWS), F32)],
        ),
        out_shape=jax.ShapeDtypeStruct((n_pad * SUBLANES, LANES), F32),
        compiler_params=pltpu.CompilerParams(
            dimension_semantics=("arbitrary",), vmem_limit_bytes=VMEM_LIMIT),
        name="experts",
    )(block_expert, n_used, slots, wgu, wd, bg, bu, bd)


def _combine_kernel(dest_ref, next_dest_ref, y_hbm, w_ref, x1_ref, o_ref, buf, sem, *, tm):
    i = pl.program_id(0)
    n = pl.num_programs(0)
    slot = lax.rem(i, 2)

    def gather(dests, into):
        def start(t, carry):
            for kk in range(TOP_K):
                pltpu.make_async_copy(
                    y_hbm.at[_token_rows(dests[t * TOP_K + kk]), :],
                    buf.at[into, kk, _token_rows(t), :], sem.at[into]).start(priority=kk % 2)
            return carry

        lax.fori_loop(0, tm, start, 0, unroll=8)

    pl.when(i == 0)(lambda: gather(dest_ref, 0))
    pl.when(i + 1 < n)(lambda: gather(next_dest_ref, 1 - slot))
    for kk in range(TOP_K):
        pltpu.make_async_copy(
            y_hbm.at[pl.ds(0, tm * SUBLANES), :], buf.at[slot, kk], sem.at[slot]).wait()
    w = w_ref[...]
    for s in range(ROW_TILES):
        cols = slice(s * LANES, (s + 1) * LANES)
        acc = x1_ref[:, cols]
        for kk in range(TOP_K):
            acc = acc + buf[slot, kk, pl.ds(s, tm, stride=SUBLANES), :] * w[:, kk:kk + 1]
        o_ref[:, cols] = acc


def _combine(dest, y_slots, w_t, x1, tm):
    t = x1.shape[0]
    assert t % tm == 0
    n = t // tm
    row = pl.BlockSpec((tm, D_MODEL), lambda i: (i, 0))
    return pl.pallas_call(
        functools.partial(_combine_kernel, tm=tm),
        grid=(n,),
        in_specs=[pl.BlockSpec((TOP_K * tm,), lambda i: (i,), memory_space=pltpu.SMEM),
                  pl.BlockSpec((TOP_K * tm,), lambda i: (jnp.minimum(i + 1, n - 1),),
                               memory_space=pltpu.SMEM),
                  pl.BlockSpec(memory_space=pl.ANY),
                  pl.BlockSpec((tm, TOP_K), lambda i: (i, 0)),
                  row],
        out_specs=row,
        out_shape=jax.ShapeDtypeStruct((t, D_MODEL), F32),
        scratch_shapes=[pltpu.VMEM((2, TOP_K, tm * SUBLANES, LANES), F32),
                        pltpu.SemaphoreType.DMA((2,))],
        compiler_params=pltpu.CompilerParams(
            dimension_semantics=("arbitrary",), vmem_limit_bytes=VMEM_LIMIT),
        name="combine",
    )(dest, dest, y_slots, w_t, x1)


def _tiles(batch, seq):
    return dict(
        in_proj=min(512, seq),
        retention=min(1024, seq),
        sb_block=min(256, seq),
        sb_heads=4,
        out_proj=min(1024, seq),
        dest=min(8192, batch * seq),
        dispatch=min(2048, seq),
        combine=min(512, seq),
        expert_block=512,
    )


def _layer(x, attn_norm_g, w_in, ret_norm_g, sb_q_norm_g, sb_k_norm_g, w_out,
           ffn_norm_g, w_router, b_router, w_gate_up, b_gate_up, w_down, b_down):
    batch, seq, d = x.shape
    t = batch * seq
    tiles = _tiles(batch, seq)
    x2 = x.reshape(t, d)

    rq, rk, rv, rg, sq, sk, sv = _in_proj(
        x2, attn_norm_g, w_in, sb_q_norm_g, sb_k_norm_g, seq, tiles["in_proj"])
    ret = _retention(rq, rk, rv, rg, ret_norm_g, batch, seq, tiles["retention"])
    sb = _sb_attention(sq, sk, sv, batch, seq, tiles["sb_block"], LANES, tiles["sb_heads"])
    x1, xf, top_idx, top_w, rank, counts = _out_proj_route(
        ret, sb, x2, w_out, ffn_norm_g, w_router, b_router, tiles["out_proj"])

    blk = tiles["expert_block"]
    n_pad = t * TOP_K + N_EXPERTS * blk
    n_blocks = n_pad // blk
    counts = counts[:, 0]
    padded = (counts + blk - 1) // blk * blk
    padded_end = jnp.cumsum(padded)
    padded_start = padded_end - padded
    block_first = jnp.arange(n_blocks, dtype=jnp.int32) * blk
    block_expert = jnp.minimum(
        jnp.sum(padded_end[None, :] <= block_first[:, None], axis=1), N_EXPERTS - 1
    ).astype(jnp.int32)
    n_used = (padded_end[-1:] // blk).astype(jnp.int32)

    dest = _dest(padded_start, top_idx, rank, tiles["dest"])
    dest = dest.T.reshape(-1)
    slots = _dispatch(padded_end, dest, xf, n_pad, tiles["dispatch"], blk)

    bgu = b_gate_up.astype(F32).reshape(N_EXPERTS, 1, D_FF, 2)
    y_slots = _experts(block_expert, n_used, slots, w_gate_up.astype(F32), w_down.astype(F32),
                       bgu[..., 0], bgu[..., 1], b_down.astype(F32)[:, None, :], blk)

    out = _combine(dest, y_slots, top_w.T, x1, tiles["combine"])
    return out.reshape(batch, seq, d)


def kernel(x, attn_norm_g, w_in, ret_norm_g, sb_q_norm_g, sb_k_norm_g, w_out, ffn_norm_g,
           w_router, b_router, w_gate_up, b_gate_up, w_down, b_down):
    depth = attn_norm_g.shape[0]
    for l in range(depth):
        x = _layer(x, attn_norm_g[l], w_in[l], ret_norm_g[l], sb_q_norm_g[l], sb_k_norm_g[l],
                   w_out[l], ffn_norm_g[l], w_router[l], b_router[l], w_gate_up[l],
                   b_gate_up[l], w_down[l], b_down[l])
    return x
```

```python
import functools

import numpy as np
import jax
import jax.numpy as jnp
from jax import lax
from jax.experimental import pallas as pl
from jax.experimental.pallas import tpu as pltpu

D_MODEL = 1024
RET_WIDTH = 512
RET_HEADS = 4
RET_HEAD_DIM = 128
SB_WIDTH = 512
SB_HEADS = 8
SB_HEAD_DIM = 64
IN_PROJ_WIDTH = 4 * RET_WIDTH + 3 * SB_WIDTH
RET_CHUNK = 128
ROPE_BASE = 10000.0
N_EXPERTS = 32
TOP_K = 4
D_FF = D_MODEL
SWIGLU_LIMIT = 7.0
SWIGLU_ALPHA = 1.702
EPS = 1e-6

LANES = 128
MXU_WIDTH = 256
F32_EXP_UNDERFLOW = -88.0
VMEM_LIMIT = 56 * 1024 * 1024

BF16 = jnp.bfloat16
F32 = jnp.float32


def _split_bf16(v):
    hi = v.astype(BF16)
    lo = (v - hi.astype(F32)).astype(BF16)
    return hi, lo


SUBLANES = 8
ROW_TILES = D_MODEL // LANES
assert ROW_TILES == SUBLANES


def _store_token_tiles(ref, value):
    n = value.shape[0]
    for s in range(ROW_TILES):
        ref[pl.ds(s, n, stride=SUBLANES), :] = value[:, s * LANES:(s + 1) * LANES]


def _load_token_tiles(ref, n):
    return jnp.concatenate(
        [ref[pl.ds(s, n, stride=SUBLANES), :] for s in range(ROW_TILES)], axis=1)


def _token_rows(i):
    return pl.ds(pl.multiple_of(i * SUBLANES, SUBLANES), SUBLANES)


def _dot(a, b):
    return jnp.dot(a, b, preferred_element_type=F32)


def _dot_nt(a, b):
    return lax.dot_general(a, b, (((1,), (1,)), ((), ())), preferred_element_type=F32)


def _in_proj_kernel(x_ref, g_ref, w_ref, cos_ref, sin_ref, qg_ref, kg_ref, seg_ref,
                    rq_ref, rk_ref, rv_ref, rg_ref, sq_ref, sk_ref, sv_ref):
    x = x_ref[...]
    h = x * lax.rsqrt(jnp.mean(x * x, axis=-1, keepdims=True) + EPS) * g_ref[...]
    p = _dot(h.astype(BF16), w_ref[...])
    cos2 = cos_ref[...]
    sin2 = sin_ref[...]
    k_scale = RET_HEAD_DIM ** -0.5
    for hd in range(RET_HEADS):
        lo = hd * RET_HEAD_DIM
        q = p[:, lo:lo + RET_HEAD_DIM]
        k = p[:, RET_WIDTH + lo:RET_WIDTH + lo + RET_HEAD_DIM]
        q = q * cos2 + pltpu.roll(q, RET_HEAD_DIM // 2, axis=1) * sin2
        k = k * cos2 + pltpu.roll(k, RET_HEAD_DIM // 2, axis=1) * sin2
        rq_ref[:, lo:lo + RET_HEAD_DIM] = q.astype(rq_ref.dtype)
        rk_ref[:, lo:lo + RET_HEAD_DIM] = (k * k_scale).astype(rk_ref.dtype)
    rv_ref[...] = p[:, 2 * RET_WIDTH:3 * RET_WIDTH].astype(rv_ref.dtype)
    rg_ref[...] = p[:, 3 * RET_WIDTH:4 * RET_WIDTH]
    base = 4 * RET_WIDTH
    seg = seg_ref[...]

    def head_norm(v, gain):
        hi, lo = _split_bf16(v * v)
        group = seg.shape[0]
        ms = jnp.concatenate(
            [_dot(hi[:, g:g + group], seg) + _dot(lo[:, g:g + group], seg)
             for g in range(0, SB_WIDTH, group)], axis=1) * (1.0 / SB_HEAD_DIM)
        return v * lax.rsqrt(ms + EPS) * gain

    sq = head_norm(p[:, base:base + SB_WIDTH], qg_ref[...])
    sk = head_norm(p[:, base + SB_WIDTH:base + 2 * SB_WIDTH], kg_ref[...])
    sq_ref[...] = (sq * (SB_HEAD_DIM ** -0.5)).astype(sq_ref.dtype)
    sk_ref[...] = sk.astype(sk_ref.dtype)
    sv_ref[...] = p[:, base + 2 * SB_WIDTH:base + 3 * SB_WIDTH].astype(sv_ref.dtype)


def _rope_tables(seq):
    half = RET_HEAD_DIM // 2
    inv = ROPE_BASE ** (-np.arange(half, dtype=np.float64) / half)
    n_a = seq // LANES
    ang_a = (np.arange(n_a, dtype=np.float64) * LANES)[:, None] * inv[None, :]
    ang_b = np.arange(LANES, dtype=np.float64)[:, None] * inv[None, :]
    ca, sa = jnp.asarray(np.cos(ang_a), F32), jnp.asarray(np.sin(ang_a), F32)
    cb, sb = jnp.asarray(np.cos(ang_b), F32), jnp.asarray(np.sin(ang_b), F32)
    cos = (ca[:, None, :] * cb[None] - sa[:, None, :] * sb[None]).reshape(seq, half)
    sin = (sa[:, None, :] * cb[None] + ca[:, None, :] * sb[None]).reshape(seq, half)
    return jnp.concatenate([cos, cos], axis=-1), jnp.concatenate([-sin, sin], axis=-1)


def _in_proj(x2, norm_g, w_in, sb_q_g, sb_k_g, seq, tm):
    t = x2.shape[0]
    assert seq % tm == 0 and t % tm == 0
    cos2, sin2 = _rope_tables(seq)
    seg = np.kron(np.eye(MXU_WIDTH // SB_HEAD_DIM), np.ones((SB_HEAD_DIM, SB_HEAD_DIM)))
    seg = jnp.asarray(seg, BF16)
    qg = jnp.tile(sb_q_g.astype(F32), SB_HEADS)[None, :]
    kg = jnp.tile(sb_k_g.astype(F32), SB_HEADS)[None, :]
    n_pos = seq // tm
    row = lambda w: pl.BlockSpec((tm, w), lambda i: (i, 0))
    full = lambda a: pl.BlockSpec(a.shape, lambda i: (0,) * a.ndim)
    pos = pl.BlockSpec((tm, RET_HEAD_DIM), lambda i: (i % n_pos, 0))
    w_bf = w_in.astype(BF16)
    g2 = norm_g.astype(F32)[None, :]
    out = lambda dt: jax.ShapeDtypeStruct((t, RET_WIDTH), dt)
    return pl.pallas_call(
        _in_proj_kernel,
        grid=(t // tm,),
        in_specs=[row(D_MODEL), full(g2), full(w_bf), pos, pos, full(qg), full(kg), full(seg)],
        out_specs=[row(RET_WIDTH)] * 7,
        out_shape=[out(BF16), out(F32), out(BF16), out(F32), out(BF16), out(BF16), out(BF16)],
        compiler_params=pltpu.CompilerParams(
            dimension_semantics=("arbitrary",), vmem_limit_bytes=VMEM_LIMIT),
        name="in_proj",
    )(x2, g2, w_bf, cos2, sin2, qg, kg, seg)


def _retention_kernel(q_ref, k_ref, v_ref, g_ref, ng_ref, intra_ref, qd_ref, kd_ref, cd_ref,
                      o_ref, state_ref, *, chunks):
    @pl.when(pl.program_id(1) == 0)
    def _():
        state_ref[...] = jnp.zeros_like(state_ref)

    def chunk(c, carry):
        rows = pl.ds(pl.multiple_of(c * RET_CHUNK, RET_CHUNK), RET_CHUNK)
        for hd in range(RET_HEADS):
            cols = slice(hd * RET_HEAD_DIM, (hd + 1) * RET_HEAD_DIM)
            q = q_ref[rows, cols]
            k = k_ref[rows, cols]
            v = v_ref[rows, cols]
            state = state_ref[hd]
            scores = _dot_nt(q, k.astype(BF16)) * intra_ref[hd]
            inner = _dot(scores.astype(BF16), v)
            cross = _dot(q, state.astype(BF16)) * qd_ref[:, cols]
            kd = (k * kd_ref[:, cols]).astype(BF16)
            state_ref[hd] = state * cd_ref[:, cols] + _dot(kd.T, v)
            o = inner + cross
            o = o * lax.rsqrt(jnp.mean(o * o, axis=-1, keepdims=True) + EPS) * ng_ref[:, cols]
            gate = g_ref[rows, cols]
            o_ref[rows, cols] = (o * (gate * jax.nn.sigmoid(gate))).astype(o_ref.dtype)
        return carry

    lax.fori_loop(0, chunks, chunk, 0, unroll=True)


def _retention(rq, rk, rv, rg, ret_norm_g, batch, seq, rows):
    assert seq % rows == 0 and rows % RET_CHUNK == 0
    log_gamma = np.log(1.0 - 2.0 ** (-5.0 - np.arange(RET_HEADS, dtype=np.float64)))
    idx = np.arange(RET_CHUNK, dtype=np.float64)
    diff = idx[:, None] - idx[None, :]
    intra = np.where(diff >= 0, np.exp(log_gamma[:, None, None] * np.maximum(diff, 0.0)), 0.0)
    q_decay = np.exp(log_gamma[:, None] * (idx + 1.0))
    k_decay = np.exp(log_gamma[:, None] * (RET_CHUNK - 1.0 - idx))
    chunk_decay = np.exp(log_gamma * RET_CHUNK)
    lane_rep = lambda a: np.repeat(a.T, RET_HEAD_DIM, axis=1)
    intra = jnp.asarray(intra, F32)
    qd = jnp.asarray(lane_rep(q_decay), F32)
    kd = jnp.asarray(lane_rep(k_decay), F32)
    cd = jnp.asarray(np.repeat(chunk_decay, RET_HEAD_DIM)[None, :], F32)
    ng = ret_norm_g.astype(F32).reshape(1, RET_WIDTH)
    n_r = seq // rows
    blk = pl.BlockSpec((rows, RET_WIDTH), lambda b, r: (b * n_r + r, 0))
    full = lambda a: pl.BlockSpec(a.shape, lambda b, r: (0,) * a.ndim)
    return pl.pallas_call(
        functools.partial(_retention_kernel, chunks=rows // RET_CHUNK),
        grid=(batch, n_r),
        in_specs=[blk, blk, blk, blk, full(ng), full(intra), full(qd), full(kd), full(cd)],
        out_specs=blk,
        out_shape=jax.ShapeDtypeStruct((batch * seq, RET_WIDTH), BF16),
        scratch_shapes=[pltpu.VMEM((RET_HEADS, RET_HEAD_DIM, RET_HEAD_DIM), F32)],
        compiler_params=pltpu.CompilerParams(
            dimension_semantics=("arbitrary", "arbitrary"), vmem_limit_bytes=VMEM_LIMIT),
        name="retention",
    )(rq, rk, rv, rg, ng, intra, qd, kd, cd)


def _sb_attn_kernel(q_ref, k_ref, v_ref, tri_ref, o_ref, *, blk, halo, heads):
    i = pl.program_id(2)
    lane = lax.broadcasted_iota(jnp.int32, (blk, LANES), 1)
    first_head = lane < SB_HEAD_DIM
    tri_own = tri_ref[...]
    tri_halo = tri_ref[:halo, :halo]
    below_diag = (lax.broadcasted_iota(jnp.int32, (blk, blk), 1)
                  < lax.broadcasted_iota(jnp.int32, (blk, blk), 0))
    has_halo = i > 0

    def scores(qm, kb, mask):
        z = _dot_nt(qm, kb)
        log_beta = jnp.minimum(z, 0.0) - jnp.log(1.0 + jnp.exp(-jnp.abs(z)))
        log_rest = log_beta - z
        if mask is not None:
            log_rest = jnp.where(mask, log_rest, 0.0)
        return log_beta, log_rest

    def later_sums(log_rest, tri):
        hi, lo = _split_bf16(log_rest)
        later = _dot(hi, tri) + _dot(lo, tri)
        return later, later[:, 0:1] + log_rest[:, 0:1]

    def weights(log_beta, log_remaining, mask):
        w = jnp.exp(log_beta + log_remaining)
        if mask is not None:
            w = jnp.where(mask, w, 0.0)
        return w.astype(BF16)

    def head_q(h):
        cols = slice((h // 2) * LANES, (h // 2 + 1) * LANES)
        q = q_ref[:, cols]
        keep = first_head if h % 2 == 0 else jnp.logical_not(first_head)
        return jnp.where(keep, q, jnp.zeros_like(q)), cols

    own_rows = pl.ds(pl.multiple_of(i * blk, blk), blk)
    halo_rows = pl.ds(pl.multiple_of(jnp.maximum(i * blk - halo, 0), halo), halo)

    def first_step(h):
        qm, cols = head_q(h)
        beta_o, rest_o = scores(qm, k_ref[own_rows, cols], below_diag)
        beta_h, rest_h = scores(qm, k_ref[halo_rows, cols], has_halo)
        later_o, total_o = later_sums(rest_o, tri_own)
        later_h, total_h = later_sums(rest_h, tri_halo)
        acc = _dot(weights(beta_o, later_o, below_diag), v_ref[own_rows, cols])
        acc = acc + _dot(weights(beta_h, total_o + later_h, has_halo), v_ref[halo_rows, cols])
        return total_o + total_h, acc

    def chunk_step(h, j, c, acc):
        qm, cols = head_q(h)
        rows = pl.ds(pl.multiple_of(j * halo, halo), halo)
        beta, rest = scores(qm, k_ref[rows, cols], None)
        later, total = later_sums(rest, tri_halo)
        return c + total, acc + _dot(weights(beta, c + later, None), v_ref[rows, cols])

    def cond(carry):
        j, cs, _ = carry
        c_max = functools.reduce(jnp.maximum, cs)
        return jnp.logical_and(j >= 0, jnp.max(c_max) > F32_EXP_UNDERFLOW)

    def body(carry):
        j, cs, accs = carry
        out = [chunk_step(h, j, cs[h], accs[h]) for h in range(heads)]
        return j - 1, tuple(o[0] for o in out), tuple(o[1] for o in out)

    first = [first_step(h) for h in range(heads)]
    init = (i * (blk // halo) - 2, tuple(f[0] for f in first), tuple(f[1] for f in first))
    _, _, accs = lax.while_loop(cond, body, init)
    for p in range(heads // 2):
        o_ref[:, p * LANES:(p + 1) * LANES] = jnp.where(
            first_head, accs[2 * p], accs[2 * p + 1]).astype(o_ref.dtype)


def _sb_attention(sq, sk, sv, batch, seq, blk, halo, heads):
    assert seq % blk == 0 and blk % halo == 0 and halo == LANES
    assert heads % 2 == 0 and SB_HEADS % heads == 0
    nq = seq // blk
    width = heads * SB_HEAD_DIM
    groups = SB_WIDTH // width
    j = np.arange(blk)
    tri = jnp.asarray((j[:, None] > j[None, :]).astype(np.float32), BF16)
    qblk = pl.BlockSpec((blk, width), lambda b, g, i: (b * nq + i, g))
    kvblk = pl.BlockSpec((seq, width), lambda b, g, i: (b, g))
    return pl.pallas_call(
        functools.partial(_sb_attn_kernel, blk=blk, halo=halo, heads=heads),
        grid=(batch, groups, nq),
        in_specs=[qblk, kvblk, kvblk, pl.BlockSpec(tri.shape, lambda b, g, i: (0, 0))],
        out_specs=qblk,
        out_shape=jax.ShapeDtypeStruct((batch * seq, SB_WIDTH), BF16),
        compiler_params=pltpu.CompilerParams(
            dimension_semantics=("arbitrary", "arbitrary", "arbitrary"),
            vmem_limit_bytes=VMEM_LIMIT),
        name="sb_attn",
    )(sq, sk, sv, tri)


def _out_proj_kernel(ret_ref, sb_ref, x_ref, wtop_ref, wbot_ref, g_ref, wrh_ref, wrl_ref,
                     br_ref, tri_ref,
                     x1_ref, xf_ref, idx_ref, w_ref, rank_ref, cnt_ref, run_ref):
    @pl.when(pl.program_id(0) == 0)
    def _():
        run_ref[...] = jnp.zeros_like(run_ref)

    x1 = x_ref[...] + _dot(ret_ref[...], wtop_ref[...]) + _dot(sb_ref[...], wbot_ref[...])
    x1_ref[...] = x1
    xf = x1 * lax.rsqrt(jnp.mean(x1 * x1, axis=-1, keepdims=True) + EPS) * g_ref[...]
    _store_token_tiles(xf_ref, xf)
    xh, xl = _split_bf16(xf)
    wrh = wrh_ref[...]
    logits = _dot_nt(wrh, xh) + _dot_nt(wrh, xl) + _dot_nt(wrl_ref[...], xh) + br_ref[...]
    n_e, tm = logits.shape
    e_iota = lax.broadcasted_iota(jnp.int32, (n_e, tm), 0)
    cur = logits
    tops, sels, hots = [], [], []
    for _ in range(TOP_K):
        m = jnp.max(cur, axis=0, keepdims=True)
        sel = jnp.min(jnp.where(cur == m, e_iota, n_e), axis=0, keepdims=True)
        hot = e_iota == sel
        cur = jnp.where(hot, -jnp.inf, cur)
        tops.append(m)
        sels.append(sel)
        hots.append(hot)
    ps = [jnp.exp(m - tops[0]) for m in tops]
    denom = ps[0] + ps[1] + ps[2] + ps[3]
    chosen = jnp.zeros((n_e, tm), F32)
    for hot in hots:
        chosen = chosen + hot.astype(F32)
    before = run_ref[:, 0:1] + _dot(chosen.astype(BF16), tri_ref[...])
    for kk in range(TOP_K):
        idx_ref[kk:kk + 1, :] = sels[kk]
        w_ref[kk:kk + 1, :] = ps[kk] / denom
        rank = jnp.sum(jnp.where(hots[kk], before, 0.0), axis=0, keepdims=True)
        rank_ref[kk:kk + 1, :] = rank.astype(jnp.int32)
    run = run_ref[...] + jnp.sum(chosen, axis=1, keepdims=True)
    run_ref[...] = run
    cnt_ref[...] = run.astype(jnp.int32)


def _out_proj_route(ret, sb, x2, w_out, ffn_g, w_router, b_router, tm):
    t = x2.shape[0]
    assert t % tm == 0
    w_bf = w_out.astype(BF16)
    wtop, wbot = w_bf[:RET_WIDTH], w_bf[RET_WIDTH:]
    g2 = ffn_g.astype(F32)[None, :]
    wr_t = w_router.astype(F32).T
    wrh = wr_t.astype(BF16)
    wrl = (wr_t - wrh.astype(F32)).astype(BF16)
    br = b_router.astype(F32)[:, None]
    tt = np.arange(tm)
    tri = jnp.asarray((tt[:, None] < tt[None, :]).astype(np.float32), BF16)
    row = lambda w: pl.BlockSpec((tm, w), lambda i: (i, 0))
    full = lambda a: pl.BlockSpec(a.shape, lambda i: (0,) * a.ndim)
    col = pl.BlockSpec((TOP_K, tm), lambda i: (0, i))
    return pl.pallas_call(
        _out_proj_kernel,
        grid=(t // tm,),
        in_specs=[row(RET_WIDTH), row(SB_WIDTH), row(D_MODEL), full(wtop), full(wbot), full(g2),
                  full(wrh), full(wrl), full(br), full(tri)],
        out_specs=[row(D_MODEL), pl.BlockSpec((tm * SUBLANES, LANES), lambda i: (i, 0)),
                   col, col, col, pl.BlockSpec((N_EXPERTS, LANES), lambda i: (0, 0))],
        out_shape=[jax.ShapeDtypeStruct((t, D_MODEL), F32),
                   jax.ShapeDtypeStruct((t * SUBLANES, LANES), F32),
                   jax.ShapeDtypeStruct((TOP_K, t), jnp.int32),
                   jax.ShapeDtypeStruct((TOP_K, t), F32),
                   jax.ShapeDtypeStruct((TOP_K, t), jnp.int32),
                   jax.ShapeDtypeStruct((N_EXPERTS, LANES), jnp.int32)],
        scratch_shapes=[pltpu.VMEM((N_EXPERTS, LANES), F32)],
        compiler_params=pltpu.CompilerParams(
            dimension_semantics=("arbitrary",), vmem_limit_bytes=VMEM_LIMIT),
        name="out_proj_route",
    )(ret, sb, x2, wtop, wbot, g2, wrh, wrl, br, tri)


def _dest_kernel(pstart_ref, idx_ref, rank_ref, dest_ref):
    idx = idx_ref[...]
    dest = rank_ref[...]
    for e in range(N_EXPERTS):
        dest = dest + jnp.where(idx == e, pstart_ref[e], 0)
    dest_ref[...] = dest


def _dest(padded_start, top_idx, rank, tn):
    t = top_idx.shape[1]
    assert t % tn == 0
    col = pl.BlockSpec((TOP_K, tn), lambda i, ps: (0, i))
    return pl.pallas_call(
        _dest_kernel,
        grid_spec=pltpu.PrefetchScalarGridSpec(
            num_scalar_prefetch=1, grid=(t // tn,), in_specs=[col, col], out_specs=col),
        out_shape=jax.ShapeDtypeStruct((TOP_K, t), jnp.int32),
        name="dest",
    )(padded_start, top_idx, rank)


def _dispatch_kernel(pend_ref, dest_ref, xf_ref, slots_hbm, zbuf, sem, zsem, *, tm, blk):
    @pl.when(pl.program_id(0) == 0)
    def _():
        zbuf[...] = jnp.zeros_like(zbuf)

        def tail_copy(e):
            first = pl.multiple_of((pend_ref[e] - blk) * SUBLANES, blk * SUBLANES)
            return pltpu.make_async_copy(
                zbuf, slots_hbm.at[pl.ds(first, blk * SUBLANES), :], zsem)

        def nonempty(e):
            return pend_ref[e] > (pend_ref[e - 1] if e else 0)

        for e in range(N_EXPERTS):
            pl.when(nonempty(e))(lambda e=e: tail_copy(e).start())
        for e in range(N_EXPERTS):
            pl.when(nonempty(e))(lambda e=e: tail_copy(e).wait())

    def row_copy(t, kk):
        return pltpu.make_async_copy(
            xf_ref.at[_token_rows(t), :],
            slots_hbm.at[_token_rows(dest_ref[kk, t]), :], sem)

    def start(t, carry):
        for kk in range(TOP_K):
            row_copy(t, kk).start(priority=kk % 2)
        return carry

    lax.fori_loop(0, tm, start, 0, unroll=8)
    for kk in range(TOP_K):
        pltpu.make_async_copy(
            xf_ref, slots_hbm.at[pl.ds(0, tm * SUBLANES), :], sem).wait()


def _dispatch(padded_end, dest, xf, n_pad, tm, blk):
    t = xf.shape[0] // SUBLANES
    assert t % tm == 0
    return pl.pallas_call(
        functools.partial(_dispatch_kernel, tm=tm, blk=blk),
        grid_spec=pltpu.PrefetchScalarGridSpec(
            num_scalar_prefetch=1,
            grid=(t // tm,),
            in_specs=[pl.BlockSpec((TOP_K, tm), lambda i, pe: (0, i), memory_space=pltpu.SMEM),
                      pl.BlockSpec((tm * SUBLANES, LANES), lambda i, pe: (i, 0))],
            out_specs=pl.BlockSpec(memory_space=pl.ANY),
            scratch_shapes=[pltpu.VMEM((blk * SUBLANES, LANES), F32),
                            pltpu.SemaphoreType.DMA, pltpu.SemaphoreType.DMA],
        ),
        out_shape=jax.ShapeDtypeStruct((n_pad * SUBLANES, LANES), F32),
        compiler_params=pltpu.CompilerParams(
            dimension_semantics=("arbitrary",), vmem_limit_bytes=VMEM_LIMIT,
            has_side_effects=True),
        name="dispatch",
    )(padded_end, dest, xf)


WEIGHT_ROWS = 128


def _experts_kernel(be_ref, nu_ref, run_ref, next_ref, x_ref, wgu_hbm, wd_hbm, bg_ref, bu_ref,
                    bd_ref, y_ref, wgu_buf, wd_buf, wg_s, wu_s, wd_s, t_s, sem, *, blk):
    j = pl.program_id(0)
    used = j < nu_ref[0]
    expert = be_ref[j]
    new_expert = jnp.logical_or(j == 0, expert != be_ref[jnp.maximum(j - 1, 0)])

    def weight_copies(e, slot):
        return (pltpu.make_async_copy(wgu_hbm.at[e], wgu_buf.at[slot], sem.at[0, slot]),
                pltpu.make_async_copy(wd_hbm.at[e], wd_buf.at[slot], sem.at[1, slot]))

    @pl.when(jnp.logical_and(used, new_expert))
    def _():
        slot = lax.rem(run_ref[expert], 2)

        @pl.when(j == 0)
        def _():
            for c in weight_copies(expert, slot):
                c.start()

        for c in weight_copies(expert, slot):
            c.wait()
        following = next_ref[expert]

        @pl.when(following >= 0)
        def _():
            for c in weight_copies(following, 1 - slot):
                c.start()

        def rows_step(r, carry):
            rows = pl.ds(pl.multiple_of(r * WEIGHT_ROWS, WEIGHT_ROWS), WEIGHT_ROWS)
            t_s[...] = wgu_buf[slot, rows, :].T
            wg_s[rows, :] = t_s[pl.ds(0, D_FF, stride=2), :].T.astype(BF16)
            wu_s[rows, :] = t_s[pl.ds(1, D_FF, stride=2), :].T.astype(BF16)
            wd_s[rows, :] = wd_buf[slot, rows, :].astype(BF16)
            return carry

        lax.fori_loop(0, D_MODEL // WEIGHT_ROWS, rows_step, 0)

    @pl.when(used)
    def _():
        xb = _load_token_tiles(x_ref, blk).astype(BF16)
        gate = jnp.minimum(_dot(xb, wg_s[...]) + bg_ref[...], SWIGLU_LIMIT)
        up = jnp.clip(_dot(xb, wu_s[...]) + bu_ref[...], -SWIGLU_LIMIT, SWIGLU_LIMIT)
        hidden = (up + 1.0) * gate * jax.nn.sigmoid(SWIGLU_ALPHA * gate)
        _store_token_tiles(y_ref, _dot(hidden.astype(BF16), wd_s[...]) + bd_ref[...])

    @pl.when(jnp.logical_not(used))
    def _():
        y_ref[...] = jnp.zeros_like(y_ref)


def _experts(block_expert, n_used, expert_run, next_expert, slots, wgu, wd, bg, bu, bd, blk):
    assert D_FF == D_MODEL
    n_pad = slots.shape[0] // SUBLANES
    n_blocks = n_pad // blk
    xmap = lambda j, be, nu, run, nxt: (jnp.minimum(j, nu[0] - 1), 0)
    wmap = lambda j, be, nu, run, nxt: (be[j], 0, 0)
    bspec = pl.BlockSpec((None, 1, D_FF), wmap)
    tiles = (blk * SUBLANES, LANES)
    hbm = pl.BlockSpec(memory_space=pl.ANY)
    return pl.pallas_call(
        functools.partial(_experts_kernel, blk=blk),
        grid_spec=pltpu.PrefetchScalarGridSpec(
            num_scalar_prefetch=4,
            grid=(n_blocks,),
            in_specs=[pl.BlockSpec(tiles, xmap), hbm, hbm, bspec, bspec, bspec],
            out_specs=pl.BlockSpec(tiles, lambda j, be, nu, run, nxt: (j, 0)),
            scratch_shapes=[pltpu.VMEM((2, D_MODEL, 2 * D_FF), F32),
                            pltpu.VMEM((2, D_FF, D_MODEL), F32),
                            pltpu.VMEM((D_MODEL, D_FF), BF16), pltpu.VMEM((D_MODEL, D_FF), BF16),
                            pltpu.VMEM((D_FF, D_MODEL), BF16),
                            pltpu.VMEM((2 * D_FF, WEIGHT_ROWS), F32),
                            pltpu.SemaphoreType.DMA((2, 2))],
        ),
        out_shape=jax.ShapeDtypeStruct((n_pad * SUBLANES, LANES), F32),
        compiler_params=pltpu.CompilerParams(
            dimension_semantics=("arbitrary",), vmem_limit_bytes=VMEM_LIMIT),
        name="experts",
    )(block_expert, n_used, expert_run, next_expert, slots, wgu, wd, bg, bu, bd)


def _combine_kernel(dest_ref, next_dest_ref, y_hbm, w_ref, x1_ref, o_ref, buf, sem, *, tm):
    i = pl.program_id(0)
    n = pl.num_programs(0)
    slot = lax.rem(i, 2)

    def gather(dests, into):
        def start(t, carry):
            for kk in range(TOP_K):
                pltpu.make_async_copy(
                    y_hbm.at[_token_rows(dests[kk, t]), :],
                    buf.at[into, kk, _token_rows(t), :], sem.at[into]).start(priority=kk % 2)
            return carry

        lax.fori_loop(0, tm, start, 0, unroll=8)

    pl.when(i == 0)(lambda: gather(dest_ref, 0))
    pl.when(i + 1 < n)(lambda: gather(next_dest_ref, 1 - slot))
    for kk in range(TOP_K):
        pltpu.make_async_copy(
            y_hbm.at[pl.ds(0, tm * SUBLANES), :], buf.at[slot, kk], sem.at[slot]).wait()
    w = w_ref[...]
    for s in range(ROW_TILES):
        cols = slice(s * LANES, (s + 1) * LANES)
        acc = x1_ref[:, cols]
        for kk in range(TOP_K):
            acc = acc + buf[slot, kk, pl.ds(s, tm, stride=SUBLANES), :] * w[:, kk:kk + 1]
        o_ref[:, cols] = acc


def _combine(dest, y_slots, w_t, x1, tm):
    t = x1.shape[0]
    assert t % tm == 0
    n = t // tm
    row = pl.BlockSpec((tm, D_MODEL), lambda i: (i, 0))
    return pl.pallas_call(
        functools.partial(_combine_kernel, tm=tm),
        grid=(n,),
        in_specs=[pl.BlockSpec((TOP_K, tm), lambda i: (0, i), memory_space=pltpu.SMEM),
                  pl.BlockSpec((TOP_K, tm), lambda i: (0, jnp.minimum(i + 1, n - 1)),
                               memory_space=pltpu.SMEM),
                  pl.BlockSpec(memory_space=pl.ANY),
                  pl.BlockSpec((tm, TOP_K), lambda i: (i, 0)),
                  row],
        out_specs=row,
        out_shape=jax.ShapeDtypeStruct((t, D_MODEL), F32),
        scratch_shapes=[pltpu.VMEM((2, TOP_K, tm * SUBLANES, LANES), F32),
                        pltpu.SemaphoreType.DMA((2,))],
        compiler_params=pltpu.CompilerParams(
            dimension_semantics=("arbitrary",), vmem_limit_bytes=VMEM_LIMIT),
        name="combine",
    )(dest, dest, y_slots, w_t, x1)


def _tiles(batch, seq):
    return dict(
        in_proj=min(512, seq),
        retention=min(1024, seq),
        sb_block=min(256, seq),
        sb_heads=4,
        out_proj=min(1024, seq),
        dest=min(8192, batch * seq),
        dispatch=min(2048, seq),
        combine=min(512, seq),
        expert_block=512,
    )


def _layer(x, attn_norm_g, w_in, ret_norm_g, sb_q_norm_g, sb_k_norm_g, w_out,
           ffn_norm_g, w_router, b_router, w_gate_up, b_gate_up, w_down, b_down):
    batch, seq, d = x.shape
    t = batch * seq
    tiles = _tiles(batch, seq)
    x2 = x.reshape(t, d)

    rq, rk, rv, rg, sq, sk, sv = _in_proj(
        x2, attn_norm_g, w_in, sb_q_norm_g, sb_k_norm_g, seq, tiles["in_proj"])
    ret = _retention(rq, rk, rv, rg, ret_norm_g, batch, seq, tiles["retention"])
    sb = _sb_attention(sq, sk, sv, batch, seq, tiles["sb_block"], LANES, tiles["sb_heads"])
    x1, xf, top_idx, top_w, rank, counts = _out_proj_route(
        ret, sb, x2, w_out, ffn_norm_g, w_router, b_router, tiles["out_proj"])

    blk = tiles["expert_block"]
    n_pad = t * TOP_K + N_EXPERTS * blk
    n_blocks = n_pad // blk
    counts = counts[:, 0]
    padded = (counts + blk - 1) // blk * blk
    padded_end = jnp.cumsum(padded)
    padded_start = padded_end - padded
    block_first = jnp.arange(n_blocks, dtype=jnp.int32) * blk
    block_expert = jnp.minimum(
        jnp.sum(padded_end[None, :] <= block_first[:, None], axis=1), N_EXPERTS - 1
    ).astype(jnp.int32)
    n_used = (padded_end[-1:] // blk).astype(jnp.int32)
    nonempty = counts > 0
    expert_run = (jnp.cumsum(nonempty) - nonempty).astype(jnp.int32)
    experts = jnp.arange(N_EXPERTS, dtype=jnp.int32)
    later = jnp.logical_and(nonempty[None, :], experts[None, :] > experts[:, None])
    next_expert = jnp.min(jnp.where(later, experts[None, :], N_EXPERTS), axis=1)
    next_expert = jnp.where(next_expert < N_EXPERTS, next_expert, -1).astype(jnp.int32)

    dest = _dest(padded_start, top_idx, rank, tiles["dest"])
    slots = _dispatch(padded_end, dest, xf, n_pad, tiles["dispatch"], blk)

    bgu = b_gate_up.astype(F32).reshape(N_EXPERTS, 1, D_FF, 2)
    y_slots = _experts(block_expert, n_used, expert_run, next_expert, slots,
                       w_gate_up.astype(F32), w_down.astype(F32),
                       bgu[..., 0], bgu[..., 1], b_down.astype(F32)[:, None, :], blk)

    out = _combine(dest, y_slots, top_w.T, x1, tiles["combine"])
    return out.reshape(batch, seq, d)


def kernel(x, attn_norm_g, w_in, ret_norm_g, sb_q_norm_g, sb_k_norm_g, w_out, ffn_norm_g,
           w_router, b_router, w_gate_up, b_gate_up, w_down, b_down):
    depth = attn_norm_g.shape[0]
    for l in range(depth):
        x = _layer(x, attn_norm_g[l], w_in[l], ret_norm_g[l], sb_q_norm_g[l], sb_k_norm_g[l],
                   w_out[l], ffn_norm_g[l], w_router[l], b_router[l], w_gate_up[l],
                   b_gate_up[l], w_down[l], b_down[l])
    return x
```

```python
import functools

import numpy as np
import jax
import jax.numpy as jnp
from jax import lax
from jax.experimental import pallas as pl
from jax.experimental.pallas import tpu as pltpu

D_MODEL = 1024
RET_WIDTH = 512
RET_HEADS = 4
RET_HEAD_DIM = 128
SB_WIDTH = 512
SB_HEADS = 8
SB_HEAD_DIM = 64
IN_PROJ_WIDTH = 4 * RET_WIDTH + 3 * SB_WIDTH
RET_CHUNK = 128
ROPE_BASE = 10000.0
N_EXPERTS = 32
TOP_K = 4
D_FF = D_MODEL
SWIGLU_LIMIT = 7.0
SWIGLU_ALPHA = 1.702
EPS = 1e-6

LANES = 128
MXU_WIDTH = 256
F32_EXP_UNDERFLOW = -88.0
VMEM_LIMIT = 56 * 1024 * 1024

BF16 = jnp.bfloat16
F32 = jnp.float32


def _split_bf16(v):
    hi = v.astype(BF16)
    lo = (v - hi.astype(F32)).astype(BF16)
    return hi, lo


SUBLANES = 8
ROW_TILES = D_MODEL // LANES
assert ROW_TILES == SUBLANES


def _store_token_tiles(ref, value):
    n = value.shape[0]
    for s in range(ROW_TILES):
        ref[pl.ds(s, n, stride=SUBLANES), :] = value[:, s * LANES:(s + 1) * LANES]


def _load_token_tiles(ref, n):
    return jnp.concatenate(
        [ref[pl.ds(s, n, stride=SUBLANES), :] for s in range(ROW_TILES)], axis=1)


def _token_rows(i):
    return pl.ds(pl.multiple_of(i * SUBLANES, SUBLANES), SUBLANES)


def _dot(a, b):
    return jnp.dot(a, b, preferred_element_type=F32)


def _dot_nt(a, b):
    return lax.dot_general(a, b, (((1,), (1,)), ((), ())), preferred_element_type=F32)


def _in_proj_kernel(x_ref, g_ref, w_ref, cos_ref, sin_ref, qg_ref, kg_ref, seg_ref,
                    rq_ref, rk_ref, rv_ref, rg_ref, sq_ref, sk_ref, sv_ref):
    x = x_ref[...]
    h = x * lax.rsqrt(jnp.mean(x * x, axis=-1, keepdims=True) + EPS) * g_ref[...]
    p = _dot(h.astype(BF16), w_ref[...])
    cos2 = cos_ref[...]
    sin2 = sin_ref[...]
    k_scale = RET_HEAD_DIM ** -0.5
    for hd in range(RET_HEADS):
        lo = hd * RET_HEAD_DIM
        q = p[:, lo:lo + RET_HEAD_DIM]
        k = p[:, RET_WIDTH + lo:RET_WIDTH + lo + RET_HEAD_DIM]
        q = q * cos2 + pltpu.roll(q, RET_HEAD_DIM // 2, axis=1) * sin2
        k = k * cos2 + pltpu.roll(k, RET_HEAD_DIM // 2, axis=1) * sin2
        rq_ref[:, lo:lo + RET_HEAD_DIM] = q.astype(rq_ref.dtype)
        rk_ref[:, lo:lo + RET_HEAD_DIM] = (k * k_scale).astype(rk_ref.dtype)
    rv_ref[...] = p[:, 2 * RET_WIDTH:3 * RET_WIDTH].astype(rv_ref.dtype)
    rg_ref[...] = p[:, 3 * RET_WIDTH:4 * RET_WIDTH]
    base = 4 * RET_WIDTH
    seg = seg_ref[...]

    def head_norm(v, gain):
        hi, lo = _split_bf16(v * v)
        group = seg.shape[0]
        ms = jnp.concatenate(
            [_dot(hi[:, g:g + group], seg) + _dot(lo[:, g:g + group], seg)
             for g in range(0, SB_WIDTH, group)], axis=1) * (1.0 / SB_HEAD_DIM)
        return v * lax.rsqrt(ms + EPS) * gain

    sq = head_norm(p[:, base:base + SB_WIDTH], qg_ref[...])
    sk = head_norm(p[:, base + SB_WIDTH:base + 2 * SB_WIDTH], kg_ref[...])
    sq_ref[...] = (sq * (SB_HEAD_DIM ** -0.5)).astype(sq_ref.dtype)
    sk_ref[...] = sk.astype(sk_ref.dtype)
    sv_ref[...] = p[:, base + 2 * SB_WIDTH:base + 3 * SB_WIDTH].astype(sv_ref.dtype)


def _rope_tables(seq):
    half = RET_HEAD_DIM // 2
    inv = ROPE_BASE ** (-np.arange(half, dtype=np.float64) / half)
    n_a = seq // LANES
    ang_a = (np.arange(n_a, dtype=np.float64) * LANES)[:, None] * inv[None, :]
    ang_b = np.arange(LANES, dtype=np.float64)[:, None] * inv[None, :]
    ca, sa = jnp.asarray(np.cos(ang_a), F32), jnp.asarray(np.sin(ang_a), F32)
    cb, sb = jnp.asarray(np.cos(ang_b), F32), jnp.asarray(np.sin(ang_b), F32)
    cos = (ca[:, None, :] * cb[None] - sa[:, None, :] * sb[None]).reshape(seq, half)
    sin = (sa[:, None, :] * cb[None] + ca[:, None, :] * sb[None]).reshape(seq, half)
    return jnp.concatenate([cos, cos], axis=-1), jnp.concatenate([-sin, sin], axis=-1)


def _in_proj(x2, norm_g, w_in, sb_q_g, sb_k_g, seq, tm):
    t = x2.shape[0]
    assert seq % tm == 0 and t % tm == 0
    cos2, sin2 = _rope_tables(seq)
    seg = np.kron(np.eye(MXU_WIDTH // SB_HEAD_DIM), np.ones((SB_HEAD_DIM, SB_HEAD_DIM)))
    seg = jnp.asarray(seg, BF16)
    qg = jnp.tile(sb_q_g.astype(F32), SB_HEADS)[None, :]
    kg = jnp.tile(sb_k_g.astype(F32), SB_HEADS)[None, :]
    n_pos = seq // tm
    row = lambda w: pl.BlockSpec((tm, w), lambda i: (i, 0))
    full = lambda a: pl.BlockSpec(a.shape, lambda i: (0,) * a.ndim)
    pos = pl.BlockSpec((tm, RET_HEAD_DIM), lambda i: (i % n_pos, 0))
    w_bf = w_in.astype(BF16)
    g2 = norm_g.astype(F32)[None, :]
    out = lambda dt: jax.ShapeDtypeStruct((t, RET_WIDTH), dt)
    return pl.pallas_call(
        _in_proj_kernel,
        grid=(t // tm,),
        in_specs=[row(D_MODEL), full(g2), full(w_bf), pos, pos, full(qg), full(kg), full(seg)],
        out_specs=[row(RET_WIDTH)] * 7,
        out_shape=[out(BF16), out(F32), out(BF16), out(F32), out(BF16), out(BF16), out(BF16)],
        compiler_params=pltpu.CompilerParams(
            dimension_semantics=("arbitrary",), vmem_limit_bytes=VMEM_LIMIT),
        name="in_proj",
    )(x2, g2, w_bf, cos2, sin2, qg, kg, seg)


def _retention_kernel(q_ref, k_ref, v_ref, g_ref, ng_ref, intra_ref, qd_ref, kd_ref, cd_ref,
                      o_ref, state_ref, *, chunks):
    @pl.when(pl.program_id(1) == 0)
    def _():
        state_ref[...] = jnp.zeros_like(state_ref)

    def chunk(c, carry):
        rows = pl.ds(pl.multiple_of(c * RET_CHUNK, RET_CHUNK), RET_CHUNK)
        heads = range(RET_HEADS)
        cols = [slice(hd * RET_HEAD_DIM, (hd + 1) * RET_HEAD_DIM) for hd in heads]
        q = [q_ref[rows, cl] for cl in cols]
        k = [k_ref[rows, cl] for cl in cols]
        v = [v_ref[rows, cl] for cl in cols]
        state = [state_ref[hd] for hd in heads]
        scores = [_dot_nt(q[hd], k[hd].astype(BF16)) * intra_ref[hd] for hd in heads]
        cross = [_dot(q[hd], state[hd].astype(BF16)) * qd_ref[:, cols[hd]] for hd in heads]
        inner = [_dot(scores[hd].astype(BF16), v[hd]) for hd in heads]
        kd = [(k[hd] * kd_ref[:, cols[hd]]).astype(BF16) for hd in heads]
        for hd in heads:
            state_ref[hd] = state[hd] * cd_ref[:, cols[hd]] + _dot(kd[hd].T, v[hd])
        for hd in heads:
            o = inner[hd] + cross[hd]
            o = o * lax.rsqrt(jnp.mean(o * o, axis=-1, keepdims=True) + EPS) * ng_ref[:, cols[hd]]
            gate = g_ref[rows, cols[hd]]
            o_ref[rows, cols[hd]] = (o * (gate * jax.nn.sigmoid(gate))).astype(o_ref.dtype)
        return carry

    lax.fori_loop(0, chunks, chunk, 0, unroll=True)


def _retention(rq, rk, rv, rg, ret_norm_g, batch, seq, rows):
    assert seq % rows == 0 and rows % RET_CHUNK == 0
    log_gamma = np.log(1.0 - 2.0 ** (-5.0 - np.arange(RET_HEADS, dtype=np.float64)))
    idx = np.arange(RET_CHUNK, dtype=np.float64)
    diff = idx[:, None] - idx[None, :]
    intra = np.where(diff >= 0, np.exp(log_gamma[:, None, None] * np.maximum(diff, 0.0)), 0.0)
    q_decay = np.exp(log_gamma[:, None] * (idx + 1.0))
    k_decay = np.exp(log_gamma[:, None] * (RET_CHUNK - 1.0 - idx))
    chunk_decay = np.exp(log_gamma * RET_CHUNK)
    lane_rep = lambda a: np.repeat(a.T, RET_HEAD_DIM, axis=1)
    intra = jnp.asarray(intra, F32)
    qd = jnp.asarray(lane_rep(q_decay), F32)
    kd = jnp.asarray(lane_rep(k_decay), F32)
    cd = jnp.asarray(np.repeat(chunk_decay, RET_HEAD_DIM)[None, :], F32)
    ng = ret_norm_g.astype(F32).reshape(1, RET_WIDTH)
    n_r = seq // rows
    blk = pl.BlockSpec((rows, RET_WIDTH), lambda b, r: (b * n_r + r, 0))
    full = lambda a: pl.BlockSpec(a.shape, lambda b, r: (0,) * a.ndim)
    return pl.pallas_call(
        functools.partial(_retention_kernel, chunks=rows // RET_CHUNK),
        grid=(batch, n_r),
        in_specs=[blk, blk, blk, blk, full(ng), full(intra), full(qd), full(kd), full(cd)],
        out_specs=blk,
        out_shape=jax.ShapeDtypeStruct((batch * seq, RET_WIDTH), BF16),
        scratch_shapes=[pltpu.VMEM((RET_HEADS, RET_HEAD_DIM, RET_HEAD_DIM), F32)],
        compiler_params=pltpu.CompilerParams(
            dimension_semantics=("arbitrary", "arbitrary"), vmem_limit_bytes=VMEM_LIMIT),
        name="retention",
    )(rq, rk, rv, rg, ng, intra, qd, kd, cd)


def _sb_attn_kernel(q_ref, k_ref, v_ref, tri_ref, o_ref, *, blk, halo, heads):
    i = pl.program_id(2)
    lane = lax.broadcasted_iota(jnp.int32, (blk, LANES), 1)
    first_head = lane < SB_HEAD_DIM
    tri_own = tri_ref[...]
    tri_halo = tri_ref[:halo, :halo]
    below_diag = (lax.broadcasted_iota(jnp.int32, (blk, blk), 1)
                  < lax.broadcasted_iota(jnp.int32, (blk, blk), 0))
    has_halo = i > 0


    def scores(z, mask):
        log_beta = jnp.minimum(z, 0.0) - jnp.log(1.0 + jnp.exp(-jnp.abs(z)))
        log_rest = log_beta - z
        if mask is not None:
            log_rest = jnp.where(mask, log_rest, 0.0)
        return log_beta, log_rest

    def score_phase(qms, kbs, mask):
        zs = [_dot_nt(qm, kb) for qm, kb in zip(qms, kbs)]
        return [scores(z, mask) for z in zs]

    def later_phase(log_rests, tri):
        parts = [_split_bf16(r) for r in log_rests]
        laters = [_dot(hi, tri) + _dot(lo, tri) for hi, lo in parts]
        return [(later, later[:, 0:1] + r[:, 0:1]) for later, r in zip(laters, log_rests)]

    def weight_phase(log_betas, log_remainings, mask):
        ws = [jnp.exp(b + r) for b, r in zip(log_betas, log_remainings)]
        if mask is not None:
            ws = [jnp.where(mask, w, 0.0) for w in ws]
        return [w.astype(BF16) for w in ws]

    col_groups = [slice((h // 2) * LANES, (h // 2 + 1) * LANES) for h in range(heads)]

    def head_queries():
        out = []
        for h in range(heads):
            q = q_ref[:, col_groups[h]]
            keep = first_head if h % 2 == 0 else jnp.logical_not(first_head)
            out.append(jnp.where(keep, q, jnp.zeros_like(q)))
        return out

    own_rows = pl.ds(pl.multiple_of(i * blk, blk), blk)
    halo_rows = pl.ds(pl.multiple_of(jnp.maximum(i * blk - halo, 0), halo), halo)

    def first_steps():
        qms = head_queries()
        sc_o = score_phase(qms, [k_ref[own_rows, c] for c in col_groups], below_diag)
        sc_h = score_phase(qms, [k_ref[halo_rows, c] for c in col_groups], has_halo)
        ls_o = later_phase([rest for _, rest in sc_o], tri_own)
        ls_h = later_phase([rest for _, rest in sc_h], tri_halo)
        w_o = weight_phase([b for b, _ in sc_o], [later for later, _ in ls_o], below_diag)
        w_h = weight_phase([b for b, _ in sc_h],
                           [ls_o[h][1] + ls_h[h][0] for h in range(heads)], has_halo)
        accs = [_dot(w_o[h], v_ref[own_rows, col_groups[h]]) for h in range(heads)]
        accs = [accs[h] + _dot(w_h[h], v_ref[halo_rows, col_groups[h]]) for h in range(heads)]
        return tuple(ls_o[h][1] + ls_h[h][1] for h in range(heads)), tuple(accs)

    def cond(carry):
        j, cs, _ = carry
        c_max = functools.reduce(jnp.maximum, cs)
        return jnp.logical_and(j >= 0, jnp.max(c_max) > F32_EXP_UNDERFLOW)

    def body(carry):
        j, cs, accs = carry
        rows = pl.ds(pl.multiple_of(j * halo, halo), halo)
        sc = score_phase(head_queries(), [k_ref[rows, c] for c in col_groups], None)
        ls = later_phase([rest for _, rest in sc], tri_halo)
        ws = weight_phase([b for b, _ in sc], [cs[h] + ls[h][0] for h in range(heads)], None)
        new_accs = tuple(accs[h] + _dot(ws[h], v_ref[rows, col_groups[h]]) for h in range(heads))
        return j - 1, tuple(cs[h] + ls[h][1] for h in range(heads)), new_accs

    init = (i * (blk // halo) - 2,) + first_steps()
    _, _, accs = lax.while_loop(cond, body, init)
    for p in range(heads // 2):
        o_ref[:, p * LANES:(p + 1) * LANES] = jnp.where(
            first_head, accs[2 * p], accs[2 * p + 1]).astype(o_ref.dtype)


def _sb_attention(sq, sk, sv, batch, seq, blk, halo, heads):
    assert seq % blk == 0 and blk % halo == 0 and halo == LANES
    assert heads % 2 == 0 and SB_HEADS % heads == 0
    nq = seq // blk
    width = heads * SB_HEAD_DIM
    groups = SB_WIDTH // width
    j = np.arange(blk)
    tri = jnp.asarray((j[:, None] > j[None, :]).astype(np.float32), BF16)
    qblk = pl.BlockSpec((blk, width), lambda b, g, i: (b * nq + i, g))
    kvblk = pl.BlockSpec((seq, width), lambda b, g, i: (b, g))
    return pl.pallas_call(
        functools.partial(_sb_attn_kernel, blk=blk, halo=halo, heads=heads),
        grid=(batch, groups, nq),
        in_specs=[qblk, kvblk, kvblk, pl.BlockSpec(tri.shape, lambda b, g, i: (0, 0))],
        out_specs=qblk,
        out_shape=jax.ShapeDtypeStruct((batch * seq, SB_WIDTH), BF16),
        compiler_params=pltpu.CompilerParams(
            dimension_semantics=("arbitrary", "arbitrary", "arbitrary"),
            vmem_limit_bytes=VMEM_LIMIT),
        name="sb_attn",
    )(sq, sk, sv, tri)


def _out_proj_kernel(ret_ref, sb_ref, x_ref, wtop_ref, wbot_ref, g_ref, wrh_ref, wrl_ref,
                     br_ref, tri_ref,
                     x1_ref, xf_ref, idx_ref, w_ref, rank_ref, cnt_ref, run_ref):
    @pl.when(pl.program_id(0) == 0)
    def _():
        run_ref[...] = jnp.zeros_like(run_ref)

    x1 = x_ref[...] + _dot(ret_ref[...], wtop_ref[...]) + _dot(sb_ref[...], wbot_ref[...])
    x1_ref[...] = x1
    xf = x1 * lax.rsqrt(jnp.mean(x1 * x1, axis=-1, keepdims=True) + EPS) * g_ref[...]
    _store_token_tiles(xf_ref, xf)
    xh, xl = _split_bf16(xf)
    wrh = wrh_ref[...]
    logits = _dot_nt(wrh, xh) + _dot_nt(wrh, xl) + _dot_nt(wrl_ref[...], xh) + br_ref[...]
    n_e, tm = logits.shape
    e_iota = lax.broadcasted_iota(jnp.int32, (n_e, tm), 0)
    cur = logits
    tops, sels, hots = [], [], []
    for _ in range(TOP_K):
        m = jnp.max(cur, axis=0, keepdims=True)
        sel = jnp.min(jnp.where(cur == m, e_iota, n_e), axis=0, keepdims=True)
        hot = e_iota == sel
        cur = jnp.where(hot, -jnp.inf, cur)
        tops.append(m)
        sels.append(sel)
        hots.append(hot)
    ps = [jnp.exp(m - tops[0]) for m in tops]
    denom = ps[0] + ps[1] + ps[2] + ps[3]
    chosen = jnp.zeros((n_e, tm), F32)
    for hot in hots:
        chosen = chosen + hot.astype(F32)
    before = run_ref[:, 0:1] + _dot(chosen.astype(BF16), tri_ref[...])
    for kk in range(TOP_K):
        idx_ref[kk:kk + 1, :] = sels[kk]
        w_ref[kk:kk + 1, :] = ps[kk] / denom
        rank = jnp.sum(jnp.where(hots[kk], before, 0.0), axis=0, keepdims=True)
        rank_ref[kk:kk + 1, :] = rank.astype(jnp.int32)
    run = run_ref[...] + jnp.sum(chosen, axis=1, keepdims=True)
    run_ref[...] = run
    cnt_ref[...] = run.astype(jnp.int32)


def _out_proj_route(ret, sb, x2, w_out, ffn_g, w_router, b_router, tm):
    t = x2.shape[0]
    assert t % tm == 0
    w_bf = w_out.astype(BF16)
    wtop, wbot = w_bf[:RET_WIDTH], w_bf[RET_WIDTH:]
    g2 = ffn_g.astype(F32)[None, :]
    wr_t = w_router.astype(F32).T
    wrh = wr_t.astype(BF16)
    wrl = (wr_t - wrh.astype(F32)).astype(BF16)
    br = b_router.astype(F32)[:, None]
    tt = np.arange(tm)
    tri = jnp.asarray((tt[:, None] < tt[None, :]).astype(np.float32), BF16)
    row = lambda w: pl.BlockSpec((tm, w), lambda i: (i, 0))
    full = lambda a: pl.BlockSpec(a.shape, lambda i: (0,) * a.ndim)
    col = pl.BlockSpec((TOP_K, tm), lambda i: (0, i))
    return pl.pallas_call(
        _out_proj_kernel,
        grid=(t // tm,),
        in_specs=[row(RET_WIDTH), row(SB_WIDTH), row(D_MODEL), full(wtop), full(wbot), full(g2),
                  full(wrh), full(wrl), full(br), full(tri)],
        out_specs=[row(D_MODEL), pl.BlockSpec((tm * SUBLANES, LANES), lambda i: (i, 0)),
                   col, col, col, pl.BlockSpec((N_EXPERTS, LANES), lambda i: (0, 0))],
        out_shape=[jax.ShapeDtypeStruct((t, D_MODEL), F32),
                   jax.ShapeDtypeStruct((t * SUBLANES, LANES), F32),
                   jax.ShapeDtypeStruct((TOP_K, t), jnp.int32),
                   jax.ShapeDtypeStruct((TOP_K, t), F32),
                   jax.ShapeDtypeStruct((TOP_K, t), jnp.int32),
                   jax.ShapeDtypeStruct((N_EXPERTS, LANES), jnp.int32)],
        scratch_shapes=[pltpu.VMEM((N_EXPERTS, LANES), F32)],
        compiler_params=pltpu.CompilerParams(
            dimension_semantics=("arbitrary",), vmem_limit_bytes=VMEM_LIMIT),
        name="out_proj_route",
    )(ret, sb, x2, wtop, wbot, g2, wrh, wrl, br, tri)


def _dest_kernel(pstart_ref, idx_ref, rank_ref, dest_ref):
    idx = idx_ref[...]
    dest = rank_ref[...]
    for e in range(N_EXPERTS):
        dest = dest + jnp.where(idx == e, pstart_ref[e], 0)
    dest_ref[...] = dest


def _dest(padded_start, top_idx, rank, tn):
    t = top_idx.shape[1]
    assert t % tn == 0
    col = pl.BlockSpec((TOP_K, tn), lambda i, ps: (0, i))
    return pl.pallas_call(
        _dest_kernel,
        grid_spec=pltpu.PrefetchScalarGridSpec(
            num_scalar_prefetch=1, grid=(t // tn,), in_specs=[col, col], out_specs=col),
        out_shape=jax.ShapeDtypeStruct((TOP_K, t), jnp.int32),
        name="dest",
    )(padded_start, top_idx, rank)


def _dispatch_kernel(pend_ref, dest_ref, xf_ref, slots_hbm, zbuf, sem, zsem, *, tm, blk):
    @pl.when(pl.program_id(0) == 0)
    def _():
        zbuf[...] = jnp.zeros_like(zbuf)

        def tail_copy(e):
            first = pl.multiple_of((pend_ref[e] - blk) * SUBLANES, blk * SUBLANES)
            return pltpu.make_async_copy(
                zbuf, slots_hbm.at[pl.ds(first, blk * SUBLANES), :], zsem)

        def nonempty(e):
            return pend_ref[e] > (pend_ref[e - 1] if e else 0)

        for e in range(N_EXPERTS):
            pl.when(nonempty(e))(lambda e=e: tail_copy(e).start())
        for e in range(N_EXPERTS):
            pl.when(nonempty(e))(lambda e=e: tail_copy(e).wait())

    def row_copy(t, kk):
        return pltpu.make_async_copy(
            xf_ref.at[_token_rows(t), :],
            slots_hbm.at[_token_rows(dest_ref[kk, t]), :], sem)

    def start(t, carry):
        for kk in range(TOP_K):
            row_copy(t, kk).start(priority=kk % 2)
        return carry

    lax.fori_loop(0, tm, start, 0, unroll=8)
    for kk in range(TOP_K):
        pltpu.make_async_copy(
            xf_ref, slots_hbm.at[pl.ds(0, tm * SUBLANES), :], sem).wait()


def _dispatch(padded_end, dest, xf, n_pad, tm, blk):
    t = xf.shape[0] // SUBLANES
    assert t % tm == 0
    return pl.pallas_call(
        functools.partial(_dispatch_kernel, tm=tm, blk=blk),
        grid_spec=pltpu.PrefetchScalarGridSpec(
            num_scalar_prefetch=1,
            grid=(t // tm,),
            in_specs=[pl.BlockSpec((TOP_K, tm), lambda i, pe: (0, i), memory_space=pltpu.SMEM),
                      pl.BlockSpec((tm * SUBLANES, LANES), lambda i, pe: (i, 0))],
            out_specs=pl.BlockSpec(memory_space=pl.ANY),
            scratch_shapes=[pltpu.VMEM((blk * SUBLANES, LANES), F32),
                            pltpu.SemaphoreType.DMA, pltpu.SemaphoreType.DMA],
        ),
        out_shape=jax.ShapeDtypeStruct((n_pad * SUBLANES, LANES), F32),
        compiler_params=pltpu.CompilerParams(
            dimension_semantics=("arbitrary",), vmem_limit_bytes=VMEM_LIMIT,
            has_side_effects=True),
        name="dispatch",
    )(padded_end, dest, xf)


WEIGHT_ROWS = 128


def _experts_kernel(be_ref, nu_ref, run_ref, next_ref, x_ref, wgu_hbm, wd_hbm, bg_ref, bu_ref,
                    bd_ref, y_ref, wgu_buf, wd_buf, wg_s, wu_s, wd_s, t_s, sem, *, blk):
    j = pl.program_id(0)
    used = j < nu_ref[0]
    expert = be_ref[j]
    new_expert = jnp.logical_or(j == 0, expert != be_ref[jnp.maximum(j - 1, 0)])

    def weight_copies(e, slot):
        return (pltpu.make_async_copy(wgu_hbm.at[e], wgu_buf.at[slot], sem.at[0, slot]),
                pltpu.make_async_copy(wd_hbm.at[e], wd_buf.at[slot], sem.at[1, slot]))

    @pl.when(jnp.logical_and(used, new_expert))
    def _():
        slot = lax.rem(run_ref[expert], 2)

        @pl.when(j == 0)
        def _():
            for c in weight_copies(expert, slot):
                c.start()

        for c in weight_copies(expert, slot):
            c.wait()
        following = next_ref[expert]

        @pl.when(following >= 0)
        def _():
            for c in weight_copies(following, 1 - slot):
                c.start()

        def rows_step(r, carry):
            rows = pl.ds(pl.multiple_of(r * WEIGHT_ROWS, WEIGHT_ROWS), WEIGHT_ROWS)
            t_s[...] = wgu_buf[slot, rows, :].T
            wg_s[rows, :] = t_s[pl.ds(0, D_FF, stride=2), :].T.astype(BF16)
            wu_s[rows, :] = t_s[pl.ds(1, D_FF, stride=2), :].T.astype(BF16)
            wd_s[rows, :] = wd_buf[slot, rows, :].astype(BF16)
            return carry

        lax.fori_loop(0, D_MODEL // WEIGHT_ROWS, rows_step, 0)

    @pl.when(used)
    def _():
        xb = _load_token_tiles(x_ref, blk).astype(BF16)
        gate = jnp.minimum(_dot(xb, wg_s[...]) + bg_ref[...], SWIGLU_LIMIT)
        up = jnp.clip(_dot(xb, wu_s[...]) + bu_ref[...], -SWIGLU_LIMIT, SWIGLU_LIMIT)
        hidden = (up + 1.0) * gate * jax.nn.sigmoid(SWIGLU_ALPHA * gate)
        _store_token_tiles(y_ref, _dot(hidden.astype(BF16), wd_s[...]) + bd_ref[...])

    @pl.when(jnp.logical_not(used))
    def _():
        y_ref[...] = jnp.zeros_like(y_ref)


def _experts(block_expert, n_used, expert_run, next_expert, slots, wgu, wd, bg, bu, bd, blk):
    assert D_FF == D_MODEL
    n_pad = slots.shape[0] // SUBLANES
    n_blocks = n_pad // blk
    xmap = lambda j, be, nu, run, nxt: (jnp.minimum(j, nu[0] - 1), 0)
    wmap = lambda j, be, nu, run, nxt: (be[j], 0, 0)
    bspec = pl.BlockSpec((None, 1, D_FF), wmap)
    tiles = (blk * SUBLANES, LANES)
    hbm = pl.BlockSpec(memory_space=pl.ANY)
    return pl.pallas_call(
        functools.partial(_experts_kernel, blk=blk),
        grid_spec=pltpu.PrefetchScalarGridSpec(
            num_scalar_prefetch=4,
            grid=(n_blocks,),
            in_specs=[pl.BlockSpec(tiles, xmap), hbm, hbm, bspec, bspec, bspec],
            out_specs=pl.BlockSpec(tiles, lambda j, be, nu, run, nxt: (j, 0)),
            scratch_shapes=[pltpu.VMEM((2, D_MODEL, 2 * D_FF), F32),
                            pltpu.VMEM((2, D_FF, D_MODEL), F32),
                            pltpu.VMEM((D_MODEL, D_FF), BF16), pltpu.VMEM((D_MODEL, D_FF), BF16),
                            pltpu.VMEM((D_FF, D_MODEL), BF16),
                            pltpu.VMEM((2 * D_FF, WEIGHT_ROWS), F32),
                            pltpu.SemaphoreType.DMA((2, 2))],
        ),
        out_shape=jax.ShapeDtypeStruct((n_pad * SUBLANES, LANES), F32),
        compiler_params=pltpu.CompilerParams(
            dimension_semantics=("arbitrary",), vmem_limit_bytes=VMEM_LIMIT),
        name="experts",
    )(block_expert, n_used, expert_run, next_expert, slots, wgu, wd, bg, bu, bd)


def _combine_kernel(dest_ref, next_dest_ref, y_hbm, w_ref, x1_ref, o_ref, buf, sem, *, tm):
    i = pl.program_id(0)
    n = pl.num_programs(0)
    slot = lax.rem(i, 2)

    def gather(dests, into):
        def start(t, carry):
            for kk in range(TOP_K):
                pltpu.make_async_copy(
                    y_hbm.at[_token_rows(dests[kk, t]), :],
                    buf.at[into, kk, _token_rows(t), :], sem.at[into]).start(priority=kk % 2)
            return carry

        lax.fori_loop(0, tm, start, 0, unroll=8)

    pl.when(i == 0)(lambda: gather(dest_ref, 0))
    pl.when(i + 1 < n)(lambda: gather(next_dest_ref, 1 - slot))
    for kk in range(TOP_K):
        pltpu.make_async_copy(
            y_hbm.at[pl.ds(0, tm * SUBLANES), :], buf.at[slot, kk], sem.at[slot]).wait()
    w = w_ref[...]
    for s in range(ROW_TILES):
        cols = slice(s * LANES, (s + 1) * LANES)
        acc = x1_ref[:, cols]
        for kk in range(TOP_K):
            acc = acc + buf[slot, kk, pl.ds(s, tm, stride=SUBLANES), :] * w[:, kk:kk + 1]
        o_ref[:, cols] = acc


def _combine(dest, y_slots, w_t, x1, tm):
    t = x1.shape[0]
    assert t % tm == 0
    n = t // tm
    row = pl.BlockSpec((tm, D_MODEL), lambda i: (i, 0))
    return pl.pallas_call(
        functools.partial(_combine_kernel, tm=tm),
        grid=(n,),
        in_specs=[pl.BlockSpec((TOP_K, tm), lambda i: (0, i), memory_space=pltpu.SMEM),
                  pl.BlockSpec((TOP_K, tm), lambda i: (0, jnp.minimum(i + 1, n - 1)),
                               memory_space=pltpu.SMEM),
                  pl.BlockSpec(memory_space=pl.ANY),
                  pl.BlockSpec((tm, TOP_K), lambda i: (i, 0)),
                  row],
        out_specs=row,
        out_shape=jax.ShapeDtypeStruct((t, D_MODEL), F32),
        scratch_shapes=[pltpu.VMEM((2, TOP_K, tm * SUBLANES, LANES), F32),
                        pltpu.SemaphoreType.DMA((2,))],
        compiler_params=pltpu.CompilerParams(
            dimension_semantics=("arbitrary",), vmem_limit_bytes=VMEM_LIMIT),
        name="combine",
    )(dest, dest, y_slots, w_t, x1)


def _tiles(batch, seq):
    return dict(
        in_proj=min(512, seq),
        retention=min(1024, seq),
        sb_block=min(256, seq),
        sb_heads=4,
        out_proj=min(1024, seq),
        dest=min(8192, batch * seq),
        dispatch=min(2048, seq),
        combine=min(512, seq),
        expert_block=512,
    )


def _layer(x, attn_norm_g, w_in, ret_norm_g, sb_q_norm_g, sb_k_norm_g, w_out,
           ffn_norm_g, w_router, b_router, w_gate_up, b_gate_up, w_down, b_down):
    batch, seq, d = x.shape
    t = batch * seq
    tiles = _tiles(batch, seq)
    x2 = x.reshape(t, d)

    rq, rk, rv, rg, sq, sk, sv = _in_proj(
        x2, attn_norm_g, w_in, sb_q_norm_g, sb_k_norm_g, seq, tiles["in_proj"])
    ret = _retention(rq, rk, rv, rg, ret_norm_g, batch, seq, tiles["retention"])
    sb = _sb_attention(sq, sk, sv, batch, seq, tiles["sb_block"], LANES, tiles["sb_heads"])
    x1, xf, top_idx, top_w, rank, counts = _out_proj_route(
        ret, sb, x2, w_out, ffn_norm_g, w_router, b_router, tiles["out_proj"])

    blk = tiles["expert_block"]
    n_pad = t * TOP_K + N_EXPERTS * blk
    n_blocks = n_pad // blk
    counts = counts[:, 0]
    padded = (counts + blk - 1) // blk * blk
    padded_end = jnp.cumsum(padded)
    padded_start = padded_end - padded
    block_first = jnp.arange(n_blocks, dtype=jnp.int32) * blk
    block_expert = jnp.minimum(
        jnp.sum(padded_end[None, :] <= block_first[:, None], axis=1), N_EXPERTS - 1
    ).astype(jnp.int32)
    n_used = (padded_end[-1:] // blk).astype(jnp.int32)
    nonempty = counts > 0
    expert_run = (jnp.cumsum(nonempty) - nonempty).astype(jnp.int32)
    experts = jnp.arange(N_EXPERTS, dtype=jnp.int32)
    later = jnp.logical_and(nonempty[None, :], experts[None, :] > experts[:, None])
    next_expert = jnp.min(jnp.where(later, experts[None, :], N_EXPERTS), axis=1)
    next_expert = jnp.where(next_expert < N_EXPERTS, next_expert, -1).astype(jnp.int32)

    dest = _dest(padded_start, top_idx, rank, tiles["dest"])
    slots = _dispatch(padded_end, dest, xf, n_pad, tiles["dispatch"], blk)

    bgu = b_gate_up.astype(F32).reshape(N_EXPERTS, 1, D_FF, 2)
    y_slots = _experts(block_expert, n_used, expert_run, next_expert, slots,
                       w_gate_up.astype(F32), w_down.astype(F32),
                       bgu[..., 0], bgu[..., 1], b_down.astype(F32)[:, None, :], blk)

    out = _combine(dest, y_slots, top_w.T, x1, tiles["combine"])
    return out.reshape(batch, seq, d)


def kernel(x, attn_norm_g, w_in, ret_norm_g, sb_q_norm_g, sb_k_norm_g, w_out, ffn_norm_g,
           w_router, b_router, w_gate_up, b_gate_up, w_down, b_down):
    depth = attn_norm_g.shape[0]
    for l in range(depth):
        x = _layer(x, attn_norm_g[l], w_in[l], ret_norm_g[l], sb_q_norm_g[l], sb_k_norm_g[l],
                   w_out[l], ffn_norm_g[l], w_router[l], b_router[l], w_gate_up[l],
                   b_gate_up[l], w_down[l], b_down[l])
    return x
```

```python
import functools

import numpy as np
import jax
import jax.numpy as jnp
from jax import lax
from jax.experimental import pallas as pl
from jax.experimental.pallas import tpu as pltpu

D_MODEL = 1024
RET_WIDTH = 512
RET_HEADS = 4
RET_HEAD_DIM = 128
SB_WIDTH = 512
SB_HEADS = 8
SB_HEAD_DIM = 64
IN_PROJ_WIDTH = 4 * RET_WIDTH + 3 * SB_WIDTH
RET_CHUNK = 128
ROPE_BASE = 10000.0
N_EXPERTS = 32
TOP_K = 4
D_FF = D_MODEL
SWIGLU_LIMIT = 7.0
SWIGLU_ALPHA = 1.702
EPS = 1e-6

LANES = 128
MXU_WIDTH = 256
F32_EXP_UNDERFLOW = -88.0
VMEM_LIMIT = 56 * 1024 * 1024

BF16 = jnp.bfloat16
F32 = jnp.float32


def _split_bf16(v):
    hi = v.astype(BF16)
    lo = (v - hi.astype(F32)).astype(BF16)
    return hi, lo


SUBLANES = 8
ROW_TILES = D_MODEL // LANES
assert ROW_TILES == SUBLANES


def _store_token_tiles(ref, value):
    n = value.shape[0]
    for s in range(ROW_TILES):
        ref[pl.ds(s, n, stride=SUBLANES), :] = value[:, s * LANES:(s + 1) * LANES]


def _load_token_tiles(ref, n):
    return jnp.concatenate(
        [ref[pl.ds(s, n, stride=SUBLANES), :] for s in range(ROW_TILES)], axis=1)


def _token_rows(i):
    return pl.ds(pl.multiple_of(i * SUBLANES, SUBLANES), SUBLANES)


def _dot(a, b):
    return jnp.dot(a, b, preferred_element_type=F32)


def _dot_nt(a, b):
    return lax.dot_general(a, b, (((1,), (1,)), ((), ())), preferred_element_type=F32)


def _in_proj_kernel(x_ref, g_ref, w_ref, cos_ref, sin_ref, qg_ref, kg_ref, seg_ref,
                    rq_ref, rk_ref, rv_ref, rg_ref, sq_ref, sk_ref, sv_ref):
    x = x_ref[...]
    h = x * lax.rsqrt(jnp.mean(x * x, axis=-1, keepdims=True) + EPS) * g_ref[...]
    p = _dot(h.astype(BF16), w_ref[...])
    cos2 = cos_ref[...]
    sin2 = sin_ref[...]
    k_scale = RET_HEAD_DIM ** -0.5
    for hd in range(RET_HEADS):
        lo = hd * RET_HEAD_DIM
        q = p[:, lo:lo + RET_HEAD_DIM]
        k = p[:, RET_WIDTH + lo:RET_WIDTH + lo + RET_HEAD_DIM]
        q = q * cos2 + pltpu.roll(q, RET_HEAD_DIM // 2, axis=1) * sin2
        k = k * cos2 + pltpu.roll(k, RET_HEAD_DIM // 2, axis=1) * sin2
        rq_ref[:, lo:lo + RET_HEAD_DIM] = q.astype(rq_ref.dtype)
        rk_ref[:, lo:lo + RET_HEAD_DIM] = (k * k_scale).astype(rk_ref.dtype)
    rv_ref[...] = p[:, 2 * RET_WIDTH:3 * RET_WIDTH].astype(rv_ref.dtype)
    rg_ref[...] = p[:, 3 * RET_WIDTH:4 * RET_WIDTH]
    base = 4 * RET_WIDTH
    seg = seg_ref[...]

    def head_norm(v, gain):
        hi, lo = _split_bf16(v * v)
        group = seg.shape[0]
        ms = jnp.concatenate(
            [_dot(hi[:, g:g + group], seg) + _dot(lo[:, g:g + group], seg)
             for g in range(0, SB_WIDTH, group)], axis=1) * (1.0 / SB_HEAD_DIM)
        return v * lax.rsqrt(ms + EPS) * gain

    sq = head_norm(p[:, base:base + SB_WIDTH], qg_ref[...])
    sk = head_norm(p[:, base + SB_WIDTH:base + 2 * SB_WIDTH], kg_ref[...])
    sq_ref[...] = (sq * (SB_HEAD_DIM ** -0.5)).astype(sq_ref.dtype)
    sk_ref[...] = sk.astype(sk_ref.dtype)
    sv_ref[...] = p[:, base + 2 * SB_WIDTH:base + 3 * SB_WIDTH].astype(sv_ref.dtype)


def _rope_tables(seq):
    half = RET_HEAD_DIM // 2
    inv = ROPE_BASE ** (-np.arange(half, dtype=np.float64) / half)
    n_a = seq // LANES
    ang_a = (np.arange(n_a, dtype=np.float64) * LANES)[:, None] * inv[None, :]
    ang_b = np.arange(LANES, dtype=np.float64)[:, None] * inv[None, :]
    ca, sa = jnp.asarray(np.cos(ang_a), F32), jnp.asarray(np.sin(ang_a), F32)
    cb, sb = jnp.asarray(np.cos(ang_b), F32), jnp.asarray(np.sin(ang_b), F32)
    cos = (ca[:, None, :] * cb[None] - sa[:, None, :] * sb[None]).reshape(seq, half)
    sin = (sa[:, None, :] * cb[None] + ca[:, None, :] * sb[None]).reshape(seq, half)
    return jnp.concatenate([cos, cos], axis=-1), jnp.concatenate([-sin, sin], axis=-1)


def _in_proj(x2, norm_g, w_in, sb_q_g, sb_k_g, seq, tm):
    t = x2.shape[0]
    assert seq % tm == 0 and t % tm == 0
    cos2, sin2 = _rope_tables(seq)
    seg = np.kron(np.eye(MXU_WIDTH // SB_HEAD_DIM), np.ones((SB_HEAD_DIM, SB_HEAD_DIM)))
    seg = jnp.asarray(seg, BF16)
    qg = jnp.tile(sb_q_g.astype(F32), SB_HEADS)[None, :]
    kg = jnp.tile(sb_k_g.astype(F32), SB_HEADS)[None, :]
    n_pos = seq // tm
    row = lambda w: pl.BlockSpec((tm, w), lambda i: (i, 0))
    full = lambda a: pl.BlockSpec(a.shape, lambda i: (0,) * a.ndim)
    pos = pl.BlockSpec((tm, RET_HEAD_DIM), lambda i: (i % n_pos, 0))
    w_bf = w_in.astype(BF16)
    g2 = norm_g.astype(F32)[None, :]
    out = lambda dt: jax.ShapeDtypeStruct((t, RET_WIDTH), dt)
    return pl.pallas_call(
        _in_proj_kernel,
        grid=(t // tm,),
        in_specs=[row(D_MODEL), full(g2), full(w_bf), pos, pos, full(qg), full(kg), full(seg)],
        out_specs=[row(RET_WIDTH)] * 7,
        out_shape=[out(BF16), out(F32), out(BF16), out(F32), out(BF16), out(BF16), out(BF16)],
        compiler_params=pltpu.CompilerParams(
            dimension_semantics=("arbitrary",), vmem_limit_bytes=VMEM_LIMIT),
        name="in_proj",
    )(x2, g2, w_bf, cos2, sin2, qg, kg, seg)


def _retention_kernel(q_ref, k_ref, v_ref, g_ref, ng_ref, intra_ref, qd_ref, kd_ref, cd_ref,
                      o_ref, state_ref, *, chunks):
    @pl.when(pl.program_id(1) == 0)
    def _():
        state_ref[...] = jnp.zeros_like(state_ref)

    def chunk(c, carry):
        rows = pl.ds(pl.multiple_of(c * RET_CHUNK, RET_CHUNK), RET_CHUNK)
        heads = range(RET_HEADS)
        cols = [slice(hd * RET_HEAD_DIM, (hd + 1) * RET_HEAD_DIM) for hd in heads]
        q = [q_ref[rows, cl] for cl in cols]
        k = [k_ref[rows, cl] for cl in cols]
        v = [v_ref[rows, cl] for cl in cols]
        state = [state_ref[hd] for hd in heads]
        scores = [_dot_nt(q[hd], k[hd].astype(BF16)) * intra_ref[hd] for hd in heads]
        cross = [_dot(q[hd], state[hd].astype(BF16)) * qd_ref[:, cols[hd]] for hd in heads]
        inner = [_dot(scores[hd].astype(BF16), v[hd]) for hd in heads]
        kd = [(k[hd] * kd_ref[:, cols[hd]]).astype(BF16) for hd in heads]
        for hd in heads:
            state_ref[hd] = state[hd] * cd_ref[:, cols[hd]] + _dot(kd[hd].T, v[hd])
        for hd in heads:
            o = inner[hd] + cross[hd]
            o = o * lax.rsqrt(jnp.mean(o * o, axis=-1, keepdims=True) + EPS) * ng_ref[:, cols[hd]]
            gate = g_ref[rows, cols[hd]]
            o_ref[rows, cols[hd]] = (o * (gate * jax.nn.sigmoid(gate))).astype(o_ref.dtype)
        return carry

    lax.fori_loop(0, chunks, chunk, 0, unroll=True)


def _retention(rq, rk, rv, rg, ret_norm_g, batch, seq, rows):
    assert seq % rows == 0 and rows % RET_CHUNK == 0
    log_gamma = np.log(1.0 - 2.0 ** (-5.0 - np.arange(RET_HEADS, dtype=np.float64)))
    idx = np.arange(RET_CHUNK, dtype=np.float64)
    diff = idx[:, None] - idx[None, :]
    intra = np.where(diff >= 0, np.exp(log_gamma[:, None, None] * np.maximum(diff, 0.0)), 0.0)
    q_decay = np.exp(log_gamma[:, None] * (idx + 1.0))
    k_decay = np.exp(log_gamma[:, None] * (RET_CHUNK - 1.0 - idx))
    chunk_decay = np.exp(log_gamma * RET_CHUNK)
    lane_rep = lambda a: np.repeat(a.T, RET_HEAD_DIM, axis=1)
    intra = jnp.asarray(intra, F32)
    qd = jnp.asarray(lane_rep(q_decay), F32)
    kd = jnp.asarray(lane_rep(k_decay), F32)
    cd = jnp.asarray(np.repeat(chunk_decay, RET_HEAD_DIM)[None, :], F32)
    ng = ret_norm_g.astype(F32).reshape(1, RET_WIDTH)
    n_r = seq // rows
    blk = pl.BlockSpec((rows, RET_WIDTH), lambda b, r: (b * n_r + r, 0))
    full = lambda a: pl.BlockSpec(a.shape, lambda b, r: (0,) * a.ndim)
    return pl.pallas_call(
        functools.partial(_retention_kernel, chunks=rows // RET_CHUNK),
        grid=(batch, n_r),
        in_specs=[blk, blk, blk, blk, full(ng), full(intra), full(qd), full(kd), full(cd)],
        out_specs=blk,
        out_shape=jax.ShapeDtypeStruct((batch * seq, RET_WIDTH), BF16),
        scratch_shapes=[pltpu.VMEM((RET_HEADS, RET_HEAD_DIM, RET_HEAD_DIM), F32)],
        compiler_params=pltpu.CompilerParams(
            dimension_semantics=("arbitrary", "arbitrary"), vmem_limit_bytes=VMEM_LIMIT),
        name="retention",
    )(rq, rk, rv, rg, ng, intra, qd, kd, cd)


def _sb_attn_kernel(q_ref, k_ref, v_ref, tri_ref, o_ref, *, blk, halo, heads):
    i = pl.program_id(2)
    lane = lax.broadcasted_iota(jnp.int32, (blk, LANES), 1)
    first_head = lane < SB_HEAD_DIM
    tri_own = tri_ref[...]
    tri_halo = tri_ref[:halo, :halo]
    below_diag = (lax.broadcasted_iota(jnp.int32, (blk, blk), 1)
                  < lax.broadcasted_iota(jnp.int32, (blk, blk), 0))
    has_halo = i > 0


    def scores(z, mask):
        log_beta = jnp.minimum(z, 0.0) - jnp.log(1.0 + jnp.exp(-jnp.abs(z)))
        log_rest = log_beta - z
        if mask is not None:
            log_rest = jnp.where(mask, log_rest, 0.0)
        return log_beta, log_rest

    def score_phase(qms, kbs, mask):
        zs = [_dot_nt(qm, kb) for qm, kb in zip(qms, kbs)]
        return [scores(z, mask) for z in zs]

    def later_phase(log_rests, tri):
        parts = [_split_bf16(r) for r in log_rests]
        laters = [_dot(hi, tri) + _dot(lo, tri) for hi, lo in parts]
        return [(later, later[:, 0:1] + r[:, 0:1]) for later, r in zip(laters, log_rests)]

    def weight_phase(log_betas, log_remainings, mask):
        ws = [jnp.exp(b + r) for b, r in zip(log_betas, log_remainings)]
        if mask is not None:
            ws = [jnp.where(mask, w, 0.0) for w in ws]
        return [w.astype(BF16) for w in ws]

    col_groups = [slice((h // 2) * LANES, (h // 2 + 1) * LANES) for h in range(heads)]

    def head_queries():
        out = []
        for h in range(heads):
            q = q_ref[:, col_groups[h]]
            keep = first_head if h % 2 == 0 else jnp.logical_not(first_head)
            out.append(jnp.where(keep, q, jnp.zeros_like(q)))
        return out

    own_rows = pl.ds(pl.multiple_of(i * blk, blk), blk)
    halo_rows = pl.ds(pl.multiple_of(jnp.maximum(i * blk - halo, 0), halo), halo)

    def first_steps():
        qms = head_queries()
        sc_o = score_phase(qms, [k_ref[own_rows, c] for c in col_groups], below_diag)
        sc_h = score_phase(qms, [k_ref[halo_rows, c] for c in col_groups], has_halo)
        ls_o = later_phase([rest for _, rest in sc_o], tri_own)
        ls_h = later_phase([rest for _, rest in sc_h], tri_halo)
        w_o = weight_phase([b for b, _ in sc_o], [later for later, _ in ls_o], below_diag)
        w_h = weight_phase([b for b, _ in sc_h],
                           [ls_o[h][1] + ls_h[h][0] for h in range(heads)], has_halo)
        accs = [_dot(w_o[h], v_ref[own_rows, col_groups[h]]) for h in range(heads)]
        accs = [accs[h] + _dot(w_h[h], v_ref[halo_rows, col_groups[h]]) for h in range(heads)]
        return tuple(ls_o[h][1] + ls_h[h][1] for h in range(heads)), tuple(accs)

    def cond(carry):
        j, cs, _ = carry
        c_max = functools.reduce(jnp.maximum, cs)
        return jnp.logical_and(j >= 0, jnp.max(c_max) > F32_EXP_UNDERFLOW)

    def body(carry):
        j, cs, accs = carry
        rows = pl.ds(pl.multiple_of(j * halo, halo), halo)
        sc = score_phase(head_queries(), [k_ref[rows, c] for c in col_groups], None)
        ls = later_phase([rest for _, rest in sc], tri_halo)
        ws = weight_phase([b for b, _ in sc], [cs[h] + ls[h][0] for h in range(heads)], None)
        new_accs = tuple(accs[h] + _dot(ws[h], v_ref[rows, col_groups[h]]) for h in range(heads))
        return j - 1, tuple(cs[h] + ls[h][1] for h in range(heads)), new_accs

    init = (i * (blk // halo) - 2,) + first_steps()
    _, _, accs = lax.while_loop(cond, body, init)
    for p in range(heads // 2):
        o_ref[:, p * LANES:(p + 1) * LANES] = jnp.where(
            first_head, accs[2 * p], accs[2 * p + 1]).astype(o_ref.dtype)


def _sb_attention(sq, sk, sv, batch, seq, blk, halo, heads):
    assert seq % blk == 0 and blk % halo == 0 and halo == LANES
    assert heads % 2 == 0 and SB_HEADS % heads == 0
    nq = seq // blk
    width = heads * SB_HEAD_DIM
    groups = SB_WIDTH // width
    j = np.arange(blk)
    tri = jnp.asarray((j[:, None] > j[None, :]).astype(np.float32), BF16)
    qblk = pl.BlockSpec((blk, width), lambda b, g, i: (b * nq + i, g))
    kvblk = pl.BlockSpec((seq, width), lambda b, g, i: (b, g))
    return pl.pallas_call(
        functools.partial(_sb_attn_kernel, blk=blk, halo=halo, heads=heads),
        grid=(batch, groups, nq),
        in_specs=[qblk, kvblk, kvblk, pl.BlockSpec(tri.shape, lambda b, g, i: (0, 0))],
        out_specs=qblk,
        out_shape=jax.ShapeDtypeStruct((batch * seq, SB_WIDTH), BF16),
        compiler_params=pltpu.CompilerParams(
            dimension_semantics=("arbitrary", "arbitrary", "arbitrary"),
            vmem_limit_bytes=VMEM_LIMIT),
        name="sb_attn",
    )(sq, sk, sv, tri)


OUT_PROJ_ROW_GROUPS = 4


def _out_proj_kernel(ret_ref, sb_ref, x_ref, wtop_ref, wbot_ref, g_ref, wrh_ref, wrl_ref,
                     br_ref, tri_ref,
                     x1_ref, xf_ref, idx_ref, w_ref, rank_ref, cnt_ref, run_ref):
    @pl.when(pl.program_id(0) == 0)
    def _():
        run_ref[...] = jnp.zeros_like(run_ref)

    rows = x_ref.shape[0] // OUT_PROJ_ROW_GROUPS
    groups = [slice(r * rows, (r + 1) * rows) for r in range(OUT_PROJ_ROW_GROUPS)]
    x1s = [x_ref[g, :] + _dot(ret_ref[g, :], wtop_ref[...]) + _dot(sb_ref[g, :], wbot_ref[...])
           for g in groups]
    for g, x1 in zip(groups, x1s):
        x1_ref[g, :] = x1
    xfs = [x1 * lax.rsqrt(jnp.mean(x1 * x1, axis=-1, keepdims=True) + EPS) * g_ref[...]
           for x1 in x1s]
    for r, xf in enumerate(xfs):
        _store_token_tiles(xf_ref.at[pl.ds(r * rows * SUBLANES, rows * SUBLANES), :], xf)
    parts = [_split_bf16(xf) for xf in xfs]
    wrh = wrh_ref[...]
    wrl = wrl_ref[...]
    logits = jnp.concatenate(
        [_dot_nt(wrh, xh) + _dot_nt(wrh, xl) + _dot_nt(wrl, xh) for xh, xl in parts],
        axis=1) + br_ref[...]
    n_e, tm = logits.shape
    e_iota = lax.broadcasted_iota(jnp.int32, (n_e, tm), 0)
    cur = logits
    tops, sels, hots = [], [], []
    for _ in range(TOP_K):
        m = jnp.max(cur, axis=0, keepdims=True)
        sel = jnp.min(jnp.where(cur == m, e_iota, n_e), axis=0, keepdims=True)
        hot = e_iota == sel
        cur = jnp.where(hot, -jnp.inf, cur)
        tops.append(m)
        sels.append(sel)
        hots.append(hot)
    ps = [jnp.exp(m - tops[0]) for m in tops]
    denom = ps[0] + ps[1] + ps[2] + ps[3]
    chosen = jnp.zeros((n_e, tm), F32)
    for hot in hots:
        chosen = chosen + hot.astype(F32)
    before = run_ref[:, 0:1] + _dot(chosen.astype(BF16), tri_ref[...])
    for kk in range(TOP_K):
        idx_ref[kk:kk + 1, :] = sels[kk]
        w_ref[kk:kk + 1, :] = ps[kk] / denom
        rank = jnp.sum(jnp.where(hots[kk], before, 0.0), axis=0, keepdims=True)
        rank_ref[kk:kk + 1, :] = rank.astype(jnp.int32)
    run = run_ref[...] + jnp.sum(chosen, axis=1, keepdims=True)
    run_ref[...] = run
    cnt_ref[...] = run.astype(jnp.int32)


def _out_proj_route(ret, sb, x2, w_out, ffn_g, w_router, b_router, tm):
    t = x2.shape[0]
    assert t % tm == 0
    w_bf = w_out.astype(BF16)
    wtop, wbot = w_bf[:RET_WIDTH], w_bf[RET_WIDTH:]
    g2 = ffn_g.astype(F32)[None, :]
    wr_t = w_router.astype(F32).T
    wrh = wr_t.astype(BF16)
    wrl = (wr_t - wrh.astype(F32)).astype(BF16)
    br = b_router.astype(F32)[:, None]
    tt = np.arange(tm)
    tri = jnp.asarray((tt[:, None] < tt[None, :]).astype(np.float32), BF16)
    row = lambda w: pl.BlockSpec((tm, w), lambda i: (i, 0))
    full = lambda a: pl.BlockSpec(a.shape, lambda i: (0,) * a.ndim)
    col = pl.BlockSpec((TOP_K, tm), lambda i: (0, i))
    return pl.pallas_call(
        _out_proj_kernel,
        grid=(t // tm,),
        in_specs=[row(RET_WIDTH), row(SB_WIDTH), row(D_MODEL), full(wtop), full(wbot), full(g2),
                  full(wrh), full(wrl), full(br), full(tri)],
        out_specs=[row(D_MODEL), pl.BlockSpec((tm * SUBLANES, LANES), lambda i: (i, 0)),
                   col, col, col, pl.BlockSpec((N_EXPERTS, LANES), lambda i: (0, 0))],
        out_shape=[jax.ShapeDtypeStruct((t, D_MODEL), F32),
                   jax.ShapeDtypeStruct((t * SUBLANES, LANES), F32),
                   jax.ShapeDtypeStruct((TOP_K, t), jnp.int32),
                   jax.ShapeDtypeStruct((TOP_K, t), F32),
                   jax.ShapeDtypeStruct((TOP_K, t), jnp.int32),
                   jax.ShapeDtypeStruct((N_EXPERTS, LANES), jnp.int32)],
        scratch_shapes=[pltpu.VMEM((N_EXPERTS, LANES), F32)],
        compiler_params=pltpu.CompilerParams(
            dimension_semantics=("arbitrary",), vmem_limit_bytes=VMEM_LIMIT),
        name="out_proj_route",
    )(ret, sb, x2, wtop, wbot, g2, wrh, wrl, br, tri)


def _dest_kernel(pstart_ref, idx_ref, rank_ref, dest_ref):
    idx = idx_ref[...]
    dest = rank_ref[...]
    for e in range(N_EXPERTS):
        dest = dest + jnp.where(idx == e, pstart_ref[e], 0)
    dest_ref[...] = dest


def _dest(padded_start, top_idx, rank, tn):
    t = top_idx.shape[1]
    assert t % tn == 0
    col = pl.BlockSpec((TOP_K, tn), lambda i, ps: (0, i))
    return pl.pallas_call(
        _dest_kernel,
        grid_spec=pltpu.PrefetchScalarGridSpec(
            num_scalar_prefetch=1, grid=(t // tn,), in_specs=[col, col], out_specs=col),
        out_shape=jax.ShapeDtypeStruct((TOP_K, t), jnp.int32),
        name="dest",
    )(padded_start, top_idx, rank)


def _dispatch_kernel(pend_ref, dest_ref, xf_ref, slots_hbm, zbuf, sem, zsem, *, tm, blk):
    @pl.when(pl.program_id(0) == 0)
    def _():
        zbuf[...] = jnp.zeros_like(zbuf)

        def tail_copy(e):
            first = pl.multiple_of((pend_ref[e] - blk) * SUBLANES, blk * SUBLANES)
            return pltpu.make_async_copy(
                zbuf, slots_hbm.at[pl.ds(first, blk * SUBLANES), :], zsem)

        def nonempty(e):
            return pend_ref[e] > (pend_ref[e - 1] if e else 0)

        for e in range(N_EXPERTS):
            pl.when(nonempty(e))(lambda e=e: tail_copy(e).start())
        for e in range(N_EXPERTS):
            pl.when(nonempty(e))(lambda e=e: tail_copy(e).wait())

    def row_copy(t, kk):
        return pltpu.make_async_copy(
            xf_ref.at[_token_rows(t), :],
            slots_hbm.at[_token_rows(dest_ref[kk, t]), :], sem)

    def start(t, carry):
        for kk in range(TOP_K):
            row_copy(t, kk).start(priority=kk % 2)
        return carry

    lax.fori_loop(0, tm, start, 0, unroll=8)
    for kk in range(TOP_K):
        pltpu.make_async_copy(
            xf_ref, slots_hbm.at[pl.ds(0, tm * SUBLANES), :], sem).wait()


def _dispatch(padded_end, dest, xf, n_pad, tm, blk):
    t = xf.shape[0] // SUBLANES
    assert t % tm == 0
    return pl.pallas_call(
        functools.partial(_dispatch_kernel, tm=tm, blk=blk),
        grid_spec=pltpu.PrefetchScalarGridSpec(
            num_scalar_prefetch=1,
            grid=(t // tm,),
            in_specs=[pl.BlockSpec((TOP_K, tm), lambda i, pe: (0, i), memory_space=pltpu.SMEM),
                      pl.BlockSpec((tm * SUBLANES, LANES), lambda i, pe: (i, 0))],
            out_specs=pl.BlockSpec(memory_space=pl.ANY),
            scratch_shapes=[pltpu.VMEM((blk * SUBLANES, LANES), F32),
                            pltpu.SemaphoreType.DMA, pltpu.SemaphoreType.DMA],
        ),
        out_shape=jax.ShapeDtypeStruct((n_pad * SUBLANES, LANES), F32),
        compiler_params=pltpu.CompilerParams(
            dimension_semantics=("arbitrary",), vmem_limit_bytes=VMEM_LIMIT,
            has_side_effects=True),
        name="dispatch",
    )(padded_end, dest, xf)


WEIGHT_ROWS = 128


def _experts_kernel(be_ref, nu_ref, run_ref, next_ref, x_ref, wgu_hbm, wd_hbm, bg_ref, bu_ref,
                    bd_ref, y_ref, wgu_buf, wd_buf, wg_s, wu_s, wd_s, t_s, sem, *, blk):
    j = pl.program_id(0)
    used = j < nu_ref[0]
    expert = be_ref[j]
    new_expert = jnp.logical_or(j == 0, expert != be_ref[jnp.maximum(j - 1, 0)])

    def weight_copies(e, slot):
        return (pltpu.make_async_copy(wgu_hbm.at[e], wgu_buf.at[slot], sem.at[0, slot]),
                pltpu.make_async_copy(wd_hbm.at[e], wd_buf.at[slot], sem.at[1, slot]))

    def expert_ffn_tail(gate, up):
        gate = jnp.minimum(gate, SWIGLU_LIMIT)
        up = jnp.clip(up, -SWIGLU_LIMIT, SWIGLU_LIMIT)
        hidden = (up + 1.0) * gate * jax.nn.sigmoid(SWIGLU_ALPHA * gate)
        _store_token_tiles(y_ref, _dot(hidden.astype(BF16), wd_s[...]) + bd_ref[...])

    @pl.when(jnp.logical_and(used, new_expert))
    def _():
        slot = lax.rem(run_ref[expert], 2)

        @pl.when(j == 0)
        def _():
            for c in weight_copies(expert, slot):
                c.start()

        for c in weight_copies(expert, slot):
            c.wait()
        following = next_ref[expert]

        @pl.when(following >= 0)
        def _():
            for c in weight_copies(following, 1 - slot):
                c.start()

        xb = _load_token_tiles(x_ref, blk).astype(BF16)
        gate = bg_ref[...]
        up = bu_ref[...]
        for r in range(0, D_MODEL, MXU_WIDTH):
            wg_rows, wu_rows = [], []
            for rr in range(r, r + MXU_WIDTH, WEIGHT_ROWS):
                rows = slice(rr, rr + WEIGHT_ROWS)
                t_s[...] = wgu_buf[slot, rows, :].T
                wg_rows.append(t_s[pl.ds(0, D_FF, stride=2), :].T.astype(BF16))
                wu_rows.append(t_s[pl.ds(1, D_FF, stride=2), :].T.astype(BF16))
                wg_s[rows, :] = wg_rows[-1]
                wu_s[rows, :] = wu_rows[-1]
                wd_s[rows, :] = wd_buf[slot, rows, :].astype(BF16)
            x_cols = xb[:, r:r + MXU_WIDTH]
            gate = gate + _dot(x_cols, jnp.concatenate(wg_rows, axis=0))
            up = up + _dot(x_cols, jnp.concatenate(wu_rows, axis=0))
        expert_ffn_tail(gate, up)

    @pl.when(jnp.logical_and(used, jnp.logical_not(new_expert)))
    def _():
        xb = _load_token_tiles(x_ref, blk).astype(BF16)
        expert_ffn_tail(_dot(xb, wg_s[...]) + bg_ref[...], _dot(xb, wu_s[...]) + bu_ref[...])

    @pl.when(jnp.logical_not(used))
    def _():
        y_ref[...] = jnp.zeros_like(y_ref)


def _experts(block_expert, n_used, expert_run, next_expert, slots, wgu, wd, bg, bu, bd, blk):
    assert D_FF == D_MODEL
    n_pad = slots.shape[0] // SUBLANES
    n_blocks = n_pad // blk
    xmap = lambda j, be, nu, run, nxt: (jnp.minimum(j, nu[0] - 1), 0)
    wmap = lambda j, be, nu, run, nxt: (be[j], 0, 0)
    bspec = pl.BlockSpec((None, 1, D_FF), wmap)
    tiles = (blk * SUBLANES, LANES)
    hbm = pl.BlockSpec(memory_space=pl.ANY)
    return pl.pallas_call(
        functools.partial(_experts_kernel, blk=blk),
        grid_spec=pltpu.PrefetchScalarGridSpec(
            num_scalar_prefetch=4,
            grid=(n_blocks,),
            in_specs=[pl.BlockSpec(tiles, xmap), hbm, hbm, bspec, bspec, bspec],
            out_specs=pl.BlockSpec(tiles, lambda j, be, nu, run, nxt: (j, 0)),
            scratch_shapes=[pltpu.VMEM((2, D_MODEL, 2 * D_FF), F32),
                            pltpu.VMEM((2, D_FF, D_MODEL), F32),
                            pltpu.VMEM((D_MODEL, D_FF), BF16), pltpu.VMEM((D_MODEL, D_FF), BF16),
                            pltpu.VMEM((D_FF, D_MODEL), BF16),
                            pltpu.VMEM((2 * D_FF, WEIGHT_ROWS), F32),
                            pltpu.SemaphoreType.DMA((2, 2))],
        ),
        out_shape=jax.ShapeDtypeStruct((n_pad * SUBLANES, LANES), F32),
        compiler_params=pltpu.CompilerParams(
            dimension_semantics=("arbitrary",), vmem_limit_bytes=VMEM_LIMIT),
        name="experts",
    )(block_expert, n_used, expert_run, next_expert, slots, wgu, wd, bg, bu, bd)


def _combine_kernel(dest_ref, next_dest_ref, y_hbm, w_ref, x1_ref, o_ref, buf, sem, *, tm):
    i = pl.program_id(0)
    n = pl.num_programs(0)
    slot = lax.rem(i, 2)

    def gather(dests, into):
        def start(t, carry):
            for kk in range(TOP_K):
                pltpu.make_async_copy(
                    y_hbm.at[_token_rows(dests[kk, t]), :],
                    buf.at[into, kk, _token_rows(t), :], sem.at[into]).start(priority=kk % 2)
            return carry

        lax.fori_loop(0, tm, start, 0, unroll=8)

    pl.when(i == 0)(lambda: gather(dest_ref, 0))
    pl.when(i + 1 < n)(lambda: gather(next_dest_ref, 1 - slot))
    for kk in range(TOP_K):
        pltpu.make_async_copy(
            y_hbm.at[pl.ds(0, tm * SUBLANES), :], buf.at[slot, kk], sem.at[slot]).wait()
    w = w_ref[...]
    for s in range(ROW_TILES):
        cols = slice(s * LANES, (s + 1) * LANES)
        acc = x1_ref[:, cols]
        for kk in range(TOP_K):
            acc = acc + buf[slot, kk, pl.ds(s, tm, stride=SUBLANES), :] * w[:, kk:kk + 1]
        o_ref[:, cols] = acc


def _combine(dest, y_slots, w_t, x1, tm):
    t = x1.shape[0]
    assert t % tm == 0
    n = t // tm
    row = pl.BlockSpec((tm, D_MODEL), lambda i: (i, 0))
    return pl.pallas_call(
        functools.partial(_combine_kernel, tm=tm),
        grid=(n,),
        in_specs=[pl.BlockSpec((TOP_K, tm), lambda i: (0, i), memory_space=pltpu.SMEM),
                  pl.BlockSpec((TOP_K, tm), lambda i: (0, jnp.minimum(i + 1, n - 1)),
                               memory_space=pltpu.SMEM),
                  pl.BlockSpec(memory_space=pl.ANY),
                  pl.BlockSpec((tm, TOP_K), lambda i: (i, 0)),
                  row],
        out_specs=row,
        out_shape=jax.ShapeDtypeStruct((t, D_MODEL), F32),
        scratch_shapes=[pltpu.VMEM((2, TOP_K, tm * SUBLANES, LANES), F32),
                        pltpu.SemaphoreType.DMA((2,))],
        compiler_params=pltpu.CompilerParams(
            dimension_semantics=("arbitrary",), vmem_limit_bytes=VMEM_LIMIT),
        name="combine",
    )(dest, dest, y_slots, w_t, x1)


def _tiles(batch, seq):
    return dict(
        in_proj=min(512, seq),
        retention=min(1024, seq),
        sb_block=min(256, seq),
        sb_heads=4,
        out_proj=min(1024, seq),
        dest=min(8192, batch * seq),
        dispatch=min(2048, seq),
        combine=min(512, seq),
        expert_block=512,
    )


def _layer(x, attn_norm_g, w_in, ret_norm_g, sb_q_norm_g, sb_k_norm_g, w_out,
           ffn_norm_g, w_router, b_router, w_gate_up, b_gate_up, w_down, b_down):
    batch, seq, d = x.shape
    t = batch * seq
    tiles = _tiles(batch, seq)
    x2 = x.reshape(t, d)

    rq, rk, rv, rg, sq, sk, sv = _in_proj(
        x2, attn_norm_g, w_in, sb_q_norm_g, sb_k_norm_g, seq, tiles["in_proj"])
    ret = _retention(rq, rk, rv, rg, ret_norm_g, batch, seq, tiles["retention"])
    sb = _sb_attention(sq, sk, sv, batch, seq, tiles["sb_block"], LANES, tiles["sb_heads"])
    x1, xf, top_idx, top_w, rank, counts = _out_proj_route(
        ret, sb, x2, w_out, ffn_norm_g, w_router, b_router, tiles["out_proj"])

    blk = tiles["expert_block"]
    n_pad = t * TOP_K + N_EXPERTS * blk
    n_blocks = n_pad // blk
    counts = counts[:, 0]
    padded = (counts + blk - 1) // blk * blk
    padded_end = jnp.cumsum(padded)
    padded_start = padded_end - padded
    block_first = jnp.arange(n_blocks, dtype=jnp.int32) * blk
    block_expert = jnp.minimum(
        jnp.sum(padded_end[None, :] <= block_first[:, None], axis=1), N_EXPERTS - 1
    ).astype(jnp.int32)
    n_used = (padded_end[-1:] // blk).astype(jnp.int32)
    nonempty = counts > 0
    expert_run = (jnp.cumsum(nonempty) - nonempty).astype(jnp.int32)
    experts = jnp.arange(N_EXPERTS, dtype=jnp.int32)
    later = jnp.logical_and(nonempty[None, :], experts[None, :] > experts[:, None])
    next_expert = jnp.min(jnp.where(later, experts[None, :], N_EXPERTS), axis=1)
    next_expert = jnp.where(next_expert < N_EXPERTS, next_expert, -1).astype(jnp.int32)

    dest = _dest(padded_start, top_idx, rank, tiles["dest"])
    slots = _dispatch(padded_end, dest, xf, n_pad, tiles["dispatch"], blk)

    bgu = b_gate_up.astype(F32).reshape(N_EXPERTS, 1, D_FF, 2)
    y_slots = _experts(block_expert, n_used, expert_run, next_expert, slots,
                       w_gate_up.astype(F32), w_down.astype(F32),
                       bgu[..., 0], bgu[..., 1], b_down.astype(F32)[:, None, :], blk)

    out = _combine(dest, y_slots, top_w.T, x1, tiles["combine"])
    return out.reshape(batch, seq, d)


def kernel(x, attn_norm_g, w_in, ret_norm_g, sb_q_norm_g, sb_k_norm_g, w_out, ffn_norm_g,
           w_router, b_router, w_gate_up, b_gate_up, w_down, b_down):
    depth = attn_norm_g.shape[0]
    for l in range(depth):
        x = _layer(x, attn_norm_g[l], w_in[l], ret_norm_g[l], sb_q_norm_g[l], sb_k_norm_g[l],
                   w_out[l], ffn_norm_g[l], w_router[l], b_router[l], w_gate_up[l],
                   b_gate_up[l], w_down[l], b_down[l])
    return x
```

```python
import functools

import numpy as np
import jax
import jax.numpy as jnp
from jax import lax
from jax.experimental import pallas as pl
from jax.experimental.pallas import tpu as pltpu

D_MODEL = 1024
RET_WIDTH = 512
RET_HEADS = 4
RET_HEAD_DIM = 128
SB_WIDTH = 512
SB_HEADS = 8
SB_HEAD_DIM = 64
IN_PROJ_WIDTH = 4 * RET_WIDTH + 3 * SB_WIDTH
RET_CHUNK = 128
ROPE_BASE = 10000.0
N_EXPERTS = 32
TOP_K = 4
D_FF = D_MODEL
SWIGLU_LIMIT = 7.0
SWIGLU_ALPHA = 1.702
EPS = 1e-6

LANES = 128
MXU_WIDTH = 256
F32_EXP_UNDERFLOW = -88.0
VMEM_LIMIT = 56 * 1024 * 1024

BF16 = jnp.bfloat16
F32 = jnp.float32


def _split_bf16(v):
    hi = v.astype(BF16)
    lo = (v - hi.astype(F32)).astype(BF16)
    return hi, lo


SUBLANES = 8
ROW_TILES = D_MODEL // LANES
assert ROW_TILES == SUBLANES


def _store_token_tiles(ref, value):
    n = value.shape[0]
    for s in range(ROW_TILES):
        ref[pl.ds(s, n, stride=SUBLANES), :] = value[:, s * LANES:(s + 1) * LANES]


def _load_token_tiles(ref, n):
    return jnp.concatenate(
        [ref[pl.ds(s, n, stride=SUBLANES), :] for s in range(ROW_TILES)], axis=1)


def _token_rows(i):
    return pl.ds(pl.multiple_of(i * SUBLANES, SUBLANES), SUBLANES)


def _dot(a, b):
    return jnp.dot(a, b, preferred_element_type=F32)


def _dot_nt(a, b):
    return lax.dot_general(a, b, (((1,), (1,)), ((), ())), preferred_element_type=F32)


def _in_proj_kernel(x_ref, g_ref, w_ref, cos_ref, sin_ref, qg_ref, kg_ref, seg_ref,
                    rq_ref, rk_ref, rv_ref, rg_ref, sq_ref, sk_ref, sv_ref):
    x = x_ref[...]
    h = x * lax.rsqrt(jnp.mean(x * x, axis=-1, keepdims=True) + EPS) * g_ref[...]
    p = _dot(h.astype(BF16), w_ref[...])
    cos2 = cos_ref[...]
    sin2 = sin_ref[...]
    k_scale = RET_HEAD_DIM ** -0.5
    for hd in range(RET_HEADS):
        lo = hd * RET_HEAD_DIM
        q = p[:, lo:lo + RET_HEAD_DIM]
        k = p[:, RET_WIDTH + lo:RET_WIDTH + lo + RET_HEAD_DIM]
        q = q * cos2 + pltpu.roll(q, RET_HEAD_DIM // 2, axis=1) * sin2
        k = k * cos2 + pltpu.roll(k, RET_HEAD_DIM // 2, axis=1) * sin2
        rq_ref[:, lo:lo + RET_HEAD_DIM] = q.astype(rq_ref.dtype)
        rk_ref[:, lo:lo + RET_HEAD_DIM] = (k * k_scale).astype(rk_ref.dtype)
    rv_ref[...] = p[:, 2 * RET_WIDTH:3 * RET_WIDTH].astype(rv_ref.dtype)
    rg_ref[...] = p[:, 3 * RET_WIDTH:4 * RET_WIDTH]
    base = 4 * RET_WIDTH
    seg = seg_ref[...]

    def head_norm(v, gain):
        hi, lo = _split_bf16(v * v)
        group = seg.shape[0]
        ms = jnp.concatenate(
            [_dot(hi[:, g:g + group], seg) + _dot(lo[:, g:g + group], seg)
             for g in range(0, SB_WIDTH, group)], axis=1) * (1.0 / SB_HEAD_DIM)
        return v * lax.rsqrt(ms + EPS) * gain

    sq = head_norm(p[:, base:base + SB_WIDTH], qg_ref[...])
    sk = head_norm(p[:, base + SB_WIDTH:base + 2 * SB_WIDTH], kg_ref[...])
    sq_ref[...] = (sq * (SB_HEAD_DIM ** -0.5)).astype(sq_ref.dtype)
    sk_ref[...] = sk.astype(sk_ref.dtype)
    sv_ref[...] = p[:, base + 2 * SB_WIDTH:base + 3 * SB_WIDTH].astype(sv_ref.dtype)


def _rope_tables(seq):
    half = RET_HEAD_DIM // 2
    inv = ROPE_BASE ** (-np.arange(half, dtype=np.float64) / half)
    n_a = seq // LANES
    ang_a = (np.arange(n_a, dtype=np.float64) * LANES)[:, None] * inv[None, :]
    ang_b = np.arange(LANES, dtype=np.float64)[:, None] * inv[None, :]
    ca, sa = jnp.asarray(np.cos(ang_a), F32), jnp.asarray(np.sin(ang_a), F32)
    cb, sb = jnp.asarray(np.cos(ang_b), F32), jnp.asarray(np.sin(ang_b), F32)
    cos = (ca[:, None, :] * cb[None] - sa[:, None, :] * sb[None]).reshape(seq, half)
    sin = (sa[:, None, :] * cb[None] + ca[:, None, :] * sb[None]).reshape(seq, half)
    return jnp.concatenate([cos, cos], axis=-1), jnp.concatenate([-sin, sin], axis=-1)


def _in_proj(x2, norm_g, w_in, sb_q_g, sb_k_g, seq, tm):
    t = x2.shape[0]
    assert seq % tm == 0 and t % tm == 0
    cos2, sin2 = _rope_tables(seq)
    seg = np.kron(np.eye(MXU_WIDTH // SB_HEAD_DIM), np.ones((SB_HEAD_DIM, SB_HEAD_DIM)))
    seg = jnp.asarray(seg, BF16)
    qg = jnp.tile(sb_q_g.astype(F32), SB_HEADS)[None, :]
    kg = jnp.tile(sb_k_g.astype(F32), SB_HEADS)[None, :]
    n_pos = seq // tm
    row = lambda w: pl.BlockSpec((tm, w), lambda i: (i, 0))
    full = lambda a: pl.BlockSpec(a.shape, lambda i: (0,) * a.ndim)
    pos = pl.BlockSpec((tm, RET_HEAD_DIM), lambda i: (i % n_pos, 0))
    w_bf = w_in.astype(BF16)
    g2 = norm_g.astype(F32)[None, :]
    out = lambda dt: jax.ShapeDtypeStruct((t, RET_WIDTH), dt)
    return pl.pallas_call(
        _in_proj_kernel,
        grid=(t // tm,),
        in_specs=[row(D_MODEL), full(g2), full(w_bf), pos, pos, full(qg), full(kg), full(seg)],
        out_specs=[row(RET_WIDTH)] * 7,
        out_shape=[out(BF16), out(F32), out(BF16), out(F32), out(BF16), out(BF16), out(BF16)],
        compiler_params=pltpu.CompilerParams(
            dimension_semantics=("arbitrary",), vmem_limit_bytes=VMEM_LIMIT),
        name="in_proj",
    )(x2, g2, w_bf, cos2, sin2, qg, kg, seg)


def _retention_kernel(q_ref, k_ref, v_ref, g_ref, ng_ref, intra_ref, qd_ref, kd_ref, cd_ref,
                      o_ref, state_ref, *, chunks):
    @pl.when(pl.program_id(1) == 0)
    def _():
        state_ref[...] = jnp.zeros_like(state_ref)

    def chunk(c, carry):
        rows = pl.ds(pl.multiple_of(c * RET_CHUNK, RET_CHUNK), RET_CHUNK)
        heads = range(RET_HEADS)
        cols = [slice(hd * RET_HEAD_DIM, (hd + 1) * RET_HEAD_DIM) for hd in heads]
        q = [q_ref[rows, cl] for cl in cols]
        k = [k_ref[rows, cl] for cl in cols]
        v = [v_ref[rows, cl] for cl in cols]
        state = [state_ref[hd] for hd in heads]
        scores = [_dot_nt(q[hd], k[hd].astype(BF16)) * intra_ref[hd] for hd in heads]
        cross = [_dot(q[hd], state[hd].astype(BF16)) * qd_ref[:, cols[hd]] for hd in heads]
        inner = [_dot(scores[hd].astype(BF16), v[hd]) for hd in heads]
        kd = [(k[hd] * kd_ref[:, cols[hd]]).astype(BF16) for hd in heads]
        for hd in heads:
            state_ref[hd] = state[hd] * cd_ref[:, cols[hd]] + _dot(kd[hd].T, v[hd])
        for hd in heads:
            o = inner[hd] + cross[hd]
            o = o * lax.rsqrt(jnp.mean(o * o, axis=-1, keepdims=True) + EPS) * ng_ref[:, cols[hd]]
            gate = g_ref[rows, cols[hd]]
            o_ref[rows, cols[hd]] = (o * (gate * jax.nn.sigmoid(gate))).astype(o_ref.dtype)
        return carry

    lax.fori_loop(0, chunks, chunk, 0, unroll=True)


def _retention(rq, rk, rv, rg, ret_norm_g, batch, seq, rows):
    assert seq % rows == 0 and rows % RET_CHUNK == 0
    log_gamma = np.log(1.0 - 2.0 ** (-5.0 - np.arange(RET_HEADS, dtype=np.float64)))
    idx = np.arange(RET_CHUNK, dtype=np.float64)
    diff = idx[:, None] - idx[None, :]
    intra = np.where(diff >= 0, np.exp(log_gamma[:, None, None] * np.maximum(diff, 0.0)), 0.0)
    q_decay = np.exp(log_gamma[:, None] * (idx + 1.0))
    k_decay = np.exp(log_gamma[:, None] * (RET_CHUNK - 1.0 - idx))
    chunk_decay = np.exp(log_gamma * RET_CHUNK)
    lane_rep = lambda a: np.repeat(a.T, RET_HEAD_DIM, axis=1)
    intra = jnp.asarray(intra, F32)
    qd = jnp.asarray(lane_rep(q_decay), F32)
    kd = jnp.asarray(lane_rep(k_decay), F32)
    cd = jnp.asarray(np.repeat(chunk_decay, RET_HEAD_DIM)[None, :], F32)
    ng = ret_norm_g.astype(F32).reshape(1, RET_WIDTH)
    n_r = seq // rows
    blk = pl.BlockSpec((rows, RET_WIDTH), lambda b, r: (b * n_r + r, 0))
    full = lambda a: pl.BlockSpec(a.shape, lambda b, r: (0,) * a.ndim)
    return pl.pallas_call(
        functools.partial(_retention_kernel, chunks=rows // RET_CHUNK),
        grid=(batch, n_r),
        in_specs=[blk, blk, blk, blk, full(ng), full(intra), full(qd), full(kd), full(cd)],
        out_specs=blk,
        out_shape=jax.ShapeDtypeStruct((batch * seq, RET_WIDTH), BF16),
        scratch_shapes=[pltpu.VMEM((RET_HEADS, RET_HEAD_DIM, RET_HEAD_DIM), F32)],
        compiler_params=pltpu.CompilerParams(
            dimension_semantics=("arbitrary", "arbitrary"), vmem_limit_bytes=VMEM_LIMIT),
        name="retention",
    )(rq, rk, rv, rg, ng, intra, qd, kd, cd)


def _sb_attn_kernel(q_ref, k_ref, v_ref, tri_ref, o_ref, *, blk, halo, heads):
    i = pl.program_id(2)
    lane = lax.broadcasted_iota(jnp.int32, (blk, LANES), 1)
    first_head = lane < SB_HEAD_DIM
    tri_own = tri_ref[...]
    tri_halo = tri_ref[:halo, :halo]
    below_diag = (lax.broadcasted_iota(jnp.int32, (blk, blk), 1)
                  < lax.broadcasted_iota(jnp.int32, (blk, blk), 0))
    has_halo = i > 0


    def scores(z, mask):
        log_beta = jnp.minimum(z, 0.0) - jnp.log(1.0 + jnp.exp(-jnp.abs(z)))
        log_rest = log_beta - z
        if mask is not None:
            log_rest = jnp.where(mask, log_rest, 0.0)
        return log_beta, log_rest

    def score_phase(qms, kbs, mask):
        zs = [_dot_nt(qm, kb) for qm, kb in zip(qms, kbs)]
        return [scores(z, mask) for z in zs]

    def later_phase(log_rests, tri):
        parts = [_split_bf16(r) for r in log_rests]
        laters = [_dot(hi, tri) + _dot(lo, tri) for hi, lo in parts]
        return [(later, later[:, 0:1] + r[:, 0:1]) for later, r in zip(laters, log_rests)]

    def weight_phase(log_betas, log_remainings, mask):
        ws = [jnp.exp(b + r) for b, r in zip(log_betas, log_remainings)]
        if mask is not None:
            ws = [jnp.where(mask, w, 0.0) for w in ws]
        return [w.astype(BF16) for w in ws]

    col_groups = [slice((h // 2) * LANES, (h // 2 + 1) * LANES) for h in range(heads)]

    def head_queries():
        out = []
        for h in range(heads):
            q = q_ref[:, col_groups[h]]
            keep = first_head if h % 2 == 0 else jnp.logical_not(first_head)
            out.append(jnp.where(keep, q, jnp.zeros_like(q)))
        return out

    own_rows = pl.ds(pl.multiple_of(i * blk, blk), blk)
    halo_rows = pl.ds(pl.multiple_of(jnp.maximum(i * blk - halo, 0), halo), halo)

    def first_steps():
        qms = head_queries()
        sc_o = score_phase(qms, [k_ref[own_rows, c] for c in col_groups], below_diag)
        sc_h = score_phase(qms, [k_ref[halo_rows, c] for c in col_groups], has_halo)
        ls_o = later_phase([rest for _, rest in sc_o], tri_own)
        ls_h = later_phase([rest for _, rest in sc_h], tri_halo)
        w_o = weight_phase([b for b, _ in sc_o], [later for later, _ in ls_o], below_diag)
        w_h = weight_phase([b for b, _ in sc_h],
                           [ls_o[h][1] + ls_h[h][0] for h in range(heads)], has_halo)
        accs = [_dot(w_o[h], v_ref[own_rows, col_groups[h]]) for h in range(heads)]
        accs = [accs[h] + _dot(w_h[h], v_ref[halo_rows, col_groups[h]]) for h in range(heads)]
        return tuple(ls_o[h][1] + ls_h[h][1] for h in range(heads)), tuple(accs)

    def cond(carry):
        j, cs, _ = carry
        c_max = functools.reduce(jnp.maximum, cs)
        return jnp.logical_and(j >= 0, jnp.max(c_max) > F32_EXP_UNDERFLOW)

    def body(carry):
        j, cs, accs = carry
        rows = pl.ds(pl.multiple_of(j * halo, halo), halo)
        sc = score_phase(head_queries(), [k_ref[rows, c] for c in col_groups], None)
        ls = later_phase([rest for _, rest in sc], tri_halo)
        ws = weight_phase([b for b, _ in sc], [cs[h] + ls[h][0] for h in range(heads)], None)
        new_accs = tuple(accs[h] + _dot(ws[h], v_ref[rows, col_groups[h]]) for h in range(heads))
        return j - 1, tuple(cs[h] + ls[h][1] for h in range(heads)), new_accs

    init = (i * (blk // halo) - 2,) + first_steps()
    _, _, accs = lax.while_loop(cond, body, init)
    for p in range(heads // 2):
        o_ref[:, p * LANES:(p + 1) * LANES] = jnp.where(
            first_head, accs[2 * p], accs[2 * p + 1]).astype(o_ref.dtype)


def _sb_attention(sq, sk, sv, batch, seq, blk, halo, heads):
    assert seq % blk == 0 and blk % halo == 0 and halo == LANES
    assert heads % 2 == 0 and SB_HEADS % heads == 0
    nq = seq // blk
    width = heads * SB_HEAD_DIM
    groups = SB_WIDTH // width
    j = np.arange(blk)
    tri = jnp.asarray((j[:, None] > j[None, :]).astype(np.float32), BF16)
    qblk = pl.BlockSpec((blk, width), lambda b, g, i: (b * nq + i, g))
    kvblk = pl.BlockSpec((seq, width), lambda b, g, i: (b, g))
    return pl.pallas_call(
        functools.partial(_sb_attn_kernel, blk=blk, halo=halo, heads=heads),
        grid=(batch, groups, nq),
        in_specs=[qblk, kvblk, kvblk, pl.BlockSpec(tri.shape, lambda b, g, i: (0, 0))],
        out_specs=qblk,
        out_shape=jax.ShapeDtypeStruct((batch * seq, SB_WIDTH), BF16),
        compiler_params=pltpu.CompilerParams(
            dimension_semantics=("arbitrary", "arbitrary", "arbitrary"),
            vmem_limit_bytes=VMEM_LIMIT),
        name="sb_attn",
    )(sq, sk, sv, tri)


OUT_PROJ_ROW_GROUPS = 4


def _out_proj_kernel(ret_ref, sb_ref, x_ref, wtop_ref, wbot_ref, g_ref, wrh_ref, wrl_ref,
                     br_ref, tri_ref,
                     x1_ref, xf_ref, idx_ref, w_ref, rank_ref, cnt_ref, run_ref):
    @pl.when(pl.program_id(0) == 0)
    def _():
        run_ref[...] = jnp.zeros_like(run_ref)

    rows = x_ref.shape[0] // OUT_PROJ_ROW_GROUPS
    groups = [slice(r * rows, (r + 1) * rows) for r in range(OUT_PROJ_ROW_GROUPS)]
    x1s = [x_ref[g, :] + _dot(ret_ref[g, :], wtop_ref[...]) + _dot(sb_ref[g, :], wbot_ref[...])
           for g in groups]
    for g, x1 in zip(groups, x1s):
        x1_ref[g, :] = x1
    xfs = [x1 * lax.rsqrt(jnp.mean(x1 * x1, axis=-1, keepdims=True) + EPS) * g_ref[...]
           for x1 in x1s]
    for r, xf in enumerate(xfs):
        _store_token_tiles(xf_ref.at[pl.ds(r * rows * SUBLANES, rows * SUBLANES), :], xf)
    parts = [_split_bf16(xf) for xf in xfs]
    wrh = wrh_ref[...]
    wrl = wrl_ref[...]
    logits = jnp.concatenate(
        [_dot_nt(wrh, xh) + _dot_nt(wrh, xl) + _dot_nt(wrl, xh) for xh, xl in parts],
        axis=1) + br_ref[...]
    n_e, tm = logits.shape
    e_iota = lax.broadcasted_iota(jnp.int32, (n_e, tm), 0)
    cur = logits
    tops, sels, hots = [], [], []
    for _ in range(TOP_K):
        m = jnp.max(cur, axis=0, keepdims=True)
        sel = jnp.min(jnp.where(cur == m, e_iota, n_e), axis=0, keepdims=True)
        hot = e_iota == sel
        cur = jnp.where(hot, -jnp.inf, cur)
        tops.append(m)
        sels.append(sel)
        hots.append(hot)
    ps = [jnp.exp(m - tops[0]) for m in tops]
    denom = ps[0] + ps[1] + ps[2] + ps[3]
    chosen = jnp.zeros((n_e, tm), F32)
    for hot in hots:
        chosen = chosen + hot.astype(F32)
    before = run_ref[:, 0:1] + _dot(chosen.astype(BF16), tri_ref[...])
    for kk in range(TOP_K):
        idx_ref[kk:kk + 1, :] = sels[kk]
        w_ref[kk:kk + 1, :] = ps[kk] / denom
        rank = jnp.sum(jnp.where(hots[kk], before, 0.0), axis=0, keepdims=True)
        rank_ref[kk:kk + 1, :] = rank.astype(jnp.int32)
    run = run_ref[...] + jnp.sum(chosen, axis=1, keepdims=True)
    run_ref[...] = run
    cnt_ref[...] = run.astype(jnp.int32)


def _out_proj_route(ret, sb, x2, w_out, ffn_g, w_router, b_router, tm):
    t = x2.shape[0]
    assert t % tm == 0
    w_bf = w_out.astype(BF16)
    wtop, wbot = w_bf[:RET_WIDTH], w_bf[RET_WIDTH:]
    g2 = ffn_g.astype(F32)[None, :]
    wr_t = w_router.astype(F32).T
    wrh = wr_t.astype(BF16)
    wrl = (wr_t - wrh.astype(F32)).astype(BF16)
    br = b_router.astype(F32)[:, None]
    tt = np.arange(tm)
    tri = jnp.asarray((tt[:, None] < tt[None, :]).astype(np.float32), BF16)
    row = lambda w: pl.BlockSpec((tm, w), lambda i: (i, 0))
    full = lambda a: pl.BlockSpec(a.shape, lambda i: (0,) * a.ndim)
    col = pl.BlockSpec((TOP_K, tm), lambda i: (0, i))
    return pl.pallas_call(
        _out_proj_kernel,
        grid=(t // tm,),
        in_specs=[row(RET_WIDTH), row(SB_WIDTH), row(D_MODEL), full(wtop), full(wbot), full(g2),
                  full(wrh), full(wrl), full(br), full(tri)],
        out_specs=[row(D_MODEL), pl.BlockSpec((tm * SUBLANES, LANES), lambda i: (i, 0)),
                   col, col, col, pl.BlockSpec((N_EXPERTS, LANES), lambda i: (0, 0))],
        out_shape=[jax.ShapeDtypeStruct((t, D_MODEL), F32),
                   jax.ShapeDtypeStruct((t * SUBLANES, LANES), F32),
                   jax.ShapeDtypeStruct((TOP_K, t), jnp.int32),
                   jax.ShapeDtypeStruct((TOP_K, t), F32),
                   jax.ShapeDtypeStruct((TOP_K, t), jnp.int32),
                   jax.ShapeDtypeStruct((N_EXPERTS, LANES), jnp.int32)],
        scratch_shapes=[pltpu.VMEM((N_EXPERTS, LANES), F32)],
        compiler_params=pltpu.CompilerParams(
            dimension_semantics=("arbitrary",), vmem_limit_bytes=VMEM_LIMIT),
        name="out_proj_route",
    )(ret, sb, x2, wtop, wbot, g2, wrh, wrl, br, tri)


def _dest_kernel(pstart_ref, idx_ref, rank_ref, dest_ref):
    idx = idx_ref[...]
    dest = rank_ref[...]
    for e in range(N_EXPERTS):
        dest = dest + jnp.where(idx == e, pstart_ref[e], 0)
    dest_ref[...] = dest


def _dest(padded_start, top_idx, rank, tn):
    t = top_idx.shape[1]
    assert t % tn == 0
    col = pl.BlockSpec((TOP_K, tn), lambda i, ps: (0, i))
    return pl.pallas_call(
        _dest_kernel,
        grid_spec=pltpu.PrefetchScalarGridSpec(
            num_scalar_prefetch=1, grid=(t // tn,), in_specs=[col, col], out_specs=col),
        out_shape=jax.ShapeDtypeStruct((TOP_K, t), jnp.int32),
        name="dest",
    )(padded_start, top_idx, rank)


def _dispatch_kernel(pend_ref, dest_ref, xf_ref, slots_hbm, zbuf, sem, zsem, *, tm, blk):
    @pl.when(pl.program_id(0) == 0)
    def _():
        zbuf[...] = jnp.zeros_like(zbuf)

        def tail_copy(e):
            first = pl.multiple_of((pend_ref[e] - blk) * SUBLANES, blk * SUBLANES)
            return pltpu.make_async_copy(
                zbuf, slots_hbm.at[pl.ds(first, blk * SUBLANES), :], zsem)

        def nonempty(e):
            return pend_ref[e] > (pend_ref[e - 1] if e else 0)

        for e in range(N_EXPERTS):
            pl.when(nonempty(e))(lambda e=e: tail_copy(e).start())
        for e in range(N_EXPERTS):
            pl.when(nonempty(e))(lambda e=e: tail_copy(e).wait())

    def row_copy(t, kk):
        return pltpu.make_async_copy(
            xf_ref.at[_token_rows(t), :],
            slots_hbm.at[_token_rows(dest_ref[kk, t]), :], sem)

    def start(t, carry):
        for kk in range(TOP_K):
            row_copy(t, kk).start(priority=kk % 2)
        return carry

    lax.fori_loop(0, tm, start, 0, unroll=8)
    for kk in range(TOP_K):
        pltpu.make_async_copy(
            xf_ref, slots_hbm.at[pl.ds(0, tm * SUBLANES), :], sem).wait()


def _dispatch(padded_end, dest, xf, n_pad, tm, blk):
    t = xf.shape[0] // SUBLANES
    assert t % tm == 0
    return pl.pallas_call(
        functools.partial(_dispatch_kernel, tm=tm, blk=blk),
        grid_spec=pltpu.PrefetchScalarGridSpec(
            num_scalar_prefetch=1,
            grid=(t // tm,),
            in_specs=[pl.BlockSpec((TOP_K, tm), lambda i, pe: (0, i), memory_space=pltpu.SMEM),
                      pl.BlockSpec((tm * SUBLANES, LANES), lambda i, pe: (i, 0))],
            out_specs=pl.BlockSpec(memory_space=pl.ANY),
            scratch_shapes=[pltpu.VMEM((blk * SUBLANES, LANES), F32),
                            pltpu.SemaphoreType.DMA, pltpu.SemaphoreType.DMA],
        ),
        out_shape=jax.ShapeDtypeStruct((n_pad * SUBLANES, LANES), F32),
        compiler_params=pltpu.CompilerParams(
            dimension_semantics=("arbitrary",), vmem_limit_bytes=VMEM_LIMIT,
            has_side_effects=True),
        name="dispatch",
    )(padded_end, dest, xf)


WEIGHT_ROWS = 128


def _experts_kernel(be_ref, nu_ref, run_ref, next_ref, x_ref, wgu_hbm, wd_hbm, bg_ref, bu_ref,
                    bd_ref, y_ref, wgu_buf, wd_buf, wg_s, wu_s, wd_s, t_s, sem, *, blk):
    j = pl.program_id(0)
    used = j < nu_ref[0]
    expert = be_ref[j]
    new_expert = jnp.logical_or(j == 0, expert != be_ref[jnp.maximum(j - 1, 0)])

    def weight_copies(e, slot):
        return (pltpu.make_async_copy(wgu_hbm.at[e], wgu_buf.at[slot], sem.at[0, slot]),
                pltpu.make_async_copy(wd_hbm.at[e], wd_buf.at[slot], sem.at[1, slot]))

    def expert_ffn_tail(gate, up):
        gate = jnp.minimum(gate, SWIGLU_LIMIT)
        up = jnp.clip(up, -SWIGLU_LIMIT, SWIGLU_LIMIT)
        hidden = (up + 1.0) * gate * jax.nn.sigmoid(SWIGLU_ALPHA * gate)
        _store_token_tiles(y_ref, _dot(hidden.astype(BF16), wd_s[...]) + bd_ref[...])

    @pl.when(jnp.logical_and(used, new_expert))
    def _():
        slot = lax.rem(run_ref[expert], 2)

        @pl.when(j == 0)
        def _():
            for c in weight_copies(expert, slot):
                c.start()

        for c in weight_copies(expert, slot):
            c.wait()
        following = next_ref[expert]

        @pl.when(following >= 0)
        def _():
            for c in weight_copies(following, 1 - slot):
                c.start()

        xb = _load_token_tiles(x_ref, blk).astype(BF16)
        gate = bg_ref[...]
        up = bu_ref[...]
        for r in range(0, D_MODEL, MXU_WIDTH):
            wg_rows, wu_rows = [], []
            for rr in range(r, r + MXU_WIDTH, WEIGHT_ROWS):
                rows = slice(rr, rr + WEIGHT_ROWS)
                t_s[...] = wgu_buf[slot, rows, :].T
                wg_rows.append(t_s[pl.ds(0, D_FF, stride=2), :].T.astype(BF16))
                wu_rows.append(t_s[pl.ds(1, D_FF, stride=2), :].T.astype(BF16))
                wg_s[rows, :] = wg_rows[-1]
                wu_s[rows, :] = wu_rows[-1]
                wd_s[rows, :] = wd_buf[slot, rows, :].astype(BF16)
            x_cols = xb[:, r:r + MXU_WIDTH]
            gate = gate + _dot(x_cols, jnp.concatenate(wg_rows, axis=0))
            up = up + _dot(x_cols, jnp.concatenate(wu_rows, axis=0))
        expert_ffn_tail(gate, up)

    @pl.when(jnp.logical_and(used, jnp.logical_not(new_expert)))
    def _():
        xb = _load_token_tiles(x_ref, blk).astype(BF16)
        expert_ffn_tail(_dot(xb, wg_s[...]) + bg_ref[...], _dot(xb, wu_s[...]) + bu_ref[...])

    @pl.when(jnp.logical_not(used))
    def _():
        y_ref[...] = jnp.zeros_like(y_ref)


def _experts(block_expert, n_used, expert_run, next_expert, slots, wgu, wd, bg, bu, bd, blk):
    assert D_FF == D_MODEL
    n_pad = slots.shape[0] // SUBLANES
    n_blocks = n_pad // blk
    xmap = lambda j, be, nu, run, nxt: (jnp.minimum(j, nu[0] - 1), 0)
    wmap = lambda j, be, nu, run, nxt: (be[j], 0, 0)
    bspec = pl.BlockSpec((None, 1, D_FF), wmap)
    tiles = (blk * SUBLANES, LANES)
    hbm = pl.BlockSpec(memory_space=pl.ANY)
    return pl.pallas_call(
        functools.partial(_experts_kernel, blk=blk),
        grid_spec=pltpu.PrefetchScalarGridSpec(
            num_scalar_prefetch=4,
            grid=(n_blocks,),
            in_specs=[pl.BlockSpec(tiles, xmap), hbm, hbm, bspec, bspec, bspec],
            out_specs=pl.BlockSpec(tiles, lambda j, be, nu, run, nxt: (j, 0)),
            scratch_shapes=[pltpu.VMEM((2, D_MODEL, 2 * D_FF), F32),
                            pltpu.VMEM((2, D_FF, D_MODEL), F32),
                            pltpu.VMEM((D_MODEL, D_FF), BF16), pltpu.VMEM((D_MODEL, D_FF), BF16),
                            pltpu.VMEM((D_FF, D_MODEL), BF16),
                            pltpu.VMEM((2 * D_FF, WEIGHT_ROWS), F32),
                            pltpu.SemaphoreType.DMA((2, 2))],
        ),
        out_shape=jax.ShapeDtypeStruct((n_pad * SUBLANES, LANES), F32),
        compiler_params=pltpu.CompilerParams(
            dimension_semantics=("arbitrary",), vmem_limit_bytes=VMEM_LIMIT),
        name="experts",
    )(block_expert, n_used, expert_run, next_expert, slots, wgu, wd, bg, bu, bd)


def _combine_kernel(dest_ref, next_dest_ref, y_hbm, w_ref, x1_ref, o_ref, buf_even, buf_odd, sem,
                    *, tm):
    i = pl.program_id(0)
    n = pl.num_programs(0)
    bufs = (buf_even, buf_odd)

    def start_tokens(dests, parity, t0):
        for t in range(SUBLANES):
            for kk in range(TOP_K):
                pltpu.make_async_copy(
                    y_hbm.at[_token_rows(dests[kk, t0 + t]), :],
                    bufs[parity].at[kk, _token_rows(t0 + t), :],
                    sem.at[parity]).start(priority=kk % 2)

    def sum_tokens(parity, t0):
        rows = pl.ds(t0, SUBLANES)
        w = w_ref[rows, :]
        for s in range(ROW_TILES):
            cols = slice(s * LANES, (s + 1) * LANES)
            acc = x1_ref[rows, cols]
            for kk in range(TOP_K):
                tile_rows = pl.ds(t0 * SUBLANES + s, SUBLANES, stride=SUBLANES)
                acc = acc + bufs[parity][kk, tile_rows, :] * w[:, kk:kk + 1]
            o_ref[rows, cols] = acc

    def token_groups(body):
        def step(g, carry):
            body(pl.multiple_of(g * SUBLANES, SUBLANES))
            return carry

        lax.fori_loop(0, tm // SUBLANES, step, 0)

    pl.when(i == 0)(lambda: token_groups(lambda t0: start_tokens(dest_ref, 0, t0)))

    def tile(parity):
        for kk in range(TOP_K):
            pltpu.make_async_copy(y_hbm.at[pl.ds(0, tm * SUBLANES), :], bufs[parity].at[kk],
                                  sem.at[parity]).wait()

        @pl.when(i + 1 < n)
        def _():
            def both(t0):
                start_tokens(next_dest_ref, 1 - parity, t0)
                sum_tokens(parity, t0)

            token_groups(both)

        pl.when(i + 1 >= n)(lambda: token_groups(lambda t0: sum_tokens(parity, t0)))

    for parity in range(2):
        pl.when(lax.rem(i, 2) == parity)(functools.partial(tile, parity))


def _combine(dest, y_slots, w_t, x1, tm):
    t = x1.shape[0]
    assert t % tm == 0
    n = t // tm
    row = pl.BlockSpec((tm, D_MODEL), lambda i: (i, 0))
    return pl.pallas_call(
        functools.partial(_combine_kernel, tm=tm),
        grid=(n,),
        in_specs=[pl.BlockSpec((TOP_K, tm), lambda i: (0, i), memory_space=pltpu.SMEM),
                  pl.BlockSpec((TOP_K, tm), lambda i: (0, jnp.minimum(i + 1, n - 1)),
                               memory_space=pltpu.SMEM),
                  pl.BlockSpec(memory_space=pl.ANY),
                  pl.BlockSpec((tm, TOP_K), lambda i: (i, 0)),
                  row],
        out_specs=row,
        out_shape=jax.ShapeDtypeStruct((t, D_MODEL), F32),
        scratch_shapes=[pltpu.VMEM((TOP_K, tm * SUBLANES, LANES), F32),
                        pltpu.VMEM((TOP_K, tm * SUBLANES, LANES), F32),
                        pltpu.SemaphoreType.DMA((2,))],
        compiler_params=pltpu.CompilerParams(
            dimension_semantics=("arbitrary",), vmem_limit_bytes=VMEM_LIMIT),
        name="combine",
    )(dest, dest, y_slots, w_t, x1)


def _tiles(batch, seq):
    return dict(
        in_proj=min(512, seq),
        retention=min(1024, seq),
        sb_block=min(256, seq),
        sb_heads=4,
        out_proj=min(1024, seq),
        dest=min(8192, batch * seq),
        dispatch=min(2048, seq),
        combine=min(512, seq),
        expert_block=512,
    )


def _layer(x, attn_norm_g, w_in, ret_norm_g, sb_q_norm_g, sb_k_norm_g, w_out,
           ffn_norm_g, w_router, b_router, w_gate_up, b_gate_up, w_down, b_down):
    batch, seq, d = x.shape
    t = batch * seq
    tiles = _tiles(batch, seq)
    x2 = x.reshape(t, d)

    rq, rk, rv, rg, sq, sk, sv = _in_proj(
        x2, attn_norm_g, w_in, sb_q_norm_g, sb_k_norm_g, seq, tiles["in_proj"])
    ret = _retention(rq, rk, rv, rg, ret_norm_g, batch, seq, tiles["retention"])
    sb = _sb_attention(sq, sk, sv, batch, seq, tiles["sb_block"], LANES, tiles["sb_heads"])
    x1, xf, top_idx, top_w, rank, counts = _out_proj_route(
        ret, sb, x2, w_out, ffn_norm_g, w_router, b_router, tiles["out_proj"])

    blk = tiles["expert_block"]
    n_pad = t * TOP_K + N_EXPERTS * blk
    n_blocks = n_pad // blk
    counts = counts[:, 0]
    padded = (counts + blk - 1) // blk * blk
    padded_end = jnp.cumsum(padded)
    padded_start = padded_end - padded
    block_first = jnp.arange(n_blocks, dtype=jnp.int32) * blk
    block_expert = jnp.minimum(
        jnp.sum(padded_end[None, :] <= block_first[:, None], axis=1), N_EXPERTS - 1
    ).astype(jnp.int32)
    n_used = (padded_end[-1:] // blk).astype(jnp.int32)
    nonempty = counts > 0
    expert_run = (jnp.cumsum(nonempty) - nonempty).astype(jnp.int32)
    experts = jnp.arange(N_EXPERTS, dtype=jnp.int32)
    later = jnp.logical_and(nonempty[None, :], experts[None, :] > experts[:, None])
    next_expert = jnp.min(jnp.where(later, experts[None, :], N_EXPERTS), axis=1)
    next_expert = jnp.where(next_expert < N_EXPERTS, next_expert, -1).astype(jnp.int32)

    dest = _dest(padded_start, top_idx, rank, tiles["dest"])
    slots = _dispatch(padded_end, dest, xf, n_pad, tiles["dispatch"], blk)

    bgu = b_gate_up.astype(F32).reshape(N_EXPERTS, 1, D_FF, 2)
    y_slots = _experts(block_expert, n_used, expert_run, next_expert, slots,
                       w_gate_up.astype(F32), w_down.astype(F32),
                       bgu[..., 0], bgu[..., 1], b_down.astype(F32)[:, None, :], blk)

    out = _combine(dest, y_slots, top_w.T, x1, tiles["combine"])
    return out.reshape(batch, seq, d)


def kernel(x, attn_norm_g, w_in, ret_norm_g, sb_q_norm_g, sb_k_norm_g, w_out, ffn_norm_g,
           w_router, b_router, w_gate_up, b_gate_up, w_down, b_down):
    depth = attn_norm_g.shape[0]
    for l in range(depth):
        x = _layer(x, attn_norm_g[l], w_in[l], ret_norm_g[l], sb_q_norm_g[l], sb_k_norm_g[l],
                   w_out[l], ffn_norm_g[l], w_router[l], b_router[l], w_gate_up[l],
                   b_gate_up[l], w_down[l], b_down[l])
    return x
```

```python
import functools

import numpy as np
import jax
import jax.numpy as jnp
from jax import lax
from jax.experimental import pallas as pl
from jax.experimental.pallas import tpu as pltpu

D_MODEL = 1024
RET_WIDTH = 512
RET_HEADS = 4
RET_HEAD_DIM = 128
SB_WIDTH = 512
SB_HEADS = 8
SB_HEAD_DIM = 64
IN_PROJ_WIDTH = 4 * RET_WIDTH + 3 * SB_WIDTH
RET_CHUNK = 128
ROPE_BASE = 10000.0
N_EXPERTS = 32
TOP_K = 4
D_FF = D_MODEL
SWIGLU_LIMIT = 7.0
SWIGLU_ALPHA = 1.702
EPS = 1e-6

LANES = 128
MXU_WIDTH = 256
F32_EXP_UNDERFLOW = -88.0
VMEM_LIMIT = 56 * 1024 * 1024

BF16 = jnp.bfloat16
F32 = jnp.float32


def _split_bf16(v):
    hi = v.astype(BF16)
    lo = (v - hi.astype(F32)).astype(BF16)
    return hi, lo


SUBLANES = 8
ROW_TILES = D_MODEL // LANES
assert ROW_TILES == SUBLANES


def _store_token_tiles(ref, value):
    n = value.shape[0]
    for s in range(ROW_TILES):
        ref[pl.ds(s, n, stride=SUBLANES), :] = value[:, s * LANES:(s + 1) * LANES]


def _load_token_tiles(ref, n):
    return jnp.concatenate(
        [ref[pl.ds(s, n, stride=SUBLANES), :] for s in range(ROW_TILES)], axis=1)


def _token_rows(i):
    return pl.ds(pl.multiple_of(i * SUBLANES, SUBLANES), SUBLANES)


def _dot(a, b):
    return jnp.dot(a, b, preferred_element_type=F32)


def _dot_nt(a, b):
    return lax.dot_general(a, b, (((1,), (1,)), ((), ())), preferred_element_type=F32)


def _in_proj_kernel(x_ref, g_ref, w_ref, cos_ref, sin_ref, qg_ref, kg_ref, seg_ref,
                    rq_ref, rk_ref, rv_ref, rg_ref, sq_ref, sk_ref, sv_ref):
    x = x_ref[...]
    h = x * lax.rsqrt(jnp.mean(x * x, axis=-1, keepdims=True) + EPS) * g_ref[...]
    p = _dot(h.astype(BF16), w_ref[...])
    cos2 = cos_ref[...]
    sin2 = sin_ref[...]
    k_scale = RET_HEAD_DIM ** -0.5
    for hd in range(RET_HEADS):
        lo = hd * RET_HEAD_DIM
        q = p[:, lo:lo + RET_HEAD_DIM]
        k = p[:, RET_WIDTH + lo:RET_WIDTH + lo + RET_HEAD_DIM]
        q = q * cos2 + pltpu.roll(q, RET_HEAD_DIM // 2, axis=1) * sin2
        k = k * cos2 + pltpu.roll(k, RET_HEAD_DIM // 2, axis=1) * sin2
        rq_ref[:, lo:lo + RET_HEAD_DIM] = q.astype(rq_ref.dtype)
        rk_ref[:, lo:lo + RET_HEAD_DIM] = (k * k_scale).astype(rk_ref.dtype)
    rv_ref[...] = p[:, 2 * RET_WIDTH:3 * RET_WIDTH].astype(rv_ref.dtype)
    rg_ref[...] = p[:, 3 * RET_WIDTH:4 * RET_WIDTH]
    base = 4 * RET_WIDTH
    seg = seg_ref[...]

    def head_norm(v, gain):
        hi, lo = _split_bf16(v * v)
        group = seg.shape[0]
        ms = jnp.concatenate(
            [_dot(hi[:, g:g + group], seg) + _dot(lo[:, g:g + group], seg)
             for g in range(0, SB_WIDTH, group)], axis=1) * (1.0 / SB_HEAD_DIM)
        return v * lax.rsqrt(ms + EPS) * gain

    sq = head_norm(p[:, base:base + SB_WIDTH], qg_ref[...])
    sk = head_norm(p[:, base + SB_WIDTH:base + 2 * SB_WIDTH], kg_ref[...])
    sq_ref[...] = (sq * (SB_HEAD_DIM ** -0.5)).astype(sq_ref.dtype)
    sk_ref[...] = sk.astype(sk_ref.dtype)
    sv_ref[...] = p[:, base + 2 * SB_WIDTH:base + 3 * SB_WIDTH].astype(sv_ref.dtype)


def _rope_tables(seq):
    half = RET_HEAD_DIM // 2
    inv = ROPE_BASE ** (-np.arange(half, dtype=np.float64) / half)
    n_a = seq // LANES
    ang_a = (np.arange(n_a, dtype=np.float64) * LANES)[:, None] * inv[None, :]
    ang_b = np.arange(LANES, dtype=np.float64)[:, None] * inv[None, :]
    ca, sa = jnp.asarray(np.cos(ang_a), F32), jnp.asarray(np.sin(ang_a), F32)
    cb, sb = jnp.asarray(np.cos(ang_b), F32), jnp.asarray(np.sin(ang_b), F32)
    cos = (ca[:, None, :] * cb[None] - sa[:, None, :] * sb[None]).reshape(seq, half)
    sin = (sa[:, None, :] * cb[None] + ca[:, None, :] * sb[None]).reshape(seq, half)
    return jnp.concatenate([cos, cos], axis=-1), jnp.concatenate([-sin, sin], axis=-1)


def _in_proj(x2, norm_g, w_in, sb_q_g, sb_k_g, seq, tm):
    t = x2.shape[0]
    assert seq % tm == 0 and t % tm == 0
    cos2, sin2 = _rope_tables(seq)
    seg = np.kron(np.eye(MXU_WIDTH // SB_HEAD_DIM), np.ones((SB_HEAD_DIM, SB_HEAD_DIM)))
    seg = jnp.asarray(seg, BF16)
    qg = jnp.tile(sb_q_g.astype(F32), SB_HEADS)[None, :]
    kg = jnp.tile(sb_k_g.astype(F32), SB_HEADS)[None, :]
    n_pos = seq // tm
    row = lambda w: pl.BlockSpec((tm, w), lambda i: (i, 0))
    full = lambda a: pl.BlockSpec(a.shape, lambda i: (0,) * a.ndim)
    pos = pl.BlockSpec((tm, RET_HEAD_DIM), lambda i: (i % n_pos, 0))
    w_bf = w_in.astype(BF16)
    g2 = norm_g.astype(F32)[None, :]
    out = lambda dt: jax.ShapeDtypeStruct((t, RET_WIDTH), dt)
    return pl.pallas_call(
        _in_proj_kernel,
        grid=(t // tm,),
        in_specs=[row(D_MODEL), full(g2), full(w_bf), pos, pos, full(qg), full(kg), full(seg)],
        out_specs=[row(RET_WIDTH)] * 7,
        out_shape=[out(BF16), out(F32), out(BF16), out(F32), out(BF16), out(BF16), out(BF16)],
        compiler_params=pltpu.CompilerParams(
            dimension_semantics=("arbitrary",), vmem_limit_bytes=VMEM_LIMIT),
        name="in_proj",
    )(x2, g2, w_bf, cos2, sin2, qg, kg, seg)


def _retention_kernel(q_ref, k_ref, v_ref, g_ref, ng_ref, intra_ref, qd_ref, kd_ref, cd_ref,
                      o_ref, state_ref, *, chunks):
    @pl.when(pl.program_id(1) == 0)
    def _():
        state_ref[...] = jnp.zeros_like(state_ref)

    def chunk(c, carry):
        rows = pl.ds(pl.multiple_of(c * RET_CHUNK, RET_CHUNK), RET_CHUNK)
        heads = range(RET_HEADS)
        cols = [slice(hd * RET_HEAD_DIM, (hd + 1) * RET_HEAD_DIM) for hd in heads]
        q = [q_ref[rows, cl] for cl in cols]
        k = [k_ref[rows, cl] for cl in cols]
        v = [v_ref[rows, cl] for cl in cols]
        state = [state_ref[hd] for hd in heads]
        scores = [_dot_nt(q[hd], k[hd].astype(BF16)) * intra_ref[hd] for hd in heads]
        cross = [_dot(q[hd], state[hd].astype(BF16)) * qd_ref[:, cols[hd]] for hd in heads]
        inner = [_dot(scores[hd].astype(BF16), v[hd]) for hd in heads]
        kd = [(k[hd] * kd_ref[:, cols[hd]]).astype(BF16) for hd in heads]
        for hd in heads:
            state_ref[hd] = state[hd] * cd_ref[:, cols[hd]] + _dot(kd[hd].T, v[hd])
        for hd in heads:
            o = inner[hd] + cross[hd]
            o = o * lax.rsqrt(jnp.mean(o * o, axis=-1, keepdims=True) + EPS) * ng_ref[:, cols[hd]]
            gate = g_ref[rows, cols[hd]]
            o_ref[rows, cols[hd]] = (o * (gate * jax.nn.sigmoid(gate))).astype(o_ref.dtype)
        return carry

    lax.fori_loop(0, chunks, chunk, 0, unroll=True)


def _retention(rq, rk, rv, rg, ret_norm_g, batch, seq, rows):
    assert seq % rows == 0 and rows % RET_CHUNK == 0
    log_gamma = np.log(1.0 - 2.0 ** (-5.0 - np.arange(RET_HEADS, dtype=np.float64)))
    idx = np.arange(RET_CHUNK, dtype=np.float64)
    diff = idx[:, None] - idx[None, :]
    intra = np.where(diff >= 0, np.exp(log_gamma[:, None, None] * np.maximum(diff, 0.0)), 0.0)
    q_decay = np.exp(log_gamma[:, None] * (idx + 1.0))
    k_decay = np.exp(log_gamma[:, None] * (RET_CHUNK - 1.0 - idx))
    chunk_decay = np.exp(log_gamma * RET_CHUNK)
    lane_rep = lambda a: np.repeat(a.T, RET_HEAD_DIM, axis=1)
    intra = jnp.asarray(intra, F32)
    qd = jnp.asarray(lane_rep(q_decay), F32)
    kd = jnp.asarray(lane_rep(k_decay), F32)
    cd = jnp.asarray(np.repeat(chunk_decay, RET_HEAD_DIM)[None, :], F32)
    ng = ret_norm_g.astype(F32).reshape(1, RET_WIDTH)
    n_r = seq // rows
    blk = pl.BlockSpec((rows, RET_WIDTH), lambda b, r: (b * n_r + r, 0))
    full = lambda a: pl.BlockSpec(a.shape, lambda b, r: (0,) * a.ndim)
    return pl.pallas_call(
        functools.partial(_retention_kernel, chunks=rows // RET_CHUNK),
        grid=(batch, n_r),
        in_specs=[blk, blk, blk, blk, full(ng), full(intra), full(qd), full(kd), full(cd)],
        out_specs=blk,
        out_shape=jax.ShapeDtypeStruct((batch * seq, RET_WIDTH), BF16),
        scratch_shapes=[pltpu.VMEM((RET_HEADS, RET_HEAD_DIM, RET_HEAD_DIM), F32)],
        compiler_params=pltpu.CompilerParams(
            dimension_semantics=("arbitrary", "arbitrary"), vmem_limit_bytes=VMEM_LIMIT),
        name="retention",
    )(rq, rk, rv, rg, ng, intra, qd, kd, cd)


def _sb_attn_kernel(q_ref, k_ref, v_ref, tri_ref, o_ref, *, blk, sub, heads):
    i = pl.program_id(2)
    nsub = blk // sub
    lane = lax.broadcasted_iota(jnp.int32, (sub, LANES), 1)
    first_head = lane < SB_HEAD_DIM
    tri = tri_ref[...]
    below_diag = (lax.broadcasted_iota(jnp.int32, (sub, sub), 1)
                  < lax.broadcasted_iota(jnp.int32, (sub, sub), 0))
    chains = [(s, h) for s in range(nsub) for h in range(heads)]
    cols = [slice((h // 2) * LANES, (h // 2 + 1) * LANES) for _, h in chains]


    def score_phase(qms, kbs, masks):
        zs = [_dot_nt(qm, kb) for qm, kb in zip(qms, kbs)]
        out = []
        for z, mask in zip(zs, masks):
            log_beta = jnp.minimum(z, 0.0) - jnp.log(1.0 + jnp.exp(-jnp.abs(z)))
            log_rest = log_beta - z
            if mask is not None:
                log_rest = jnp.where(mask, log_rest, 0.0)
            out.append((log_beta, log_rest))
        return out

    def later_phase(log_rests):
        parts = [_split_bf16(r) for r in log_rests]
        laters = [_dot(hi, tri) + _dot(lo, tri) for hi, lo in parts]
        return [(later, later[:, 0:1] + r[:, 0:1]) for later, r in zip(laters, log_rests)]

    def weight_phase(log_betas, log_remainings, masks):
        ws = [jnp.exp(b + r) for b, r in zip(log_betas, log_remainings)]
        ws = [w if m is None else jnp.where(m, w, 0.0) for w, m in zip(ws, masks)]
        return [w.astype(BF16) for w in ws]

    def queries():
        out = []
        for (s, h), c in zip(chains, cols):
            q = q_ref[s * sub:(s + 1) * sub, c]
            keep = first_head if h % 2 == 0 else jnp.logical_not(first_head)
            out.append(jnp.where(keep, q, jnp.zeros_like(q)))
        return out

    def key_rows(chunk):
        return pl.ds(pl.multiple_of(chunk * sub, sub), sub)

    def first_steps():
        qms = queries()
        own = [key_rows(i * nsub + s) for s, _ in chains]
        prev = [key_rows(jnp.maximum(i * nsub + s - 1, 0)) for s, _ in chains]
        prev_masks = [(i > 0) if s == 0 else None for s, _ in chains]
        sc_o = score_phase(qms, [k_ref[r, c] for r, c in zip(own, cols)], [below_diag] * len(chains))
        sc_p = score_phase(qms, [k_ref[r, c] for r, c in zip(prev, cols)], prev_masks)
        ls_o = later_phase([rest for _, rest in sc_o])
        ls_p = later_phase([rest for _, rest in sc_p])
        w_o = weight_phase([b for b, _ in sc_o], [later for later, _ in ls_o],
                           [below_diag] * len(chains))
        w_p = weight_phase([b for b, _ in sc_p],
                           [ls_o[n][1] + ls_p[n][0] for n in range(len(chains))], prev_masks)
        accs = [_dot(w, v_ref[r, c]) for w, r, c in zip(w_o, own, cols)]
        accs = [a + _dot(w, v_ref[r, c]) for a, w, r, c in zip(accs, w_p, prev, cols)]
        return tuple(ls_o[n][1] + ls_p[n][1] for n in range(len(chains))), tuple(accs)

    def cond(carry):
        j, cs, _ = carry
        c_max = functools.reduce(jnp.maximum, cs)
        return jnp.logical_and(j + (nsub - 1) >= 0, jnp.max(c_max) > F32_EXP_UNDERFLOW)

    def body(carry):
        j, cs, accs = carry
        rows = [key_rows(jnp.maximum(j + s, 0)) for s, _ in chains]
        masks = [None if s == nsub - 1 else (j + s >= 0) for s, _ in chains]
        sc = score_phase(queries(), [k_ref[r, c] for r, c in zip(rows, cols)], masks)
        ls = later_phase([rest for _, rest in sc])
        ws = weight_phase([b for b, _ in sc], [cs[n] + ls[n][0] for n in range(len(chains))], masks)
        new_accs = tuple(a + _dot(w, v_ref[r, c]) for a, w, r, c in zip(accs, ws, rows, cols))
        return j - 1, tuple(cs[n] + ls[n][1] for n in range(len(chains))), new_accs

    init = (i * nsub - 2,) + first_steps()
    _, _, accs = lax.while_loop(cond, body, init)
    for s in range(nsub):
        for p in range(heads // 2):
            pair = [accs[n] for n, (ss, h) in enumerate(chains) if ss == s and h // 2 == p]
            o_ref[s * sub:(s + 1) * sub, p * LANES:(p + 1) * LANES] = jnp.where(
                first_head, pair[0], pair[1]).astype(o_ref.dtype)


def _sb_attention(sq, sk, sv, batch, seq, blk, sub, heads):
    assert seq % blk == 0 and blk % sub == 0 and sub == LANES
    assert heads % 2 == 0 and SB_HEADS % heads == 0
    nq = seq // blk
    width = heads * SB_HEAD_DIM
    groups = SB_WIDTH // width
    j = np.arange(sub)
    tri = jnp.asarray((j[:, None] > j[None, :]).astype(np.float32), BF16)
    qblk = pl.BlockSpec((blk, width), lambda b, g, i: (b * nq + i, g))
    kvblk = pl.BlockSpec((seq, width), lambda b, g, i: (b, g))
    return pl.pallas_call(
        functools.partial(_sb_attn_kernel, blk=blk, sub=sub, heads=heads),
        grid=(batch, groups, nq),
        in_specs=[qblk, kvblk, kvblk, pl.BlockSpec(tri.shape, lambda b, g, i: (0, 0))],
        out_specs=qblk,
        out_shape=jax.ShapeDtypeStruct((batch * seq, SB_WIDTH), BF16),
        compiler_params=pltpu.CompilerParams(
            dimension_semantics=("arbitrary", "arbitrary", "arbitrary"),
            vmem_limit_bytes=VMEM_LIMIT),
        name="sb_attn",
    )(sq, sk, sv, tri)


OUT_PROJ_ROW_GROUPS = 4


def _out_proj_kernel(ret_ref, sb_ref, x_ref, wtop_ref, wbot_ref, g_ref, wrh_ref, wrl_ref,
                     br_ref, tri_ref,
                     x1_ref, xf_ref, idx_ref, w_ref, rank_ref, cnt_ref, run_ref):
    @pl.when(pl.program_id(0) == 0)
    def _():
        run_ref[...] = jnp.zeros_like(run_ref)

    rows = x_ref.shape[0] // OUT_PROJ_ROW_GROUPS
    groups = [slice(r * rows, (r + 1) * rows) for r in range(OUT_PROJ_ROW_GROUPS)]
    x1s = [x_ref[g, :] + _dot(ret_ref[g, :], wtop_ref[...]) + _dot(sb_ref[g, :], wbot_ref[...])
           for g in groups]
    for g, x1 in zip(groups, x1s):
        x1_ref[g, :] = x1
    xfs = [x1 * lax.rsqrt(jnp.mean(x1 * x1, axis=-1, keepdims=True) + EPS) * g_ref[...]
           for x1 in x1s]
    for r, xf in enumerate(xfs):
        _store_token_tiles(xf_ref.at[pl.ds(r * rows * SUBLANES, rows * SUBLANES), :], xf)
    parts = [_split_bf16(xf) for xf in xfs]
    wrh = wrh_ref[...]
    wrl = wrl_ref[...]
    logits = jnp.concatenate(
        [_dot_nt(wrh, xh) + _dot_nt(wrh, xl) + _dot_nt(wrl, xh) for xh, xl in parts],
        axis=1) + br_ref[...]
    n_e, tm = logits.shape
    e_iota = lax.broadcasted_iota(jnp.int32, (n_e, tm), 0)
    cur = logits
    tops, sels, hots = [], [], []
    for _ in range(TOP_K):
        m = jnp.max(cur, axis=0, keepdims=True)
        sel = jnp.min(jnp.where(cur == m, e_iota, n_e), axis=0, keepdims=True)
        hot = e_iota == sel
        cur = jnp.where(hot, -jnp.inf, cur)
        tops.append(m)
        sels.append(sel)
        hots.append(hot)
    ps = [jnp.exp(m - tops[0]) for m in tops]
    denom = ps[0] + ps[1] + ps[2] + ps[3]
    chosen = jnp.zeros((n_e, tm), F32)
    for hot in hots:
        chosen = chosen + hot.astype(F32)
    before = run_ref[:, 0:1] + _dot(chosen.astype(BF16), tri_ref[...])
    for kk in range(TOP_K):
        idx_ref[kk:kk + 1, :] = sels[kk]
        w_ref[kk:kk + 1, :] = ps[kk] / denom
        rank = jnp.sum(jnp.where(hots[kk], before, 0.0), axis=0, keepdims=True)
        rank_ref[kk:kk + 1, :] = rank.astype(jnp.int32)
    run = run_ref[...] + jnp.sum(chosen, axis=1, keepdims=True)
    run_ref[...] = run
    cnt_ref[...] = run.astype(jnp.int32)


def _out_proj_route(ret, sb, x2, w_out, ffn_g, w_router, b_router, tm):
    t = x2.shape[0]
    assert t % tm == 0
    w_bf = w_out.astype(BF16)
    wtop, wbot = w_bf[:RET_WIDTH], w_bf[RET_WIDTH:]
    g2 = ffn_g.astype(F32)[None, :]
    wr_t = w_router.astype(F32).T
    wrh = wr_t.astype(BF16)
    wrl = (wr_t - wrh.astype(F32)).astype(BF16)
    br = b_router.astype(F32)[:, None]
    tt = np.arange(tm)
    tri = jnp.asarray((tt[:, None] < tt[None, :]).astype(np.float32), BF16)
    row = lambda w: pl.BlockSpec((tm, w), lambda i: (i, 0))
    full = lambda a: pl.BlockSpec(a.shape, lambda i: (0,) * a.ndim)
    col = pl.BlockSpec((TOP_K, tm), lambda i: (0, i))
    return pl.pallas_call(
        _out_proj_kernel,
        grid=(t // tm,),
        in_specs=[row(RET_WIDTH), row(SB_WIDTH), row(D_MODEL), full(wtop), full(wbot), full(g2),
                  full(wrh), full(wrl), full(br), full(tri)],
        out_specs=[row(D_MODEL), pl.BlockSpec((tm * SUBLANES, LANES), lambda i: (i, 0)),
                   col, col, col, pl.BlockSpec((N_EXPERTS, LANES), lambda i: (0, 0))],
        out_shape=[jax.ShapeDtypeStruct((t, D_MODEL), F32),
                   jax.ShapeDtypeStruct((t * SUBLANES, LANES), F32),
                   jax.ShapeDtypeStruct((TOP_K, t), jnp.int32),
                   jax.ShapeDtypeStruct((TOP_K, t), F32),
                   jax.ShapeDtypeStruct((TOP_K, t), jnp.int32),
                   jax.ShapeDtypeStruct((N_EXPERTS, LANES), jnp.int32)],
        scratch_shapes=[pltpu.VMEM((N_EXPERTS, LANES), F32)],
        compiler_params=pltpu.CompilerParams(
            dimension_semantics=("arbitrary",), vmem_limit_bytes=VMEM_LIMIT),
        name="out_proj_route",
    )(ret, sb, x2, wtop, wbot, g2, wrh, wrl, br, tri)


def _dest_kernel(pstart_ref, idx_ref, rank_ref, dest_ref):
    idx = idx_ref[...]
    dest = rank_ref[...]
    for e in range(N_EXPERTS):
        dest = dest + jnp.where(idx == e, pstart_ref[e], 0)
    dest_ref[...] = dest


def _dest(padded_start, top_idx, rank, tn):
    t = top_idx.shape[1]
    assert t % tn == 0
    col = pl.BlockSpec((TOP_K, tn), lambda i, ps: (0, i))
    return pl.pallas_call(
        _dest_kernel,
        grid_spec=pltpu.PrefetchScalarGridSpec(
            num_scalar_prefetch=1, grid=(t // tn,), in_specs=[col, col], out_specs=col),
        out_shape=jax.ShapeDtypeStruct((TOP_K, t), jnp.int32),
        name="dest",
    )(padded_start, top_idx, rank)


def _dispatch_kernel(pend_ref, dest_ref, xf_ref, slots_hbm, zbuf, sem, zsem, *, tm, blk):
    @pl.when(pl.program_id(0) == 0)
    def _():
        zbuf[...] = jnp.zeros_like(zbuf)

        def tail_copy(e):
            first = pl.multiple_of((pend_ref[e] - blk) * SUBLANES, blk * SUBLANES)
            return pltpu.make_async_copy(
                zbuf, slots_hbm.at[pl.ds(first, blk * SUBLANES), :], zsem)

        def nonempty(e):
            return pend_ref[e] > (pend_ref[e - 1] if e else 0)

        for e in range(N_EXPERTS):
            pl.when(nonempty(e))(lambda e=e: tail_copy(e).start())
        for e in range(N_EXPERTS):
            pl.when(nonempty(e))(lambda e=e: tail_copy(e).wait())

    def row_copy(t, kk):
        return pltpu.make_async_copy(
            xf_ref.at[_token_rows(t), :],
            slots_hbm.at[_token_rows(dest_ref[kk, t]), :], sem)

    def start(t, carry):
        for kk in range(TOP_K):
            row_copy(t, kk).start(priority=kk % 2)
        return carry

    lax.fori_loop(0, tm, start, 0, unroll=8)
    for kk in range(TOP_K):
        pltpu.make_async_copy(
            xf_ref, slots_hbm.at[pl.ds(0, tm * SUBLANES), :], sem).wait()


def _dispatch(padded_end, dest, xf, n_pad, tm, blk):
    t = xf.shape[0] // SUBLANES
    assert t % tm == 0
    return pl.pallas_call(
        functools.partial(_dispatch_kernel, tm=tm, blk=blk),
        grid_spec=pltpu.PrefetchScalarGridSpec(
            num_scalar_prefetch=1,
            grid=(t // tm,),
            in_specs=[pl.BlockSpec((TOP_K, tm), lambda i, pe: (0, i), memory_space=pltpu.SMEM),
                      pl.BlockSpec((tm * SUBLANES, LANES), lambda i, pe: (i, 0))],
            out_specs=pl.BlockSpec(memory_space=pl.ANY),
            scratch_shapes=[pltpu.VMEM((blk * SUBLANES, LANES), F32),
                            pltpu.SemaphoreType.DMA, pltpu.SemaphoreType.DMA],
        ),
        out_shape=jax.ShapeDtypeStruct((n_pad * SUBLANES, LANES), F32),
        compiler_params=pltpu.CompilerParams(
            dimension_semantics=("arbitrary",), vmem_limit_bytes=VMEM_LIMIT,
            has_side_effects=True),
        name="dispatch",
    )(padded_end, dest, xf)


WEIGHT_ROWS = 128


def _experts_kernel(be_ref, nu_ref, run_ref, next_ref, x_ref, wgu_hbm, wd_hbm, bg_ref, bu_ref,
                    bd_ref, y_ref, wgu_buf, wd_buf, wg_s, wu_s, wd_s, t_s, sem, *, blk):
    j = pl.program_id(0)
    used = j < nu_ref[0]
    expert = be_ref[j]
    new_expert = jnp.logical_or(j == 0, expert != be_ref[jnp.maximum(j - 1, 0)])

    def weight_copies(e, slot):
        return (pltpu.make_async_copy(wgu_hbm.at[e], wgu_buf.at[slot], sem.at[0, slot]),
                pltpu.make_async_copy(wd_hbm.at[e], wd_buf.at[slot], sem.at[1, slot]))

    def expert_ffn_tail(gate, up):
        gate = jnp.minimum(gate, SWIGLU_LIMIT)
        up = jnp.clip(up, -SWIGLU_LIMIT, SWIGLU_LIMIT)
        hidden = (up + 1.0) * gate * jax.nn.sigmoid(SWIGLU_ALPHA * gate)
        _store_token_tiles(y_ref, _dot(hidden.astype(BF16), wd_s[...]) + bd_ref[...])

    @pl.when(jnp.logical_and(used, new_expert))
    def _():
        slot = lax.rem(run_ref[expert], 2)

        @pl.when(j == 0)
        def _():
            for c in weight_copies(expert, slot):
                c.start()

        for c in weight_copies(expert, slot):
            c.wait()
        following = next_ref[expert]

        @pl.when(following >= 0)
        def _():
            for c in weight_copies(following, 1 - slot):
                c.start()

        xb = _load_token_tiles(x_ref, blk).astype(BF16)
        gate = bg_ref[...]
        up = bu_ref[...]
        for r in range(0, D_MODEL, MXU_WIDTH):
            wg_rows, wu_rows = [], []
            for rr in range(r, r + MXU_WIDTH, WEIGHT_ROWS):
                rows = slice(rr, rr + WEIGHT_ROWS)
                t_s[...] = wgu_buf[slot, rows, :].T
                wg_rows.append(t_s[pl.ds(0, D_FF, stride=2), :].T.astype(BF16))
                wu_rows.append(t_s[pl.ds(1, D_FF, stride=2), :].T.astype(BF16))
                wg_s[rows, :] = wg_rows[-1]
                wu_s[rows, :] = wu_rows[-1]
                wd_s[rows, :] = wd_buf[slot, rows, :].astype(BF16)
            x_cols = xb[:, r:r + MXU_WIDTH]
            gate = gate + _dot(x_cols, jnp.concatenate(wg_rows, axis=0))
            up = up + _dot(x_cols, jnp.concatenate(wu_rows, axis=0))
        expert_ffn_tail(gate, up)

    @pl.when(jnp.logical_and(used, jnp.logical_not(new_expert)))
    def _():
        xb = _load_token_tiles(x_ref, blk).astype(BF16)
        expert_ffn_tail(_dot(xb, wg_s[...]) + bg_ref[...], _dot(xb, wu_s[...]) + bu_ref[...])

    @pl.when(jnp.logical_not(used))
    def _():
        y_ref[...] = jnp.zeros_like(y_ref)


def _experts(block_expert, n_used, expert_run, next_expert, slots, wgu, wd, bg, bu, bd, blk):
    assert D_FF == D_MODEL
    n_pad = slots.shape[0] // SUBLANES
    n_blocks = n_pad // blk
    xmap = lambda j, be, nu, run, nxt: (jnp.minimum(j, nu[0] - 1), 0)
    wmap = lambda j, be, nu, run, nxt: (be[j], 0, 0)
    bspec = pl.BlockSpec((None, 1, D_FF), wmap)
    tiles = (blk * SUBLANES, LANES)
    hbm = pl.BlockSpec(memory_space=pl.ANY)
    return pl.pallas_call(
        functools.partial(_experts_kernel, blk=blk),
        grid_spec=pltpu.PrefetchScalarGridSpec(
            num_scalar_prefetch=4,
            grid=(n_blocks,),
            in_specs=[pl.BlockSpec(tiles, xmap), hbm, hbm, bspec, bspec, bspec],
            out_specs=pl.BlockSpec(tiles, lambda j, be, nu, run, nxt: (j, 0)),
            scratch_shapes=[pltpu.VMEM((2, D_MODEL, 2 * D_FF), F32),
                            pltpu.VMEM((2, D_FF, D_MODEL), F32),
                            pltpu.VMEM((D_MODEL, D_FF), BF16), pltpu.VMEM((D_MODEL, D_FF), BF16),
                            pltpu.VMEM((D_FF, D_MODEL), BF16),
                            pltpu.VMEM((2 * D_FF, WEIGHT_ROWS), F32),
                            pltpu.SemaphoreType.DMA((2, 2))],
        ),
        out_shape=jax.ShapeDtypeStruct((n_pad * SUBLANES, LANES), F32),
        compiler_params=pltpu.CompilerParams(
            dimension_semantics=("arbitrary",), vmem_limit_bytes=VMEM_LIMIT),
        name="experts",
    )(block_expert, n_used, expert_run, next_expert, slots, wgu, wd, bg, bu, bd)


def _combine_kernel(dest_ref, next_dest_ref, y_hbm, w_ref, x1_ref, o_ref, buf, sem, *, tm):
    i = pl.program_id(0)
    n = pl.num_programs(0)
    slot = lax.rem(i, 2)

    def gather(dests, into):
        def start(t, carry):
            for kk in range(TOP_K):
                pltpu.make_async_copy(
                    y_hbm.at[_token_rows(dests[kk, t]), :],
                    buf.at[into, kk, _token_rows(t), :], sem.at[into]).start(priority=kk % 2)
            return carry

        lax.fori_loop(0, tm, start, 0, unroll=8)

    pl.when(i == 0)(lambda: gather(dest_ref, 0))
    pl.when(i + 1 < n)(lambda: gather(next_dest_ref, 1 - slot))
    for kk in range(TOP_K):
        pltpu.make_async_copy(
            y_hbm.at[pl.ds(0, tm * SUBLANES), :], buf.at[slot, kk], sem.at[slot]).wait()
    w = w_ref[...]
    for s in range(ROW_TILES):
        cols = slice(s * LANES, (s + 1) * LANES)
        acc = x1_ref[:, cols]
        for kk in range(TOP_K):
            acc = acc + buf[slot, kk, pl.ds(s, tm, stride=SUBLANES), :] * w[:, kk:kk + 1]
        o_ref[:, cols] = acc


def _combine(dest, y_slots, w_t, x1, tm):
    t = x1.shape[0]
    assert t % tm == 0
    n = t // tm
    row = pl.BlockSpec((tm, D_MODEL), lambda i: (i, 0))
    return pl.pallas_call(
        functools.partial(_combine_kernel, tm=tm),
        grid=(n,),
        in_specs=[pl.BlockSpec((TOP_K, tm), lambda i: (0, i), memory_space=pltpu.SMEM),
                  pl.BlockSpec((TOP_K, tm), lambda i: (0, jnp.minimum(i + 1, n - 1)),
                               memory_space=pltpu.SMEM),
                  pl.BlockSpec(memory_space=pl.ANY),
                  pl.BlockSpec((tm, TOP_K), lambda i: (i, 0)),
                  row],
        out_specs=row,
        out_shape=jax.ShapeDtypeStruct((t, D_MODEL), F32),
        scratch_shapes=[pltpu.VMEM((2, TOP_K, tm * SUBLANES, LANES), F32),
                        pltpu.SemaphoreType.DMA((2,))],
        compiler_params=pltpu.CompilerParams(
            dimension_semantics=("arbitrary",), vmem_limit_bytes=VMEM_LIMIT),
        name="combine",
    )(dest, dest, y_slots, w_t, x1)


def _tiles(batch, seq):
    return dict(
        in_proj=min(512, seq),
        retention=min(1024, seq),
        sb_block=min(512, seq),
        sb_heads=4,
        out_proj=min(1024, seq),
        dest=min(8192, batch * seq),
        dispatch=min(2048, seq),
        combine=min(512, seq),
        expert_block=512,
    )


def _layer(x, attn_norm_g, w_in, ret_norm_g, sb_q_norm_g, sb_k_norm_g, w_out,
           ffn_norm_g, w_router, b_router, w_gate_up, b_gate_up, w_down, b_down):
    batch, seq, d = x.shape
    t = batch * seq
    tiles = _tiles(batch, seq)
    x2 = x.reshape(t, d)

    rq, rk, rv, rg, sq, sk, sv = _in_proj(
        x2, attn_norm_g, w_in, sb_q_norm_g, sb_k_norm_g, seq, tiles["in_proj"])
    ret = _retention(rq, rk, rv, rg, ret_norm_g, batch, seq, tiles["retention"])
    sb = _sb_attention(sq, sk, sv, batch, seq, tiles["sb_block"], LANES, tiles["sb_heads"])
    x1, xf, top_idx, top_w, rank, counts = _out_proj_route(
        ret, sb, x2, w_out, ffn_norm_g, w_router, b_router, tiles["out_proj"])

    blk = tiles["expert_block"]
    n_pad = t * TOP_K + N_EXPERTS * blk
    n_blocks = n_pad // blk
    counts = counts[:, 0]
    padded = (counts + blk - 1) // blk * blk
    padded_end = jnp.cumsum(padded)
    padded_start = padded_end - padded
    block_first = jnp.arange(n_blocks, dtype=jnp.int32) * blk
    block_expert = jnp.minimum(
        jnp.sum(padded_end[None, :] <= block_first[:, None], axis=1), N_EXPERTS - 1
    ).astype(jnp.int32)
    n_used = (padded_end[-1:] // blk).astype(jnp.int32)
    nonempty = counts > 0
    expert_run = (jnp.cumsum(nonempty) - nonempty).astype(jnp.int32)
    experts = jnp.arange(N_EXPERTS, dtype=jnp.int32)
    later = jnp.logical_and(nonempty[None, :], experts[None, :] > experts[:, None])
    next_expert = jnp.min(jnp.where(later, experts[None, :], N_EXPERTS), axis=1)
    next_expert = jnp.where(next_expert < N_EXPERTS, next_expert, -1).astype(jnp.int32)

    dest = _dest(padded_start, top_idx, rank, tiles["dest"])
    slots = _dispatch(padded_end, dest, xf, n_pad, tiles["dispatch"], blk)

    bgu = b_gate_up.astype(F32).reshape(N_EXPERTS, 1, D_FF, 2)
    y_slots = _experts(block_expert, n_used, expert_run, next_expert, slots,
                       w_gate_up.astype(F32), w_down.astype(F32),
                       bgu[..., 0], bgu[..., 1], b_down.astype(F32)[:, None, :], blk)

    out = _combine(dest, y_slots, top_w.T, x1, tiles["combine"])
    return out.reshape(batch, seq, d)


def kernel(x, attn_norm_g, w_in, ret_norm_g, sb_q_norm_g, sb_k_norm_g, w_out, ffn_norm_g,
           w_router, b_router, w_gate_up, b_gate_up, w_down, b_down):
    depth = attn_norm_g.shape[0]
    for l in range(depth):
        x = _layer(x, attn_norm_g[l], w_in[l], ret_norm_g[l], sb_q_norm_g[l], sb_k_norm_g[l],
                   w_out[l], ffn_norm_g[l], w_router[l], b_router[l], w_gate_up[l],
                   b_gate_up[l], w_down[l], b_down[l])
    return x
```

```python
import functools

import numpy as np
import jax
import jax.numpy as jnp
from jax import lax
from jax.experimental import pallas as pl
from jax.experimental.pallas import tpu as pltpu

D_MODEL = 1024
RET_WIDTH = 512
RET_HEADS = 4
RET_HEAD_DIM = 128
SB_WIDTH = 512
SB_HEADS = 8
SB_HEAD_DIM = 64
IN_PROJ_WIDTH = 4 * RET_WIDTH + 3 * SB_WIDTH
RET_CHUNK = 128
ROPE_BASE = 10000.0
N_EXPERTS = 32
TOP_K = 4
D_FF = D_MODEL
SWIGLU_LIMIT = 7.0
SWIGLU_ALPHA = 1.702
EPS = 1e-6

LANES = 128
MXU_WIDTH = 256
F32_EXP_UNDERFLOW = -88.0
VMEM_LIMIT = 56 * 1024 * 1024

BF16 = jnp.bfloat16
F32 = jnp.float32


def _split_bf16(v):
    hi = v.astype(BF16)
    lo = (v - hi.astype(F32)).astype(BF16)
    return hi, lo


SUBLANES = 8
ROW_TILES = D_MODEL // LANES
assert ROW_TILES == SUBLANES


def _store_token_tiles(ref, value):
    n = value.shape[0]
    for s in range(ROW_TILES):
        ref[pl.ds(s, n, stride=SUBLANES), :] = value[:, s * LANES:(s + 1) * LANES]


def _load_token_tiles(ref, n):
    return jnp.concatenate(
        [ref[pl.ds(s, n, stride=SUBLANES), :] for s in range(ROW_TILES)], axis=1)


def _token_rows(i):
    return pl.ds(pl.multiple_of(i * SUBLANES, SUBLANES), SUBLANES)


def _dot(a, b):
    return jnp.dot(a, b, preferred_element_type=F32)


def _dot_nt(a, b):
    return lax.dot_general(a, b, (((1,), (1,)), ((), ())), preferred_element_type=F32)


def _in_proj_kernel(x_ref, g_ref, w_ref, cos_a_ref, sin_a_ref, cos_b_ref, sin_b_ref,
                    qg_ref, kg_ref, seg_ref,
                    rq_ref, rk_ref, rv_ref, rg_ref, sq_ref, sk_ref, sv_ref):
    x = x_ref[...]
    h = x * lax.rsqrt(jnp.mean(x * x, axis=-1, keepdims=True) + EPS) * g_ref[...]
    p = _dot(h.astype(BF16), w_ref[...])
    k_scale = RET_HEAD_DIM ** -0.5
    cos_b = cos_b_ref[...]
    sin_b = sin_b_ref[...]
    first_half = lax.broadcasted_iota(jnp.int32, cos_b.shape, 1) < RET_HEAD_DIM // 2
    for r in range(x.shape[0] // LANES):
        rows = slice(r * LANES, (r + 1) * LANES)
        cos_a = cos_a_ref[r:r + 1, :]
        sin_a = sin_a_ref[r:r + 1, :]
        cos2 = cos_a * cos_b - sin_a * sin_b
        sin = sin_a * cos_b + cos_a * sin_b
        sin2 = jnp.where(first_half, -sin, sin)
        for hd in range(RET_HEADS):
            lo = hd * RET_HEAD_DIM
            q = p[rows, lo:lo + RET_HEAD_DIM]
            k = p[rows, RET_WIDTH + lo:RET_WIDTH + lo + RET_HEAD_DIM]
            q = q * cos2 + pltpu.roll(q, RET_HEAD_DIM // 2, axis=1) * sin2
            k = k * cos2 + pltpu.roll(k, RET_HEAD_DIM // 2, axis=1) * sin2
            rq_ref[rows, lo:lo + RET_HEAD_DIM] = q.astype(rq_ref.dtype)
            rk_ref[rows, lo:lo + RET_HEAD_DIM] = (k * k_scale).astype(rk_ref.dtype)
    rv_ref[...] = p[:, 2 * RET_WIDTH:3 * RET_WIDTH].astype(rv_ref.dtype)
    rg_ref[...] = p[:, 3 * RET_WIDTH:4 * RET_WIDTH]
    base = 4 * RET_WIDTH
    seg = seg_ref[...]

    def head_norm(v, gain):
        hi, lo = _split_bf16(v * v)
        group = seg.shape[0]
        ms = jnp.concatenate(
            [_dot(hi[:, g:g + group], seg) + _dot(lo[:, g:g + group], seg)
             for g in range(0, SB_WIDTH, group)], axis=1) * (1.0 / SB_HEAD_DIM)
        return v * lax.rsqrt(ms + EPS) * gain

    sq = head_norm(p[:, base:base + SB_WIDTH], qg_ref[...])
    sk = head_norm(p[:, base + SB_WIDTH:base + 2 * SB_WIDTH], kg_ref[...])
    sq_ref[...] = (sq * (SB_HEAD_DIM ** -0.5)).astype(sq_ref.dtype)
    sk_ref[...] = sk.astype(sk_ref.dtype)
    sv_ref[...] = p[:, base + 2 * SB_WIDTH:base + 3 * SB_WIDTH].astype(sv_ref.dtype)


def _rope_tables(seq, tm):
    half = RET_HEAD_DIM // 2
    per_tile = tm // LANES
    assert tm % LANES == 0 and per_tile <= SUBLANES and seq % tm == 0
    inv = ROPE_BASE ** (-np.arange(half, dtype=np.float64) / half)
    ang_a = (np.arange(seq // LANES, dtype=np.float64) * LANES)[:, None] * inv[None, :]
    ang_b = np.arange(LANES, dtype=np.float64)[:, None] * inv[None, :]
    dup = lambda a: np.concatenate([a, a], axis=-1)

    def coarse(a):
        a = dup(a).reshape(seq // tm, per_tile, RET_HEAD_DIM)
        return jnp.asarray(np.pad(a, ((0, 0), (0, SUBLANES - per_tile), (0, 0))), F32)

    return (coarse(np.cos(ang_a)), coarse(np.sin(ang_a)),
            jnp.asarray(dup(np.cos(ang_b)), F32), jnp.asarray(dup(np.sin(ang_b)), F32))


def _in_proj(x2, norm_g, w_in, sb_q_g, sb_k_g, seq, tm):
    t = x2.shape[0]
    assert seq % tm == 0 and t % tm == 0
    cos_a, sin_a, cos_b, sin_b = _rope_tables(seq, tm)
    seg = np.kron(np.eye(MXU_WIDTH // SB_HEAD_DIM), np.ones((SB_HEAD_DIM, SB_HEAD_DIM)))
    seg = jnp.asarray(seg, BF16)
    qg = jnp.tile(sb_q_g.astype(F32), SB_HEADS)[None, :]
    kg = jnp.tile(sb_k_g.astype(F32), SB_HEADS)[None, :]
    n_pos = seq // tm
    row = lambda w: pl.BlockSpec((tm, w), lambda i: (i, 0))
    full = lambda a: pl.BlockSpec(a.shape, lambda i: (0,) * a.ndim)
    pos = pl.BlockSpec((None, SUBLANES, RET_HEAD_DIM), lambda i: (i % n_pos, 0, 0))
    w_bf = w_in.astype(BF16)
    g2 = norm_g.astype(F32)[None, :]
    out = lambda dt: jax.ShapeDtypeStruct((t, RET_WIDTH), dt)
    return pl.pallas_call(
        _in_proj_kernel,
        grid=(t // tm,),
        in_specs=[row(D_MODEL), full(g2), full(w_bf), pos, pos, full(cos_b), full(sin_b),
                  full(qg), full(kg), full(seg)],
        out_specs=[row(RET_WIDTH)] * 7,
        out_shape=[out(BF16), out(F32), out(BF16), out(F32), out(BF16), out(BF16), out(BF16)],
        compiler_params=pltpu.CompilerParams(
            dimension_semantics=("arbitrary",), vmem_limit_bytes=VMEM_LIMIT),
        name="in_proj",
    )(x2, g2, w_bf, cos_a, sin_a, cos_b, sin_b, qg, kg, seg)


def _retention_kernel(q_ref, k_ref, v_ref, g_ref, ng_ref, intra_ref, qd_ref, kd_ref, cd_ref,
                      o_ref, state_ref, *, chunks):
    @pl.when(pl.program_id(1) == 0)
    def _():
        state_ref[...] = jnp.zeros_like(state_ref)

    def chunk(c, carry):
        rows = pl.ds(pl.multiple_of(c * RET_CHUNK, RET_CHUNK), RET_CHUNK)
        heads = range(RET_HEADS)
        cols = [slice(hd * RET_HEAD_DIM, (hd + 1) * RET_HEAD_DIM) for hd in heads]
        q = [q_ref[rows, cl] for cl in cols]
        k = [k_ref[rows, cl] for cl in cols]
        v = [v_ref[rows, cl] for cl in cols]
        state = [state_ref[hd] for hd in heads]
        scores = [_dot_nt(q[hd], k[hd].astype(BF16)) * intra_ref[hd] for hd in heads]
        cross = [_dot(q[hd], state[hd].astype(BF16)) * qd_ref[:, cols[hd]] for hd in heads]
        inner = [_dot(scores[hd].astype(BF16), v[hd]) for hd in heads]
        kd = [(k[hd] * kd_ref[:, cols[hd]]).astype(BF16) for hd in heads]
        for hd in heads:
            state_ref[hd] = state[hd] * cd_ref[:, cols[hd]] + _dot(kd[hd].T, v[hd])
        for hd in heads:
            o = inner[hd] + cross[hd]
            o = o * lax.rsqrt(jnp.mean(o * o, axis=-1, keepdims=True) + EPS) * ng_ref[:, cols[hd]]
            gate = g_ref[rows, cols[hd]]
            o_ref[rows, cols[hd]] = (o * (gate * jax.nn.sigmoid(gate))).astype(o_ref.dtype)
        return carry

    lax.fori_loop(0, chunks, chunk, 0, unroll=True)


def _retention(rq, rk, rv, rg, ret_norm_g, batch, seq, rows):
    assert seq % rows == 0 and rows % RET_CHUNK == 0
    log_gamma = np.log(1.0 - 2.0 ** (-5.0 - np.arange(RET_HEADS, dtype=np.float64)))
    idx = np.arange(RET_CHUNK, dtype=np.float64)
    diff = idx[:, None] - idx[None, :]
    intra = np.where(diff >= 0, np.exp(log_gamma[:, None, None] * np.maximum(diff, 0.0)), 0.0)
    q_decay = np.exp(log_gamma[:, None] * (idx + 1.0))
    k_decay = np.exp(log_gamma[:, None] * (RET_CHUNK - 1.0 - idx))
    chunk_decay = np.exp(log_gamma * RET_CHUNK)
    lane_rep = lambda a: np.repeat(a.T, RET_HEAD_DIM, axis=1)
    intra = jnp.asarray(intra, F32)
    qd = jnp.asarray(lane_rep(q_decay), F32)
    kd = jnp.asarray(lane_rep(k_decay), F32)
    cd = jnp.asarray(np.repeat(chunk_decay, RET_HEAD_DIM)[None, :], F32)
    ng = ret_norm_g.astype(F32).reshape(1, RET_WIDTH)
    n_r = seq // rows
    blk = pl.BlockSpec((rows, RET_WIDTH), lambda b, r: (b * n_r + r, 0))
    full = lambda a: pl.BlockSpec(a.shape, lambda b, r: (0,) * a.ndim)
    return pl.pallas_call(
        functools.partial(_retention_kernel, chunks=rows // RET_CHUNK),
        grid=(batch, n_r),
        in_specs=[blk, blk, blk, blk, full(ng), full(intra), full(qd), full(kd), full(cd)],
        out_specs=blk,
        out_shape=jax.ShapeDtypeStruct((batch * seq, RET_WIDTH), BF16),
        scratch_shapes=[pltpu.VMEM((RET_HEADS, RET_HEAD_DIM, RET_HEAD_DIM), F32)],
        compiler_params=pltpu.CompilerParams(
            dimension_semantics=("arbitrary", "arbitrary"), vmem_limit_bytes=VMEM_LIMIT),
        name="retention",
    )(rq, rk, rv, rg, ng, intra, qd, kd, cd)


def _sb_attn_kernel(q_ref, k_ref, v_ref, tri_ref, o_ref, *, blk, sub, heads):
    i = pl.program_id(2)
    nsub = blk // sub
    lane = lax.broadcasted_iota(jnp.int32, (sub, LANES), 1)
    first_head = lane < SB_HEAD_DIM
    tri = tri_ref[...]
    below_diag = (lax.broadcasted_iota(jnp.int32, (sub, sub), 1)
                  < lax.broadcasted_iota(jnp.int32, (sub, sub), 0))
    chains = [(s, h) for s in range(nsub) for h in range(heads)]
    cols = [slice((h // 2) * LANES, (h // 2 + 1) * LANES) for _, h in chains]


    def score_phase(qms, kbs, masks):
        zs = [_dot_nt(qm, kb) for qm, kb in zip(qms, kbs)]
        out = []
        for z, mask in zip(zs, masks):
            log_beta = jnp.minimum(z, 0.0) - jnp.log(1.0 + jnp.exp(-jnp.abs(z)))
            log_rest = log_beta - z
            if mask is not None:
                log_rest = jnp.where(mask, log_rest, 0.0)
            out.append((log_beta, log_rest))
        return out

    def later_phase(log_rests):
        parts = [_split_bf16(r) for r in log_rests]
        laters = [_dot(hi, tri) + _dot(lo, tri) for hi, lo in parts]
        return [(later, later[:, 0:1] + r[:, 0:1]) for later, r in zip(laters, log_rests)]

    def weight_phase(log_betas, log_remainings, masks):
        ws = [jnp.exp(b + r) for b, r in zip(log_betas, log_remainings)]
        ws = [w if m is None else jnp.where(m, w, 0.0) for w, m in zip(ws, masks)]
        return [w.astype(BF16) for w in ws]

    def queries():
        out = []
        for (s, h), c in zip(chains, cols):
            q = q_ref[s * sub:(s + 1) * sub, c]
            keep = first_head if h % 2 == 0 else jnp.logical_not(first_head)
            out.append(jnp.where(keep, q, jnp.zeros_like(q)))
        return out

    def key_rows(chunk):
        return pl.ds(pl.multiple_of(chunk * sub, sub), sub)

    def first_steps():
        qms = queries()
        own = [key_rows(i * nsub + s) for s, _ in chains]
        prev = [key_rows(jnp.maximum(i * nsub + s - 1, 0)) for s, _ in chains]
        prev_masks = [(i > 0) if s == 0 else None for s, _ in chains]
        sc_o = score_phase(qms, [k_ref[r, c] for r, c in zip(own, cols)], [below_diag] * len(chains))
        sc_p = score_phase(qms, [k_ref[r, c] for r, c in zip(prev, cols)], prev_masks)
        ls_o = later_phase([rest for _, rest in sc_o])
        ls_p = later_phase([rest for _, rest in sc_p])
        w_o = weight_phase([b for b, _ in sc_o], [later for later, _ in ls_o],
                           [below_diag] * len(chains))
        w_p = weight_phase([b for b, _ in sc_p],
                           [ls_o[n][1] + ls_p[n][0] for n in range(len(chains))], prev_masks)
        accs = [_dot(w, v_ref[r, c]) for w, r, c in zip(w_o, own, cols)]
        accs = [a + _dot(w, v_ref[r, c]) for a, w, r, c in zip(accs, w_p, prev, cols)]
        return tuple(ls_o[n][1] + ls_p[n][1] for n in range(len(chains))), tuple(accs)

    def cond(carry):
        j, cs, _ = carry
        c_max = functools.reduce(jnp.maximum, cs)
        return jnp.logical_and(j + (nsub - 1) >= 0, jnp.max(c_max) > F32_EXP_UNDERFLOW)

    def body(carry):
        j, cs, accs = carry
        rows = [key_rows(jnp.maximum(j + s, 0)) for s, _ in chains]
        masks = [None if s == nsub - 1 else (j + s >= 0) for s, _ in chains]
        sc = score_phase(queries(), [k_ref[r, c] for r, c in zip(rows, cols)], masks)
        ls = later_phase([rest for _, rest in sc])
        ws = weight_phase([b for b, _ in sc], [cs[n] + ls[n][0] for n in range(len(chains))], masks)
        new_accs = tuple(a + _dot(w, v_ref[r, c]) for a, w, r, c in zip(accs, ws, rows, cols))
        return j - 1, tuple(cs[n] + ls[n][1] for n in range(len(chains))), new_accs

    init = (i * nsub - 2,) + first_steps()
    _, _, accs = lax.while_loop(cond, body, init)
    for s in range(nsub):
        for p in range(heads // 2):
            pair = [accs[n] for n, (ss, h) in enumerate(chains) if ss == s and h // 2 == p]
            o_ref[s * sub:(s + 1) * sub, p * LANES:(p + 1) * LANES] = jnp.where(
                first_head, pair[0], pair[1]).astype(o_ref.dtype)


def _sb_attention(sq, sk, sv, batch, seq, blk, sub, heads):
    assert seq % blk == 0 and blk % sub == 0 and sub == LANES
    assert heads % 2 == 0 and SB_HEADS % heads == 0
    nq = seq // blk
    width = heads * SB_HEAD_DIM
    groups = SB_WIDTH // width
    j = np.arange(sub)
    tri = jnp.asarray((j[:, None] > j[None, :]).astype(np.float32), BF16)
    qblk = pl.BlockSpec((blk, width), lambda b, g, i: (b * nq + i, g))
    kvblk = pl.BlockSpec((seq, width), lambda b, g, i: (b, g))
    return pl.pallas_call(
        functools.partial(_sb_attn_kernel, blk=blk, sub=sub, heads=heads),
        grid=(batch, groups, nq),
        in_specs=[qblk, kvblk, kvblk, pl.BlockSpec(tri.shape, lambda b, g, i: (0, 0))],
        out_specs=qblk,
        out_shape=jax.ShapeDtypeStruct((batch * seq, SB_WIDTH), BF16),
        compiler_params=pltpu.CompilerParams(
            dimension_semantics=("arbitrary", "arbitrary", "arbitrary"),
            vmem_limit_bytes=VMEM_LIMIT),
        name="sb_attn",
    )(sq, sk, sv, tri)


OUT_PROJ_ROW_GROUPS = 4


def _out_proj_kernel(ret_ref, sb_ref, x_ref, wtop_ref, wbot_ref, g_ref, wrh_ref, wrl_ref,
                     br_ref, tri_ref,
                     x1_ref, xf_ref, idx_ref, w_ref, rank_ref, cnt_ref, run_ref):
    @pl.when(pl.program_id(0) == 0)
    def _():
        run_ref[...] = jnp.zeros_like(run_ref)

    rows = x_ref.shape[0] // OUT_PROJ_ROW_GROUPS
    groups = [slice(r * rows, (r + 1) * rows) for r in range(OUT_PROJ_ROW_GROUPS)]
    x1s = [x_ref[g, :] + _dot(ret_ref[g, :], wtop_ref[...]) + _dot(sb_ref[g, :], wbot_ref[...])
           for g in groups]
    for g, x1 in zip(groups, x1s):
        x1_ref[g, :] = x1
    xfs = [x1 * lax.rsqrt(jnp.mean(x1 * x1, axis=-1, keepdims=True) + EPS) * g_ref[...]
           for x1 in x1s]
    for r, xf in enumerate(xfs):
        _store_token_tiles(xf_ref.at[pl.ds(r * rows * SUBLANES, rows * SUBLANES), :], xf)
    parts = [_split_bf16(xf) for xf in xfs]
    wrh = wrh_ref[...]
    wrl = wrl_ref[...]
    logits = jnp.concatenate(
        [_dot_nt(wrh, xh) + _dot_nt(wrh, xl) + _dot_nt(wrl, xh) for xh, xl in parts],
        axis=1) + br_ref[...]
    n_e, tm = logits.shape
    e_iota = lax.broadcasted_iota(jnp.int32, (n_e, tm), 0)
    cur = logits
    tops, sels, hots = [], [], []
    for _ in range(TOP_K):
        m = jnp.max(cur, axis=0, keepdims=True)
        sel = jnp.min(jnp.where(cur == m, e_iota, n_e), axis=0, keepdims=True)
        hot = e_iota == sel
        cur = jnp.where(hot, -jnp.inf, cur)
        tops.append(m)
        sels.append(sel)
        hots.append(hot)
    ps = [jnp.exp(m - tops[0]) for m in tops]
    denom = ps[0] + ps[1] + ps[2] + ps[3]
    chosen = jnp.zeros((n_e, tm), F32)
    for hot in hots:
        chosen = chosen + hot.astype(F32)
    before = run_ref[:, 0:1] + _dot(chosen.astype(BF16), tri_ref[...])
    for kk in range(TOP_K):
        idx_ref[kk:kk + 1, :] = sels[kk]
        w_ref[kk:kk + 1, :] = ps[kk] / denom
        rank = jnp.sum(jnp.where(hots[kk], before, 0.0), axis=0, keepdims=True)
        rank_ref[kk:kk + 1, :] = rank.astype(jnp.int32)
    run = run_ref[...] + jnp.sum(chosen, axis=1, keepdims=True)
    run_ref[...] = run
    cnt_ref[...] = run.astype(jnp.int32)


def _out_proj_route(ret, sb, x2, w_out, ffn_g, w_router, b_router, tm):
    t = x2.shape[0]
    assert t % tm == 0
    w_bf = w_out.astype(BF16)
    wtop, wbot = w_bf[:RET_WIDTH], w_bf[RET_WIDTH:]
    g2 = ffn_g.astype(F32)[None, :]
    wr_t = w_router.astype(F32).T
    wrh = wr_t.astype(BF16)
    wrl = (wr_t - wrh.astype(F32)).astype(BF16)
    br = b_router.astype(F32)[:, None]
    tt = np.arange(tm)
    tri = jnp.asarray((tt[:, None] < tt[None, :]).astype(np.float32), BF16)
    row = lambda w: pl.BlockSpec((tm, w), lambda i: (i, 0))
    full = lambda a: pl.BlockSpec(a.shape, lambda i: (0,) * a.ndim)
    col = pl.BlockSpec((TOP_K, tm), lambda i: (0, i))
    return pl.pallas_call(
        _out_proj_kernel,
        grid=(t // tm,),
        in_specs=[row(RET_WIDTH), row(SB_WIDTH), row(D_MODEL), full(wtop), full(wbot), full(g2),
                  full(wrh), full(wrl), full(br), full(tri)],
        out_specs=[row(D_MODEL), pl.BlockSpec((tm * SUBLANES, LANES), lambda i: (i, 0)),
                   col, col, col, pl.BlockSpec((N_EXPERTS, LANES), lambda i: (0, 0))],
        out_shape=[jax.ShapeDtypeStruct((t, D_MODEL), F32),
                   jax.ShapeDtypeStruct((t * SUBLANES, LANES), F32),
                   jax.ShapeDtypeStruct((TOP_K, t), jnp.int32),
                   jax.ShapeDtypeStruct((TOP_K, t), F32),
                   jax.ShapeDtypeStruct((TOP_K, t), jnp.int32),
                   jax.ShapeDtypeStruct((N_EXPERTS, LANES), jnp.int32)],
        scratch_shapes=[pltpu.VMEM((N_EXPERTS, LANES), F32)],
        compiler_params=pltpu.CompilerParams(
            dimension_semantics=("arbitrary",), vmem_limit_bytes=VMEM_LIMIT),
        name="out_proj_route",
    )(ret, sb, x2, wtop, wbot, g2, wrh, wrl, br, tri)


def _dest_kernel(pstart_ref, idx_ref, rank_ref, dest_ref):
    idx = idx_ref[...]
    dest = rank_ref[...]
    for e in range(N_EXPERTS):
        dest = dest + jnp.where(idx == e, pstart_ref[e], 0)
    dest_ref[...] = dest


def _dest(padded_start, top_idx, rank, tn):
    t = top_idx.shape[1]
    assert t % tn == 0
    col = pl.BlockSpec((TOP_K, tn), lambda i, ps: (0, i))
    return pl.pallas_call(
        _dest_kernel,
        grid_spec=pltpu.PrefetchScalarGridSpec(
            num_scalar_prefetch=1, grid=(t // tn,), in_specs=[col, col], out_specs=col),
        out_shape=jax.ShapeDtypeStruct((TOP_K, t), jnp.int32),
        name="dest",
    )(padded_start, top_idx, rank)


def _dispatch_kernel(pend_ref, dest_ref, xf_ref, slots_hbm, zbuf, sem, zsem, *, tm, blk):
    @pl.when(pl.program_id(0) == 0)
    def _():
        zbuf[...] = jnp.zeros_like(zbuf)

        def tail_copy(e):
            first = pl.multiple_of((pend_ref[e] - blk) * SUBLANES, blk * SUBLANES)
            return pltpu.make_async_copy(
                zbuf, slots_hbm.at[pl.ds(first, blk * SUBLANES), :], zsem)

        def nonempty(e):
            return pend_ref[e] > (pend_ref[e - 1] if e else 0)

        for e in range(N_EXPERTS):
            pl.when(nonempty(e))(lambda e=e: tail_copy(e).start())
        for e in range(N_EXPERTS):
            pl.when(nonempty(e))(lambda e=e: tail_copy(e).wait())

    def row_copy(t, kk):
        return pltpu.make_async_copy(
            xf_ref.at[_token_rows(t), :],
            slots_hbm.at[_token_rows(dest_ref[kk, t]), :], sem)

    def start(t, carry):
        for kk in range(TOP_K):
            row_copy(t, kk).start(priority=kk % 2)
        return carry

    lax.fori_loop(0, tm, start, 0, unroll=8)
    for kk in range(TOP_K):
        pltpu.make_async_copy(
            xf_ref, slots_hbm.at[pl.ds(0, tm * SUBLANES), :], sem).wait()


def _dispatch(padded_end, dest, xf, n_pad, tm, blk):
    t = xf.shape[0] // SUBLANES
    assert t % tm == 0
    return pl.pallas_call(
        functools.partial(_dispatch_kernel, tm=tm, blk=blk),
        grid_spec=pltpu.PrefetchScalarGridSpec(
            num_scalar_prefetch=1,
            grid=(t // tm,),
            in_specs=[pl.BlockSpec((TOP_K, tm), lambda i, pe: (0, i), memory_space=pltpu.SMEM),
                      pl.BlockSpec((tm * SUBLANES, LANES), lambda i, pe: (i, 0))],
            out_specs=pl.BlockSpec(memory_space=pl.ANY),
            scratch_shapes=[pltpu.VMEM((blk * SUBLANES, LANES), F32),
                            pltpu.SemaphoreType.DMA, pltpu.SemaphoreType.DMA],
        ),
        out_shape=jax.ShapeDtypeStruct((n_pad * SUBLANES, LANES), F32),
        compiler_params=pltpu.CompilerParams(
            dimension_semantics=("arbitrary",), vmem_limit_bytes=VMEM_LIMIT,
            has_side_effects=True),
        name="dispatch",
    )(padded_end, dest, xf)


WEIGHT_ROWS = 128


def _experts_kernel(be_ref, nu_ref, run_ref, next_ref, x_ref, wgu_hbm, wd_hbm, bg_ref, bu_ref,
                    bd_ref, y_ref, wgu_buf, wd_buf, wg_s, wu_s, wd_s, t_s, sem, *, blk):
    j = pl.program_id(0)
    used = j < nu_ref[0]
    expert = be_ref[j]
    new_expert = jnp.logical_or(j == 0, expert != be_ref[jnp.maximum(j - 1, 0)])

    def weight_copies(e, slot):
        return (pltpu.make_async_copy(wgu_hbm.at[e], wgu_buf.at[slot], sem.at[0, slot]),
                pltpu.make_async_copy(wd_hbm.at[e], wd_buf.at[slot], sem.at[1, slot]))

    def expert_ffn_tail(gate, up):
        gate = jnp.minimum(gate, SWIGLU_LIMIT)
        up = jnp.clip(up, -SWIGLU_LIMIT, SWIGLU_LIMIT)
        hidden = (up + 1.0) * gate * jax.nn.sigmoid(SWIGLU_ALPHA * gate)
        _store_token_tiles(y_ref, _dot(hidden.astype(BF16), wd_s[...]) + bd_ref[...])

    @pl.when(jnp.logical_and(used, new_expert))
    def _():
        slot = lax.rem(run_ref[expert], 2)

        @pl.when(j == 0)
        def _():
            for c in weight_copies(expert, slot):
                c.start()

        for c in weight_copies(expert, slot):
            c.wait()
        following = next_ref[expert]

        @pl.when(following >= 0)
        def _():
            for c in weight_copies(following, 1 - slot):
                c.start()

        xb = _load_token_tiles(x_ref, blk).astype(BF16)
        gate = bg_ref[...]
        up = bu_ref[...]
        for r in range(0, D_MODEL, MXU_WIDTH):
            wg_rows, wu_rows = [], []
            for rr in range(r, r + MXU_WIDTH, WEIGHT_ROWS):
                rows = slice(rr, rr + WEIGHT_ROWS)
                t_s[...] = wgu_buf[slot, rows, :].T
                wg_rows.append(t_s[pl.ds(0, D_FF, stride=2), :].T.astype(BF16))
                wu_rows.append(t_s[pl.ds(1, D_FF, stride=2), :].T.astype(BF16))
                wg_s[rows, :] = wg_rows[-1]
                wu_s[rows, :] = wu_rows[-1]
                wd_s[rows, :] = wd_buf[slot, rows, :].astype(BF16)
            x_cols = xb[:, r:r + MXU_WIDTH]
            gate = gate + _dot(x_cols, jnp.concatenate(wg_rows, axis=0))
            up = up + _dot(x_cols, jnp.concatenate(wu_rows, axis=0))
        expert_ffn_tail(gate, up)

    @pl.when(jnp.logical_and(used, jnp.logical_not(new_expert)))
    def _():
        xb = _load_token_tiles(x_ref, blk).astype(BF16)
        expert_ffn_tail(_dot(xb, wg_s[...]) + bg_ref[...], _dot(xb, wu_s[...]) + bu_ref[...])

    @pl.when(jnp.logical_not(used))
    def _():
        y_ref[...] = jnp.zeros_like(y_ref)


def _experts(block_expert, n_used, expert_run, next_expert, slots, wgu, wd, bg, bu, bd, blk):
    assert D_FF == D_MODEL
    n_pad = slots.shape[0] // SUBLANES
    n_blocks = n_pad // blk
    xmap = lambda j, be, nu, run, nxt: (jnp.minimum(j, nu[0] - 1), 0)
    wmap = lambda j, be, nu, run, nxt: (be[j], 0, 0)
    bspec = pl.BlockSpec((None, 1, D_FF), wmap)
    tiles = (blk * SUBLANES, LANES)
    hbm = pl.BlockSpec(memory_space=pl.ANY)
    return pl.pallas_call(
        functools.partial(_experts_kernel, blk=blk),
        grid_spec=pltpu.PrefetchScalarGridSpec(
            num_scalar_prefetch=4,
            grid=(n_blocks,),
            in_specs=[pl.BlockSpec(tiles, xmap), hbm, hbm, bspec, bspec, bspec],
            out_specs=pl.BlockSpec(tiles, lambda j, be, nu, run, nxt: (j, 0)),
            scratch_shapes=[pltpu.VMEM((2, D_MODEL, 2 * D_FF), F32),
                            pltpu.VMEM((2, D_FF, D_MODEL), F32),
                            pltpu.VMEM((D_MODEL, D_FF), BF16), pltpu.VMEM((D_MODEL, D_FF), BF16),
                            pltpu.VMEM((D_FF, D_MODEL), BF16),
                            pltpu.VMEM((2 * D_FF, WEIGHT_ROWS), F32),
                            pltpu.SemaphoreType.DMA((2, 2))],
        ),
        out_shape=jax.ShapeDtypeStruct((n_pad * SUBLANES, LANES), F32),
        compiler_params=pltpu.CompilerParams(
            dimension_semantics=("arbitrary",), vmem_limit_bytes=VMEM_LIMIT),
        name="experts",
    )(block_expert, n_used, expert_run, next_expert, slots, wgu, wd, bg, bu, bd)


def _combine_kernel(dest_ref, next_dest_ref, y_hbm, w_ref, x1_ref, o_ref, buf, sem, *, tm):
    i = pl.program_id(0)
    n = pl.num_programs(0)
    slot = lax.rem(i, 2)

    def gather(dests, into):
        def start(t, carry):
            for kk in range(TOP_K):
                pltpu.make_async_copy(
                    y_hbm.at[_token_rows(dests[kk, t]), :],
                    buf.at[into, kk, _token_rows(t), :], sem.at[into]).start(priority=kk % 2)
            return carry

        lax.fori_loop(0, tm, start, 0, unroll=8)

    pl.when(i == 0)(lambda: gather(dest_ref, 0))
    pl.when(i + 1 < n)(lambda: gather(next_dest_ref, 1 - slot))
    for kk in range(TOP_K):
        pltpu.make_async_copy(
            y_hbm.at[pl.ds(0, tm * SUBLANES), :], buf.at[slot, kk], sem.at[slot]).wait()
    w = w_ref[...]
    for s in range(ROW_TILES):
        cols = slice(s * LANES, (s + 1) * LANES)
        acc = x1_ref[:, cols]
        for kk in range(TOP_K):
            acc = acc + buf[slot, kk, pl.ds(s, tm, stride=SUBLANES), :] * w[:, kk:kk + 1]
        o_ref[:, cols] = acc


def _combine(dest, y_slots, w_t, x1, tm):
    t = x1.shape[0]
    assert t % tm == 0
    n = t // tm
    row = pl.BlockSpec((tm, D_MODEL), lambda i: (i, 0))
    return pl.pallas_call(
        functools.partial(_combine_kernel, tm=tm),
        grid=(n,),
        in_specs=[pl.BlockSpec((TOP_K, tm), lambda i: (0, i), memory_space=pltpu.SMEM),
                  pl.BlockSpec((TOP_K, tm), lambda i: (0, jnp.minimum(i + 1, n - 1)),
                               memory_space=pltpu.SMEM),
                  pl.BlockSpec(memory_space=pl.ANY),
                  pl.BlockSpec((tm, TOP_K), lambda i: (i, 0)),
                  row],
        out_specs=row,
        out_shape=jax.ShapeDtypeStruct((t, D_MODEL), F32),
        scratch_shapes=[pltpu.VMEM((2, TOP_K, tm * SUBLANES, LANES), F32),
                        pltpu.SemaphoreType.DMA((2,))],
        compiler_params=pltpu.CompilerParams(
            dimension_semantics=("arbitrary",), vmem_limit_bytes=VMEM_LIMIT),
        name="combine",
    )(dest, dest, y_slots, w_t, x1)


def _tiles(batch, seq):
    return dict(
        in_proj=min(512, seq),
        retention=min(1024, seq),
        sb_block=min(512, seq),
        sb_heads=4,
        out_proj=min(1024, seq),
        dest=min(8192, batch * seq),
        dispatch=min(2048, seq),
        combine=min(512, seq),
        expert_block=512,
    )


def _layer(x, attn_norm_g, w_in, ret_norm_g, sb_q_norm_g, sb_k_norm_g, w_out,
           ffn_norm_g, w_router, b_router, w_gate_up, b_gate_up, w_down, b_down):
    batch, seq, d = x.shape
    t = batch * seq
    tiles = _tiles(batch, seq)
    x2 = x.reshape(t, d)

    rq, rk, rv, rg, sq, sk, sv = _in_proj(
        x2, attn_norm_g, w_in, sb_q_norm_g, sb_k_norm_g, seq, tiles["in_proj"])
    ret = _retention(rq, rk, rv, rg, ret_norm_g, batch, seq, tiles["retention"])
    sb = _sb_attention(sq, sk, sv, batch, seq, tiles["sb_block"], LANES, tiles["sb_heads"])
    x1, xf, top_idx, top_w, rank, counts = _out_proj_route(
        ret, sb, x2, w_out, ffn_norm_g, w_router, b_router, tiles["out_proj"])

    blk = tiles["expert_block"]
    n_pad = t * TOP_K + N_EXPERTS * blk
    n_blocks = n_pad // blk
    counts = counts[:, 0]
    padded = (counts + blk - 1) // blk * blk
    padded_end = jnp.cumsum(padded)
    padded_start = padded_end - padded
    block_first = jnp.arange(n_blocks, dtype=jnp.int32) * blk
    block_expert = jnp.minimum(
        jnp.sum(padded_end[None, :] <= block_first[:, None], axis=1), N_EXPERTS - 1
    ).astype(jnp.int32)
    n_used = (padded_end[-1:] // blk).astype(jnp.int32)
    nonempty = counts > 0
    expert_run = (jnp.cumsum(nonempty) - nonempty).astype(jnp.int32)
    experts = jnp.arange(N_EXPERTS, dtype=jnp.int32)
    later = jnp.logical_and(nonempty[None, :], experts[None, :] > experts[:, None])
    next_expert = jnp.min(jnp.where(later, experts[None, :], N_EXPERTS), axis=1)
    next_expert = jnp.where(next_expert < N_EXPERTS, next_expert, -1).astype(jnp.int32)

    dest = _dest(padded_start, top_idx, rank, tiles["dest"])
    slots = _dispatch(padded_end, dest, xf, n_pad, tiles["dispatch"], blk)

    bgu = b_gate_up.astype(F32).reshape(N_EXPERTS, 1, D_FF, 2)
    y_slots = _experts(block_expert, n_used, expert_run, next_expert, slots,
                       w_gate_up.astype(F32), w_down.astype(F32),
                       bgu[..., 0], bgu[..., 1], b_down.astype(F32)[:, None, :], blk)

    out = _combine(dest, y_slots, top_w.T, x1, tiles["combine"])
    return out.reshape(batch, seq, d)


def kernel(x, attn_norm_g, w_in, ret_norm_g, sb_q_norm_g, sb_k_norm_g, w_out, ffn_norm_g,
           w_router, b_router, w_gate_up, b_gate_up, w_down, b_down):
    depth = attn_norm_g.shape[0]
    for l in range(depth):
        x = _layer(x, attn_norm_g[l], w_in[l], ret_norm_g[l], sb_q_norm_g[l], sb_k_norm_g[l],
                   w_out[l], ffn_norm_g[l], w_router[l], b_router[l], w_gate_up[l],
                   b_gate_up[l], w_down[l], b_down[l])
    return x
```

```python
import functools

import numpy as np
import jax
import jax.numpy as jnp
from jax import lax
from jax.experimental import pallas as pl
from jax.experimental.pallas import tpu as pltpu

D_MODEL = 1024
RET_WIDTH = 512
RET_HEADS = 4
RET_HEAD_DIM = 128
SB_WIDTH = 512
SB_HEADS = 8
SB_HEAD_DIM = 64
IN_PROJ_WIDTH = 4 * RET_WIDTH + 3 * SB_WIDTH
RET_CHUNK = 128
ROPE_BASE = 10000.0
N_EXPERTS = 32
TOP_K = 4
D_FF = D_MODEL
SWIGLU_LIMIT = 7.0
SWIGLU_ALPHA = 1.702
EPS = 1e-6

LANES = 128
MXU_WIDTH = 256
F32_EXP_UNDERFLOW = -88.0
VMEM_LIMIT = 56 * 1024 * 1024

BF16 = jnp.bfloat16
F32 = jnp.float32


def _split_bf16(v):
    hi = v.astype(BF16)
    lo = (v - hi.astype(F32)).astype(BF16)
    return hi, lo


SUBLANES = 8
ROW_TILES = D_MODEL // LANES
assert ROW_TILES == SUBLANES


def _store_token_tiles(ref, value):
    n = value.shape[0]
    for s in range(ROW_TILES):
        ref[pl.ds(s, n, stride=SUBLANES), :] = value[:, s * LANES:(s + 1) * LANES]


def _load_token_tiles(ref, n):
    return jnp.concatenate(
        [ref[pl.ds(s, n, stride=SUBLANES), :] for s in range(ROW_TILES)], axis=1)


def _token_rows(i):
    return pl.ds(pl.multiple_of(i * SUBLANES, SUBLANES), SUBLANES)


def _dot(a, b):
    return jnp.dot(a, b, preferred_element_type=F32)


def _dot_nt(a, b):
    return lax.dot_general(a, b, (((1,), (1,)), ((), ())), preferred_element_type=F32)


def _in_proj_kernel(x_ref, g_ref, w_ref, cos_a_ref, sin_a_ref, cos_b_ref, sin_b_ref,
                    qg_ref, kg_ref, seg_ref,
                    rq_ref, rk_ref, rv_ref, rg_ref, sq_ref, sk_ref, sv_ref):
    x = x_ref[...]
    h = x * lax.rsqrt(jnp.mean(x * x, axis=-1, keepdims=True) + EPS) * g_ref[...]
    p = _dot(h.astype(BF16), w_ref[...])
    k_scale = RET_HEAD_DIM ** -0.5
    cos_b = cos_b_ref[...]
    sin_b = sin_b_ref[...]
    first_half = lax.broadcasted_iota(jnp.int32, cos_b.shape, 1) < RET_HEAD_DIM // 2
    for r in range(x.shape[0] // LANES):
        rows = slice(r * LANES, (r + 1) * LANES)
        cos_a = cos_a_ref[r:r + 1, :]
        sin_a = sin_a_ref[r:r + 1, :]
        cos2 = cos_a * cos_b - sin_a * sin_b
        sin = sin_a * cos_b + cos_a * sin_b
        sin2 = jnp.where(first_half, -sin, sin)
        for hd in range(RET_HEADS):
            lo = hd * RET_HEAD_DIM
            q = p[rows, lo:lo + RET_HEAD_DIM]
            k = p[rows, RET_WIDTH + lo:RET_WIDTH + lo + RET_HEAD_DIM]
            q = q * cos2 + pltpu.roll(q, RET_HEAD_DIM // 2, axis=1) * sin2
            k = k * cos2 + pltpu.roll(k, RET_HEAD_DIM // 2, axis=1) * sin2
            rq_ref[rows, lo:lo + RET_HEAD_DIM] = q.astype(rq_ref.dtype)
            rk_ref[rows, lo:lo + RET_HEAD_DIM] = (k * k_scale).astype(rk_ref.dtype)
    rv_ref[...] = p[:, 2 * RET_WIDTH:3 * RET_WIDTH].astype(rv_ref.dtype)
    rg_ref[...] = p[:, 3 * RET_WIDTH:4 * RET_WIDTH]
    base = 4 * RET_WIDTH
    seg = seg_ref[...]

    def head_norm(v, gain):
        hi, lo = _split_bf16(v * v)
        group = seg.shape[0]
        ms = jnp.concatenate(
            [_dot(hi[:, g:g + group], seg) + _dot(lo[:, g:g + group], seg)
             for g in range(0, SB_WIDTH, group)], axis=1) * (1.0 / SB_HEAD_DIM)
        return v * lax.rsqrt(ms + EPS) * gain

    sq = head_norm(p[:, base:base + SB_WIDTH], qg_ref[...])
    sk = head_norm(p[:, base + SB_WIDTH:base + 2 * SB_WIDTH], kg_ref[...])
    sq_ref[...] = (sq * (SB_HEAD_DIM ** -0.5)).astype(sq_ref.dtype)
    sk_ref[...] = sk.astype(sk_ref.dtype)
    sv_ref[...] = p[:, base + 2 * SB_WIDTH:base + 3 * SB_WIDTH].astype(sv_ref.dtype)


def _rope_tables(seq, tm):
    half = RET_HEAD_DIM // 2
    per_tile = tm // LANES
    assert tm % LANES == 0 and per_tile <= SUBLANES and seq % tm == 0
    inv = ROPE_BASE ** (-np.arange(half, dtype=np.float64) / half)
    ang_a = (np.arange(seq // LANES, dtype=np.float64) * LANES)[:, None] * inv[None, :]
    ang_b = np.arange(LANES, dtype=np.float64)[:, None] * inv[None, :]
    dup = lambda a: np.concatenate([a, a], axis=-1)

    def coarse(a):
        a = dup(a).reshape(seq // tm, per_tile, RET_HEAD_DIM)
        return jnp.asarray(np.pad(a, ((0, 0), (0, SUBLANES - per_tile), (0, 0))), F32)

    return (coarse(np.cos(ang_a)), coarse(np.sin(ang_a)),
            jnp.asarray(dup(np.cos(ang_b)), F32), jnp.asarray(dup(np.sin(ang_b)), F32))


def _in_proj(x2, norm_g, w_in, sb_q_g, sb_k_g, seq, tm):
    t = x2.shape[0]
    assert seq % tm == 0 and t % tm == 0
    cos_a, sin_a, cos_b, sin_b = _rope_tables(seq, tm)
    seg = np.kron(np.eye(MXU_WIDTH // SB_HEAD_DIM), np.ones((SB_HEAD_DIM, SB_HEAD_DIM)))
    seg = jnp.asarray(seg, BF16)
    qg = jnp.tile(sb_q_g.astype(F32), SB_HEADS)[None, :]
    kg = jnp.tile(sb_k_g.astype(F32), SB_HEADS)[None, :]
    n_pos = seq // tm
    row = lambda w: pl.BlockSpec((tm, w), lambda i: (i, 0))
    full = lambda a: pl.BlockSpec(a.shape, lambda i: (0,) * a.ndim)
    pos = pl.BlockSpec((None, SUBLANES, RET_HEAD_DIM), lambda i: (i % n_pos, 0, 0))
    w_bf = w_in.astype(BF16)
    g2 = norm_g.astype(F32)[None, :]
    out = lambda dt: jax.ShapeDtypeStruct((t, RET_WIDTH), dt)
    return pl.pallas_call(
        _in_proj_kernel,
        grid=(t // tm,),
        in_specs=[row(D_MODEL), full(g2), full(w_bf), pos, pos, full(cos_b), full(sin_b),
                  full(qg), full(kg), full(seg)],
        out_specs=[row(RET_WIDTH)] * 7,
        out_shape=[out(BF16), out(F32), out(BF16), out(F32), out(BF16), out(BF16), out(BF16)],
        compiler_params=pltpu.CompilerParams(
            dimension_semantics=("arbitrary",), vmem_limit_bytes=VMEM_LIMIT),
        name="in_proj",
    )(x2, g2, w_bf, cos_a, sin_a, cos_b, sin_b, qg, kg, seg)


def _retention_kernel(q_ref, k_ref, v_ref, g_ref, ng_ref, intra_ref, qd_ref, kd_ref, cd_ref,
                      o_ref, state_ref, *, chunks):
    @pl.when(pl.program_id(1) == 0)
    def _():
        state_ref[...] = jnp.zeros_like(state_ref)

    def chunk(c, carry):
        rows = pl.ds(pl.multiple_of(c * RET_CHUNK, RET_CHUNK), RET_CHUNK)
        heads = range(RET_HEADS)
        cols = [slice(hd * RET_HEAD_DIM, (hd + 1) * RET_HEAD_DIM) for hd in heads]
        q = [q_ref[rows, cl] for cl in cols]
        k = [k_ref[rows, cl] for cl in cols]
        v = [v_ref[rows, cl] for cl in cols]
        state = [state_ref[hd] for hd in heads]
        scores = [_dot_nt(q[hd], k[hd].astype(BF16)) * intra_ref[hd] for hd in heads]
        cross = [_dot(q[hd], state[hd].astype(BF16)) * qd_ref[:, cols[hd]] for hd in heads]
        inner = [_dot(scores[hd].astype(BF16), v[hd]) for hd in heads]
        kd = [(k[hd] * kd_ref[:, cols[hd]]).astype(BF16) for hd in heads]
        for hd in heads:
            state_ref[hd] = state[hd] * cd_ref[:, cols[hd]] + _dot(kd[hd].T, v[hd])
        for hd in heads:
            o = inner[hd] + cross[hd]
            o = o * lax.rsqrt(jnp.mean(o * o, axis=-1, keepdims=True) + EPS) * ng_ref[:, cols[hd]]
            gate = g_ref[rows, cols[hd]]
            o_ref[rows, cols[hd]] = (o * (gate * jax.nn.sigmoid(gate))).astype(o_ref.dtype)
        return carry

    lax.fori_loop(0, chunks, chunk, 0, unroll=True)


def _retention(rq, rk, rv, rg, ret_norm_g, batch, seq, rows):
    assert seq % rows == 0 and rows % RET_CHUNK == 0
    log_gamma = np.log(1.0 - 2.0 ** (-5.0 - np.arange(RET_HEADS, dtype=np.float64)))
    idx = np.arange(RET_CHUNK, dtype=np.float64)
    diff = idx[:, None] - idx[None, :]
    intra = np.where(diff >= 0, np.exp(log_gamma[:, None, None] * np.maximum(diff, 0.0)), 0.0)
    q_decay = np.exp(log_gamma[:, None] * (idx + 1.0))
    k_decay = np.exp(log_gamma[:, None] * (RET_CHUNK - 1.0 - idx))
    chunk_decay = np.exp(log_gamma * RET_CHUNK)
    lane_rep = lambda a: np.repeat(a.T, RET_HEAD_DIM, axis=1)
    intra = jnp.asarray(intra, F32)
    qd = jnp.asarray(lane_rep(q_decay), F32)
    kd = jnp.asarray(lane_rep(k_decay), F32)
    cd = jnp.asarray(np.repeat(chunk_decay, RET_HEAD_DIM)[None, :], F32)
    ng = ret_norm_g.astype(F32).reshape(1, RET_WIDTH)
    n_r = seq // rows
    blk = pl.BlockSpec((rows, RET_WIDTH), lambda b, r: (b * n_r + r, 0))
    full = lambda a: pl.BlockSpec(a.shape, lambda b, r: (0,) * a.ndim)
    return pl.pallas_call(
        functools.partial(_retention_kernel, chunks=rows // RET_CHUNK),
        grid=(batch, n_r),
        in_specs=[blk, blk, blk, blk, full(ng), full(intra), full(qd), full(kd), full(cd)],
        out_specs=blk,
        out_shape=jax.ShapeDtypeStruct((batch * seq, RET_WIDTH), BF16),
        scratch_shapes=[pltpu.VMEM((RET_HEADS, RET_HEAD_DIM, RET_HEAD_DIM), F32)],
        compiler_params=pltpu.CompilerParams(
            dimension_semantics=("arbitrary", "arbitrary"), vmem_limit_bytes=VMEM_LIMIT),
        name="retention",
    )(rq, rk, rv, rg, ng, intra, qd, kd, cd)


def _sb_attn_kernel(q_ref, k_ref, v_ref, tri_ref, o_ref, *, blk, sub, heads):
    i = pl.program_id(2)
    nsub = blk // sub
    lane = lax.broadcasted_iota(jnp.int32, (sub, LANES), 1)
    first_head = lane < SB_HEAD_DIM
    tri = tri_ref[...]
    below_diag = (lax.broadcasted_iota(jnp.int32, (sub, sub), 1)
                  < lax.broadcasted_iota(jnp.int32, (sub, sub), 0))
    chains = [(s, h) for s in range(nsub) for h in range(heads)]
    cols = [slice((h // 2) * LANES, (h // 2 + 1) * LANES) for _, h in chains]


    def score_phase(qms, kbs, masks):
        zs = [_dot_nt(qm, kb) for qm, kb in zip(qms, kbs)]
        out = []
        for z, mask in zip(zs, masks):
            log_beta = jnp.minimum(z, 0.0) - jnp.log(1.0 + jnp.exp(-jnp.abs(z)))
            log_rest = log_beta - z
            if mask is not None:
                log_rest = jnp.where(mask, log_rest, 0.0)
            out.append((log_beta, log_rest))
        return out

    def later_phase(log_rests):
        parts = [_split_bf16(r) for r in log_rests]
        laters = [_dot(hi, tri) + _dot(lo, tri) for hi, lo in parts]
        return [(later, later[:, 0:1] + r[:, 0:1]) for later, r in zip(laters, log_rests)]

    def weight_phase(log_betas, log_remainings, masks):
        ws = [jnp.exp(b + r) for b, r in zip(log_betas, log_remainings)]
        ws = [w if m is None else jnp.where(m, w, 0.0) for w, m in zip(ws, masks)]
        return [w.astype(BF16) for w in ws]

    def queries():
        out = []
        for (s, h), c in zip(chains, cols):
            q = q_ref[s * sub:(s + 1) * sub, c]
            keep = first_head if h % 2 == 0 else jnp.logical_not(first_head)
            out.append(jnp.where(keep, q, jnp.zeros_like(q)))
        return out

    def key_rows(chunk):
        return pl.ds(pl.multiple_of(chunk * sub, sub), sub)

    def first_steps():
        qms = queries()
        own = [key_rows(i * nsub + s) for s, _ in chains]
        prev = [key_rows(jnp.maximum(i * nsub + s - 1, 0)) for s, _ in chains]
        prev_masks = [(i > 0) if s == 0 else None for s, _ in chains]
        sc_o = score_phase(qms, [k_ref[r, c] for r, c in zip(own, cols)], [below_diag] * len(chains))
        sc_p = score_phase(qms, [k_ref[r, c] for r, c in zip(prev, cols)], prev_masks)
        ls_o = later_phase([rest for _, rest in sc_o])
        ls_p = later_phase([rest for _, rest in sc_p])
        w_o = weight_phase([b for b, _ in sc_o], [later for later, _ in ls_o],
                           [below_diag] * len(chains))
        w_p = weight_phase([b for b, _ in sc_p],
                           [ls_o[n][1] + ls_p[n][0] for n in range(len(chains))], prev_masks)
        accs = [_dot(w, v_ref[r, c]) for w, r, c in zip(w_o, own, cols)]
        accs = [a + _dot(w, v_ref[r, c]) for a, w, r, c in zip(accs, w_p, prev, cols)]
        return tuple(ls_o[n][1] + ls_p[n][1] for n in range(len(chains))), tuple(accs)

    def cond(carry):
        j, cs, _ = carry
        c_max = functools.reduce(jnp.maximum, cs)
        return jnp.logical_and(j + (nsub - 1) >= 0, jnp.max(c_max) > F32_EXP_UNDERFLOW)

    def body(carry):
        j, cs, accs = carry
        rows = [key_rows(jnp.maximum(j + s, 0)) for s, _ in chains]
        masks = [None if s == nsub - 1 else (j + s >= 0) for s, _ in chains]
        sc = score_phase(queries(), [k_ref[r, c] for r, c in zip(rows, cols)], masks)
        ls = later_phase([rest for _, rest in sc])
        ws = weight_phase([b for b, _ in sc], [cs[n] + ls[n][0] for n in range(len(chains))], masks)
        new_accs = tuple(a + _dot(w, v_ref[r, c]) for a, w, r, c in zip(accs, ws, rows, cols))
        return j - 1, tuple(cs[n] + ls[n][1] for n in range(len(chains))), new_accs

    init = (i * nsub - 2,) + first_steps()
    _, _, accs = lax.while_loop(cond, body, init)
    for s in range(nsub):
        for p in range(heads // 2):
            pair = [accs[n] for n, (ss, h) in enumerate(chains) if ss == s and h // 2 == p]
            o_ref[s * sub:(s + 1) * sub, p * LANES:(p + 1) * LANES] = jnp.where(
                first_head, pair[0], pair[1]).astype(o_ref.dtype)


def _sb_attention(sq, sk, sv, batch, seq, blk, sub, heads):
    assert seq % blk == 0 and blk % sub == 0 and sub == LANES
    assert heads % 2 == 0 and SB_HEADS % heads == 0
    nq = seq // blk
    width = heads * SB_HEAD_DIM
    groups = SB_WIDTH // width
    j = np.arange(sub)
    tri = jnp.asarray((j[:, None] > j[None, :]).astype(np.float32), BF16)
    qblk = pl.BlockSpec((blk, width), lambda b, g, i: (b * nq + i, g))
    kvblk = pl.BlockSpec((seq, width), lambda b, g, i: (b, g))
    return pl.pallas_call(
        functools.partial(_sb_attn_kernel, blk=blk, sub=sub, heads=heads),
        grid=(batch, groups, nq),
        in_specs=[qblk, kvblk, kvblk, pl.BlockSpec(tri.shape, lambda b, g, i: (0, 0))],
        out_specs=qblk,
        out_shape=jax.ShapeDtypeStruct((batch * seq, SB_WIDTH), BF16),
        compiler_params=pltpu.CompilerParams(
            dimension_semantics=("arbitrary", "arbitrary", "arbitrary"),
            vmem_limit_bytes=VMEM_LIMIT),
        name="sb_attn",
    )(sq, sk, sv, tri)


OUT_PROJ_ROW_GROUPS = 4


def _out_proj_kernel(ret_ref, sb_ref, x_ref, wtop_ref, wbot_ref, g_ref, wrh_ref, wrl_ref,
                     br_ref, tri_ref,
                     x1_ref, xf_ref, idx_ref, w_ref, rank_ref, cnt_ref, run_ref):
    @pl.when(pl.program_id(0) == 0)
    def _():
        run_ref[...] = jnp.zeros_like(run_ref)

    rows = x_ref.shape[0] // OUT_PROJ_ROW_GROUPS
    groups = [slice(r * rows, (r + 1) * rows) for r in range(OUT_PROJ_ROW_GROUPS)]
    x1s = [x_ref[g, :] + _dot(ret_ref[g, :], wtop_ref[...]) + _dot(sb_ref[g, :], wbot_ref[...])
           for g in groups]
    for g, x1 in zip(groups, x1s):
        x1_ref[g, :] = x1
    xfs = [x1 * lax.rsqrt(jnp.mean(x1 * x1, axis=-1, keepdims=True) + EPS) * g_ref[...]
           for x1 in x1s]
    for r, xf in enumerate(xfs):
        _store_token_tiles(xf_ref.at[pl.ds(r * rows * SUBLANES, rows * SUBLANES), :], xf)
    parts = [_split_bf16(xf) for xf in xfs]
    wrh = wrh_ref[...]
    wrl = wrl_ref[...]
    logits = jnp.concatenate(
        [_dot_nt(wrh, xh) + _dot_nt(wrh, xl) + _dot_nt(wrl, xh) for xh, xl in parts],
        axis=1) + br_ref[...]
    n_e, tm = logits.shape
    e_iota = lax.broadcasted_iota(jnp.int32, (n_e, tm), 0)
    cur = logits
    tops, sels, hots = [], [], []
    for _ in range(TOP_K):
        m = jnp.max(cur, axis=0, keepdims=True)
        sel = jnp.min(jnp.where(cur == m, e_iota, n_e), axis=0, keepdims=True)
        hot = e_iota == sel
        cur = jnp.where(hot, -jnp.inf, cur)
        tops.append(m)
        sels.append(sel)
        hots.append(hot)
    ps = [jnp.exp(m - tops[0]) for m in tops]
    denom = ps[0] + ps[1] + ps[2] + ps[3]
    chosen = jnp.zeros((n_e, tm), F32)
    for hot in hots:
        chosen = chosen + hot.astype(F32)
    before = run_ref[:, 0:1] + _dot(chosen.astype(BF16), tri_ref[...])
    for kk in range(TOP_K):
        idx_ref[kk:kk + 1, :] = sels[kk]
        w_ref[kk:kk + 1, :] = ps[kk] / denom
        rank = jnp.sum(jnp.where(hots[kk], before, 0.0), axis=0, keepdims=True)
        rank_ref[kk:kk + 1, :] = rank.astype(jnp.int32)
    run = run_ref[...] + jnp.sum(chosen, axis=1, keepdims=True)
    run_ref[...] = run
    cnt_ref[...] = run.astype(jnp.int32)


def _out_proj_route(ret, sb, x2, w_out, ffn_g, w_router, b_router, tm):
    t = x2.shape[0]
    assert t % tm == 0
    w_bf = w_out.astype(BF16)
    wtop, wbot = w_bf[:RET_WIDTH], w_bf[RET_WIDTH:]
    g2 = ffn_g.astype(F32)[None, :]
    wr_t = w_router.astype(F32).T
    wrh = wr_t.astype(BF16)
    wrl = (wr_t - wrh.astype(F32)).astype(BF16)
    br = b_router.astype(F32)[:, None]
    tt = np.arange(tm)
    tri = jnp.asarray((tt[:, None] < tt[None, :]).astype(np.float32), BF16)
    row = lambda w: pl.BlockSpec((tm, w), lambda i: (i, 0))
    full = lambda a: pl.BlockSpec(a.shape, lambda i: (0,) * a.ndim)
    col = pl.BlockSpec((TOP_K, tm), lambda i: (0, i))
    return pl.pallas_call(
        _out_proj_kernel,
        grid=(t // tm,),
        in_specs=[row(RET_WIDTH), row(SB_WIDTH), row(D_MODEL), full(wtop), full(wbot), full(g2),
                  full(wrh), full(wrl), full(br), full(tri)],
        out_specs=[row(D_MODEL), pl.BlockSpec((tm * SUBLANES, LANES), lambda i: (i, 0)),
                   col, col, col, pl.BlockSpec((N_EXPERTS, LANES), lambda i: (0, 0))],
        out_shape=[jax.ShapeDtypeStruct((t, D_MODEL), F32),
                   jax.ShapeDtypeStruct((t * SUBLANES, LANES), F32),
                   jax.ShapeDtypeStruct((TOP_K, t), jnp.int32),
                   jax.ShapeDtypeStruct((TOP_K, t), F32),
                   jax.ShapeDtypeStruct((TOP_K, t), jnp.int32),
                   jax.ShapeDtypeStruct((N_EXPERTS, LANES), jnp.int32)],
        scratch_shapes=[pltpu.VMEM((N_EXPERTS, LANES), F32)],
        compiler_params=pltpu.CompilerParams(
            dimension_semantics=("arbitrary",), vmem_limit_bytes=VMEM_LIMIT),
        name="out_proj_route",
    )(ret, sb, x2, wtop, wbot, g2, wrh, wrl, br, tri)


def _dest_kernel(pstart_ref, idx_ref, rank_ref, dest_ref):
    idx = idx_ref[...]
    dest = rank_ref[...]
    for e in range(N_EXPERTS):
        dest = dest + jnp.where(idx == e, pstart_ref[e], 0)
    dest_ref[...] = dest


def _dest(padded_start, top_idx, rank, tn):
    t = top_idx.shape[1]
    assert t % tn == 0
    col = pl.BlockSpec((TOP_K, tn), lambda i, ps: (0, i))
    return pl.pallas_call(
        _dest_kernel,
        grid_spec=pltpu.PrefetchScalarGridSpec(
            num_scalar_prefetch=1, grid=(t // tn,), in_specs=[col, col], out_specs=col),
        out_shape=jax.ShapeDtypeStruct((TOP_K, t), jnp.int32),
        name="dest",
    )(padded_start, top_idx, rank)


def _dispatch_kernel(pend_ref, dest_ref, xf_ref, slots_hbm, zbuf, sem, zsem, *, tm, blk):
    @pl.when(pl.program_id(0) == 0)
    def _():
        zbuf[...] = jnp.zeros_like(zbuf)

        def tail_copy(e):
            first = pl.multiple_of((pend_ref[e] - blk) * SUBLANES, blk * SUBLANES)
            return pltpu.make_async_copy(
                zbuf, slots_hbm.at[pl.ds(first, blk * SUBLANES), :], zsem)

        def nonempty(e):
            return pend_ref[e] > (pend_ref[e - 1] if e else 0)

        for e in range(N_EXPERTS):
            pl.when(nonempty(e))(lambda e=e: tail_copy(e).start())
        for e in range(N_EXPERTS):
            pl.when(nonempty(e))(lambda e=e: tail_copy(e).wait())

    def row_copy(t, kk):
        return pltpu.make_async_copy(
            xf_ref.at[_token_rows(t), :],
            slots_hbm.at[_token_rows(dest_ref[kk, t]), :], sem)

    def start(t, carry):
        for kk in range(TOP_K):
            row_copy(t, kk).start(priority=kk % 2)
        return carry

    lax.fori_loop(0, tm, start, 0, unroll=8)
    for kk in range(TOP_K):
        pltpu.make_async_copy(
            xf_ref, slots_hbm.at[pl.ds(0, tm * SUBLANES), :], sem).wait()


def _dispatch(padded_end, dest, xf, n_pad, tm, blk):
    t = xf.shape[0] // SUBLANES
    assert t % tm == 0
    return pl.pallas_call(
        functools.partial(_dispatch_kernel, tm=tm, blk=blk),
        grid_spec=pltpu.PrefetchScalarGridSpec(
            num_scalar_prefetch=1,
            grid=(t // tm,),
            in_specs=[pl.BlockSpec((TOP_K, tm), lambda i, pe: (0, i), memory_space=pltpu.SMEM),
                      pl.BlockSpec((tm * SUBLANES, LANES), lambda i, pe: (i, 0))],
            out_specs=pl.BlockSpec(memory_space=pl.ANY),
            scratch_shapes=[pltpu.VMEM((blk * SUBLANES, LANES), F32),
                            pltpu.SemaphoreType.DMA, pltpu.SemaphoreType.DMA],
        ),
        out_shape=jax.ShapeDtypeStruct((n_pad * SUBLANES, LANES), F32),
        compiler_params=pltpu.CompilerParams(
            dimension_semantics=("arbitrary",), vmem_limit_bytes=VMEM_LIMIT,
            has_side_effects=True),
        name="dispatch",
    )(padded_end, dest, xf)


WEIGHT_ROWS = 128


def _experts_kernel(be_ref, nu_ref, run_ref, next_ref, x_ref, wgu_hbm, wd_hbm, bg_ref, bu_ref,
                    bd_ref, y_ref, wgu_buf, wd_buf, wg_s, wu_s, wd_s, t_s, sem, *, blk):
    j = pl.program_id(0)
    used = j < nu_ref[0]
    expert = be_ref[j]
    new_expert = jnp.logical_or(j == 0, expert != be_ref[jnp.maximum(j - 1, 0)])

    def weight_copies(e, slot):
        return (pltpu.make_async_copy(wgu_hbm.at[e], wgu_buf.at[slot], sem.at[0, slot]),
                pltpu.make_async_copy(wd_hbm.at[e], wd_buf.at[slot], sem.at[1, slot]))

    def expert_ffn_tail(gate, up):
        gate = jnp.minimum(gate, SWIGLU_LIMIT)
        up = jnp.clip(up, -SWIGLU_LIMIT, SWIGLU_LIMIT)
        hidden = (up + 1.0) * gate * jax.nn.sigmoid(SWIGLU_ALPHA * gate)
        _store_token_tiles(y_ref, _dot(hidden.astype(BF16), wd_s[...]) + bd_ref[...])

    @pl.when(jnp.logical_and(used, new_expert))
    def _():
        slot = lax.rem(run_ref[expert], 2)

        @pl.when(j == 0)
        def _():
            for c in weight_copies(expert, slot):
                c.start()

        for c in weight_copies(expert, slot):
            c.wait()
        following = next_ref[expert]

        @pl.when(following >= 0)
        def _():
            for c in weight_copies(following, 1 - slot):
                c.start()

        xb = _load_token_tiles(x_ref, blk).astype(BF16)
        gate = bg_ref[...]
        up = bu_ref[...]
        for r in range(0, D_MODEL, MXU_WIDTH):
            wg_rows, wu_rows = [], []
            for rr in range(r, r + MXU_WIDTH, WEIGHT_ROWS):
                rows = slice(rr, rr + WEIGHT_ROWS)
                t_s[...] = wgu_buf[slot, rows, :].T
                wg_rows.append(t_s[pl.ds(0, D_FF, stride=2), :].T.astype(BF16))
                wu_rows.append(t_s[pl.ds(1, D_FF, stride=2), :].T.astype(BF16))
                wg_s[rows, :] = wg_rows[-1]
                wu_s[rows, :] = wu_rows[-1]
                wd_s[rows, :] = wd_buf[slot, rows, :].astype(BF16)
            x_cols = xb[:, r:r + MXU_WIDTH]
            gate = gate + _dot(x_cols, jnp.concatenate(wg_rows, axis=0))
            up = up + _dot(x_cols, jnp.concatenate(wu_rows, axis=0))
        expert_ffn_tail(gate, up)

    @pl.when(jnp.logical_and(used, jnp.logical_not(new_expert)))
    def _():
        xb = _load_token_tiles(x_ref, blk).astype(BF16)
        expert_ffn_tail(_dot(xb, wg_s[...]) + bg_ref[...], _dot(xb, wu_s[...]) + bu_ref[...])

    @pl.when(jnp.logical_not(used))
    def _():
        y_ref[...] = jnp.zeros_like(y_ref)


def _experts(block_expert, n_used, expert_run, next_expert, slots, wgu, wd, bg, bu, bd, blk):
    assert D_FF == D_MODEL
    n_pad = slots.shape[0] // SUBLANES
    n_blocks = n_pad // blk
    xmap = lambda j, be, nu, run, nxt: (jnp.minimum(j, nu[0] - 1), 0)
    wmap = lambda j, be, nu, run, nxt: (be[j], 0, 0)
    bspec = pl.BlockSpec((None, 1, D_FF), wmap)
    tiles = (blk * SUBLANES, LANES)
    hbm = pl.BlockSpec(memory_space=pl.ANY)
    return pl.pallas_call(
        functools.partial(_experts_kernel, blk=blk),
        grid_spec=pltpu.PrefetchScalarGridSpec(
            num_scalar_prefetch=4,
            grid=(n_blocks,),
            in_specs=[pl.BlockSpec(tiles, xmap), hbm, hbm, bspec, bspec, bspec],
            out_specs=pl.BlockSpec(tiles, lambda j, be, nu, run, nxt: (j, 0)),
            scratch_shapes=[pltpu.VMEM((2, D_MODEL, 2 * D_FF), F32),
                            pltpu.VMEM((2, D_FF, D_MODEL), F32),
                            pltpu.VMEM((D_MODEL, D_FF), BF16), pltpu.VMEM((D_MODEL, D_FF), BF16),
                            pltpu.VMEM((D_FF, D_MODEL), BF16),
                            pltpu.VMEM((2 * D_FF, WEIGHT_ROWS), F32),
                            pltpu.SemaphoreType.DMA((2, 2))],
        ),
        out_shape=jax.ShapeDtypeStruct((n_pad * SUBLANES, LANES), F32),
        compiler_params=pltpu.CompilerParams(
            dimension_semantics=("arbitrary",), vmem_limit_bytes=VMEM_LIMIT),
        name="experts",
    )(block_expert, n_used, expert_run, next_expert, slots, wgu, wd, bg, bu, bd)


def _combine_kernel(dest_ref, next_dest_ref, y_hbm, w_ref, x1_ref, o_ref, buf, sem, *, tm):
    i = pl.program_id(0)
    n = pl.num_programs(0)
    slot = lax.rem(i, 2)

    def gather(dests, into):
        def start(t, carry):
            for kk in range(TOP_K):
                pltpu.make_async_copy(
                    y_hbm.at[_token_rows(dests[kk, t]), :],
                    buf.at[into, kk, _token_rows(t), :], sem.at[into]).start(priority=kk % 2)
            return carry

        lax.fori_loop(0, tm, start, 0, unroll=8)

    pl.when(i == 0)(lambda: gather(dest_ref, 0))
    pl.when(i + 1 < n)(lambda: gather(next_dest_ref, 1 - slot))
    for kk in range(TOP_K):
        pltpu.make_async_copy(
            y_hbm.at[pl.ds(0, tm * SUBLANES), :], buf.at[slot, kk], sem.at[slot]).wait()
    w = w_ref[...]
    for s in range(ROW_TILES):
        cols = slice(s * LANES, (s + 1) * LANES)
        acc = x1_ref[:, cols]
        for kk in range(TOP_K):
            acc = acc + buf[slot, kk, pl.ds(s, tm, stride=SUBLANES), :] * w[:, kk:kk + 1]
        o_ref[:, cols] = acc


def _combine(dest, y_slots, w_t, x1, tm):
    t = x1.shape[0]
    assert t % tm == 0
    n = t // tm
    row = pl.BlockSpec((tm, D_MODEL), lambda i: (i, 0))
    return pl.pallas_call(
        functools.partial(_combine_kernel, tm=tm),
        grid=(n,),
        in_specs=[pl.BlockSpec((TOP_K, tm), lambda i: (0, i), memory_space=pltpu.SMEM),
                  pl.BlockSpec((TOP_K, tm), lambda i: (0, jnp.minimum(i + 1, n - 1)),
                               memory_space=pltpu.SMEM),
                  pl.BlockSpec(memory_space=pl.ANY),
                  pl.BlockSpec((tm, TOP_K), lambda i: (i, 0)),
                  row],
        out_specs=row,
        out_shape=jax.ShapeDtypeStruct((t, D_MODEL), F32),
        scratch_shapes=[pltpu.VMEM((2, TOP_K, tm * SUBLANES, LANES), F32),
                        pltpu.SemaphoreType.DMA((2,))],
        compiler_params=pltpu.CompilerParams(
            dimension_semantics=("arbitrary",), vmem_limit_bytes=VMEM_LIMIT),
        name="combine",
    )(dest, dest, y_slots, w_t, x1)


def _tiles(batch, seq):
    return dict(
        in_proj=min(512, seq),
        retention=min(1024, seq),
        sb_block=min(512, seq),
        sb_heads=4,
        out_proj=min(1024, seq),
        dest=min(8192, batch * seq),
        dispatch=min(4096, seq),
        combine=min(512, seq),
        expert_block=512,
    )


def _layer(x, attn_norm_g, w_in, ret_norm_g, sb_q_norm_g, sb_k_norm_g, w_out,
           ffn_norm_g, w_router, b_router, w_gate_up, b_gate_up, w_down, b_down):
    batch, seq, d = x.shape
    t = batch * seq
    tiles = _tiles(batch, seq)
    x2 = x.reshape(t, d)

    rq, rk, rv, rg, sq, sk, sv = _in_proj(
        x2, attn_norm_g, w_in, sb_q_norm_g, sb_k_norm_g, seq, tiles["in_proj"])
    ret = _retention(rq, rk, rv, rg, ret_norm_g, batch, seq, tiles["retention"])
    sb = _sb_attention(sq, sk, sv, batch, seq, tiles["sb_block"], LANES, tiles["sb_heads"])
    x1, xf, top_idx, top_w, rank, counts = _out_proj_route(
        ret, sb, x2, w_out, ffn_norm_g, w_router, b_router, tiles["out_proj"])

    blk = tiles["expert_block"]
    n_pad = t * TOP_K + N_EXPERTS * blk
    n_blocks = n_pad // blk
    counts = counts[:, 0]
    padded = (counts + blk - 1) // blk * blk
    padded_end = jnp.cumsum(padded)
    padded_start = padded_end - padded
    block_first = jnp.arange(n_blocks, dtype=jnp.int32) * blk
    block_expert = jnp.minimum(
        jnp.sum(padded_end[None, :] <= block_first[:, None], axis=1), N_EXPERTS - 1
    ).astype(jnp.int32)
    n_used = (padded_end[-1:] // blk).astype(jnp.int32)
    nonempty = counts > 0
    expert_run = (jnp.cumsum(nonempty) - nonempty).astype(jnp.int32)
    experts = jnp.arange(N_EXPERTS, dtype=jnp.int32)
    later = jnp.logical_and(nonempty[None, :], experts[None, :] > experts[:, None])
    next_expert = jnp.min(jnp.where(later, experts[None, :], N_EXPERTS), axis=1)
    next_expert = jnp.where(next_expert < N_EXPERTS, next_expert, -1).astype(jnp.int32)

    dest = _dest(padded_start, top_idx, rank, tiles["dest"])
    slots = _dispatch(padded_end, dest, xf, n_pad, tiles["dispatch"], blk)

    bgu = b_gate_up.astype(F32).reshape(N_EXPERTS, 1, D_FF, 2)
    y_slots = _experts(block_expert, n_used, expert_run, next_expert, slots,
                       w_gate_up.astype(F32), w_down.astype(F32),
                       bgu[..., 0], bgu[..., 1], b_down.astype(F32)[:, None, :], blk)

    out = _combine(dest, y_slots, top_w.T, x1, tiles["combine"])
    return out.reshape(batch, seq, d)


def kernel(x, attn_norm_g, w_in, ret_norm_g, sb_q_norm_g, sb_k_norm_g, w_out, ffn_norm_g,
           w_router, b_router, w_gate_up, b_gate_up, w_down, b_down):
    depth = attn_norm_g.shape[0]
    for l in range(depth):
        x = _layer(x, attn_norm_g[l], w_in[l], ret_norm_g[l], sb_q_norm_g[l], sb_k_norm_g[l],
                   w_out[l], ffn_norm_g[l], w_router[l], b_router[l], w_gate_up[l],
                   b_gate_up[l], w_down[l], b_down[l])
    return x
```

```python
import functools

import numpy as np
import jax
import jax.numpy as jnp
from jax import lax
from jax.experimental import pallas as pl
from jax.experimental.pallas import tpu as pltpu

D_MODEL = 1024
RET_WIDTH = 512
RET_HEADS = 4
RET_HEAD_DIM = 128
SB_WIDTH = 512
SB_HEADS = 8
SB_HEAD_DIM = 64
IN_PROJ_WIDTH = 4 * RET_WIDTH + 3 * SB_WIDTH
RET_CHUNK = 128
ROPE_BASE = 10000.0
N_EXPERTS = 32
TOP_K = 4
D_FF = D_MODEL
SWIGLU_LIMIT = 7.0
SWIGLU_ALPHA = 1.702
EPS = 1e-6

LANES = 128
MXU_WIDTH = 256
F32_EXP_UNDERFLOW = -88.0
VMEM_LIMIT = 56 * 1024 * 1024

BF16 = jnp.bfloat16
F32 = jnp.float32


def _split_bf16(v):
    hi = v.astype(BF16)
    lo = (v - hi.astype(F32)).astype(BF16)
    return hi, lo


SUBLANES = 8
ROW_TILES = D_MODEL // LANES
assert ROW_TILES == SUBLANES


def _store_token_tiles(ref, value):
    n = value.shape[0]
    for s in range(ROW_TILES):
        ref[pl.ds(s, n, stride=SUBLANES), :] = value[:, s * LANES:(s + 1) * LANES]


def _load_token_tiles(ref, n):
    return jnp.concatenate(
        [ref[pl.ds(s, n, stride=SUBLANES), :] for s in range(ROW_TILES)], axis=1)


def _token_rows(i):
    return pl.ds(pl.multiple_of(i * SUBLANES, SUBLANES), SUBLANES)


def _dot(a, b):
    return jnp.dot(a, b, preferred_element_type=F32)


def _dot_nt(a, b):
    return lax.dot_general(a, b, (((1,), (1,)), ((), ())), preferred_element_type=F32)


def _in_proj_kernel(x_ref, g_ref, w_ref, cos_a_ref, sin_a_ref, cos_b_ref, sin_b_ref,
                    qg_ref, kg_ref, seg_ref,
                    rq_ref, rk_ref, rv_ref, rg_ref, sq_ref, sk_ref, sv_ref):
    x = x_ref[...]
    h = x * lax.rsqrt(jnp.mean(x * x, axis=-1, keepdims=True) + EPS) * g_ref[...]
    p = _dot(h.astype(BF16), w_ref[...])
    k_scale = RET_HEAD_DIM ** -0.5
    cos_b = cos_b_ref[...]
    sin_b = sin_b_ref[...]
    first_half = lax.broadcasted_iota(jnp.int32, cos_b.shape, 1) < RET_HEAD_DIM // 2
    for r in range(x.shape[0] // LANES):
        rows = slice(r * LANES, (r + 1) * LANES)
        cos_a = cos_a_ref[r:r + 1, :]
        sin_a = sin_a_ref[r:r + 1, :]
        cos2 = cos_a * cos_b - sin_a * sin_b
        sin = sin_a * cos_b + cos_a * sin_b
        sin2 = jnp.where(first_half, -sin, sin)
        for hd in range(RET_HEADS):
            lo = hd * RET_HEAD_DIM
            q = p[rows, lo:lo + RET_HEAD_DIM]
            k = p[rows, RET_WIDTH + lo:RET_WIDTH + lo + RET_HEAD_DIM]
            q = q * cos2 + pltpu.roll(q, RET_HEAD_DIM // 2, axis=1) * sin2
            k = k * cos2 + pltpu.roll(k, RET_HEAD_DIM // 2, axis=1) * sin2
            rq_ref[rows, lo:lo + RET_HEAD_DIM] = q.astype(rq_ref.dtype)
            rk_ref[rows, lo:lo + RET_HEAD_DIM] = (k * k_scale).astype(rk_ref.dtype)
    rv_ref[...] = p[:, 2 * RET_WIDTH:3 * RET_WIDTH].astype(rv_ref.dtype)
    rg_ref[...] = p[:, 3 * RET_WIDTH:4 * RET_WIDTH]
    base = 4 * RET_WIDTH
    seg = seg_ref[...]

    def head_norm(v, gain):
        hi, lo = _split_bf16(v * v)
        group = seg.shape[0]
        ms = jnp.concatenate(
            [_dot(hi[:, g:g + group], seg) + _dot(lo[:, g:g + group], seg)
             for g in range(0, SB_WIDTH, group)], axis=1) * (1.0 / SB_HEAD_DIM)
        return v * lax.rsqrt(ms + EPS) * gain

    sq = head_norm(p[:, base:base + SB_WIDTH], qg_ref[...])
    sk = head_norm(p[:, base + SB_WIDTH:base + 2 * SB_WIDTH], kg_ref[...])
    sq_ref[...] = (sq * (SB_HEAD_DIM ** -0.5)).astype(sq_ref.dtype)
    sk_ref[...] = sk.astype(sk_ref.dtype)
    sv_ref[...] = p[:, base + 2 * SB_WIDTH:base + 3 * SB_WIDTH].astype(sv_ref.dtype)


def _rope_tables(seq, tm):
    half = RET_HEAD_DIM // 2
    per_tile = tm // LANES
    assert tm % LANES == 0 and per_tile <= SUBLANES and seq % tm == 0
    inv = ROPE_BASE ** (-np.arange(half, dtype=np.float64) / half)
    ang_a = (np.arange(seq // LANES, dtype=np.float64) * LANES)[:, None] * inv[None, :]
    ang_b = np.arange(LANES, dtype=np.float64)[:, None] * inv[None, :]
    dup = lambda a: np.concatenate([a, a], axis=-1)

    def coarse(a):
        a = dup(a).reshape(seq // tm, per_tile, RET_HEAD_DIM)
        return jnp.asarray(np.pad(a, ((0, 0), (0, SUBLANES - per_tile), (0, 0))), F32)

    return (coarse(np.cos(ang_a)), coarse(np.sin(ang_a)),
            jnp.asarray(dup(np.cos(ang_b)), F32), jnp.asarray(dup(np.sin(ang_b)), F32))


def _in_proj(x2, norm_g, w_in, sb_q_g, sb_k_g, seq, tm):
    t = x2.shape[0]
    assert seq % tm == 0 and t % tm == 0
    cos_a, sin_a, cos_b, sin_b = _rope_tables(seq, tm)
    seg = np.kron(np.eye(MXU_WIDTH // SB_HEAD_DIM), np.ones((SB_HEAD_DIM, SB_HEAD_DIM)))
    seg = jnp.asarray(seg, BF16)
    qg = jnp.tile(sb_q_g.astype(F32), SB_HEADS)[None, :]
    kg = jnp.tile(sb_k_g.astype(F32), SB_HEADS)[None, :]
    n_pos = seq // tm
    row = lambda w: pl.BlockSpec((tm, w), lambda i: (i, 0))
    full = lambda a: pl.BlockSpec(a.shape, lambda i: (0,) * a.ndim)
    pos = pl.BlockSpec((None, SUBLANES, RET_HEAD_DIM), lambda i: (i % n_pos, 0, 0))
    w_bf = w_in.astype(BF16)
    g2 = norm_g.astype(F32)[None, :]
    out = lambda dt: jax.ShapeDtypeStruct((t, RET_WIDTH), dt)
    return pl.pallas_call(
        _in_proj_kernel,
        grid=(t // tm,),
        in_specs=[row(D_MODEL), full(g2), full(w_bf), pos, pos, full(cos_b), full(sin_b),
                  full(qg), full(kg), full(seg)],
        out_specs=[row(RET_WIDTH)] * 7,
        out_shape=[out(BF16), out(F32), out(BF16), out(F32), out(BF16), out(BF16), out(BF16)],
        compiler_params=pltpu.CompilerParams(
            dimension_semantics=("arbitrary",), vmem_limit_bytes=VMEM_LIMIT),
        name="in_proj",
    )(x2, g2, w_bf, cos_a, sin_a, cos_b, sin_b, qg, kg, seg)


def _retention_kernel(q_ref, k_ref, v_ref, g_ref, ng_ref, intra_ref, qd_ref, kd_ref, cd_ref,
                      o_ref, state_ref, *, chunks):
    @pl.when(pl.program_id(1) == 0)
    def _():
        state_ref[...] = jnp.zeros_like(state_ref)

    def chunk(c, carry):
        rows = pl.ds(pl.multiple_of(c * RET_CHUNK, RET_CHUNK), RET_CHUNK)
        heads = range(RET_HEADS)
        cols = [slice(hd * RET_HEAD_DIM, (hd + 1) * RET_HEAD_DIM) for hd in heads]
        q = [q_ref[rows, cl] for cl in cols]
        k = [k_ref[rows, cl] for cl in cols]
        v = [v_ref[rows, cl] for cl in cols]
        state = [state_ref[hd] for hd in heads]
        scores = [_dot_nt(q[hd], k[hd].astype(BF16)) * intra_ref[hd] for hd in heads]
        cross = [_dot(q[hd], state[hd].astype(BF16)) * qd_ref[:, cols[hd]] for hd in heads]
        inner = [_dot(scores[hd].astype(BF16), v[hd]) for hd in heads]
        kd = [(k[hd] * kd_ref[:, cols[hd]]).astype(BF16) for hd in heads]
        for hd in heads:
            state_ref[hd] = state[hd] * cd_ref[:, cols[hd]] + _dot(kd[hd].T, v[hd])
        for hd in heads:
            o = inner[hd] + cross[hd]
            o = o * lax.rsqrt(jnp.mean(o * o, axis=-1, keepdims=True) + EPS) * ng_ref[:, cols[hd]]
            gate = g_ref[rows, cols[hd]]
            o_ref[rows, cols[hd]] = (o * (gate * jax.nn.sigmoid(gate))).astype(o_ref.dtype)
        return carry

    lax.fori_loop(0, chunks, chunk, 0, unroll=True)


def _retention(rq, rk, rv, rg, ret_norm_g, batch, seq, rows):
    assert seq % rows == 0 and rows % RET_CHUNK == 0
    log_gamma = np.log(1.0 - 2.0 ** (-5.0 - np.arange(RET_HEADS, dtype=np.float64)))
    idx = np.arange(RET_CHUNK, dtype=np.float64)
    diff = idx[:, None] - idx[None, :]
    intra = np.where(diff >= 0, np.exp(log_gamma[:, None, None] * np.maximum(diff, 0.0)), 0.0)
    q_decay = np.exp(log_gamma[:, None] * (idx + 1.0))
    k_decay = np.exp(log_gamma[:, None] * (RET_CHUNK - 1.0 - idx))
    chunk_decay = np.exp(log_gamma * RET_CHUNK)
    lane_rep = lambda a: np.repeat(a.T, RET_HEAD_DIM, axis=1)
    intra = jnp.asarray(intra, F32)
    qd = jnp.asarray(lane_rep(q_decay), F32)
    kd = jnp.asarray(lane_rep(k_decay), F32)
    cd = jnp.asarray(np.repeat(chunk_decay, RET_HEAD_DIM)[None, :], F32)
    ng = ret_norm_g.astype(F32).reshape(1, RET_WIDTH)
    n_r = seq // rows
    blk = pl.BlockSpec((rows, RET_WIDTH), lambda b, r: (b * n_r + r, 0))
    full = lambda a: pl.BlockSpec(a.shape, lambda b, r: (0,) * a.ndim)
    return pl.pallas_call(
        functools.partial(_retention_kernel, chunks=rows // RET_CHUNK),
        grid=(batch, n_r),
        in_specs=[blk, blk, blk, blk, full(ng), full(intra), full(qd), full(kd), full(cd)],
        out_specs=blk,
        out_shape=jax.ShapeDtypeStruct((batch * seq, RET_WIDTH), BF16),
        scratch_shapes=[pltpu.VMEM((RET_HEADS, RET_HEAD_DIM, RET_HEAD_DIM), F32)],
        compiler_params=pltpu.CompilerParams(
            dimension_semantics=("arbitrary", "arbitrary"), vmem_limit_bytes=VMEM_LIMIT),
        name="retention",
    )(rq, rk, rv, rg, ng, intra, qd, kd, cd)


def _sb_attn_kernel(q_ref, k_ref, v_ref, tri_ref, o_ref, *, blk, sub, heads):
    i = pl.program_id(2)
    nsub = blk // sub
    lane = lax.broadcasted_iota(jnp.int32, (sub, LANES), 1)
    first_head = lane < SB_HEAD_DIM
    tri = tri_ref[...]
    below_diag = (lax.broadcasted_iota(jnp.int32, (sub, sub), 1)
                  < lax.broadcasted_iota(jnp.int32, (sub, sub), 0))
    chains = [(s, h) for s in range(nsub) for h in range(heads)]
    cols = [slice((h // 2) * LANES, (h // 2 + 1) * LANES) for _, h in chains]


    def score_phase(qms, kbs, masks):
        zs = [_dot_nt(qm, kb) for qm, kb in zip(qms, kbs)]
        out = []
        for z, mask in zip(zs, masks):
            log_beta = jnp.minimum(z, 0.0) - jnp.log(1.0 + jnp.exp(-jnp.abs(z)))
            log_rest = log_beta - z
            if mask is not None:
                log_rest = jnp.where(mask, log_rest, 0.0)
            out.append((log_beta, log_rest))
        return out

    def later_phase(log_rests):
        parts = [_split_bf16(r) for r in log_rests]
        laters = [_dot(hi, tri) + _dot(lo, tri) for hi, lo in parts]
        return [(later, later[:, 0:1] + r[:, 0:1]) for later, r in zip(laters, log_rests)]

    def weight_phase(log_betas, log_remainings, masks):
        ws = [jnp.exp(b + r) for b, r in zip(log_betas, log_remainings)]
        ws = [w if m is None else jnp.where(m, w, 0.0) for w, m in zip(ws, masks)]
        return [w.astype(BF16) for w in ws]

    def queries():
        out = []
        for (s, h), c in zip(chains, cols):
            q = q_ref[s * sub:(s + 1) * sub, c]
            keep = first_head if h % 2 == 0 else jnp.logical_not(first_head)
            out.append(jnp.where(keep, q, jnp.zeros_like(q)))
        return out

    def key_rows(chunk):
        return pl.ds(pl.multiple_of(chunk * sub, sub), sub)

    def first_steps():
        qms = queries()
        own = [key_rows(i * nsub + s) for s, _ in chains]
        prev = [key_rows(jnp.maximum(i * nsub + s - 1, 0)) for s, _ in chains]
        prev_masks = [(i > 0) if s == 0 else None for s, _ in chains]
        sc_o = score_phase(qms, [k_ref[r, c] for r, c in zip(own, cols)], [below_diag] * len(chains))
        sc_p = score_phase(qms, [k_ref[r, c] for r, c in zip(prev, cols)], prev_masks)
        ls_o = later_phase([rest for _, rest in sc_o])
        ls_p = later_phase([rest for _, rest in sc_p])
        w_o = weight_phase([b for b, _ in sc_o], [later for later, _ in ls_o],
                           [below_diag] * len(chains))
        w_p = weight_phase([b for b, _ in sc_p],
                           [ls_o[n][1] + ls_p[n][0] for n in range(len(chains))], prev_masks)
        accs = [_dot(w, v_ref[r, c]) for w, r, c in zip(w_o, own, cols)]
        accs = [a + _dot(w, v_ref[r, c]) for a, w, r, c in zip(accs, w_p, prev, cols)]
        return tuple(ls_o[n][1] + ls_p[n][1] for n in range(len(chains))), tuple(accs)

    def cond(carry):
        j, cs, _ = carry
        c_max = functools.reduce(jnp.maximum, cs)
        return jnp.logical_and(j + (nsub - 1) >= 0, jnp.max(c_max) > F32_EXP_UNDERFLOW)

    def body(carry):
        j, cs, accs = carry
        rows = [key_rows(jnp.maximum(j + s, 0)) for s, _ in chains]
        masks = [None if s == nsub - 1 else (j + s >= 0) for s, _ in chains]
        sc = score_phase(queries(), [k_ref[r, c] for r, c in zip(rows, cols)], masks)
        ls = later_phase([rest for _, rest in sc])
        ws = weight_phase([b for b, _ in sc], [cs[n] + ls[n][0] for n in range(len(chains))], masks)
        new_accs = tuple(a + _dot(w, v_ref[r, c]) for a, w, r, c in zip(accs, ws, rows, cols))
        return j - 1, tuple(cs[n] + ls[n][1] for n in range(len(chains))), new_accs

    init = (i * nsub - 2,) + first_steps()
    _, _, accs = lax.while_loop(cond, body, init)
    for s in range(nsub):
        for p in range(heads // 2):
            pair = [accs[n] for n, (ss, h) in enumerate(chains) if ss == s and h // 2 == p]
            o_ref[s * sub:(s + 1) * sub, p * LANES:(p + 1) * LANES] = jnp.where(
                first_head, pair[0], pair[1]).astype(o_ref.dtype)


def _sb_attention(sq, sk, sv, batch, seq, blk, sub, heads):
    assert seq % blk == 0 and blk % sub == 0 and sub == LANES
    assert heads % 2 == 0 and SB_HEADS % heads == 0
    nq = seq // blk
    width = heads * SB_HEAD_DIM
    groups = SB_WIDTH // width
    j = np.arange(sub)
    tri = jnp.asarray((j[:, None] > j[None, :]).astype(np.float32), BF16)
    qblk = pl.BlockSpec((blk, width), lambda b, g, i: (b * nq + i, g))
    kvblk = pl.BlockSpec((seq, width), lambda b, g, i: (b, g))
    return pl.pallas_call(
        functools.partial(_sb_attn_kernel, blk=blk, sub=sub, heads=heads),
        grid=(batch, groups, nq),
        in_specs=[qblk, kvblk, kvblk, pl.BlockSpec(tri.shape, lambda b, g, i: (0, 0))],
        out_specs=qblk,
        out_shape=jax.ShapeDtypeStruct((batch * seq, SB_WIDTH), BF16),
        compiler_params=pltpu.CompilerParams(
            dimension_semantics=("arbitrary", "arbitrary", "arbitrary"),
            vmem_limit_bytes=VMEM_LIMIT),
        name="sb_attn",
    )(sq, sk, sv, tri)


OUT_PROJ_ROW_GROUPS = 4


def _out_proj_kernel(ret_ref, sb_ref, x_ref, wtop_ref, wbot_ref, g_ref, wrh_ref, wrl_ref,
                     br_ref, tri_ref,
                     x1_ref, xf_ref, idx_ref, w_ref, rank_ref, cnt_ref, run_ref):
    @pl.when(pl.program_id(0) == 0)
    def _():
        run_ref[...] = jnp.zeros_like(run_ref)

    rows = x_ref.shape[0] // OUT_PROJ_ROW_GROUPS
    groups = [slice(r * rows, (r + 1) * rows) for r in range(OUT_PROJ_ROW_GROUPS)]
    x1s = [x_ref[g, :] + _dot(ret_ref[g, :], wtop_ref[...]) + _dot(sb_ref[g, :], wbot_ref[...])
           for g in groups]
    for g, x1 in zip(groups, x1s):
        x1_ref[g, :] = x1
    xfs = [x1 * lax.rsqrt(jnp.mean(x1 * x1, axis=-1, keepdims=True) + EPS) * g_ref[...]
           for x1 in x1s]
    for r, xf in enumerate(xfs):
        _store_token_tiles(xf_ref.at[pl.ds(r * rows * SUBLANES, rows * SUBLANES), :], xf)
    parts = [_split_bf16(xf) for xf in xfs]
    wrh = wrh_ref[...]
    wrl = wrl_ref[...]
    logits = jnp.concatenate(
        [_dot_nt(wrh, xh) + _dot_nt(wrh, xl) + _dot_nt(wrl, xh) for xh, xl in parts],
        axis=1) + br_ref[...]
    n_e, tm = logits.shape
    e_iota = lax.broadcasted_iota(jnp.int32, (n_e, tm), 0)
    cur = logits
    tops, sels, hots = [], [], []
    for _ in range(TOP_K):
        m = jnp.max(cur, axis=0, keepdims=True)
        sel = jnp.min(jnp.where(cur == m, e_iota, n_e), axis=0, keepdims=True)
        hot = e_iota == sel
        cur = jnp.where(hot, -jnp.inf, cur)
        tops.append(m)
        sels.append(sel)
        hots.append(hot)
    ps = [jnp.exp(m - tops[0]) for m in tops]
    denom = ps[0] + ps[1] + ps[2] + ps[3]
    chosen = jnp.zeros((n_e, tm), F32)
    for hot in hots:
        chosen = chosen + hot.astype(F32)
    before = run_ref[:, 0:1] + _dot(chosen.astype(BF16), tri_ref[...])
    for kk in range(TOP_K):
        idx_ref[kk:kk + 1, :] = sels[kk]
        w_ref[kk:kk + 1, :] = ps[kk] / denom
        rank = jnp.sum(jnp.where(hots[kk], before, 0.0), axis=0, keepdims=True)
        rank_ref[kk:kk + 1, :] = rank.astype(jnp.int32)
    run = run_ref[...] + jnp.sum(chosen, axis=1, keepdims=True)
    run_ref[...] = run
    cnt_ref[...] = run.astype(jnp.int32)


def _out_proj_route(ret, sb, x2, w_out, ffn_g, w_router, b_router, tm):
    t = x2.shape[0]
    assert t % tm == 0
    w_bf = w_out.astype(BF16)
    wtop, wbot = w_bf[:RET_WIDTH], w_bf[RET_WIDTH:]
    g2 = ffn_g.astype(F32)[None, :]
    wr_t = w_router.astype(F32).T
    wrh = wr_t.astype(BF16)
    wrl = (wr_t - wrh.astype(F32)).astype(BF16)
    br = b_router.astype(F32)[:, None]
    tt = np.arange(tm)
    tri = jnp.asarray((tt[:, None] < tt[None, :]).astype(np.float32), BF16)
    row = lambda w: pl.BlockSpec((tm, w), lambda i: (i, 0))
    full = lambda a: pl.BlockSpec(a.shape, lambda i: (0,) * a.ndim)
    col = pl.BlockSpec((TOP_K, tm), lambda i: (0, i))
    return pl.pallas_call(
        _out_proj_kernel,
        grid=(t // tm,),
        in_specs=[row(RET_WIDTH), row(SB_WIDTH), row(D_MODEL), full(wtop), full(wbot), full(g2),
                  full(wrh), full(wrl), full(br), full(tri)],
        out_specs=[row(D_MODEL), pl.BlockSpec((tm * SUBLANES, LANES), lambda i: (i, 0)),
                   col, col, col, pl.BlockSpec((N_EXPERTS, LANES), lambda i: (0, 0))],
        out_shape=[jax.ShapeDtypeStruct((t, D_MODEL), F32),
                   jax.ShapeDtypeStruct((t * SUBLANES, LANES), F32),
                   jax.ShapeDtypeStruct((TOP_K, t), jnp.int32),
                   jax.ShapeDtypeStruct((TOP_K, t), F32),
                   jax.ShapeDtypeStruct((TOP_K, t), jnp.int32),
                   jax.ShapeDtypeStruct((N_EXPERTS, LANES), jnp.int32)],
        scratch_shapes=[pltpu.VMEM((N_EXPERTS, LANES), F32)],
        compiler_params=pltpu.CompilerParams(
            dimension_semantics=("arbitrary",), vmem_limit_bytes=VMEM_LIMIT),
        name="out_proj_route",
    )(ret, sb, x2, wtop, wbot, g2, wrh, wrl, br, tri)


def _dest_kernel(pstart_ref, idx_ref, rank_ref, dest_ref):
    idx = idx_ref[...]
    dest = rank_ref[...]
    for e in range(N_EXPERTS):
        dest = dest + jnp.where(idx == e, pstart_ref[e], 0)
    dest_ref[...] = dest


def _dest(padded_start, top_idx, rank, tn):
    t = top_idx.shape[1]
    assert t % tn == 0
    col = pl.BlockSpec((TOP_K, tn), lambda i, ps: (0, i))
    return pl.pallas_call(
        _dest_kernel,
        grid_spec=pltpu.PrefetchScalarGridSpec(
            num_scalar_prefetch=1, grid=(t // tn,), in_specs=[col, col], out_specs=col),
        out_shape=jax.ShapeDtypeStruct((TOP_K, t), jnp.int32),
        name="dest",
    )(padded_start, top_idx, rank)


def _dispatch_kernel(pend_ref, dest_ref, xf_ref, slots_hbm, zbuf, sem, zsem, *, tm, blk):
    @pl.when(pl.program_id(0) == 0)
    def _():
        zbuf[...] = jnp.zeros_like(zbuf)

        def tail_copy(e):
            first = pl.multiple_of((pend_ref[e] - blk) * SUBLANES, blk * SUBLANES)
            return pltpu.make_async_copy(
                zbuf, slots_hbm.at[pl.ds(first, blk * SUBLANES), :], zsem)

        def nonempty(e):
            return pend_ref[e] > (pend_ref[e - 1] if e else 0)

        for e in range(N_EXPERTS):
            pl.when(nonempty(e))(lambda e=e: tail_copy(e).start())
        for e in range(N_EXPERTS):
            pl.when(nonempty(e))(lambda e=e: tail_copy(e).wait())

    def row_copy(t, kk):
        return pltpu.make_async_copy(
            xf_ref.at[_token_rows(t), :],
            slots_hbm.at[_token_rows(dest_ref[kk, t]), :], sem)

    def start(t, carry):
        for kk in range(TOP_K):
            row_copy(t, kk).start(priority=kk % 2)
        return carry

    lax.fori_loop(0, tm, start, 0, unroll=8)
    for kk in range(TOP_K):
        pltpu.make_async_copy(
            xf_ref, slots_hbm.at[pl.ds(0, tm * SUBLANES), :], sem).wait()


def _dispatch(padded_end, dest, xf, n_pad, tm, blk):
    t = xf.shape[0] // SUBLANES
    assert t % tm == 0
    return pl.pallas_call(
        functools.partial(_dispatch_kernel, tm=tm, blk=blk),
        grid_spec=pltpu.PrefetchScalarGridSpec(
            num_scalar_prefetch=1,
            grid=(t // tm,),
            in_specs=[pl.BlockSpec((TOP_K, tm), lambda i, pe: (0, i), memory_space=pltpu.SMEM),
                      pl.BlockSpec((tm * SUBLANES, LANES), lambda i, pe: (i, 0))],
            out_specs=pl.BlockSpec(memory_space=pl.ANY),
            scratch_shapes=[pltpu.VMEM((blk * SUBLANES, LANES), F32),
                            pltpu.SemaphoreType.DMA, pltpu.SemaphoreType.DMA],
        ),
        out_shape=jax.ShapeDtypeStruct((n_pad * SUBLANES, LANES), F32),
        compiler_params=pltpu.CompilerParams(
            dimension_semantics=("arbitrary",), vmem_limit_bytes=VMEM_LIMIT,
            has_side_effects=True),
        name="dispatch",
    )(padded_end, dest, xf)


WEIGHT_ROWS = 128


def _experts_kernel(be_ref, nu_ref, run_ref, next_ref, rows_ref, x_ref, wgu_hbm, wd_hbm, bg_ref,
                    bu_ref, bd_ref, y_ref, wgu_buf, wd_buf, wg_s, wu_s, wd_s, t_s, sem, *, blk):
    j = pl.program_id(0)
    used = j < nu_ref[0]
    expert = be_ref[j]
    new_expert = jnp.logical_or(j == 0, expert != be_ref[jnp.maximum(j - 1, 0)])

    def weight_copies(e, slot):
        return (pltpu.make_async_copy(wgu_hbm.at[e], wgu_buf.at[slot], sem.at[0, slot]),
                pltpu.make_async_copy(wd_hbm.at[e], wd_buf.at[slot], sem.at[1, slot]))

    def expert_ffn_tail(gate, up, out_ref=y_ref):
        gate = jnp.minimum(gate, SWIGLU_LIMIT)
        up = jnp.clip(up, -SWIGLU_LIMIT, SWIGLU_LIMIT)
        hidden = (up + 1.0) * gate * jax.nn.sigmoid(SWIGLU_ALPHA * gate)
        _store_token_tiles(out_ref, _dot(hidden.astype(BF16), wd_s[...]) + bd_ref[...])

    @pl.when(jnp.logical_and(used, new_expert))
    def _():
        slot = lax.rem(run_ref[expert], 2)

        @pl.when(j == 0)
        def _():
            for c in weight_copies(expert, slot):
                c.start()

        for c in weight_copies(expert, slot):
            c.wait()
        following = next_ref[expert]

        @pl.when(following >= 0)
        def _():
            for c in weight_copies(following, 1 - slot):
                c.start()

        xb = _load_token_tiles(x_ref, blk).astype(BF16)
        gate = bg_ref[...]
        up = bu_ref[...]
        for r in range(0, D_MODEL, MXU_WIDTH):
            wg_rows, wu_rows = [], []
            for rr in range(r, r + MXU_WIDTH, WEIGHT_ROWS):
                rows = slice(rr, rr + WEIGHT_ROWS)
                t_s[...] = wgu_buf[slot, rows, :].T
                wg_rows.append(t_s[pl.ds(0, D_FF, stride=2), :].T.astype(BF16))
                wu_rows.append(t_s[pl.ds(1, D_FF, stride=2), :].T.astype(BF16))
                wg_s[rows, :] = wg_rows[-1]
                wu_s[rows, :] = wu_rows[-1]
                wd_s[rows, :] = wd_buf[slot, rows, :].astype(BF16)
            x_cols = xb[:, r:r + MXU_WIDTH]
            gate = gate + _dot(x_cols, jnp.concatenate(wg_rows, axis=0))
            up = up + _dot(x_cols, jnp.concatenate(wu_rows, axis=0))
        expert_ffn_tail(gate, up)

    later_block = jnp.logical_and(used, jnp.logical_not(new_expert))
    half = blk // 2
    half_full = rows_ref[j] <= half

    @pl.when(jnp.logical_and(later_block, jnp.logical_not(half_full)))
    def _():
        xb = _load_token_tiles(x_ref, blk).astype(BF16)
        expert_ffn_tail(_dot(xb, wg_s[...]) + bg_ref[...], _dot(xb, wu_s[...]) + bu_ref[...])

    @pl.when(jnp.logical_and(later_block, half_full))
    def _():
        xb = _load_token_tiles(x_ref.at[pl.ds(0, half * SUBLANES), :], half).astype(BF16)
        expert_ffn_tail(_dot(xb, wg_s[...]) + bg_ref[...], _dot(xb, wu_s[...]) + bu_ref[...],
                        y_ref.at[pl.ds(0, half * SUBLANES), :])
        y_ref[pl.ds(half * SUBLANES, half * SUBLANES), :] = jnp.zeros(
            (half * SUBLANES, LANES), y_ref.dtype)

    @pl.when(jnp.logical_not(used))
    def _():
        y_ref[...] = jnp.zeros_like(y_ref)


def _experts(block_expert, n_used, expert_run, next_expert, block_rows, slots, wgu, wd, bg, bu,
             bd, blk):
    assert D_FF == D_MODEL
    n_pad = slots.shape[0] // SUBLANES
    n_blocks = n_pad // blk
    xmap = lambda j, be, nu, run, nxt, rows: (jnp.minimum(j, nu[0] - 1), 0)
    wmap = lambda j, be, nu, run, nxt, rows: (be[j], 0, 0)
    bspec = pl.BlockSpec((None, 1, D_FF), wmap)
    tiles = (blk * SUBLANES, LANES)
    hbm = pl.BlockSpec(memory_space=pl.ANY)
    return pl.pallas_call(
        functools.partial(_experts_kernel, blk=blk),
        grid_spec=pltpu.PrefetchScalarGridSpec(
            num_scalar_prefetch=5,
            grid=(n_blocks,),
            in_specs=[pl.BlockSpec(tiles, xmap), hbm, hbm, bspec, bspec, bspec],
            out_specs=pl.BlockSpec(tiles, lambda j, be, nu, run, nxt, rows: (j, 0)),
            scratch_shapes=[pltpu.VMEM((2, D_MODEL, 2 * D_FF), F32),
                            pltpu.VMEM((2, D_FF, D_MODEL), F32),
                            pltpu.VMEM((D_MODEL, D_FF), BF16), pltpu.VMEM((D_MODEL, D_FF), BF16),
                            pltpu.VMEM((D_FF, D_MODEL), BF16),
                            pltpu.VMEM((2 * D_FF, WEIGHT_ROWS), F32),
                            pltpu.SemaphoreType.DMA((2, 2))],
        ),
        out_shape=jax.ShapeDtypeStruct((n_pad * SUBLANES, LANES), F32),
        compiler_params=pltpu.CompilerParams(
            dimension_semantics=("arbitrary",), vmem_limit_bytes=VMEM_LIMIT),
        name="experts",
    )(block_expert, n_used, expert_run, next_expert, block_rows, slots, wgu, wd, bg, bu, bd)


def _combine_kernel(dest_ref, next_dest_ref, y_hbm, w_ref, x1_ref, o_ref, buf, sem, *, tm):
    i = pl.program_id(0)
    n = pl.num_programs(0)
    slot = lax.rem(i, 2)

    def gather(dests, into):
        def start(t, carry):
            for kk in range(TOP_K):
                pltpu.make_async_copy(
                    y_hbm.at[_token_rows(dests[kk, t]), :],
                    buf.at[into, kk, _token_rows(t), :], sem.at[into]).start(priority=kk % 2)
            return carry

        lax.fori_loop(0, tm, start, 0, unroll=8)

    pl.when(i == 0)(lambda: gather(dest_ref, 0))
    pl.when(i + 1 < n)(lambda: gather(next_dest_ref, 1 - slot))
    for kk in range(TOP_K):
        pltpu.make_async_copy(
            y_hbm.at[pl.ds(0, tm * SUBLANES), :], buf.at[slot, kk], sem.at[slot]).wait()
    w = w_ref[...]
    for s in range(ROW_TILES):
        cols = slice(s * LANES, (s + 1) * LANES)
        acc = x1_ref[:, cols]
        for kk in range(TOP_K):
            acc = acc + buf[slot, kk, pl.ds(s, tm, stride=SUBLANES), :] * w[:, kk:kk + 1]
        o_ref[:, cols] = acc


def _combine(dest, y_slots, w_t, x1, tm):
    t = x1.shape[0]
    assert t % tm == 0
    n = t // tm
    row = pl.BlockSpec((tm, D_MODEL), lambda i: (i, 0))
    return pl.pallas_call(
        functools.partial(_combine_kernel, tm=tm),
        grid=(n,),
        in_specs=[pl.BlockSpec((TOP_K, tm), lambda i: (0, i), memory_space=pltpu.SMEM),
                  pl.BlockSpec((TOP_K, tm), lambda i: (0, jnp.minimum(i + 1, n - 1)),
                               memory_space=pltpu.SMEM),
                  pl.BlockSpec(memory_space=pl.ANY),
                  pl.BlockSpec((tm, TOP_K), lambda i: (i, 0)),
                  row],
        out_specs=row,
        out_shape=jax.ShapeDtypeStruct((t, D_MODEL), F32),
        scratch_shapes=[pltpu.VMEM((2, TOP_K, tm * SUBLANES, LANES), F32),
                        pltpu.SemaphoreType.DMA((2,))],
        compiler_params=pltpu.CompilerParams(
            dimension_semantics=("arbitrary",), vmem_limit_bytes=VMEM_LIMIT),
        name="combine",
    )(dest, dest, y_slots, w_t, x1)


def _tiles(batch, seq):
    return dict(
        in_proj=min(512, seq),
        retention=min(1024, seq),
        sb_block=min(512, seq),
        sb_heads=4,
        out_proj=min(1024, seq),
        dest=min(8192, batch * seq),
        dispatch=min(4096, seq),
        combine=min(512, seq),
        expert_block=512,
    )


def _layer(x, attn_norm_g, w_in, ret_norm_g, sb_q_norm_g, sb_k_norm_g, w_out,
           ffn_norm_g, w_router, b_router, w_gate_up, b_gate_up, w_down, b_down):
    batch, seq, d = x.shape
    t = batch * seq
    tiles = _tiles(batch, seq)
    x2 = x.reshape(t, d)

    rq, rk, rv, rg, sq, sk, sv = _in_proj(
        x2, attn_norm_g, w_in, sb_q_norm_g, sb_k_norm_g, seq, tiles["in_proj"])
    ret = _retention(rq, rk, rv, rg, ret_norm_g, batch, seq, tiles["retention"])
    sb = _sb_attention(sq, sk, sv, batch, seq, tiles["sb_block"], LANES, tiles["sb_heads"])
    x1, xf, top_idx, top_w, rank, counts = _out_proj_route(
        ret, sb, x2, w_out, ffn_norm_g, w_router, b_router, tiles["out_proj"])

    blk = tiles["expert_block"]
    n_pad = t * TOP_K + N_EXPERTS * blk
    n_blocks = n_pad // blk
    counts = counts[:, 0]
    padded = (counts + blk - 1) // blk * blk
    padded_end = jnp.cumsum(padded)
    padded_start = padded_end - padded
    block_first = jnp.arange(n_blocks, dtype=jnp.int32) * blk
    block_expert = jnp.minimum(
        jnp.sum(padded_end[None, :] <= block_first[:, None], axis=1), N_EXPERTS - 1
    ).astype(jnp.int32)
    n_used = (padded_end[-1:] // blk).astype(jnp.int32)
    nonempty = counts > 0
    expert_run = (jnp.cumsum(nonempty) - nonempty).astype(jnp.int32)
    experts = jnp.arange(N_EXPERTS, dtype=jnp.int32)
    later = jnp.logical_and(nonempty[None, :], experts[None, :] > experts[:, None])
    next_expert = jnp.min(jnp.where(later, experts[None, :], N_EXPERTS), axis=1)
    next_expert = jnp.where(next_expert < N_EXPERTS, next_expert, -1).astype(jnp.int32)

    dest = _dest(padded_start, top_idx, rank, tiles["dest"])
    slots = _dispatch(padded_end, dest, xf, n_pad, tiles["dispatch"], blk)

    bgu = b_gate_up.astype(F32).reshape(N_EXPERTS, 1, D_FF, 2)
    block_rows = jnp.clip(counts[block_expert] - (block_first - padded_start[block_expert]),
                          0, blk).astype(jnp.int32)
    y_slots = _experts(block_expert, n_used, expert_run, next_expert, block_rows, slots,
                       w_gate_up.astype(F32), w_down.astype(F32),
                       bgu[..., 0], bgu[..., 1], b_down.astype(F32)[:, None, :], blk)

    out = _combine(dest, y_slots, top_w.T, x1, tiles["combine"])
    return out.reshape(batch, seq, d)


def kernel(x, attn_norm_g, w_in, ret_norm_g, sb_q_norm_g, sb_k_norm_g, w_out, ffn_norm_g,
           w_router, b_router, w_gate_up, b_gate_up, w_down, b_down):
    depth = attn_norm_g.shape[0]
    for l in range(depth):
        x = _layer(x, attn_norm_g[l], w_in[l], ret_norm_g[l], sb_q_norm_g[l], sb_k_norm_g[l],
                   w_out[l], ffn_norm_g[l], w_router[l], b_router[l], w_gate_up[l],
                   b_gate_up[l], w_down[l], b_down[l])
    return x
```

```python
import functools

import numpy as np
import jax
import jax.numpy as jnp
from jax import lax
from jax.experimental import pallas as pl
from jax.experimental.pallas import tpu as pltpu

D_MODEL = 1024
RET_WIDTH = 512
RET_HEADS = 4
RET_HEAD_DIM = 128
SB_WIDTH = 512
SB_HEADS = 8
SB_HEAD_DIM = 64
IN_PROJ_WIDTH = 4 * RET_WIDTH + 3 * SB_WIDTH
RET_CHUNK = 128
ROPE_BASE = 10000.0
N_EXPERTS = 32
TOP_K = 4
D_FF = D_MODEL
SWIGLU_LIMIT = 7.0
SWIGLU_ALPHA = 1.702
EPS = 1e-6

LANES = 128
MXU_WIDTH = 256
F32_EXP_UNDERFLOW = -88.0
VMEM_LIMIT = 56 * 1024 * 1024

BF16 = jnp.bfloat16
F32 = jnp.float32


def _split_bf16(v):
    hi = v.astype(BF16)
    lo = (v - hi.astype(F32)).astype(BF16)
    return hi, lo


SUBLANES = 8
ROW_TILES = D_MODEL // LANES
assert ROW_TILES == SUBLANES


def _store_token_tiles(ref, value):
    n = value.shape[0]
    for s in range(ROW_TILES):
        ref[pl.ds(s, n, stride=SUBLANES), :] = value[:, s * LANES:(s + 1) * LANES]


def _load_token_tiles(ref, n):
    return jnp.concatenate(
        [ref[pl.ds(s, n, stride=SUBLANES), :] for s in range(ROW_TILES)], axis=1)


def _token_rows(i):
    return pl.ds(pl.multiple_of(i * SUBLANES, SUBLANES), SUBLANES)


def _dot(a, b):
    return jnp.dot(a, b, preferred_element_type=F32)


def _dot_nt(a, b):
    return lax.dot_general(a, b, (((1,), (1,)), ((), ())), preferred_element_type=F32)


def _in_proj_kernel(x_ref, g_ref, w_ref, cos_a_ref, sin_a_ref, cos_b_ref, sin_b_ref,
                    qg_ref, kg_ref, seg_ref,
                    rq_ref, rk_ref, rv_ref, rg_ref, sq_ref, sk_ref, sv_ref):
    x = x_ref[...]
    h = x * lax.rsqrt(jnp.mean(x * x, axis=-1, keepdims=True) + EPS) * g_ref[...]
    p = _dot(h.astype(BF16), w_ref[...])
    k_scale = RET_HEAD_DIM ** -0.5
    cos_b = cos_b_ref[...]
    sin_b = sin_b_ref[...]
    first_half = lax.broadcasted_iota(jnp.int32, cos_b.shape, 1) < RET_HEAD_DIM // 2
    for r in range(x.shape[0] // LANES):
        rows = slice(r * LANES, (r + 1) * LANES)
        cos_a = cos_a_ref[r:r + 1, :]
        sin_a = sin_a_ref[r:r + 1, :]
        cos2 = cos_a * cos_b - sin_a * sin_b
        sin = sin_a * cos_b + cos_a * sin_b
        sin2 = jnp.where(first_half, -sin, sin)
        for hd in range(RET_HEADS):
            lo = hd * RET_HEAD_DIM
            q = p[rows, lo:lo + RET_HEAD_DIM]
            k = p[rows, RET_WIDTH + lo:RET_WIDTH + lo + RET_HEAD_DIM]
            q = q * cos2 + pltpu.roll(q, RET_HEAD_DIM // 2, axis=1) * sin2
            k = k * cos2 + pltpu.roll(k, RET_HEAD_DIM // 2, axis=1) * sin2
            rq_ref[rows, lo:lo + RET_HEAD_DIM] = q.astype(rq_ref.dtype)
            rk_ref[rows, lo:lo + RET_HEAD_DIM] = (k * k_scale).astype(rk_ref.dtype)
    rv_ref[...] = p[:, 2 * RET_WIDTH:3 * RET_WIDTH].astype(rv_ref.dtype)
    rg_ref[...] = p[:, 3 * RET_WIDTH:4 * RET_WIDTH]
    base = 4 * RET_WIDTH
    seg = seg_ref[...]

    def head_norm(v, gain):
        hi, lo = _split_bf16(v * v)
        group = seg.shape[0]
        ms = jnp.concatenate(
            [_dot(hi[:, g:g + group], seg) + _dot(lo[:, g:g + group], seg)
             for g in range(0, SB_WIDTH, group)], axis=1) * (1.0 / SB_HEAD_DIM)
        return v * lax.rsqrt(ms + EPS) * gain

    sq = head_norm(p[:, base:base + SB_WIDTH], qg_ref[...])
    sk = head_norm(p[:, base + SB_WIDTH:base + 2 * SB_WIDTH], kg_ref[...])
    sq_ref[...] = (sq * (SB_HEAD_DIM ** -0.5)).astype(sq_ref.dtype)
    sk_ref[...] = sk.astype(sk_ref.dtype)
    sv_ref[...] = p[:, base + 2 * SB_WIDTH:base + 3 * SB_WIDTH].astype(sv_ref.dtype)


def _rope_tables(seq, tm):
    half = RET_HEAD_DIM // 2
    per_tile = tm // LANES
    assert tm % LANES == 0 and per_tile <= SUBLANES and seq % tm == 0
    inv = ROPE_BASE ** (-np.arange(half, dtype=np.float64) / half)
    ang_a = (np.arange(seq // LANES, dtype=np.float64) * LANES)[:, None] * inv[None, :]
    ang_b = np.arange(LANES, dtype=np.float64)[:, None] * inv[None, :]
    dup = lambda a: np.concatenate([a, a], axis=-1)

    def coarse(a):
        a = dup(a).reshape(seq // tm, per_tile, RET_HEAD_DIM)
        return jnp.asarray(np.pad(a, ((0, 0), (0, SUBLANES - per_tile), (0, 0))), F32)

    return (coarse(np.cos(ang_a)), coarse(np.sin(ang_a)),
            jnp.asarray(dup(np.cos(ang_b)), F32), jnp.asarray(dup(np.sin(ang_b)), F32))


def _in_proj(x2, norm_g, w_in, sb_q_g, sb_k_g, seq, tm):
    t = x2.shape[0]
    assert seq % tm == 0 and t % tm == 0
    cos_a, sin_a, cos_b, sin_b = _rope_tables(seq, tm)
    seg = np.kron(np.eye(MXU_WIDTH // SB_HEAD_DIM), np.ones((SB_HEAD_DIM, SB_HEAD_DIM)))
    seg = jnp.asarray(seg, BF16)
    qg = jnp.tile(sb_q_g.astype(F32), SB_HEADS)[None, :]
    kg = jnp.tile(sb_k_g.astype(F32), SB_HEADS)[None, :]
    n_pos = seq // tm
    row = lambda w: pl.BlockSpec((tm, w), lambda i: (i, 0))
    full = lambda a: pl.BlockSpec(a.shape, lambda i: (0,) * a.ndim)
    pos = pl.BlockSpec((None, SUBLANES, RET_HEAD_DIM), lambda i: (i % n_pos, 0, 0))
    w_bf = w_in.astype(BF16)
    g2 = norm_g.astype(F32)[None, :]
    out = lambda dt: jax.ShapeDtypeStruct((t, RET_WIDTH), dt)
    return pl.pallas_call(
        _in_proj_kernel,
        grid=(t // tm,),
        in_specs=[row(D_MODEL), full(g2), full(w_bf), pos, pos, full(cos_b), full(sin_b),
                  full(qg), full(kg), full(seg)],
        out_specs=[row(RET_WIDTH)] * 7,
        out_shape=[out(BF16), out(F32), out(BF16), out(F32), out(BF16), out(BF16), out(BF16)],
        compiler_params=pltpu.CompilerParams(
            dimension_semantics=("arbitrary",), vmem_limit_bytes=VMEM_LIMIT),
        name="in_proj",
    )(x2, g2, w_bf, cos_a, sin_a, cos_b, sin_b, qg, kg, seg)


def _retention_kernel(q_ref, k_ref, v_ref, g_ref, ng_ref, intra_ref, qd_ref, kd_ref, cd_ref,
                      o_ref, state_ref, *, chunks):
    @pl.when(pl.program_id(1) == 0)
    def _():
        state_ref[...] = jnp.zeros_like(state_ref)

    def chunk(c, carry):
        rows = pl.ds(pl.multiple_of(c * RET_CHUNK, RET_CHUNK), RET_CHUNK)
        heads = range(RET_HEADS)
        cols = [slice(hd * RET_HEAD_DIM, (hd + 1) * RET_HEAD_DIM) for hd in heads]
        q = [q_ref[rows, cl] for cl in cols]
        k = [k_ref[rows, cl] for cl in cols]
        v = [v_ref[rows, cl] for cl in cols]
        state = [state_ref[hd] for hd in heads]
        scores = [_dot_nt(q[hd], k[hd].astype(BF16)) * intra_ref[hd] for hd in heads]
        cross = [_dot(q[hd], state[hd].astype(BF16)) * qd_ref[:, cols[hd]] for hd in heads]
        inner = [_dot(scores[hd].astype(BF16), v[hd]) for hd in heads]
        kd = [(k[hd] * kd_ref[:, cols[hd]]).astype(BF16) for hd in heads]
        for hd in heads:
            state_ref[hd] = state[hd] * cd_ref[:, cols[hd]] + _dot(kd[hd].T, v[hd])
        for hd in heads:
            o = inner[hd] + cross[hd]
            o = o * lax.rsqrt(jnp.mean(o * o, axis=-1, keepdims=True) + EPS) * ng_ref[:, cols[hd]]
            gate = g_ref[rows, cols[hd]]
            o_ref[rows, cols[hd]] = (o * (gate * jax.nn.sigmoid(gate))).astype(o_ref.dtype)
        return carry

    lax.fori_loop(0, chunks, chunk, 0, unroll=True)


def _retention(rq, rk, rv, rg, ret_norm_g, batch, seq, rows):
    assert seq % rows == 0 and rows % RET_CHUNK == 0
    log_gamma = np.log(1.0 - 2.0 ** (-5.0 - np.arange(RET_HEADS, dtype=np.float64)))
    idx = np.arange(RET_CHUNK, dtype=np.float64)
    diff = idx[:, None] - idx[None, :]
    intra = np.where(diff >= 0, np.exp(log_gamma[:, None, None] * np.maximum(diff, 0.0)), 0.0)
    q_decay = np.exp(log_gamma[:, None] * (idx + 1.0))
    k_decay = np.exp(log_gamma[:, None] * (RET_CHUNK - 1.0 - idx))
    chunk_decay = np.exp(log_gamma * RET_CHUNK)
    lane_rep = lambda a: np.repeat(a.T, RET_HEAD_DIM, axis=1)
    intra = jnp.asarray(intra, F32)
    qd = jnp.asarray(lane_rep(q_decay), F32)
    kd = jnp.asarray(lane_rep(k_decay), F32)
    cd = jnp.asarray(np.repeat(chunk_decay, RET_HEAD_DIM)[None, :], F32)
    ng = ret_norm_g.astype(F32).reshape(1, RET_WIDTH)
    n_r = seq // rows
    blk = pl.BlockSpec((rows, RET_WIDTH), lambda b, r: (b * n_r + r, 0))
    full = lambda a: pl.BlockSpec(a.shape, lambda b, r: (0,) * a.ndim)
    return pl.pallas_call(
        functools.partial(_retention_kernel, chunks=rows // RET_CHUNK),
        grid=(batch, n_r),
        in_specs=[blk, blk, blk, blk, full(ng), full(intra), full(qd), full(kd), full(cd)],
        out_specs=blk,
        out_shape=jax.ShapeDtypeStruct((batch * seq, RET_WIDTH), BF16),
        scratch_shapes=[pltpu.VMEM((RET_HEADS, RET_HEAD_DIM, RET_HEAD_DIM), F32)],
        compiler_params=pltpu.CompilerParams(
            dimension_semantics=("arbitrary", "arbitrary"), vmem_limit_bytes=VMEM_LIMIT),
        name="retention",
    )(rq, rk, rv, rg, ng, intra, qd, kd, cd)


def _sb_attn_kernel(q_ref, k_ref, v_ref, tri_ref, o_ref, *, blk, sub, heads):
    i = pl.program_id(2)
    nsub = blk // sub
    lane = lax.broadcasted_iota(jnp.int32, (sub, LANES), 1)
    first_head = lane < SB_HEAD_DIM
    tri = tri_ref[...]
    below_diag = (lax.broadcasted_iota(jnp.int32, (sub, sub), 1)
                  < lax.broadcasted_iota(jnp.int32, (sub, sub), 0))
    chains = [(s, h) for s in range(nsub) for h in range(heads)]
    cols = [slice((h // 2) * LANES, (h // 2 + 1) * LANES) for _, h in chains]


    def score_phase(qms, kbs, masks):
        zs = [_dot_nt(qm, kb) for qm, kb in zip(qms, kbs)]
        out = []
        for z, mask in zip(zs, masks):
            log_beta = jnp.minimum(z, 0.0) - jnp.log(1.0 + jnp.exp(-jnp.abs(z)))
            log_rest = log_beta - z
            if mask is not None:
                log_rest = jnp.where(mask, log_rest, 0.0)
            out.append((log_beta, log_rest))
        return out

    def later_phase(log_rests):
        parts = [_split_bf16(r) for r in log_rests]
        laters = [_dot(hi, tri) + _dot(lo, tri) for hi, lo in parts]
        return [(later, later[:, 0:1] + r[:, 0:1]) for later, r in zip(laters, log_rests)]

    def weight_phase(log_betas, log_remainings, masks):
        ws = [jnp.exp(b + r) for b, r in zip(log_betas, log_remainings)]
        ws = [w if m is None else jnp.where(m, w, 0.0) for w, m in zip(ws, masks)]
        return [w.astype(BF16) for w in ws]

    def queries():
        out = []
        for (s, h), c in zip(chains, cols):
            q = q_ref[s * sub:(s + 1) * sub, c]
            keep = first_head if h % 2 == 0 else jnp.logical_not(first_head)
            out.append(jnp.where(keep, q, jnp.zeros_like(q)))
        return out

    def key_rows(chunk):
        return pl.ds(pl.multiple_of(chunk * sub, sub), sub)

    def first_steps():
        qms = queries()
        own = [key_rows(i * nsub + s) for s, _ in chains]
        prev = [key_rows(jnp.maximum(i * nsub + s - 1, 0)) for s, _ in chains]
        prev_masks = [(i > 0) if s == 0 else None for s, _ in chains]
        sc_o = score_phase(qms, [k_ref[r, c] for r, c in zip(own, cols)], [below_diag] * len(chains))
        sc_p = score_phase(qms, [k_ref[r, c] for r, c in zip(prev, cols)], prev_masks)
        ls_o = later_phase([rest for _, rest in sc_o])
        ls_p = later_phase([rest for _, rest in sc_p])
        w_o = weight_phase([b for b, _ in sc_o], [later for later, _ in ls_o],
                           [below_diag] * len(chains))
        w_p = weight_phase([b for b, _ in sc_p],
                           [ls_o[n][1] + ls_p[n][0] for n in range(len(chains))], prev_masks)
        accs = [_dot(w, v_ref[r, c]) for w, r, c in zip(w_o, own, cols)]
        accs = [a + _dot(w, v_ref[r, c]) for a, w, r, c in zip(accs, w_p, prev, cols)]
        return tuple(ls_o[n][1] + ls_p[n][1] for n in range(len(chains))), tuple(accs)

    def cond(carry):
        j, cs, _ = carry
        c_max = functools.reduce(jnp.maximum, cs)
        return jnp.logical_and(j + (nsub - 1) >= 0, jnp.max(c_max) > F32_EXP_UNDERFLOW)

    def body(carry):
        j, cs, accs = carry
        rows = [key_rows(jnp.maximum(j + s, 0)) for s, _ in chains]
        masks = [None if s == nsub - 1 else (j + s >= 0) for s, _ in chains]
        sc = score_phase(queries(), [k_ref[r, c] for r, c in zip(rows, cols)], masks)
        ls = later_phase([rest for _, rest in sc])
        ws = weight_phase([b for b, _ in sc], [cs[n] + ls[n][0] for n in range(len(chains))], masks)
        new_accs = tuple(a + _dot(w, v_ref[r, c]) for a, w, r, c in zip(accs, ws, rows, cols))
        return j - 1, tuple(cs[n] + ls[n][1] for n in range(len(chains))), new_accs

    init = (i * nsub - 2,) + first_steps()
    _, _, accs = lax.while_loop(cond, body, init)
    for s in range(nsub):
        for p in range(heads // 2):
            pair = [accs[n] for n, (ss, h) in enumerate(chains) if ss == s and h // 2 == p]
            o_ref[s * sub:(s + 1) * sub, p * LANES:(p + 1) * LANES] = jnp.where(
                first_head, pair[0], pair[1]).astype(o_ref.dtype)


def _sb_attention(sq, sk, sv, batch, seq, blk, sub, heads):
    assert seq % blk == 0 and blk % sub == 0 and sub == LANES
    assert heads % 2 == 0 and SB_HEADS % heads == 0
    nq = seq // blk
    width = heads * SB_HEAD_DIM
    groups = SB_WIDTH // width
    j = np.arange(sub)
    tri = jnp.asarray((j[:, None] > j[None, :]).astype(np.float32), BF16)
    qblk = pl.BlockSpec((blk, width), lambda b, g, i: (b * nq + i, g))
    kvblk = pl.BlockSpec((seq, width), lambda b, g, i: (b, g))
    return pl.pallas_call(
        functools.partial(_sb_attn_kernel, blk=blk, sub=sub, heads=heads),
        grid=(batch, groups, nq),
        in_specs=[qblk, kvblk, kvblk, pl.BlockSpec(tri.shape, lambda b, g, i: (0, 0))],
        out_specs=qblk,
        out_shape=jax.ShapeDtypeStruct((batch * seq, SB_WIDTH), BF16),
        compiler_params=pltpu.CompilerParams(
            dimension_semantics=("arbitrary", "arbitrary", "arbitrary"),
            vmem_limit_bytes=VMEM_LIMIT),
        name="sb_attn",
    )(sq, sk, sv, tri)


OUT_PROJ_ROW_GROUPS = 4


def _out_proj_kernel(ret_ref, sb_ref, x_ref, wtop_ref, wbot_ref, g_ref, wrh_ref, wrl_ref,
                     br_ref, tri_ref,
                     x1_ref, xf_ref, idx_ref, w_ref, rank_ref, cnt_ref, run_ref):
    @pl.when(pl.program_id(0) == 0)
    def _():
        run_ref[...] = jnp.zeros_like(run_ref)

    rows = x_ref.shape[0] // OUT_PROJ_ROW_GROUPS
    groups = [slice(r * rows, (r + 1) * rows) for r in range(OUT_PROJ_ROW_GROUPS)]
    x1s = [x_ref[g, :] + _dot(ret_ref[g, :], wtop_ref[...]) + _dot(sb_ref[g, :], wbot_ref[...])
           for g in groups]
    for g, x1 in zip(groups, x1s):
        x1_ref[g, :] = x1
    xfs = [x1 * lax.rsqrt(jnp.mean(x1 * x1, axis=-1, keepdims=True) + EPS) * g_ref[...]
           for x1 in x1s]
    for r, xf in enumerate(xfs):
        _store_token_tiles(xf_ref.at[pl.ds(r * rows * SUBLANES, rows * SUBLANES), :], xf)
    parts = [_split_bf16(xf) for xf in xfs]
    wrh = wrh_ref[...]
    wrl = wrl_ref[...]
    logits = jnp.concatenate(
        [_dot_nt(wrh, xh) + _dot_nt(wrh, xl) + _dot_nt(wrl, xh) for xh, xl in parts],
        axis=1) + br_ref[...]
    n_e, tm = logits.shape
    e_iota = lax.broadcasted_iota(jnp.int32, (n_e, tm), 0)
    cur = logits
    tops, sels, hots = [], [], []
    for _ in range(TOP_K):
        m = jnp.max(cur, axis=0, keepdims=True)
        sel = jnp.min(jnp.where(cur == m, e_iota, n_e), axis=0, keepdims=True)
        hot = e_iota == sel
        cur = jnp.where(hot, -jnp.inf, cur)
        tops.append(m)
        sels.append(sel)
        hots.append(hot)
    ps = [jnp.exp(m - tops[0]) for m in tops]
    denom = ps[0] + ps[1] + ps[2] + ps[3]
    chosen = jnp.zeros((n_e, tm), F32)
    for hot in hots:
        chosen = chosen + hot.astype(F32)
    before = run_ref[:, 0:1] + _dot(chosen.astype(BF16), tri_ref[...])
    for kk in range(TOP_K):
        idx_ref[kk:kk + 1, :] = sels[kk]
        w_ref[kk:kk + 1, :] = ps[kk] / denom
        rank = jnp.sum(jnp.where(hots[kk], before, 0.0), axis=0, keepdims=True)
        rank_ref[kk:kk + 1, :] = rank.astype(jnp.int32)
    run = run_ref[...] + jnp.sum(chosen, axis=1, keepdims=True)
    run_ref[...] = run
    cnt_ref[...] = run.astype(jnp.int32)


def _out_proj_route(ret, sb, x2, w_out, ffn_g, w_router, b_router, tm):
    t = x2.shape[0]
    assert t % tm == 0
    w_bf = w_out.astype(BF16)
    wtop, wbot = w_bf[:RET_WIDTH], w_bf[RET_WIDTH:]
    g2 = ffn_g.astype(F32)[None, :]
    wr_t = w_router.astype(F32).T
    wrh = wr_t.astype(BF16)
    wrl = (wr_t - wrh.astype(F32)).astype(BF16)
    br = b_router.astype(F32)[:, None]
    tt = np.arange(tm)
    tri = jnp.asarray((tt[:, None] < tt[None, :]).astype(np.float32), BF16)
    row = lambda w: pl.BlockSpec((tm, w), lambda i: (i, 0))
    full = lambda a: pl.BlockSpec(a.shape, lambda i: (0,) * a.ndim)
    col = pl.BlockSpec((TOP_K, tm), lambda i: (0, i))
    return pl.pallas_call(
        _out_proj_kernel,
        grid=(t // tm,),
        in_specs=[row(RET_WIDTH), row(SB_WIDTH), row(D_MODEL), full(wtop), full(wbot), full(g2),
                  full(wrh), full(wrl), full(br), full(tri)],
        out_specs=[row(D_MODEL), pl.BlockSpec((tm * SUBLANES, LANES), lambda i: (i, 0)),
                   col, col, col, pl.BlockSpec((N_EXPERTS, LANES), lambda i: (0, 0))],
        out_shape=[jax.ShapeDtypeStruct((t, D_MODEL), F32),
                   jax.ShapeDtypeStruct((t * SUBLANES, LANES), F32),
                   jax.ShapeDtypeStruct((TOP_K, t), jnp.int32),
                   jax.ShapeDtypeStruct((TOP_K, t), F32),
                   jax.ShapeDtypeStruct((TOP_K, t), jnp.int32),
                   jax.ShapeDtypeStruct((N_EXPERTS, LANES), jnp.int32)],
        scratch_shapes=[pltpu.VMEM((N_EXPERTS, LANES), F32)],
        compiler_params=pltpu.CompilerParams(
            dimension_semantics=("arbitrary",), vmem_limit_bytes=VMEM_LIMIT),
        name="out_proj_route",
    )(ret, sb, x2, wtop, wbot, g2, wrh, wrl, br, tri)


def _dest_kernel(pstart_ref, idx_ref, rank_ref, dest_ref):
    idx = idx_ref[...]
    dest = rank_ref[...]
    for e in range(N_EXPERTS):
        dest = dest + jnp.where(idx == e, pstart_ref[e], 0)
    dest_ref[...] = dest


def _dest(padded_start, top_idx, rank, tn):
    t = top_idx.shape[1]
    assert t % tn == 0
    col = pl.BlockSpec((TOP_K, tn), lambda i, ps: (0, i))
    return pl.pallas_call(
        _dest_kernel,
        grid_spec=pltpu.PrefetchScalarGridSpec(
            num_scalar_prefetch=1, grid=(t // tn,), in_specs=[col, col], out_specs=col),
        out_shape=jax.ShapeDtypeStruct((TOP_K, t), jnp.int32),
        name="dest",
    )(padded_start, top_idx, rank)


def _dispatch_kernel(pend_ref, dest_ref, xf_ref, slots_hbm, zbuf, sem, zsem, *, tm, blk):
    @pl.when(pl.program_id(0) == 0)
    def _():
        zbuf[...] = jnp.zeros_like(zbuf)

        def tail_copy(e):
            first = pl.multiple_of((pend_ref[e] - blk) * SUBLANES, blk * SUBLANES)
            return pltpu.make_async_copy(
                zbuf, slots_hbm.at[pl.ds(first, blk * SUBLANES), :], zsem)

        def nonempty(e):
            return pend_ref[e] > (pend_ref[e - 1] if e else 0)

        for e in range(N_EXPERTS):
            pl.when(nonempty(e))(lambda e=e: tail_copy(e).start())
        for e in range(N_EXPERTS):
            pl.when(nonempty(e))(lambda e=e: tail_copy(e).wait())

    def row_copy(t, kk):
        return pltpu.make_async_copy(
            xf_ref.at[_token_rows(t), :],
            slots_hbm.at[_token_rows(dest_ref[kk, t]), :], sem)

    def start(t, carry):
        for kk in range(TOP_K):
            row_copy(t, kk).start(priority=kk % 2)
        return carry

    lax.fori_loop(0, tm, start, 0, unroll=8)
    for kk in range(TOP_K):
        pltpu.make_async_copy(
            xf_ref, slots_hbm.at[pl.ds(0, tm * SUBLANES), :], sem).wait()


def _dispatch(padded_end, dest, xf, n_pad, tm, blk):
    t = xf.shape[0] // SUBLANES
    assert t % tm == 0
    return pl.pallas_call(
        functools.partial(_dispatch_kernel, tm=tm, blk=blk),
        grid_spec=pltpu.PrefetchScalarGridSpec(
            num_scalar_prefetch=1,
            grid=(t // tm,),
            in_specs=[pl.BlockSpec((TOP_K, tm), lambda i, pe: (0, i), memory_space=pltpu.SMEM),
                      pl.BlockSpec((tm * SUBLANES, LANES), lambda i, pe: (i, 0))],
            out_specs=pl.BlockSpec(memory_space=pl.ANY),
            scratch_shapes=[pltpu.VMEM((blk * SUBLANES, LANES), F32),
                            pltpu.SemaphoreType.DMA, pltpu.SemaphoreType.DMA],
        ),
        out_shape=jax.ShapeDtypeStruct((n_pad * SUBLANES, LANES), F32),
        compiler_params=pltpu.CompilerParams(
            dimension_semantics=("arbitrary",), vmem_limit_bytes=VMEM_LIMIT,
            has_side_effects=True),
        name="dispatch",
    )(padded_end, dest, xf)


WEIGHT_ROWS = 128


def _experts_kernel(be_ref, nu_ref, run_ref, next_ref, rows_ref, x_ref, wgu_hbm, wd_hbm, bg_ref,
                    bu_ref, bd_ref, y_ref, wgu_buf, wd_buf, wg_s, wu_s, wd_s, t_s, sem, *, blk):
    j = pl.program_id(0)
    used = j < nu_ref[0]
    expert = be_ref[j]
    new_expert = jnp.logical_or(j == 0, expert != be_ref[jnp.maximum(j - 1, 0)])

    def weight_copies(e, slot):
        return (pltpu.make_async_copy(wgu_hbm.at[e], wgu_buf.at[slot], sem.at[0, slot]),
                pltpu.make_async_copy(wd_hbm.at[e], wd_buf.at[slot], sem.at[1, slot]))

    def expert_ffn_tail(gate, up, out_ref=y_ref):
        gate = jnp.minimum(gate, SWIGLU_LIMIT)
        up = jnp.clip(up, -SWIGLU_LIMIT, SWIGLU_LIMIT)
        hidden = (up + 1.0) * gate * jax.nn.sigmoid(SWIGLU_ALPHA * gate)
        _store_token_tiles(out_ref, _dot(hidden.astype(BF16), wd_s[...]) + bd_ref[...])

    @pl.when(jnp.logical_and(used, new_expert))
    def _():
        slot = lax.rem(run_ref[expert], 2)

        @pl.when(j == 0)
        def _():
            for c in weight_copies(expert, slot):
                c.start()

        for c in weight_copies(expert, slot):
            c.wait()
        following = next_ref[expert]

        @pl.when(following >= 0)
        def _():
            for c in weight_copies(following, 1 - slot):
                c.start()

        xb = _load_token_tiles(x_ref, blk).astype(BF16)
        gate = bg_ref[...]
        up = bu_ref[...]
        for r in range(0, D_MODEL, MXU_WIDTH):
            wg_rows, wu_rows = [], []
            for rr in range(r, r + MXU_WIDTH, WEIGHT_ROWS):
                rows = slice(rr, rr + WEIGHT_ROWS)
                t_s[...] = wgu_buf[slot, rows, :].T
                wg_rows.append(t_s[pl.ds(0, D_FF, stride=2), :].T.astype(BF16))
                wu_rows.append(t_s[pl.ds(1, D_FF, stride=2), :].T.astype(BF16))
                wg_s[rows, :] = wg_rows[-1]
                wu_s[rows, :] = wu_rows[-1]
                wd_s[rows, :] = wd_buf[slot, rows, :].astype(BF16)
            x_cols = xb[:, r:r + MXU_WIDTH]
            gate = gate + _dot(x_cols, jnp.concatenate(wg_rows, axis=0))
            up = up + _dot(x_cols, jnp.concatenate(wu_rows, axis=0))
        expert_ffn_tail(gate, up)

    later_block = jnp.logical_and(used, jnp.logical_not(new_expert))
    half = blk // 2
    half_full = rows_ref[j] <= half

    @pl.when(jnp.logical_and(later_block, jnp.logical_not(half_full)))
    def _():
        xb = _load_token_tiles(x_ref, blk).astype(BF16)
        expert_ffn_tail(_dot(xb, wg_s[...]) + bg_ref[...], _dot(xb, wu_s[...]) + bu_ref[...])

    @pl.when(jnp.logical_and(later_block, half_full))
    def _():
        xb = _load_token_tiles(x_ref.at[pl.ds(0, half * SUBLANES), :], half).astype(BF16)
        expert_ffn_tail(_dot(xb, wg_s[...]) + bg_ref[...], _dot(xb, wu_s[...]) + bu_ref[...],
                        y_ref.at[pl.ds(0, half * SUBLANES), :])
        y_ref[pl.ds(half * SUBLANES, half * SUBLANES), :] = jnp.zeros(
            (half * SUBLANES, LANES), y_ref.dtype)

    @pl.when(jnp.logical_not(used))
    def _():
        y_ref[...] = jnp.zeros_like(y_ref)


def _experts(block_expert, n_used, expert_run, next_expert, block_rows, slots, wgu, wd, bg, bu,
             bd, blk):
    assert D_FF == D_MODEL
    n_pad = slots.shape[0] // SUBLANES
    n_blocks = n_pad // blk
    xmap = lambda j, be, nu, run, nxt, rows: (jnp.minimum(j, nu[0] - 1), 0)
    wmap = lambda j, be, nu, run, nxt, rows: (be[j], 0, 0)
    bspec = pl.BlockSpec((None, 1, D_FF), wmap)
    tiles = (blk * SUBLANES, LANES)
    hbm = pl.BlockSpec(memory_space=pl.ANY)
    return pl.pallas_call(
        functools.partial(_experts_kernel, blk=blk),
        grid_spec=pltpu.PrefetchScalarGridSpec(
            num_scalar_prefetch=5,
            grid=(n_blocks,),
            in_specs=[pl.BlockSpec(tiles, xmap), hbm, hbm, bspec, bspec, bspec],
            out_specs=pl.BlockSpec(tiles, lambda j, be, nu, run, nxt, rows: (j, 0)),
            scratch_shapes=[pltpu.VMEM((2, D_MODEL, 2 * D_FF), F32),
                            pltpu.VMEM((2, D_FF, D_MODEL), F32),
                            pltpu.VMEM((D_MODEL, D_FF), BF16), pltpu.VMEM((D_MODEL, D_FF), BF16),
                            pltpu.VMEM((D_FF, D_MODEL), BF16),
                            pltpu.VMEM((2 * D_FF, WEIGHT_ROWS), F32),
                            pltpu.SemaphoreType.DMA((2, 2))],
        ),
        out_shape=jax.ShapeDtypeStruct((n_pad * SUBLANES, LANES), F32),
        compiler_params=pltpu.CompilerParams(
            dimension_semantics=("arbitrary",), vmem_limit_bytes=VMEM_LIMIT),
        name="experts",
    )(block_expert, n_used, expert_run, next_expert, block_rows, slots, wgu, wd, bg, bu, bd)


def _combine_kernel(dest_ref, next_dest_ref, y_hbm, w_ref, x1_ref, o_ref, buf, sem, *, tm):
    i = pl.program_id(0)
    n = pl.num_programs(0)
    slot = lax.rem(i, 2)

    def gather(dests, into):
        def start(t, carry):
            for kk in range(TOP_K):
                pltpu.make_async_copy(
                    y_hbm.at[_token_rows(dests[kk, t]), :],
                    buf.at[into, kk, _token_rows(t), :], sem.at[into]).start(priority=kk % 2)
            return carry

        lax.fori_loop(0, tm, start, 0, unroll=8)

    pl.when(i == 0)(lambda: gather(dest_ref, 0))
    pl.when(i + 1 < n)(lambda: gather(next_dest_ref, 1 - slot))
    for kk in range(TOP_K):
        pltpu.make_async_copy(
            y_hbm.at[pl.ds(0, tm * SUBLANES), :], buf.at[slot, kk], sem.at[slot]).wait()
    w = w_ref[...]
    for s in range(ROW_TILES):
        cols = slice(s * LANES, (s + 1) * LANES)
        acc = x1_ref[:, cols]
        for kk in range(TOP_K):
            acc = acc + buf[slot, kk, pl.ds(s, tm, stride=SUBLANES), :] * w[:, kk:kk + 1]
        o_ref[:, cols] = acc


def _combine(dest, y_slots, w_t, x1, tm):
    t = x1.shape[0]
    assert t % tm == 0
    n = t // tm
    row = pl.BlockSpec((tm, D_MODEL), lambda i: (i, 0))
    return pl.pallas_call(
        functools.partial(_combine_kernel, tm=tm),
        grid=(n,),
        in_specs=[pl.BlockSpec((TOP_K, tm), lambda i: (0, i), memory_space=pltpu.SMEM),
                  pl.BlockSpec((TOP_K, tm), lambda i: (0, jnp.minimum(i + 1, n - 1)),
                               memory_space=pltpu.SMEM),
                  pl.BlockSpec(memory_space=pl.ANY),
                  pl.BlockSpec((tm, TOP_K), lambda i: (i, 0)),
                  row],
        out_specs=row,
        out_shape=jax.ShapeDtypeStruct((t, D_MODEL), F32),
        scratch_shapes=[pltpu.VMEM((2, TOP_K, tm * SUBLANES, LANES), F32),
                        pltpu.SemaphoreType.DMA((2,))],
        compiler_params=pltpu.CompilerParams(
            dimension_semantics=("arbitrary",), vmem_limit_bytes=VMEM_LIMIT),
        name="combine",
    )(dest, dest, y_slots, w_t, x1)


def _tiles(batch, seq):
    return dict(
        in_proj=min(512, seq),
        retention=min(1024, seq),
        sb_block=min(512, seq),
        sb_heads=4,
        out_proj=min(1024, seq),
        dest=min(8192, batch * seq),
        dispatch=min(4096, seq),
        combine=min(512, seq),
        expert_block=512,
    )


def _layer(x, attn_norm_g, w_in, ret_norm_g, sb_q_norm_g, sb_k_norm_g, w_out,
           ffn_norm_g, w_router, b_router, w_gate_up, b_gate_up, w_down, b_down):
    batch, seq, d = x.shape
    t = batch * seq
    tiles = _tiles(batch, seq)
    x2 = x.reshape(t, d)

    rq, rk, rv, rg, sq, sk, sv = _in_proj(
        x2, attn_norm_g, w_in, sb_q_norm_g, sb_k_norm_g, seq, tiles["in_proj"])
    ret = _retention(rq, rk, rv, rg, ret_norm_g, batch, seq, tiles["retention"])
    sb = _sb_attention(sq, sk, sv, batch, seq, tiles["sb_block"], LANES, tiles["sb_heads"])
    x1, xf, top_idx, top_w, rank, counts = _out_proj_route(
        ret, sb, x2, w_out, ffn_norm_g, w_router, b_router, tiles["out_proj"])

    blk = tiles["expert_block"]
    n_pad = t * TOP_K + N_EXPERTS * blk
    n_blocks = n_pad // blk
    counts = counts[:, 0]
    padded = (counts + blk - 1) // blk * blk
    padded_end = jnp.cumsum(padded)
    padded_start = padded_end - padded
    block_first = jnp.arange(n_blocks, dtype=jnp.int32) * blk
    block_expert = jnp.minimum(
        jnp.sum(padded_end[None, :] <= block_first[:, None], axis=1), N_EXPERTS - 1
    ).astype(jnp.int32)
    n_used = (padded_end[-1:] // blk).astype(jnp.int32)
    nonempty = counts > 0
    expert_run = (jnp.cumsum(nonempty) - nonempty).astype(jnp.int32)
    experts = jnp.arange(N_EXPERTS, dtype=jnp.int32)
    later = jnp.logical_and(nonempty[None, :], experts[None, :] > experts[:, None])
    next_expert = jnp.min(jnp.where(later, experts[None, :], N_EXPERTS), axis=1)
    next_expert = jnp.where(next_expert < N_EXPERTS, next_expert, -1).astype(jnp.int32)

    dest = _dest(padded_start, top_idx, rank, tiles["dest"])
    slots = _dispatch(padded_end, dest, xf, n_pad, tiles["dispatch"], blk)

    bgu = b_gate_up.astype(F32).reshape(N_EXPERTS, 1, D_FF, 2)
    owner = block_expert[:, None] == experts[None, :]
    owner_end = jnp.sum(jnp.where(owner, (padded_start + counts)[None, :], 0), axis=1)
    block_rows = jnp.clip(owner_end - block_first, 0, blk).astype(jnp.int32)
    y_slots = _experts(block_expert, n_used, expert_run, next_expert, block_rows, slots,
                       w_gate_up.astype(F32), w_down.astype(F32),
                       bgu[..., 0], bgu[..., 1], b_down.astype(F32)[:, None, :], blk)

    out = _combine(dest, y_slots, top_w.T, x1, tiles["combine"])
    return out.reshape(batch, seq, d)


def kernel(x, attn_norm_g, w_in, ret_norm_g, sb_q_norm_g, sb_k_norm_g, w_out, ffn_norm_g,
           w_router, b_router, w_gate_up, b_gate_up, w_down, b_down):
    depth = attn_norm_g.shape[0]
    for l in range(depth):
        x = _layer(x, attn_norm_g[l], w_in[l], ret_norm_g[l], sb_q_norm_g[l], sb_k_norm_g[l],
                   w_out[l], ffn_norm_g[l], w_router[l], b_router[l], w_gate_up[l],
                   b_gate_up[l], w_down[l], b_down[l])
    return x
```
